```python
import math
import jax, jax.numpy as jnp
from jax import lax
import numpy as np

D_MODEL = 1024
BATCH = 4
SEQ = 8192
DEPTH = 1

PLE_DIM = 256
NORM_EPS = 1e-6
DA_HEADS = 4
DA_HEAD_DIM = 64
DA_V_DIM = 2 * DA_HEAD_DIM
DA_QK = DA_HEADS * 2 * DA_HEAD_DIM
DA_WIDTH = DA_HEADS * DA_V_DIM
DA_IN_WIDTH = 2 * DA_QK + DA_WIDTH
ROPE_THETA = 500000.0
ROT_DIM = DA_HEAD_DIM // 4
Q_BLOCK = 128
RW_HEAD = 64
RW_WIDTH = D_MODEL - DA_WIDTH
RW_HEADS = RW_WIDTH // RW_HEAD
DECAY_LORA = 64
AAA_LORA = 64
GATE_LORA = 128
RW_IN_WIDTH = 3 * RW_WIDTH + DECAY_LORA + AAA_LORA + GATE_LORA
RW_GN_EPS = 64e-5
IN_WIDTH = DA_IN_WIDTH + RW_IN_WIDTH
PEER_HEADS = 8
N_KEYS = 128
N_EXPERTS = N_KEYS * N_KEYS
PEER_TOPK = 16
PEER_QDIM = 256
PEER_HALF = PEER_QDIM // 2
TOKEN_BLOCK = 128

kernel_name = "hymba_diffattn_rwkv7_peer_block"


def rms_norm(x, g, eps=NORM_EPS):
    xf = x.astype(jnp.float32)
    y = xf * lax.rsqrt(jnp.mean(xf * xf, axis=-1, keepdims=True) + eps)
    return (y * g.astype(jnp.float32)).astype(x.dtype)


def partial_rope(t, positions):
    half = ROT_DIM // 2
    inv_freq = ROPE_THETA ** (-jnp.arange(half, dtype=jnp.float32) * 2.0 / ROT_DIM)
    ang = positions.astype(jnp.float32)[..., None] * inv_freq
    cos = jnp.cos(ang)[:, :, None, None, :]
    sin = jnp.sin(ang)[:, :, None, None, :]
    tf = t.astype(jnp.float32)
    t1 = tf[..., :half]
    t2 = tf[..., half:ROT_DIM]
    out = jnp.concatenate([t1 * cos - t2 * sin, t2 * cos + t1 * sin, tf[..., ROT_DIM:]], axis=-1)
    return out.astype(t.dtype)


def diff_attention(z_da, positions, lam_q1, lam_k1, lam_q2, lam_k2, subln_g, lam_init):
    B, S, _ = z_da.shape
    q, k, v = jnp.split(z_da, [DA_QK, 2 * DA_QK], axis=-1)
    q = partial_rope(q.reshape(B, S, DA_HEADS, 2, DA_HEAD_DIM), positions)
    k = partial_rope(k.reshape(B, S, DA_HEADS, 2, DA_HEAD_DIM), positions)
    q = (q * DA_HEAD_DIM ** -0.5).transpose(0, 2, 3, 1, 4)
    k = k.transpose(0, 2, 3, 1, 4)
    v = v.reshape(B, S, DA_HEADS, DA_V_DIM).transpose(0, 2, 1, 3)
    f32 = jnp.float32
    lam = (jnp.exp(jnp.sum(lam_q1.astype(f32) * lam_k1.astype(f32)))
           - jnp.exp(jnp.sum(lam_q2.astype(f32) * lam_k2.astype(f32))) + lam_init)
    key_pos = jnp.arange(S)

    def block(bi):
        start = bi * Q_BLOCK
        qb = lax.dynamic_slice_in_dim(q, start, Q_BLOCK, axis=3)
        s = jnp.einsum('bhcqd,bhckd->bhcqk', qb, k).astype(f32)
        causal = key_pos[None, :] <= (start + jnp.arange(Q_BLOCK))[:, None]
        pr = jax.nn.softmax(jnp.where(causal, s, -jnp.inf), axis=-1)
        a = pr[:, :, 0] - lam * pr[:, :, 1]
        return jnp.einsum('bhqk,bhkv->bhqv', a.astype(v.dtype), v)

    o = lax.map(block, jnp.arange(S // Q_BLOCK))
    o = o.transpose(1, 0, 3, 2, 4).reshape(B, S, DA_HEADS, DA_V_DIM)
    o = rms_norm(o, subln_g) * (1.0 - lam_init)
    return o.reshape(B, S, DA_WIDTH).astype(z_da.dtype)


def rwkv7_mix(z_rw, mu, w0, w_up, a0, a_up, g_up, k_k, k_a, r_k, ln_g, ln_b):
    B, S, _ = z_rw.shape
    f32 = jnp.float32
    prev = jnp.pad(z_rw[:, :-1], ((0, 0), (1, 0), (0, 0)))
    zs = z_rw + (prev - z_rw) * mu
    o1 = 3 * RW_WIDTH
    r, k, v, xw, xa, xg = jnp.split(
        zs, [RW_WIDTH, 2 * RW_WIDTH, o1, o1 + DECAY_LORA, o1 + DECAY_LORA + AAA_LORA], axis=-1)
    w = -jax.nn.softplus(-(w0 + jnp.tanh(xw) @ w_up)) - 0.5
    a = jax.nn.sigmoid(a0 + xa @ a_up)
    g = jax.nn.sigmoid(xg) @ g_up
    kk = k * k_k
    k = k * (1.0 + (a - 1.0) * k_a)

    def heads(t):
        return t.reshape(B, S, RW_HEADS, RW_HEAD).astype(f32)

    kk = heads(kk)
    kk = kk / jnp.maximum(jnp.sqrt(jnp.sum(kk * kk, axis=-1, keepdims=True)), 1e-12)
    r, k, v, a = heads(r), heads(k), heads(v), heads(a)
    decay = jnp.exp(-jnp.exp(heads(w)))

    def step(state, inp):
        r_t, d_t, k_t, v_t, kk_t, a_t = inp
        sa = jnp.einsum('bhij,bhj->bhi', state, -kk_t)
        state = (state * d_t[:, :, None, :] + sa[..., None] * (kk_t * a_t)[:, :, None, :]
                 + v_t[..., None] * k_t[:, :, None, :])
        return state, jnp.einsum('bhij,bhj->bhi', state, r_t)

    def tm(t):
        return jnp.moveaxis(t, 1, 0)

    s0 = jnp.zeros((B, RW_HEADS, RW_HEAD, RW_HEAD), f32)
    _, y = lax.scan(step, s0, (tm(r), tm(decay), tm(k), tm(v), tm(kk), tm(a)))
    y = jnp.moveaxis(y, 0, 1)
    mean = jnp.mean(y, axis=-1, keepdims=True)
    var = jnp.mean(jnp.square(y - mean), axis=-1, keepdims=True)
    y = ((y - mean) * lax.rsqrt(var + RW_GN_EPS) * ln_g.astype(f32).reshape(RW_HEADS, RW_HEAD)
         + ln_b.astype(f32).reshape(RW_HEADS, RW_HEAD))
    y = y + jnp.sum(r * k * r_k.astype(f32), axis=-1, keepdims=True) * v
    y = y.reshape(B, S, RW_WIDTH) * g.astype(f32)
    return y.astype(z_rw.dtype)


def peer_ffn(u, w_q, sub_keys, exp_u, exp_v):
    B, S, D = u.shape
    f32 = jnp.float32
    tokens = u.reshape(-1, TOKEN_BLOCK, D)

    def block(xc):
        q = (xc @ w_q).reshape(TOKEN_BLOCK, PEER_HEADS, 2, PEER_HALF)
        s = jnp.einsum('chpd,hpnd->chpn', q, sub_keys).astype(f32)
        s_top, i_top = lax.top_k(s, PEER_TOPK)
        cand = (s_top[:, :, 0, :, None] + s_top[:, :, 1, None, :]).reshape(
            TOKEN_BLOCK, PEER_HEADS, PEER_TOPK * PEER_TOPK)
        cand_idx = (i_top[:, :, 0, :, None] * N_KEYS + i_top[:, :, 1, None, :]).reshape(
            TOKEN_BLOCK, PEER_HEADS, PEER_TOPK * PEER_TOPK)
        best, pos = lax.top_k(cand, PEER_TOPK)
        idx = jnp.take_along_axis(cand_idx, pos, axis=-1)
        gate = jax.nn.softmax(best, axis=-1)
        uu = jnp.take(exp_u, idx, axis=0)
        vv = jnp.take(exp_v, idx, axis=0)
        hid = jax.nn.gelu(jnp.einsum('chkd,cd->chk', uu, xc).astype(f32), approximate=False)
        return jnp.einsum('chk,chkd->cd', (gate * hid).astype(vv.dtype), vv)

    out = lax.map(block, tokens)
    return out.reshape(B, S, D).astype(u.dtype)


def setup_inputs(seed: int = 0) -> dict:
    key = jax.random.key(seed)
    ks = iter(jax.random.split(key, 40))
    f32 = jnp.float32
    L, D = DEPTH, D_MODEL

    def nrm(shape, scale):
        return jax.random.normal(next(ks), shape, f32) * scale

    def gain(shape):
        return 1.0 + nrm(shape, 0.02)

    return {
        "x": nrm((BATCH, SEQ, D), 1.0),
        "p": nrm((DEPTH, BATCH, SEQ, PLE_DIM), 1.0),
        "positions": jnp.broadcast_to(jnp.arange(SEQ, dtype=jnp.int32), (BATCH, SEQ)),
        "norm_mix_g": gain((L, D)),
        "w_in": nrm((L, D, IN_WIDTH), D ** -0.5),
        "lam_q1": nrm((L, DA_HEAD_DIM), 0.1),
        "lam_k1": nrm((L, DA_HEAD_DIM), 0.1),
        "lam_q2": nrm((L, DA_HEAD_DIM), 0.1),
        "lam_k2": nrm((L, DA_HEAD_DIM), 0.1),
        "da_subln_g": gain((L, DA_V_DIM)),
        "rw_mu": jax.random.uniform(next(ks), (L, RW_IN_WIDTH), f32),
        "rw_w0": jax.random.uniform(next(ks), (L, RW_WIDTH), f32, -4.0, 1.0),
        "rw_w_up": nrm((L, DECAY_LORA, RW_WIDTH), 0.1 * DECAY_LORA ** -0.5),
        "rw_a0": nrm((L, RW_WIDTH), 0.1),
        "rw_a_up": nrm((L, AAA_LORA, RW_WIDTH), 0.5 * AAA_LORA ** -0.5),
        "rw_g_up": nrm((L, GATE_LORA, RW_WIDTH), GATE_LORA ** -0.5),
        "rw_k_k": 0.85 + nrm((L, RW_WIDTH), 0.05),
        "rw_k_a": 1.0 + nrm((L, RW_WIDTH), 0.05),
        "rw_r_k": nrm((L, RW_HEADS, RW_HEAD), 0.1),
        "rw_ln_g": gain((L, RW_WIDTH)),
        "rw_ln_b": nrm((L, RW_WIDTH), 0.02),
        "w_out": nrm((L, D, D), D ** -0.5),
        "norm_ffn_g": gain((L, D)),
        "peer_w_q": nrm((L, D, PEER_HEADS * PEER_QDIM), D ** -0.5),
        "peer_sub_keys": nrm((L, PEER_HEADS, 2, N_KEYS, PEER_HALF), PEER_HALF ** -0.5),
        "peer_u": nrm((L, N_EXPERTS, D), D ** -0.5),
        "peer_v": nrm((L, N_EXPERTS, D), 0.5 * PEER_HEADS ** -0.5),
        "norm_ple_g": gain((L, D)),
        "ple_gate_w": nrm((L, D, D), D ** -0.5),
        "ple_proj_w": nrm((L, PLE_DIM, D), PLE_DIM ** -0.5),
        "norm_final_g": gain((D,)),
    }


def reference(x, p, positions, norm_mix_g, w_in, lam_q1, lam_k1, lam_q2, lam_k2, da_subln_g,
              rw_mu, rw_w0, rw_w_up, rw_a0, rw_a_up, rw_g_up, rw_k_k, rw_k_a, rw_r_k,
              rw_ln_g, rw_ln_b, w_out, norm_ffn_g, peer_w_q, peer_sub_keys, peer_u, peer_v,
              norm_ple_g, ple_gate_w, ple_proj_w, norm_final_g):
    h = x
    for i in range(DEPTH):
        u = rms_norm(h, norm_mix_g[i])
        z = u @ w_in[i]
        lam_init = 0.8 - 0.6 * math.exp(-0.3 * i)
        o_da = diff_attention(z[..., :DA_IN_WIDTH], positions, lam_q1[i], lam_k1[i],
                              lam_q2[i], lam_k2[i], da_subln_g[i], lam_init)
        o_rw = rwkv7_mix(z[..., DA_IN_WIDTH:], rw_mu[i], rw_w0[i], rw_w_up[i], rw_a0[i],
                         rw_a_up[i], rw_g_up[i], rw_k_k[i], rw_k_a[i], rw_r_k[i],
                         rw_ln_g[i], rw_ln_b[i])
        h = h + jnp.concatenate([o_da, o_rw], axis=-1) @ w_out[i]
        h = h + peer_ffn(rms_norm(h, norm_ffn_g[i]), peer_w_q[i], peer_sub_keys[i],
                         peer_u[i], peer_v[i])
        gate = jax.nn.sigmoid(rms_norm(h, norm_ple_g[i]) @ ple_gate_w[i])
        h = h + gate * (p[i] @ ple_proj_w[i])
    return rms_norm(h, norm_final_g)
```

```python
import functools
import math

import jax
import jax.numpy as jnp
from jax import lax
from jax.experimental import pallas as pl
from jax.experimental.pallas import tpu as pltpu
from jax.experimental.pallas import tpu_sc as plsc

F32 = jnp.float32
BF16 = jnp.bfloat16

NORM_EPS = 1e-6
DA_HEADS = 4
DA_HEAD_DIM = 64
ROPE_THETA = 500000.0
ROT_DIM = DA_HEAD_DIM // 4
RW_HEAD = 64
RW_GN_EPS = 64e-5
PEER_HEADS = 8
N_KEYS = 128
PEER_TOPK = 16
LAM_INIT = 0.8 - 0.6 * math.exp(-0.3 * 0)

LANES = 128
VMEM_LIMIT = 56 * 1024 * 1024
RW_CHUNK = 64
RW_UNIT = 2 * RW_CHUNK

NT = (((1,), (1,)), ((), ()))
TN = (((0,), (0,)), ((), ()))
HI = lax.Precision.HIGHEST


def _mm(a, b):
    return jnp.dot(a.astype(BF16), b.astype(BF16), preferred_element_type=F32)


def _mm_nt(a, b):
    return lax.dot_general(a.astype(BF16), b.astype(BF16), NT, preferred_element_type=F32)


def _mm_tn(a, b):
    return lax.dot_general(a.astype(BF16), b.astype(BF16), TN, preferred_element_type=F32)


def _mm_hi(a, b):
    return jnp.dot(a, b, precision=HI, preferred_element_type=F32)


def _params(*sem):
    return pltpu.CompilerParams(dimension_semantics=sem, vmem_limit_bytes=VMEM_LIMIT)


def _rms(x, g):
    return x * lax.rsqrt(jnp.mean(x * x, axis=-1, keepdims=True) + NORM_EPS) * g


def _inproj_kernel(x_ref, g_ref, w_ref, c_ref, s1_ref, s2_ref, qkv_ref, zrw_ref, *, n_qk, n_da):
    u = _rms(x_ref[...], g_ref[...]).astype(BF16)
    z = jnp.dot(u, w_ref[...], preferred_element_type=F32)
    c, s1, s2 = c_ref[...], s1_ref[...], s2_ref[...]
    half = ROT_DIM // 2
    for blk in range(n_da // LANES):
        t = z[:, blk * LANES:(blk + 1) * LANES]
        if blk < 2 * n_qk // LANES:
            t = t * c + pltpu.roll(t, LANES - half, 1) * s1 + pltpu.roll(t, half, 1) * s2
        if blk < n_qk // LANES:
            t = t * (DA_HEAD_DIM ** -0.5)
        qkv_ref[:, blk * LANES:(blk + 1) * LANES] = t.astype(BF16)
    zrw_ref[...] = z[:, n_da:]


def _inproj(x2, g, w_in_bf, rc, rs1, rs2, tm):
    n, d = x2.shape
    n_in = w_in_bf.shape[1]
    n_qk = DA_HEADS * 2 * DA_HEAD_DIM
    n_da = 3 * n_qk
    row = lambda i: (i, 0)
    fix = lambda i: (0, 0)
    return pl.pallas_call(
        functools.partial(_inproj_kernel, n_qk=n_qk, n_da=n_da),
        grid=(n // tm,),
        in_specs=[pl.BlockSpec((tm, d), row), pl.BlockSpec((1, d), fix), pl.BlockSpec((d, n_in), fix),
                  pl.BlockSpec((tm, LANES), row), pl.BlockSpec((tm, LANES), row), pl.BlockSpec((tm, LANES), row)],
        out_specs=[pl.BlockSpec((tm, n_da), row), pl.BlockSpec((tm, n_in - n_da), row)],
        out_shape=[jax.ShapeDtypeStruct((n, n_da), BF16), jax.ShapeDtypeStruct((n, n_in - n_da), F32)],
        compiler_params=_params("parallel"),
    )(x2, g, w_in_bf, rc, rs1, rs2)


def _attn_kernel(lam_ref, q_ref, k_ref, v_ref, sg_ref, o_ref, m_ref, l_ref, acc_ref, *, tq):
    i = pl.program_id(2)
    q = q_ref[...]
    lane = lax.broadcasted_iota(jnp.int32, q.shape, 1)
    zero = jnp.zeros_like(q)
    qs = (jnp.where(lane < DA_HEAD_DIM, q, zero), jnp.where(lane >= DA_HEAD_DIM, q, zero))
    m_ref[...] = jnp.full(m_ref.shape, -jnp.inf, F32)
    l_ref[...] = jnp.zeros(l_ref.shape, F32)
    acc_ref[...] = jnp.zeros(acc_ref.shape, F32)

    def block(j, masked):
        kj = k_ref[pl.ds(pl.multiple_of(j * tq, tq), tq), :]
        vj = v_ref[pl.ds(pl.multiple_of(j * tq, tq), tq), :]
        for c in range(2):
            s = lax.dot_general(qs[c], kj, NT, preferred_element_type=F32)
            if masked:
                r_id = lax.broadcasted_iota(jnp.int32, s.shape, 0)
                c_id = lax.broadcasted_iota(jnp.int32, s.shape, 1)
                s = jnp.where(c_id <= r_id, s, -jnp.inf)
            m_old = m_ref[c]
            m_new = jnp.maximum(m_old, jnp.max(s, axis=-1, keepdims=True))
            alpha = jnp.exp(m_old - m_new)
            p = jnp.exp(s - m_new)
            l_ref[c] = alpha * l_ref[c] + jnp.sum(p, axis=-1, keepdims=True)
            acc_ref[c] = alpha * acc_ref[c] + jnp.dot(p.astype(BF16), vj, preferred_element_type=F32)
            m_ref[c] = m_new

    def body(j, carry):
        block(j, False)
        return carry

    lax.fori_loop(0, i, body, 0)
    block(i, True)
    lam = lam_ref[0, 0]
    o = acc_ref[0] / l_ref[0] - lam * (acc_ref[1] / l_ref[1])
    o = o * lax.rsqrt(jnp.mean(o * o, axis=-1, keepdims=True) + NORM_EPS) * sg_ref[...] * (1.0 - LAM_INIT)
    o_ref[...] = o.astype(o_ref.dtype)


def _attention(qkv, lam, subln_g, batch, seq, tq):
    n = qkv.shape[0]
    nq = seq // tq
    h = DA_HEADS
    return pl.pallas_call(
        functools.partial(_attn_kernel, tq=tq),
        grid=(batch, h, nq),
        in_specs=[pl.BlockSpec(memory_space=pltpu.SMEM),
                  pl.BlockSpec((tq, LANES), lambda b, hh, i: (b * nq + i, hh)),
                  pl.BlockSpec((seq, LANES), lambda b, hh, i: (b, h + hh)),
                  pl.BlockSpec((seq, LANES), lambda b, hh, i: (b, 2 * h + hh)),
                  pl.BlockSpec((1, LANES), lambda b, hh, i: (0, 0))],
        out_specs=pl.BlockSpec((tq, LANES), lambda b, hh, i: (b * nq + i, hh)),
        out_shape=jax.ShapeDtypeStruct((n, h * LANES), BF16),
        scratch_shapes=[pltpu.VMEM((2, tq, 1), F32), pltpu.VMEM((2, tq, 1), F32), pltpu.VMEM((2, tq, LANES), F32)],
        compiler_params=_params("parallel", "parallel", "arbitrary"),
    )(lam, qkv, qkv, qkv, subln_g)


def _rwprep_kernel(z_ref, zp_ref, mu_ref, w0_ref, wup_ref, a0_ref, aup_ref, gup_ref, kk_ref, ka_ref, bd_ref,
                   r_o, ld_o, k_o, v_o, kk_o, b_o, g_o, *, tiles_per_seq, width):
    i = pl.program_id(0)
    z = z_ref[...]
    last = zp_ref[7:8, :]
    first = jnp.where(i % tiles_per_seq == 0, jnp.zeros_like(last), last)
    row = lax.broadcasted_iota(jnp.int32, z.shape, 0)
    prev = jnp.where(row == 0, first, pltpu.roll(z, 1, 0))
    zs = z + (prev - z) * mu_ref[...]
    r = zs[:, 0:width]
    k = zs[:, width:2 * width]
    v = zs[:, 2 * width:3 * width]
    xwa = zs[:, 3 * width:3 * width + LANES]
    xg = zs[:, 3 * width + LANES:3 * width + 2 * LANES]
    w = -jax.nn.softplus(-(w0_ref[...] + _mm_hi(jnp.tanh(xwa), wup_ref[...]))) - 0.5
    a = jax.nn.sigmoid(a0_ref[...] + _mm_hi(xwa, aup_ref[...]))
    g = _mm_hi(jax.nn.sigmoid(xg), gup_ref[...])
    kk = k * kk_ref[...]
    kk = kk / jnp.maximum(jnp.sqrt(_mm_hi(kk * kk, bd_ref[...])), 1e-12)
    r_o[...] = r
    ld_o[...] = -jnp.exp(w)
    k_o[...] = k * (1.0 + (a - 1.0) * ka_ref[...])
    v_o[...] = v
    kk_o[...] = kk
    b_o[...] = kk * a
    g_o[...] = g


def _rwprep(zrw, mu, w0, wup_pad, a0, aup_pad, gup, k_k, k_a, bd, seq, tm):
    n, zin = zrw.shape
    width = w0.shape[1]
    row = lambda i: (i, 0)
    fix = lambda i: (0, 0)
    prev = lambda i: (jnp.maximum(i * (tm // 8) - 1, 0), 0)
    out = jax.ShapeDtypeStruct((n, width), F32)
    return pl.pallas_call(
        functools.partial(_rwprep_kernel, tiles_per_seq=seq // tm, width=width),
        grid=(n // tm,),
        in_specs=[pl.BlockSpec((tm, zin), row), pl.BlockSpec((8, zin), prev), pl.BlockSpec((1, zin), fix),
                  pl.BlockSpec((1, width), fix), pl.BlockSpec((LANES, width), fix),
                  pl.BlockSpec((1, width), fix), pl.BlockSpec((LANES, width), fix), pl.BlockSpec((LANES, width), fix),
                  pl.BlockSpec((1, width), fix), pl.BlockSpec((1, width), fix), pl.BlockSpec((width, width), fix)],
        out_specs=[pl.BlockSpec((tm, width), row)] * 7,
        out_shape=[out] * 7,
        compiler_params=_params("parallel"),
    )(zrw, zrw, mu, w0, wup_pad, a0, aup_pad, gup, k_k, k_a, bd)


def _rwcore_kernel(r_ref, ld_ref, k_ref, v_ref, kk_ref, b_ref, g_ref, lng_ref, lnb_ref, rk_ref, o_ref, s_ref, *, units):
    U, C, HD = RW_UNIT, RW_CHUNK, RW_HEAD

    @pl.when(pl.program_id(2) == 0)
    def _():
        s_ref[...] = jnp.zeros(s_ref.shape, F32)

    ri = lax.broadcasted_iota(jnp.int32, (U, U), 0)
    ci = lax.broadcasted_iota(jnp.int32, (U, U), 1)
    same = (ri // C) == (ci // C)
    tri_s = same & (ci < ri)
    tri_i = same & (ci <= ri)
    eye = (ri == ci).astype(F32)
    cum_w = tri_i.astype(F32)
    head_avg = same.astype(F32) * (1.0 / HD)
    head_sum = same.astype(F32)
    lane = lax.broadcasted_iota(jnp.int32, (U, U), 1)
    rowi = lax.broadcasted_iota(jnp.int32, (U, U), 0)
    hmask = (lane < HD, lane >= HD)
    cmask = (rowi < C, rowi >= C)
    zero = jnp.zeros((U, U), F32)

    local = []
    for u in range(units):
        sl = pl.ds(u * U, U)
        r, ld, k, v, kk, b = r_ref[sl, :], ld_ref[sl, :], k_ref[sl, :], v_ref[sl, :], kk_ref[sl, :], b_ref[sl, :]
        cum = _mm_hi(cum_w, ld)
        gam = jnp.exp(cum)
        ginv = jnp.exp(-cum)
        at = -kk * jnp.exp(cum - ld)
        bt = b * ginv
        kt = k * ginv
        rt = r * gam
        bk = jnp.concatenate([bt, kt], axis=0)
        ap = zero
        uu = zero
        rp = zero
        yp = zero
        for h in range(2):
            ar = jnp.concatenate([jnp.where(hmask[h], at, zero), jnp.where(hmask[h], rt, zero)], axis=0)
            m = _mm_nt(ar, bk)
            mab = jnp.where(tri_s, m[0:U, 0:U], zero)
            mak = jnp.where(tri_s, m[0:U, U:2 * U], zero)
            mrb = jnp.where(tri_i, m[U:2 * U, 0:U], zero)
            mrk = jnp.where(tri_i, m[U:2 * U, U:2 * U], zero)
            t = eye + mab
            p = mab
            for _ in range(int(math.log2(C)) - 1):
                p = _mm(p, p)
                t = t + _mm(t, p)
            aph = _mm(t, at)
            uh = _mm(t, _mm(mak, v))
            rph = _mm(mrb, aph)
            yph = _mm(mrb, uh) + _mm(mrk, v)
            ap = jnp.where(hmask[h], aph, ap)
            uu = jnp.where(hmask[h], uh, uu)
            rp = jnp.where(hmask[h], rph, rp)
            yp = jnp.where(hmask[h], yph, yp)
        rp = rp + rt
        gs, hs = [], []
        for c in range(2):
            gl = gam[(c + 1) * C - 1:(c + 1) * C, :]
            apc = jnp.where(cmask[c], ap, zero)
            uvc = jnp.concatenate([jnp.where(cmask[c], uu, zero), jnp.where(cmask[c], v, zero)], axis=0)
            gs.append(jnp.where(same, eye + _mm_tn(apc, bt), zero) * gl)
            hs.append(jnp.where(same, _mm_tn(uvc, bk), zero) * gl)
        local.append((rp, yp, gs, hs))

    s = s_ref[...]
    for u in range(units):
        rp, yp, gs, hs = local[u]
        y0 = lax.dot_general(rp, s, NT, precision=HI, preferred_element_type=F32)
        s = _mm_hi(s, gs[0]) + hs[0]
        y1 = lax.dot_general(rp, s, NT, precision=HI, preferred_element_type=F32)
        s = _mm_hi(s, gs[1]) + hs[1]
        y = jnp.where(cmask[0], y0, y1) + yp
        sl = pl.ds(u * U, U)
        r, k, v = r_ref[sl, :], k_ref[sl, :], v_ref[sl, :]
        mean = _mm_hi(y, head_avg)
        yc = y - mean
        var = _mm_hi(yc * yc, head_avg)
        yn = yc * lax.rsqrt(var + RW_GN_EPS) * lng_ref[...] + lnb_ref[...]
        yn = yn + _mm_hi(r * k * rk_ref[...], head_sum) * v
        o_ref[sl, :] = (yn * g_ref[sl, :]).astype(o_ref.dtype)
    s_ref[...] = s


def _rwcore(r, ld, k, v, kk, b, g, ln_g, ln_b, r_k, batch, seq, units):
    n, width = r.shape
    rows = units * RW_UNIT
    steps = seq // rows
    blk = pl.BlockSpec((rows, LANES), lambda bb, hp, i: (bb * steps + i, hp))
    vec = pl.BlockSpec((1, LANES), lambda bb, hp, i: (0, hp))
    return pl.pallas_call(
        functools.partial(_rwcore_kernel, units=units),
        grid=(batch, width // LANES, steps),
        in_specs=[blk] * 7 + [vec] * 3,
        out_specs=blk,
        out_shape=jax.ShapeDtypeStruct((n, width), BF16),
        scratch_shapes=[pltpu.VMEM((RW_UNIT, RW_UNIT), F32)],
        compiler_params=_params("parallel", "parallel", "arbitrary"),
    )(r, ld, k, v, kk, b, g, ln_g, ln_b, r_k)


def _outproj_kernel(x_ref, oda_ref, orw_ref, wa_ref, wb_ref, g_ref, h_ref, u_ref):
    h = (x_ref[...] + jnp.dot(oda_ref[...], wa_ref[...], preferred_element_type=F32)
         + jnp.dot(orw_ref[...], wb_ref[...], preferred_element_type=F32))
    h_ref[...] = h
    u_ref[...] = _rms(h, g_ref[...]).astype(BF16)


def _outproj(x2, o_da, o_rw, wa, wb, g, tm):
    n, d = x2.shape
    da = o_da.shape[1]
    rw = o_rw.shape[1]
    row = lambda i: (i, 0)
    fix = lambda i: (0, 0)
    return pl.pallas_call(
        _outproj_kernel,
        grid=(n // tm,),
        in_specs=[pl.BlockSpec((tm, d), row), pl.BlockSpec((tm, da), row), pl.BlockSpec((tm, rw), row),
                  pl.BlockSpec((da, d), fix), pl.BlockSpec((rw, d), fix), pl.BlockSpec((1, d), fix)],
        out_specs=[pl.BlockSpec((tm, d), row), pl.BlockSpec((tm, d), row)],
        out_shape=[jax.ShapeDtypeStruct((n, d), F32), jax.ShapeDtypeStruct((n, d), BF16)],
        compiler_params=_params("parallel"),
    )(x2, o_da, o_rw, wa, wb, g)


def _topk_rows(s, k, payload=None):
    rows = s.shape[0]
    iota = lax.broadcasted_iota(jnp.int32, s.shape, 0).astype(F32)
    vals, sel = [], []
    for _ in range(k):
        m = jnp.max(s, axis=0, keepdims=True)
        am = jnp.min(jnp.where(s == m, iota, float(rows)), axis=0, keepdims=True)
        hit = iota == am
        vals.append(m)
        sel.append(am if payload is None else jnp.sum(jnp.where(hit, payload, 0.0), axis=0, keepdims=True))
        s = jnp.where(hit, -jnp.inf, s)
    return vals, sel


def _stack_rows(rows_list):
    k = len(rows_list)
    iota = lax.broadcasted_iota(jnp.int32, (k, rows_list[0].shape[1]), 0)
    out = jnp.zeros(iota.shape, rows_list[0].dtype)
    for j, r in enumerate(rows_list):
        out = jnp.where(iota == j, r, out)
    return out


def _peertopk_kernel(u_ref, wq_ref, keys_ref, idx_ref, gate_ref):
    u = u_ref[...]
    half = N_KEYS
    idx_rows, gate_rows = [], []
    for h in range(PEER_HEADS):
        tops = []
        for p in range(2):
            hp = h * 2 + p
            q_t = lax.dot_general(wq_ref[hp * half:(hp + 1) * half, :], u, NT, preferred_element_type=F32)
            s_t = jnp.dot(keys_ref[hp], q_t.astype(BF16), preferred_element_type=F32)
            tops.append(_topk_rows(s_t, PEER_TOPK))
        (v1, i1), (v2, i2) = tops
        v2s = _stack_rows(v2)
        i2s = _stack_rows(i2)
        cand = jnp.concatenate([v1[i] + v2s for i in range(PEER_TOPK)], axis=0)
        cidx = jnp.concatenate([i1[i] * float(N_KEYS) + i2s for i in range(PEER_TOPK)], axis=0)
        best, idx = _topk_rows(cand, PEER_TOPK, payload=cidx)
        e = [jnp.exp(b - best[0]) for b in best]
        den = functools.reduce(lambda a, b: a + b, e)
        idx_rows.append(_stack_rows(idx))
        gate_rows.append(_stack_rows([x / den for x in e]))
    idx_ref[...] = jnp.concatenate(idx_rows, axis=0).T.astype(jnp.int32)
    gate_ref[...] = jnp.concatenate(gate_rows, axis=0).T


def _peertopk(u_bf, wq_t, keys, tt):
    n, d = u_bf.shape
    hk = PEER_HEADS * PEER_TOPK
    row = lambda i: (i, 0)
    return pl.pallas_call(
        _peertopk_kernel,
        grid=(n // tt,),
        in_specs=[pl.BlockSpec((tt, d), row), pl.BlockSpec(wq_t.shape, lambda i: (0, 0)),
                  pl.BlockSpec(keys.shape, lambda i: (0, 0, 0))],
        out_specs=[pl.BlockSpec((tt, hk), row), pl.BlockSpec((tt, hk), row)],
        out_shape=[jax.ShapeDtypeStruct((n, hk), jnp.int32), jax.ShapeDtypeStruct((n, hk), F32)],
        compiler_params=_params("parallel"),
    )(u_bf, wq_t, keys)


SC_CORES = 2
SC_SUBCORES = 16
GATHER_ROWS = 32
GATHER_INDEX_WINDOW = 1024


def _gather_rows(u_tab, v_tab, idx_flat):
    p = idx_flat.shape[0]
    w = u_tab.shape[1]
    workers = SC_CORES * SC_SUBCORES
    per_worker = p // workers
    iw = min(GATHER_INDEX_WINDOW, per_worker)
    r = GATHER_ROWS
    assert p % workers == 0 and per_worker % iw == 0 and iw % (2 * r) == 0
    nsub = iw // r
    mesh = plsc.VectorSubcoreMesh(core_axis_name="core", subcore_axis_name="subcore")
    out = jax.ShapeDtypeStruct((p, w), u_tab.dtype)
    buf = pltpu.VMEM((r, w), u_tab.dtype)
    sem = pltpu.SemaphoreType.DMA

    @pl.kernel(out_type=(out, out), mesh=mesh,
               scratch_types=[pltpu.VMEM((iw,), jnp.int32), buf, buf, buf, buf, sem, sem, sem, sem, sem, sem, sem, sem])
    def gather(u_hbm, v_hbm, i_hbm, ou_hbm, ov_hbm, idx_v, ub0, ub1, vb0, vb1, gu0, gu1, gv0, gv1, wu0, wu1, wv0, wv1):
        wid = lax.axis_index("core") * SC_SUBCORES + lax.axis_index("subcore")
        ubuf, vbuf = (ub0, ub1), (vb0, vb1)
        gsu, gsv, wsu, wsv = (gu0, gu1), (gv0, gv1), (wu0, wu1), (wv0, wv1)

        @pl.loop(0, per_worker // iw)
        def _(o):
            start = wid * per_worker + o * iw
            pltpu.sync_copy(i_hbm.at[pl.ds(start, iw)], idx_v)

            def gathers(j):
                s = j % 2
                ids = idx_v.at[pl.ds(j * r, r)]
                return (pltpu.make_async_copy(u_hbm.at[ids], ubuf[s], gsu[s]),
                        pltpu.make_async_copy(v_hbm.at[ids], vbuf[s], gsv[s]))

            def writes(j):
                s = j % 2
                dst = pl.ds(start + j * r, r)
                return (pltpu.make_async_copy(ubuf[s], ou_hbm.at[dst], wsu[s]),
                        pltpu.make_async_copy(vbuf[s], ov_hbm.at[dst], wsv[s]))

            for j in range(nsub + 1):
                if j < nsub:
                    if j >= 2:
                        for c in writes(j - 2):
                            c.wait()
                    for c in gathers(j):
                        c.start()
                if j >= 1:
                    for c in gathers(j - 1):
                        c.wait()
                    for c in writes(j - 1):
                        c.start()
            for j in (nsub - 2, nsub - 1):
                for c in writes(j):
                    c.wait()

    return gather(u_tab, v_tab, idx_flat)


def _peermix_kernel(h_ref, xa_ref, xb_ref, gate_ref, ug_ref, vg_ref, o_ref, *, tt):
    hk2 = 2 * PEER_HEADS * PEER_TOPK
    x2 = jnp.concatenate([xa_ref[...], xb_ref[...]], axis=0)
    lane = lax.broadcasted_iota(jnp.int32, (tt, hk2), 1)
    row = lax.broadcasted_iota(jnp.int32, (tt, hk2), 0)
    even = (lane % 2) == 0
    part = jnp.zeros((tt, hk2), F32)
    for t in range(tt):
        ub = pltpu.bitcast(ug_ref[t * (hk2 // 2):(t + 1) * (hk2 // 2), :], BF16)
        r = lax.dot_general(x2, ub, NT, preferred_element_type=F32)
        part = jnp.where(row == t, jnp.where(even, r[0:tt], r[tt:2 * tt]), part)
    hid = part + jnp.where(even, pltpu.roll(part, hk2 - 1, 1), pltpu.roll(part, 1, 1))
    w = gate_ref[...] * (0.5 * hid * (1.0 + lax.erf(hid * (2.0 ** -0.5))))
    w2 = jnp.concatenate([jnp.where(even, w, 0.0), jnp.where(even, 0.0, w)], axis=0).astype(BF16)
    half = o_ref.shape[1] // 2
    orow = lax.broadcasted_iota(jnp.int32, (tt, half), 0)
    out_a = jnp.zeros((tt, half), F32)
    out_b = jnp.zeros((tt, half), F32)
    for t in range(tt):
        vb = pltpu.bitcast(vg_ref[t * (hk2 // 2):(t + 1) * (hk2 // 2), :], BF16)
        o = jnp.dot(w2, vb, preferred_element_type=F32)
        out_a = jnp.where(orow == t, o[0:tt], out_a)
        out_b = jnp.where(orow == t, o[tt:2 * tt], out_b)
    o_ref[:, 0:half] = h_ref[:, 0:half] + out_a
    o_ref[:, half:2 * half] = h_ref[:, half:2 * half] + out_b


def _peermix(h1, xa, xb, gate2, ug, vg, tt):
    n, d = h1.shape
    hk = PEER_HEADS * PEER_TOPK
    row = lambda i: (i, 0)
    return pl.pallas_call(
        functools.partial(_peermix_kernel, tt=tt),
        grid=(n // tt,),
        in_specs=[pl.BlockSpec((tt, d), row), pl.BlockSpec((tt, d // 2), row), pl.BlockSpec((tt, d // 2), row),
                  pl.BlockSpec((tt, 2 * hk), row), pl.BlockSpec((tt * hk, d // 2), row),
                  pl.BlockSpec((tt * hk, d // 2), row)],
        out_specs=pl.BlockSpec((tt, d), row),
        out_shape=jax.ShapeDtypeStruct((n, d), F32),
        compiler_params=_params("parallel"),
    )(h1, xa, xb, gate2, ug, vg)


def _ple_kernel(h_ref, p_ref, g_ref, wg_ref, wp_ref, gf_ref, o_ref):
    h = h_ref[...]
    gate = jax.nn.sigmoid(jnp.dot(_rms(h, g_ref[...]).astype(BF16), wg_ref[...], preferred_element_type=F32))
    pp = jnp.dot(p_ref[...].astype(BF16), wp_ref[...], preferred_element_type=F32)
    o_ref[...] = _rms(h + gate * pp, gf_ref[...])


def _ple(h2, p2, g, wg, wp, gf, tm):
    n, d = h2.shape
    pd = p2.shape[1]
    row = lambda i: (i, 0)
    fix = lambda i: (0, 0)
    return pl.pallas_call(
        _ple_kernel,
        grid=(n // tm,),
        in_specs=[pl.BlockSpec((tm, d), row), pl.BlockSpec((tm, pd), row), pl.BlockSpec((1, d), fix),
                  pl.BlockSpec((d, d), fix), pl.BlockSpec((pd, d), fix), pl.BlockSpec((1, d), fix)],
        out_specs=pl.BlockSpec((tm, d), row),
        out_shape=jax.ShapeDtypeStruct((n, d), F32),
        compiler_params=_params("parallel"),
    )(h2, p2, g, wg, wp, gf)


def _rope_tables(positions):
    half = ROT_DIM // 2
    inv_freq = ROPE_THETA ** (-jnp.arange(half, dtype=F32) * 2.0 / ROT_DIM)
    ang = positions.astype(F32).reshape(-1, 1) * inv_freq
    lane = jnp.arange(LANES)
    d = lane % DA_HEAD_DIM
    cos = jnp.take(jnp.cos(ang), d % half, axis=1)
    sin = jnp.take(jnp.sin(ang), d % half, axis=1)
    c = jnp.where(d < ROT_DIM, cos, 1.0)
    s1 = jnp.where(d < half, -sin, 0.0)
    s2 = jnp.where((d >= half) & (d < ROT_DIM), sin, 0.0)
    return c, s1, s2


def _pack_rows(tab):
    d = tab.shape[1]
    t = tab.astype(BF16)
    pair = jnp.stack([t[:, :d // 2], t[:, d // 2:]], axis=-1)
    return lax.bitcast_convert_type(pair, jnp.uint32)


def _block_diag_ones(width, head):
    i = jnp.arange(width)
    return (i[:, None] // head == i[None, :] // head).astype(F32)


PEER_TOKEN_CHUNKS = 8


def kernel(x, p, positions, norm_mix_g, w_in, lam_q1, lam_k1, lam_q2, lam_k2, da_subln_g, rw_mu, rw_w0, rw_w_up, rw_a0, rw_a_up, rw_g_up, rw_k_k, rw_k_a, rw_r_k, rw_ln_g, rw_ln_b, w_out, norm_ffn_g, peer_w_q, peer_sub_keys, peer_u, peer_v, norm_ple_g, ple_gate_w, ple_proj_w, norm_final_g):
    batch, seq, d = x.shape
    n = batch * seq
    x2 = x.reshape(n, d)
    row = lambda a: a.reshape(1, -1)
    tm = min(256, seq)

    rc, rs1, rs2 = _rope_tables(positions)
    qkv, zrw = _inproj(x2, row(norm_mix_g[0]), w_in[0].astype(BF16), rc, rs1, rs2, tm)
    f32 = F32
    lam = (jnp.exp(jnp.sum(lam_q1[0].astype(f32) * lam_k1[0].astype(f32)))
           - jnp.exp(jnp.sum(lam_q2[0].astype(f32) * lam_k2[0].astype(f32))) + LAM_INIT).reshape(1, 1)
    o_da = _attention(qkv, lam, row(da_subln_g[0]), batch, seq, min(256, seq))

    width = rw_w0.shape[1]
    lora = rw_w_up.shape[1]
    wup_pad = jnp.concatenate([rw_w_up[0], jnp.zeros((LANES - lora, width), f32)], axis=0)
    aup_pad = jnp.concatenate([jnp.zeros((LANES - rw_a_up.shape[1], width), f32), rw_a_up[0]], axis=0)
    rw = _rwprep(zrw, row(rw_mu[0]), row(rw_w0[0]), wup_pad, row(rw_a0[0]), aup_pad, rw_g_up[0],
                 row(rw_k_k[0]), row(rw_k_a[0]), _block_diag_ones(width, RW_HEAD), seq, tm)
    o_rw = _rwcore(*rw, row(rw_ln_g[0]), row(rw_ln_b[0]), row(rw_r_k[0]), batch, seq, min(4, seq // RW_UNIT))

    da_w = o_da.shape[1]
    w_out_bf = w_out[0].astype(BF16)
    h1, u2 = _outproj(x2, o_da, o_rw, w_out_bf[:da_w], w_out_bf[da_w:], row(norm_ffn_g[0]), tm)

    keys = peer_sub_keys[0].reshape(PEER_HEADS * 2, N_KEYS, -1).astype(BF16)
    idx, gate = _peertopk(u2, peer_w_q[0].T.astype(BF16), keys, min(256, n))
    gate2 = jnp.repeat(gate, 2, axis=1)
    u_tab = _pack_rows(peer_u[0])
    v_tab = _pack_rows(peer_v[0])
    xa, xb = u2[:, :d // 2], u2[:, d // 2:]
    chunks = PEER_TOKEN_CHUNKS if n % (PEER_TOKEN_CHUNKS * 256) == 0 else 1
    nc = n // chunks
    h2_parts = []
    for c in range(chunks):
        sl = slice(c * nc, (c + 1) * nc)
        ug, vg = _gather_rows(u_tab, v_tab, idx[sl].reshape(-1))
        h2_parts.append(_peermix(h1[sl], xa[sl], xb[sl], gate2[sl], ug, vg, 16))
    h2 = jnp.concatenate(h2_parts, axis=0) if chunks > 1 else h2_parts[0]

    out = _ple(h2, p[0].reshape(n, -1), row(norm_ple_g[0]), ple_gate_w[0].astype(BF16), ple_proj_w[0].astype(BF16),
               row(norm_final_g), tm)
    return out.reshape(batch, seq, d)
```

```python
import functools
import math

import jax
import jax.numpy as jnp
from jax import lax
from jax.experimental import pallas as pl
from jax.experimental.pallas import tpu as pltpu
from jax.experimental.pallas import tpu_sc as plsc

F32 = jnp.float32
BF16 = jnp.bfloat16

NORM_EPS = 1e-6
DA_HEADS = 4
DA_HEAD_DIM = 64
ROPE_THETA = 500000.0
ROT_DIM = DA_HEAD_DIM // 4
RW_HEAD = 64
RW_GN_EPS = 64e-5
PEER_HEADS = 8
N_KEYS = 128
PEER_TOPK = 16
LAM_INIT = 0.8 - 0.6 * math.exp(-0.3 * 0)

LANES = 128
VMEM_LIMIT = 56 * 1024 * 1024
RW_CHUNK = 64
RW_UNIT = 2 * RW_CHUNK

NT = (((1,), (1,)), ((), ()))
TN = (((0,), (0,)), ((), ()))
HI = lax.Precision.HIGHEST


def _mm(a, b):
    return jnp.dot(a.astype(BF16), b.astype(BF16), preferred_element_type=F32)


def _mm_nt(a, b):
    return lax.dot_general(a.astype(BF16), b.astype(BF16), NT, preferred_element_type=F32)


def _mm_tn(a, b):
    return lax.dot_general(a.astype(BF16), b.astype(BF16), TN, preferred_element_type=F32)


def _mm_hi(a, b):
    return jnp.dot(a, b, precision=HI, preferred_element_type=F32)


def _params(*sem):
    return pltpu.CompilerParams(dimension_semantics=sem, vmem_limit_bytes=VMEM_LIMIT)


def _rms(x, g):
    return x * lax.rsqrt(jnp.mean(x * x, axis=-1, keepdims=True) + NORM_EPS) * g


def _inproj_kernel(x_ref, g_ref, w_ref, c_ref, s1_ref, s2_ref, qkv_ref, zrw_ref, *, n_qk, n_da):
    u = _rms(x_ref[...], g_ref[...]).astype(BF16)
    z = jnp.dot(u, w_ref[...], preferred_element_type=F32)
    c, s1, s2 = c_ref[...], s1_ref[...], s2_ref[...]
    half = ROT_DIM // 2
    for blk in range(n_da // LANES):
        t = z[:, blk * LANES:(blk + 1) * LANES]
        if blk < 2 * n_qk // LANES:
            t = t * c + pltpu.roll(t, LANES - half, 1) * s1 + pltpu.roll(t, half, 1) * s2
        if blk < n_qk // LANES:
            t = t * (DA_HEAD_DIM ** -0.5 * math.log2(math.e))
        qkv_ref[:, blk * LANES:(blk + 1) * LANES] = t.astype(BF16)
    zrw_ref[...] = z[:, n_da:]


def _inproj(x2, g, w_in_bf, rc, rs1, rs2, tm):
    n, d = x2.shape
    n_in = w_in_bf.shape[1]
    n_qk = DA_HEADS * 2 * DA_HEAD_DIM
    n_da = 3 * n_qk
    row = lambda i: (i, 0)
    fix = lambda i: (0, 0)
    return pl.pallas_call(
        functools.partial(_inproj_kernel, n_qk=n_qk, n_da=n_da),
        grid=(n // tm,),
        in_specs=[pl.BlockSpec((tm, d), row), pl.BlockSpec((1, d), fix), pl.BlockSpec((d, n_in), fix),
                  pl.BlockSpec((tm, LANES), row), pl.BlockSpec((tm, LANES), row), pl.BlockSpec((tm, LANES), row)],
        out_specs=[pl.BlockSpec((tm, n_da), row), pl.BlockSpec((tm, n_in - n_da), row)],
        out_shape=[jax.ShapeDtypeStruct((n, n_da), BF16), jax.ShapeDtypeStruct((n, n_in - n_da), F32)],
        compiler_params=_params("parallel"),
    )(x2, g, w_in_bf, rc, rs1, rs2)


def _attn_kernel(lam_ref, q_ref, k_ref, v_ref, sg_ref, o_ref, m_ref, acc_ref, *, tq):
    i = pl.program_id(2)
    q = q_ref[...]
    lane = lax.broadcasted_iota(jnp.int32, q.shape, 1)
    zero = jnp.zeros_like(q)
    qs = (jnp.where(lane < DA_HEAD_DIM, q, zero), jnp.where(lane >= DA_HEAD_DIM, q, zero))
    m_ref[...] = jnp.full(m_ref.shape, -jnp.inf, F32)
    acc_ref[...] = jnp.zeros(acc_ref.shape, F32)
    ones = jnp.ones((tq, LANES), BF16)

    def block(j, masked):
        kj = k_ref[pl.ds(pl.multiple_of(j * tq, tq), tq), :]
        vj = jnp.concatenate([v_ref[pl.ds(pl.multiple_of(j * tq, tq), tq), :], ones], axis=1)
        for c in range(2):
            s = lax.dot_general(qs[c], kj, NT, preferred_element_type=F32)
            if masked:
                r_id = lax.broadcasted_iota(jnp.int32, s.shape, 0)
                c_id = lax.broadcasted_iota(jnp.int32, s.shape, 1)
                s = jnp.where(c_id <= r_id, s, -jnp.inf)
            m_old = m_ref[c]
            m_new = jnp.maximum(m_old, jnp.max(s, axis=-1, keepdims=True))
            alpha = jnp.exp2(m_old - m_new)
            p = jnp.exp2(s - jnp.tile(m_new, (1, tq // LANES)))
            pv = jnp.dot(p.astype(BF16), vj, preferred_element_type=F32)
            acc_ref[c] = jnp.tile(alpha, (1, 2)) * acc_ref[c] + pv
            m_ref[c] = m_new

    def body(j, carry):
        block(j, False)
        return carry

    lax.fori_loop(0, i, body, 0)
    block(i, True)
    lam = lam_ref[0, 0]
    a0, a1 = acc_ref[0], acc_ref[1]
    o = a0[:, :LANES] / a0[:, LANES:] - lam * (a1[:, :LANES] / a1[:, LANES:])
    o = o * lax.rsqrt(jnp.mean(o * o, axis=-1, keepdims=True) + NORM_EPS) * sg_ref[...] * (1.0 - LAM_INIT)
    o_ref[...] = o.astype(o_ref.dtype)


def _attention(qkv, lam, subln_g, batch, seq, tq):
    n = qkv.shape[0]
    nq = seq // tq
    h = DA_HEADS
    return pl.pallas_call(
        functools.partial(_attn_kernel, tq=tq),
        grid=(batch, h, nq),
        in_specs=[pl.BlockSpec(memory_space=pltpu.SMEM),
                  pl.BlockSpec((tq, LANES), lambda b, hh, i: (b * nq + i, hh)),
                  pl.BlockSpec((seq, LANES), lambda b, hh, i: (b, h + hh)),
                  pl.BlockSpec((seq, LANES), lambda b, hh, i: (b, 2 * h + hh)),
                  pl.BlockSpec((1, LANES), lambda b, hh, i: (0, 0))],
        out_specs=pl.BlockSpec((tq, LANES), lambda b, hh, i: (b * nq + i, hh)),
        out_shape=jax.ShapeDtypeStruct((n, h * LANES), BF16),
        scratch_shapes=[pltpu.VMEM((2, tq, LANES), F32), pltpu.VMEM((2, tq, 2 * LANES), F32)],
        compiler_params=_params("parallel", "parallel", "arbitrary"),
    )(lam, qkv, qkv, qkv, subln_g)


def _rwprep_kernel(z_ref, zp_ref, mu_ref, w0_ref, wup_ref, a0_ref, aup_ref, gup_ref, kk_ref, ka_ref, bd_ref,
                   r_o, ld_o, k_o, v_o, kk_o, b_o, g_o, *, tiles_per_seq, width):
    i = pl.program_id(0)
    z = z_ref[...]
    last = zp_ref[7:8, :]
    first = jnp.where(i % tiles_per_seq == 0, jnp.zeros_like(last), last)
    row = lax.broadcasted_iota(jnp.int32, z.shape, 0)
    prev = jnp.where(row == 0, first, pltpu.roll(z, 1, 0))
    zs = z + (prev - z) * mu_ref[...]
    r = zs[:, 0:width]
    k = zs[:, width:2 * width]
    v = zs[:, 2 * width:3 * width]
    xwa = zs[:, 3 * width:3 * width + LANES]
    xg = zs[:, 3 * width + LANES:3 * width + 2 * LANES]
    w = -jax.nn.softplus(-(w0_ref[...] + _mm_hi(jnp.tanh(xwa), wup_ref[...]))) - 0.5
    a = jax.nn.sigmoid(a0_ref[...] + _mm_hi(xwa, aup_ref[...]))
    g = _mm_hi(jax.nn.sigmoid(xg), gup_ref[...])
    kk = k * kk_ref[...]
    kk = kk / jnp.maximum(jnp.sqrt(_mm_hi(kk * kk, bd_ref[...])), 1e-12)
    r_o[...] = r
    ld_o[...] = -jnp.exp(w)
    k_o[...] = k * (1.0 + (a - 1.0) * ka_ref[...])
    v_o[...] = v
    kk_o[...] = kk
    b_o[...] = kk * a
    g_o[...] = g


def _rwprep(zrw, mu, w0, wup_pad, a0, aup_pad, gup, k_k, k_a, bd, seq, tm):
    n, zin = zrw.shape
    width = w0.shape[1]
    row = lambda i: (i, 0)
    fix = lambda i: (0, 0)
    prev = lambda i: (jnp.maximum(i * (tm // 8) - 1, 0), 0)
    out = jax.ShapeDtypeStruct((n, width), F32)
    return pl.pallas_call(
        functools.partial(_rwprep_kernel, tiles_per_seq=seq // tm, width=width),
        grid=(n // tm,),
        in_specs=[pl.BlockSpec((tm, zin), row), pl.BlockSpec((8, zin), prev), pl.BlockSpec((1, zin), fix),
                  pl.BlockSpec((1, width), fix), pl.BlockSpec((LANES, width), fix),
                  pl.BlockSpec((1, width), fix), pl.BlockSpec((LANES, width), fix), pl.BlockSpec((LANES, width), fix),
                  pl.BlockSpec((1, width), fix), pl.BlockSpec((1, width), fix), pl.BlockSpec((width, width), fix)],
        out_specs=[pl.BlockSpec((tm, width), row)] * 7,
        out_shape=[out] * 7,
        compiler_params=_params("parallel"),
    )(zrw, zrw, mu, w0, wup_pad, a0, aup_pad, gup, k_k, k_a, bd)


def _rwcore_kernel(r_ref, ld_ref, k_ref, v_ref, kk_ref, b_ref, g_ref, lng_ref, lnb_ref, rk_ref, o_ref, s_ref, *, units):
    U, C, HD = RW_UNIT, RW_CHUNK, RW_HEAD

    @pl.when(pl.program_id(2) == 0)
    def _():
        s_ref[...] = jnp.zeros(s_ref.shape, F32)

    ri = lax.broadcasted_iota(jnp.int32, (U, U), 0)
    ci = lax.broadcasted_iota(jnp.int32, (U, U), 1)
    same = (ri // C) == (ci // C)
    tri_s = same & (ci < ri)
    tri_i = same & (ci <= ri)
    eye = (ri == ci).astype(F32)
    cum_w = tri_i.astype(F32)
    head_avg = same.astype(F32) * (1.0 / HD)
    head_sum = same.astype(F32)
    lane = lax.broadcasted_iota(jnp.int32, (U, U), 1)
    rowi = lax.broadcasted_iota(jnp.int32, (U, U), 0)
    hmask = (lane < HD, lane >= HD)
    cmask = (rowi < C, rowi >= C)
    zero = jnp.zeros((U, U), F32)

    local = []
    for u in range(units):
        sl = pl.ds(u * U, U)
        r, ld, k, v, kk, b = r_ref[sl, :], ld_ref[sl, :], k_ref[sl, :], v_ref[sl, :], kk_ref[sl, :], b_ref[sl, :]
        cum = _mm_hi(cum_w, ld)
        gam = jnp.exp(cum)
        ginv = jnp.exp(-cum)
        at = -kk * jnp.exp(cum - ld)
        bt = b * ginv
        kt = k * ginv
        rt = r * gam
        bk = jnp.concatenate([bt, kt], axis=0)
        ap = zero
        uu = zero
        rp = zero
        yp = zero
        for h in range(2):
            ar = jnp.concatenate([jnp.where(hmask[h], at, zero), jnp.where(hmask[h], rt, zero)], axis=0)
            m = _mm_nt(ar, bk)
            mab = jnp.where(tri_s, m[0:U, 0:U], zero)
            mak = jnp.where(tri_s, m[0:U, U:2 * U], zero)
            mrb = jnp.where(tri_i, m[U:2 * U, 0:U], zero)
            mrk = jnp.where(tri_i, m[U:2 * U, U:2 * U], zero)
            t = eye + mab
            p = mab
            for _ in range(int(math.log2(C)) - 1):
                p = _mm(p, p)
                t = t + _mm(t, p)
            aph = _mm(t, at)
            uh = _mm(t, _mm(mak, v))
            rph = _mm(mrb, aph)
            yph = _mm(mrb, uh) + _mm(mrk, v)
            ap = jnp.where(hmask[h], aph, ap)
            uu = jnp.where(hmask[h], uh, uu)
            rp = jnp.where(hmask[h], rph, rp)
            yp = jnp.where(hmask[h], yph, yp)
        rp = rp + rt
        gs, hs = [], []
        for c in range(2):
            gl = gam[(c + 1) * C - 1:(c + 1) * C, :]
            apc = jnp.where(cmask[c], ap, zero)
            uvc = jnp.concatenate([jnp.where(cmask[c], uu, zero), jnp.where(cmask[c], v, zero)], axis=0)
            gs.append(jnp.where(same, eye + _mm_tn(apc, bt), zero) * gl)
            hs.append(jnp.where(same, _mm_tn(uvc, bk), zero) * gl)
        local.append((rp, yp, gs, hs))

    s = s_ref[...]
    for u in range(units):
        rp, yp, gs, hs = local[u]
        y0 = lax.dot_general(rp, s, NT, precision=HI, preferred_element_type=F32)
        s = _mm_hi(s, gs[0]) + hs[0]
        y1 = lax.dot_general(rp, s, NT, precision=HI, preferred_element_type=F32)
        s = _mm_hi(s, gs[1]) + hs[1]
        y = jnp.where(cmask[0], y0, y1) + yp
        sl = pl.ds(u * U, U)
        r, k, v = r_ref[sl, :], k_ref[sl, :], v_ref[sl, :]
        mean = _mm_hi(y, head_avg)
        yc = y - mean
        var = _mm_hi(yc * yc, head_avg)
        yn = yc * lax.rsqrt(var + RW_GN_EPS) * lng_ref[...] + lnb_ref[...]
        yn = yn + _mm_hi(r * k * rk_ref[...], head_sum) * v
        o_ref[sl, :] = (yn * g_ref[sl, :]).astype(o_ref.dtype)
    s_ref[...] = s


def _rwcore(r, ld, k, v, kk, b, g, ln_g, ln_b, r_k, batch, seq, units):
    n, width = r.shape
    rows = units * RW_UNIT
    steps = seq // rows
    blk = pl.BlockSpec((rows, LANES), lambda bb, hp, i: (bb * steps + i, hp))
    vec = pl.BlockSpec((1, LANES), lambda bb, hp, i: (0, hp))
    return pl.pallas_call(
        functools.partial(_rwcore_kernel, units=units),
        grid=(batch, width // LANES, steps),
        in_specs=[blk] * 7 + [vec] * 3,
        out_specs=blk,
        out_shape=jax.ShapeDtypeStruct((n, width), BF16),
        scratch_shapes=[pltpu.VMEM((RW_UNIT, RW_UNIT), F32)],
        compiler_params=_params("parallel", "parallel", "arbitrary"),
    )(r, ld, k, v, kk, b, g, ln_g, ln_b, r_k)


def _outproj_kernel(x_ref, oda_ref, orw_ref, wa_ref, wb_ref, g_ref, h_ref, u_ref):
    h = (x_ref[...] + jnp.dot(oda_ref[...], wa_ref[...], preferred_element_type=F32)
         + jnp.dot(orw_ref[...], wb_ref[...], preferred_element_type=F32))
    h_ref[...] = h
    u_ref[...] = _rms(h, g_ref[...]).astype(BF16)


def _outproj(x2, o_da, o_rw, wa, wb, g, tm):
    n, d = x2.shape
    da = o_da.shape[1]
    rw = o_rw.shape[1]
    row = lambda i: (i, 0)
    fix = lambda i: (0, 0)
    return pl.pallas_call(
        _outproj_kernel,
        grid=(n // tm,),
        in_specs=[pl.BlockSpec((tm, d), row), pl.BlockSpec((tm, da), row), pl.BlockSpec((tm, rw), row),
                  pl.BlockSpec((da, d), fix), pl.BlockSpec((rw, d), fix), pl.BlockSpec((1, d), fix)],
        out_specs=[pl.BlockSpec((tm, d), row), pl.BlockSpec((tm, d), row)],
        out_shape=[jax.ShapeDtypeStruct((n, d), F32), jax.ShapeDtypeStruct((n, d), BF16)],
        compiler_params=_params("parallel"),
    )(x2, o_da, o_rw, wa, wb, g)


def _topk_rows(s, k, payload=None):
    rows = s.shape[0]
    iota = lax.broadcasted_iota(jnp.int32, s.shape, 0).astype(F32)
    vals, sel = [], []
    for _ in range(k):
        m = jnp.max(s, axis=0, keepdims=True)
        am = jnp.min(jnp.where(s == m, iota, float(rows)), axis=0, keepdims=True)
        hit = iota == am
        vals.append(m)
        sel.append(am if payload is None else jnp.sum(jnp.where(hit, payload, 0.0), axis=0, keepdims=True))
        s = jnp.where(hit, -jnp.inf, s)
    return vals, sel


def _stack_rows(rows_list):
    k = len(rows_list)
    iota = lax.broadcasted_iota(jnp.int32, (k, rows_list[0].shape[1]), 0)
    out = jnp.zeros(iota.shape, rows_list[0].dtype)
    for j, r in enumerate(rows_list):
        out = jnp.where(iota == j, r, out)
    return out


def _peertopk_kernel(u_ref, wq_ref, keys_ref, idx_ref, gate_ref):
    u = u_ref[...]
    half = N_KEYS
    idx_rows, gate_rows = [], []
    for h in range(PEER_HEADS):
        tops = []
        for p in range(2):
            hp = h * 2 + p
            q_t = lax.dot_general(wq_ref[hp * half:(hp + 1) * half, :], u, NT, preferred_element_type=F32)
            s_t = jnp.dot(keys_ref[hp], q_t.astype(BF16), preferred_element_type=F32)
            tops.append(_topk_rows(s_t, PEER_TOPK))
        (v1, i1), (v2, i2) = tops
        v2s = _stack_rows(v2)
        i2s = _stack_rows(i2)
        cand = jnp.concatenate([v1[i] + v2s for i in range(PEER_TOPK)], axis=0)
        cidx = jnp.concatenate([i1[i] * float(N_KEYS) + i2s for i in range(PEER_TOPK)], axis=0)
        best, idx = _topk_rows(cand, PEER_TOPK, payload=cidx)
        e = [jnp.exp(b - best[0]) for b in best]
        den = functools.reduce(lambda a, b: a + b, e)
        idx_rows.append(_stack_rows(idx))
        gate_rows.append(_stack_rows([x / den for x in e]))
    idx_ref[...] = jnp.concatenate(idx_rows, axis=0).T.astype(jnp.int32)
    gate_ref[...] = jnp.concatenate(gate_rows, axis=0).T


def _peertopk(u_bf, wq_t, keys, tt):
    n, d = u_bf.shape
    hk = PEER_HEADS * PEER_TOPK
    row = lambda i: (i, 0)
    return pl.pallas_call(
        _peertopk_kernel,
        grid=(n // tt,),
        in_specs=[pl.BlockSpec((tt, d), row), pl.BlockSpec(wq_t.shape, lambda i: (0, 0)),
                  pl.BlockSpec(keys.shape, lambda i: (0, 0, 0))],
        out_specs=[pl.BlockSpec((tt, hk), row), pl.BlockSpec((tt, hk), row)],
        out_shape=[jax.ShapeDtypeStruct((n, hk), jnp.int32), jax.ShapeDtypeStruct((n, hk), F32)],
        compiler_params=_params("parallel"),
    )(u_bf, wq_t, keys)


SC_CORES = 2
SC_SUBCORES = 16
GATHER_ROWS = 32
GATHER_INDEX_WINDOW = 1024


def _gather_rows(u_tab, v_tab, idx_flat):
    p = idx_flat.shape[0]
    w = u_tab.shape[1]
    workers = SC_CORES * SC_SUBCORES
    per_worker = p // workers
    iw = min(GATHER_INDEX_WINDOW, per_worker)
    r = GATHER_ROWS
    assert p % workers == 0 and per_worker % iw == 0 and iw % (2 * r) == 0
    nsub = iw // r
    mesh = plsc.VectorSubcoreMesh(core_axis_name="core", subcore_axis_name="subcore")
    out = jax.ShapeDtypeStruct((p, w), u_tab.dtype)
    buf = pltpu.VMEM((r, w), u_tab.dtype)
    sem = pltpu.SemaphoreType.DMA

    @pl.kernel(out_type=(out, out), mesh=mesh,
               scratch_types=[pltpu.VMEM((iw,), jnp.int32), buf, buf, buf, buf, sem, sem, sem, sem, sem, sem, sem, sem])
    def gather(u_hbm, v_hbm, i_hbm, ou_hbm, ov_hbm, idx_v, ub0, ub1, vb0, vb1, gu0, gu1, gv0, gv1, wu0, wu1, wv0, wv1):
        wid = lax.axis_index("core") * SC_SUBCORES + lax.axis_index("subcore")
        ubuf, vbuf = (ub0, ub1), (vb0, vb1)
        gsu, gsv, wsu, wsv = (gu0, gu1), (gv0, gv1), (wu0, wu1), (wv0, wv1)

        @pl.loop(0, per_worker // iw)
        def _(o):
            start = wid * per_worker + o * iw
            pltpu.sync_copy(i_hbm.at[pl.ds(start, iw)], idx_v)

            def gathers(j):
                s = j % 2
                ids = idx_v.at[pl.ds(j * r, r)]
                return (pltpu.make_async_copy(u_hbm.at[ids], ubuf[s], gsu[s]),
                        pltpu.make_async_copy(v_hbm.at[ids], vbuf[s], gsv[s]))

            def writes(j):
                s = j % 2
                dst = pl.ds(start + j * r, r)
                return (pltpu.make_async_copy(ubuf[s], ou_hbm.at[dst], wsu[s]),
                        pltpu.make_async_copy(vbuf[s], ov_hbm.at[dst], wsv[s]))

            for j in range(nsub + 1):
                if j < nsub:
                    if j >= 2:
                        for c in writes(j - 2):
                            c.wait()
                    for c in gathers(j):
                        c.start()
                if j >= 1:
                    for c in gathers(j - 1):
                        c.wait()
                    for c in writes(j - 1):
                        c.start()
            for j in (nsub - 2, nsub - 1):
                for c in writes(j):
                    c.wait()

    return gather(u_tab, v_tab, idx_flat)


def _peermix_kernel(h_ref, xa_ref, xb_ref, gate_ref, ug_ref, vg_ref, o_ref, *, tt):
    hk2 = 2 * PEER_HEADS * PEER_TOPK
    x2 = jnp.concatenate([xa_ref[...], xb_ref[...]], axis=0)
    lane = lax.broadcasted_iota(jnp.int32, (tt, hk2), 1)
    row = lax.broadcasted_iota(jnp.int32, (tt, hk2), 0)
    even = (lane % 2) == 0
    part = jnp.zeros((tt, hk2), F32)
    for t in range(tt):
        ub = pltpu.bitcast(ug_ref[t * (hk2 // 2):(t + 1) * (hk2 // 2), :], BF16)
        r = lax.dot_general(x2, ub, NT, preferred_element_type=F32)
        part = jnp.where(row == t, jnp.where(even, r[0:tt], r[tt:2 * tt]), part)
    hid = part + jnp.where(even, pltpu.roll(part, hk2 - 1, 1), pltpu.roll(part, 1, 1))
    w = gate_ref[...] * (0.5 * hid * (1.0 + lax.erf(hid * (2.0 ** -0.5))))
    w2 = jnp.concatenate([jnp.where(even, w, 0.0), jnp.where(even, 0.0, w)], axis=0).astype(BF16)
    half = o_ref.shape[1] // 2
    orow = lax.broadcasted_iota(jnp.int32, (tt, half), 0)
    out_a = jnp.zeros((tt, half), F32)
    out_b = jnp.zeros((tt, half), F32)
    for t in range(tt):
        vb = pltpu.bitcast(vg_ref[t * (hk2 // 2):(t + 1) * (hk2 // 2), :], BF16)
        o = jnp.dot(w2, vb, preferred_element_type=F32)
        out_a = jnp.where(orow == t, o[0:tt], out_a)
        out_b = jnp.where(orow == t, o[tt:2 * tt], out_b)
    o_ref[:, 0:half] = h_ref[:, 0:half] + out_a
    o_ref[:, half:2 * half] = h_ref[:, half:2 * half] + out_b


def _peermix(h1, xa, xb, gate2, ug, vg, tt):
    n, d = h1.shape
    hk = PEER_HEADS * PEER_TOPK
    row = lambda i: (i, 0)
    return pl.pallas_call(
        functools.partial(_peermix_kernel, tt=tt),
        grid=(n // tt,),
        in_specs=[pl.BlockSpec((tt, d), row), pl.BlockSpec((tt, d // 2), row), pl.BlockSpec((tt, d // 2), row),
                  pl.BlockSpec((tt, 2 * hk), row), pl.BlockSpec((tt * hk, d // 2), row),
                  pl.BlockSpec((tt * hk, d // 2), row)],
        out_specs=pl.BlockSpec((tt, d), row),
        out_shape=jax.ShapeDtypeStruct((n, d), F32),
        compiler_params=_params("parallel"),
    )(h1, xa, xb, gate2, ug, vg)


def _ple_kernel(h_ref, p_ref, g_ref, wg_ref, wp_ref, gf_ref, o_ref):
    h = h_ref[...]
    gate = jax.nn.sigmoid(jnp.dot(_rms(h, g_ref[...]).astype(BF16), wg_ref[...], preferred_element_type=F32))
    pp = jnp.dot(p_ref[...].astype(BF16), wp_ref[...], preferred_element_type=F32)
    o_ref[...] = _rms(h + gate * pp, gf_ref[...])


def _ple(h2, p2, g, wg, wp, gf, tm):
    n, d = h2.shape
    pd = p2.shape[1]
    row = lambda i: (i, 0)
    fix = lambda i: (0, 0)
    return pl.pallas_call(
        _ple_kernel,
        grid=(n // tm,),
        in_specs=[pl.BlockSpec((tm, d), row), pl.BlockSpec((tm, pd), row), pl.BlockSpec((1, d), fix),
                  pl.BlockSpec((d, d), fix), pl.BlockSpec((pd, d), fix), pl.BlockSpec((1, d), fix)],
        out_specs=pl.BlockSpec((tm, d), row),
        out_shape=jax.ShapeDtypeStruct((n, d), F32),
        compiler_params=_params("parallel"),
    )(h2, p2, g, wg, wp, gf)


def _rope_tables(positions):
    half = ROT_DIM // 2
    inv_freq = ROPE_THETA ** (-jnp.arange(half, dtype=F32) * 2.0 / ROT_DIM)
    ang = positions.astype(F32).reshape(-1, 1) * inv_freq
    lane = jnp.arange(LANES)
    d = lane % DA_HEAD_DIM
    cos = jnp.take(jnp.cos(ang), d % half, axis=1)
    sin = jnp.take(jnp.sin(ang), d % half, axis=1)
    c = jnp.where(d < ROT_DIM, cos, 1.0)
    s1 = jnp.where(d < half, -sin, 0.0)
    s2 = jnp.where((d >= half) & (d < ROT_DIM), sin, 0.0)
    return c, s1, s2


def _pack_rows(tab):
    d = tab.shape[1]
    t = tab.astype(BF16)
    pair = jnp.stack([t[:, :d // 2], t[:, d // 2:]], axis=-1)
    return lax.bitcast_convert_type(pair, jnp.uint32)


def _block_diag_ones(width, head):
    i = jnp.arange(width)
    return (i[:, None] // head == i[None, :] // head).astype(F32)


def _tiles(seq):
    return dict(tm=min(256, seq), tq=min(512, seq), units=min(4, seq // RW_UNIT), tt_topk=min(256, seq), tt_mix=16,
                peer_chunks=2 if seq % 4096 == 0 else 1)


def kernel(x, p, positions, norm_mix_g, w_in, lam_q1, lam_k1, lam_q2, lam_k2, da_subln_g, rw_mu, rw_w0, rw_w_up, rw_a0, rw_a_up, rw_g_up, rw_k_k, rw_k_a, rw_r_k, rw_ln_g, rw_ln_b, w_out, norm_ffn_g, peer_w_q, peer_sub_keys, peer_u, peer_v, norm_ple_g, ple_gate_w, ple_proj_w, norm_final_g):
    batch, seq, d = x.shape
    t = _tiles(seq)
    row = lambda a: a.reshape(1, -1)
    f32 = F32

    w_in_bf = w_in[0].astype(BF16)
    lam = (jnp.exp(jnp.sum(lam_q1[0].astype(f32) * lam_k1[0].astype(f32)))
           - jnp.exp(jnp.sum(lam_q2[0].astype(f32) * lam_k2[0].astype(f32))) + LAM_INIT).reshape(1, 1)
    width = rw_w0.shape[1]
    wup_pad = jnp.concatenate([rw_w_up[0], jnp.zeros((LANES - rw_w_up.shape[1], width), f32)], axis=0)
    aup_pad = jnp.concatenate([jnp.zeros((LANES - rw_a_up.shape[1], width), f32), rw_a_up[0]], axis=0)
    head_ones = _block_diag_ones(width, RW_HEAD)
    w_out_bf = w_out[0].astype(BF16)
    da_w = DA_HEADS * 2 * DA_HEAD_DIM
    keys = peer_sub_keys[0].reshape(PEER_HEADS * 2, N_KEYS, -1).astype(BF16)
    wq_t = peer_w_q[0].T.astype(BF16)
    u_tab = _pack_rows(peer_u[0])
    v_tab = _pack_rows(peer_v[0])
    wg_bf = ple_gate_w[0].astype(BF16)
    wp_bf = ple_proj_w[0].astype(BF16)

    outs = []
    for b in range(batch):
        xs = x[b]
        rc, rs1, rs2 = _rope_tables(positions[b])
        qkv, zrw = _inproj(xs, row(norm_mix_g[0]), w_in_bf, rc, rs1, rs2, t["tm"])
        o_da = _attention(qkv, lam, row(da_subln_g[0]), 1, seq, t["tq"])
        rw = _rwprep(zrw, row(rw_mu[0]), row(rw_w0[0]), wup_pad, row(rw_a0[0]), aup_pad, rw_g_up[0],
                     row(rw_k_k[0]), row(rw_k_a[0]), head_ones, seq, t["tm"])
        o_rw = _rwcore(*rw, row(rw_ln_g[0]), row(rw_ln_b[0]), row(rw_r_k[0]), 1, seq, t["units"])
        h1, u2 = _outproj(xs, o_da, o_rw, w_out_bf[:da_w], w_out_bf[da_w:], row(norm_ffn_g[0]), t["tm"])

        idx, gate = _peertopk(u2, wq_t, keys, t["tt_topk"])
        gate2 = jnp.repeat(gate, 2, axis=1)
        xa, xb = u2[:, :d // 2], u2[:, d // 2:]
        nc = seq // t["peer_chunks"]
        h2_parts = []
        for c in range(t["peer_chunks"]):
            sl = slice(c * nc, (c + 1) * nc)
            ug, vg = _gather_rows(u_tab, v_tab, idx[sl].reshape(-1))
            h2_parts.append(_peermix(h1[sl], xa[sl], xb[sl], gate2[sl], ug, vg, t["tt_mix"]))
        h2 = jnp.concatenate(h2_parts, axis=0) if len(h2_parts) > 1 else h2_parts[0]
        outs.append(_ple(h2, p[0, b], row(norm_ple_g[0]), wg_bf, wp_bf, row(norm_final_g), t["tm"]))
    return jnp.stack(outs, axis=0)
```

```python
import functools
import math

import jax
import jax.numpy as jnp
from jax import lax
from jax.experimental import pallas as pl
from jax.experimental.pallas import tpu as pltpu
from jax.experimental.pallas import tpu_sc as plsc

F32 = jnp.float32
BF16 = jnp.bfloat16

NORM_EPS = 1e-6
DA_HEADS = 4
DA_HEAD_DIM = 64
ROPE_THETA = 500000.0
ROT_DIM = DA_HEAD_DIM // 4
RW_HEAD = 64
RW_GN_EPS = 64e-5
PEER_HEADS = 8
N_KEYS = 128
PEER_TOPK = 16
LAM_INIT = 0.8 - 0.6 * math.exp(-0.3 * 0)

LANES = 128
VMEM_LIMIT = 56 * 1024 * 1024
RW_CHUNK = 64
RW_UNIT = 2 * RW_CHUNK

NT = (((1,), (1,)), ((), ()))
TN = (((0,), (0,)), ((), ()))
HI = lax.Precision.HIGHEST


def _mm(a, b):
    return jnp.dot(a.astype(BF16), b.astype(BF16), preferred_element_type=F32)


def _mm_nt(a, b):
    return lax.dot_general(a.astype(BF16), b.astype(BF16), NT, preferred_element_type=F32)


def _mm_tn(a, b):
    return lax.dot_general(a.astype(BF16), b.astype(BF16), TN, preferred_element_type=F32)


def _mm_hi(a, b):
    return jnp.dot(a, b, precision=HI, preferred_element_type=F32)


def _params(*sem):
    return pltpu.CompilerParams(dimension_semantics=sem, vmem_limit_bytes=VMEM_LIMIT)


def _rms(x, g):
    return x * lax.rsqrt(jnp.mean(x * x, axis=-1, keepdims=True) + NORM_EPS) * g


def _inproj_kernel(x_ref, g_ref, w_ref, c_ref, s1_ref, s2_ref, qkv_ref, zrw_ref, *, n_qk, n_da):
    u = _rms(x_ref[...], g_ref[...]).astype(BF16)
    z = jnp.dot(u, w_ref[...], preferred_element_type=F32)
    c, s1, s2 = c_ref[...], s1_ref[...], s2_ref[...]
    half = ROT_DIM // 2
    for blk in range(n_da // LANES):
        t = z[:, blk * LANES:(blk + 1) * LANES]
        if blk < 2 * n_qk // LANES:
            t = t * c + pltpu.roll(t, LANES - half, 1) * s1 + pltpu.roll(t, half, 1) * s2
        if blk < n_qk // LANES:
            t = t * (DA_HEAD_DIM ** -0.5 * math.log2(math.e))
        qkv_ref[:, blk * LANES:(blk + 1) * LANES] = t.astype(BF16)
    zrw_ref[...] = z[:, n_da:]


def _inproj(x2, g, w_in_bf, rc, rs1, rs2, tm):
    n, d = x2.shape
    n_in = w_in_bf.shape[1]
    n_qk = DA_HEADS * 2 * DA_HEAD_DIM
    n_da = 3 * n_qk
    row = lambda i: (i, 0)
    fix = lambda i: (0, 0)
    return pl.pallas_call(
        functools.partial(_inproj_kernel, n_qk=n_qk, n_da=n_da),
        grid=(n // tm,),
        in_specs=[pl.BlockSpec((tm, d), row), pl.BlockSpec((1, d), fix), pl.BlockSpec((d, n_in), fix),
                  pl.BlockSpec((tm, LANES), row), pl.BlockSpec((tm, LANES), row), pl.BlockSpec((tm, LANES), row)],
        out_specs=[pl.BlockSpec((tm, n_da), row), pl.BlockSpec((tm, n_in - n_da), row)],
        out_shape=[jax.ShapeDtypeStruct((n, n_da), BF16), jax.ShapeDtypeStruct((n, n_in - n_da), F32)],
        compiler_params=_params("parallel"),
    )(x2, g, w_in_bf, rc, rs1, rs2)


def _attn_kernel(lam_ref, q_ref, k_ref, v_ref, sg_ref, o_ref, m_ref, acc_ref, *, tq):
    i = pl.program_id(2)
    q = q_ref[...]
    lane = lax.broadcasted_iota(jnp.int32, q.shape, 1)
    zero = jnp.zeros_like(q)
    qs = (jnp.where(lane < DA_HEAD_DIM, q, zero), jnp.where(lane >= DA_HEAD_DIM, q, zero))
    m_ref[...] = jnp.full(m_ref.shape, -jnp.inf, F32)
    acc_ref[...] = jnp.zeros(acc_ref.shape, F32)
    ones = jnp.ones((tq, LANES), BF16)

    def block(j, masked):
        kj = k_ref[pl.ds(pl.multiple_of(j * tq, tq), tq), :]
        vj = jnp.concatenate([v_ref[pl.ds(pl.multiple_of(j * tq, tq), tq), :], ones], axis=1)
        for c in range(2):
            s = lax.dot_general(qs[c], kj, NT, preferred_element_type=F32)
            if masked:
                r_id = lax.broadcasted_iota(jnp.int32, s.shape, 0)
                c_id = lax.broadcasted_iota(jnp.int32, s.shape, 1)
                s = jnp.where(c_id <= r_id, s, -jnp.inf)
            m_old = m_ref[c]
            m_new = jnp.maximum(m_old, jnp.max(s, axis=-1, keepdims=True))
            alpha = jnp.exp2(m_old - m_new)
            p = jnp.exp2(s - jnp.tile(m_new, (1, tq // LANES)))
            pv = jnp.dot(p.astype(BF16), vj, preferred_element_type=F32)
            acc_ref[c] = jnp.tile(alpha, (1, 2)) * acc_ref[c] + pv
            m_ref[c] = m_new

    def body(j, carry):
        block(j, False)
        return carry

    lax.fori_loop(0, i, body, 0)
    block(i, True)
    lam = lam_ref[0, 0]
    a0, a1 = acc_ref[0], acc_ref[1]
    o = a0[:, :LANES] / a0[:, LANES:] - lam * (a1[:, :LANES] / a1[:, LANES:])
    o = o * lax.rsqrt(jnp.mean(o * o, axis=-1, keepdims=True) + NORM_EPS) * sg_ref[...] * (1.0 - LAM_INIT)
    o_ref[...] = o.astype(o_ref.dtype)


def _attention(qkv, lam, subln_g, batch, seq, tq):
    n = qkv.shape[0]
    nq = seq // tq
    h = DA_HEADS
    return pl.pallas_call(
        functools.partial(_attn_kernel, tq=tq),
        grid=(batch, h, nq),
        in_specs=[pl.BlockSpec(memory_space=pltpu.SMEM),
                  pl.BlockSpec((tq, LANES), lambda b, hh, i: (b * nq + i, hh)),
                  pl.BlockSpec((seq, LANES), lambda b, hh, i: (b, h + hh)),
                  pl.BlockSpec((seq, LANES), lambda b, hh, i: (b, 2 * h + hh)),
                  pl.BlockSpec((1, LANES), lambda b, hh, i: (0, 0))],
        out_specs=pl.BlockSpec((tq, LANES), lambda b, hh, i: (b * nq + i, hh)),
        out_shape=jax.ShapeDtypeStruct((n, h * LANES), BF16),
        scratch_shapes=[pltpu.VMEM((2, tq, LANES), F32), pltpu.VMEM((2, tq, 2 * LANES), F32)],
        compiler_params=_params("parallel", "parallel", "arbitrary"),
    )(lam, qkv, qkv, qkv, subln_g)


def _rwprep_kernel(z_ref, zp_ref, mu_ref, w0_ref, wup_ref, a0_ref, aup_ref, gup_ref, kk_ref, ka_ref, bd_ref,
                   r_o, ld_o, k_o, v_o, kk_o, b_o, g_o, *, tiles_per_seq, width):
    i = pl.program_id(0)
    z = z_ref[...]
    last = zp_ref[7:8, :]
    first = jnp.where(i % tiles_per_seq == 0, jnp.zeros_like(last), last)
    row = lax.broadcasted_iota(jnp.int32, z.shape, 0)
    prev = jnp.where(row == 0, first, pltpu.roll(z, 1, 0))
    zs = z + (prev - z) * mu_ref[...]
    r = zs[:, 0:width]
    k = zs[:, width:2 * width]
    v = zs[:, 2 * width:3 * width]
    xwa = zs[:, 3 * width:3 * width + LANES]
    xg = zs[:, 3 * width + LANES:3 * width + 2 * LANES]
    w = -jax.nn.softplus(-(w0_ref[...] + _mm_hi(jnp.tanh(xwa), wup_ref[...]))) - 0.5
    a = jax.nn.sigmoid(a0_ref[...] + _mm_hi(xwa, aup_ref[...]))
    g = _mm_hi(jax.nn.sigmoid(xg), gup_ref[...])
    kk = k * kk_ref[...]
    kk = kk / jnp.maximum(jnp.sqrt(_mm_hi(kk * kk, bd_ref[...])), 1e-12)
    r_o[...] = r
    ld_o[...] = -jnp.exp(w)
    k_o[...] = k * (1.0 + (a - 1.0) * ka_ref[...])
    v_o[...] = v
    kk_o[...] = kk
    b_o[...] = kk * a
    g_o[...] = g


def _rwprep(zrw, mu, w0, wup_pad, a0, aup_pad, gup, k_k, k_a, bd, seq, tm):
    n, zin = zrw.shape
    width = w0.shape[1]
    row = lambda i: (i, 0)
    fix = lambda i: (0, 0)
    prev = lambda i: (jnp.maximum(i * (tm // 8) - 1, 0), 0)
    out = jax.ShapeDtypeStruct((n, width), F32)
    return pl.pallas_call(
        functools.partial(_rwprep_kernel, tiles_per_seq=seq // tm, width=width),
        grid=(n // tm,),
        in_specs=[pl.BlockSpec((tm, zin), row), pl.BlockSpec((8, zin), prev), pl.BlockSpec((1, zin), fix),
                  pl.BlockSpec((1, width), fix), pl.BlockSpec((LANES, width), fix),
                  pl.BlockSpec((1, width), fix), pl.BlockSpec((LANES, width), fix), pl.BlockSpec((LANES, width), fix),
                  pl.BlockSpec((1, width), fix), pl.BlockSpec((1, width), fix), pl.BlockSpec((width, width), fix)],
        out_specs=[pl.BlockSpec((tm, width), row)] * 7,
        out_shape=[out] * 7,
        compiler_params=_params("parallel"),
    )(zrw, zrw, mu, w0, wup_pad, a0, aup_pad, gup, k_k, k_a, bd)


def _rwcore_kernel(r_ref, ld_ref, k_ref, v_ref, kk_ref, b_ref, g_ref, lng_ref, lnb_ref, rk_ref, o_ref, s_ref, *, units):
    U, C, HD = RW_UNIT, RW_CHUNK, RW_HEAD

    @pl.when(pl.program_id(2) == 0)
    def _():
        s_ref[...] = jnp.zeros(s_ref.shape, F32)

    ri = lax.broadcasted_iota(jnp.int32, (U, U), 0)
    ci = lax.broadcasted_iota(jnp.int32, (U, U), 1)
    same = (ri // C) == (ci // C)
    tri_s = same & (ci < ri)
    tri_i = same & (ci <= ri)
    eye = (ri == ci).astype(F32)
    cum_w = tri_i.astype(F32)
    head_avg = same.astype(F32) * (1.0 / HD)
    head_sum = same.astype(F32)
    lane = lax.broadcasted_iota(jnp.int32, (U, U), 1)
    rowi = lax.broadcasted_iota(jnp.int32, (U, U), 0)
    hmask = (lane < HD, lane >= HD)
    cmask = (rowi < C, rowi >= C)
    zero = jnp.zeros((U, U), F32)

    local = []
    for u in range(units):
        sl = pl.ds(u * U, U)
        r, ld, k, v, kk, b = r_ref[sl, :], ld_ref[sl, :], k_ref[sl, :], v_ref[sl, :], kk_ref[sl, :], b_ref[sl, :]
        cum = _mm_hi(cum_w, ld)
        gam = jnp.exp(cum)
        ginv = jnp.exp(-cum)
        at = -kk * jnp.exp(cum - ld)
        bt = b * ginv
        kt = k * ginv
        rt = r * gam
        bk = jnp.concatenate([bt, kt], axis=0)
        ap = zero
        uu = zero
        rp = zero
        yp = zero
        for h in range(2):
            ar = jnp.concatenate([jnp.where(hmask[h], at, zero), jnp.where(hmask[h], rt, zero)], axis=0)
            m = _mm_nt(ar, bk)
            mab = jnp.where(tri_s, m[0:U, 0:U], zero)
            mak = jnp.where(tri_s, m[0:U, U:2 * U], zero)
            mrb = jnp.where(tri_i, m[U:2 * U, 0:U], zero)
            mrk = jnp.where(tri_i, m[U:2 * U, U:2 * U], zero)
            t = eye + mab
            p = mab
            for _ in range(int(math.log2(C)) - 1):
                p = _mm(p, p)
                t = t + _mm(t, p)
            aph = _mm(t, at)
            uh = _mm(t, _mm(mak, v))
            rph = _mm(mrb, aph)
            yph = _mm(mrb, uh) + _mm(mrk, v)
            ap = jnp.where(hmask[h], aph, ap)
            uu = jnp.where(hmask[h], uh, uu)
            rp = jnp.where(hmask[h], rph, rp)
            yp = jnp.where(hmask[h], yph, yp)
        rp = rp + rt
        gs, hs = [], []
        for c in range(2):
            gl = gam[(c + 1) * C - 1:(c + 1) * C, :]
            apc = jnp.where(cmask[c], ap, zero)
            uvc = jnp.concatenate([jnp.where(cmask[c], uu, zero), jnp.where(cmask[c], v, zero)], axis=0)
            gs.append(jnp.where(same, eye + _mm_tn(apc, bt), zero) * gl)
            hs.append(jnp.where(same, _mm_tn(uvc, bk), zero) * gl)
        local.append((rp, yp, gs, hs))

    s = s_ref[...]
    for u in range(units):
        rp, yp, gs, hs = local[u]
        y0 = lax.dot_general(rp, s, NT, precision=HI, preferred_element_type=F32)
        s = _mm_hi(s, gs[0]) + hs[0]
        y1 = lax.dot_general(rp, s, NT, precision=HI, preferred_element_type=F32)
        s = _mm_hi(s, gs[1]) + hs[1]
        y = jnp.where(cmask[0], y0, y1) + yp
        sl = pl.ds(u * U, U)
        r, k, v = r_ref[sl, :], k_ref[sl, :], v_ref[sl, :]
        mean = _mm_hi(y, head_avg)
        yc = y - mean
        var = _mm_hi(yc * yc, head_avg)
        yn = yc * lax.rsqrt(var + RW_GN_EPS) * lng_ref[...] + lnb_ref[...]
        yn = yn + _mm_hi(r * k * rk_ref[...], head_sum) * v
        o_ref[sl, :] = (yn * g_ref[sl, :]).astype(o_ref.dtype)
    s_ref[...] = s


def _rwcore(r, ld, k, v, kk, b, g, ln_g, ln_b, r_k, batch, seq, units):
    n, width = r.shape
    rows = units * RW_UNIT
    steps = seq // rows
    blk = pl.BlockSpec((rows, LANES), lambda bb, hp, i: (bb * steps + i, hp))
    vec = pl.BlockSpec((1, LANES), lambda bb, hp, i: (0, hp))
    return pl.pallas_call(
        functools.partial(_rwcore_kernel, units=units),
        grid=(batch, width // LANES, steps),
        in_specs=[blk] * 7 + [vec] * 3,
        out_specs=blk,
        out_shape=jax.ShapeDtypeStruct((n, width), BF16),
        scratch_shapes=[pltpu.VMEM((RW_UNIT, RW_UNIT), F32)],
        compiler_params=_params("parallel", "parallel", "arbitrary"),
    )(r, ld, k, v, kk, b, g, ln_g, ln_b, r_k)


def _outproj_kernel(x_ref, oda_ref, orw_ref, wa_ref, wb_ref, g_ref, h_ref, u_ref):
    h = (x_ref[...] + jnp.dot(oda_ref[...], wa_ref[...], preferred_element_type=F32)
         + jnp.dot(orw_ref[...], wb_ref[...], preferred_element_type=F32))
    h_ref[...] = h
    u_ref[...] = _rms(h, g_ref[...]).astype(BF16)


def _outproj(x2, o_da, o_rw, wa, wb, g, tm):
    n, d = x2.shape
    da = o_da.shape[1]
    rw = o_rw.shape[1]
    row = lambda i: (i, 0)
    fix = lambda i: (0, 0)
    return pl.pallas_call(
        _outproj_kernel,
        grid=(n // tm,),
        in_specs=[pl.BlockSpec((tm, d), row), pl.BlockSpec((tm, da), row), pl.BlockSpec((tm, rw), row),
                  pl.BlockSpec((da, d), fix), pl.BlockSpec((rw, d), fix), pl.BlockSpec((1, d), fix)],
        out_specs=[pl.BlockSpec((tm, d), row), pl.BlockSpec((tm, d), row)],
        out_shape=[jax.ShapeDtypeStruct((n, d), F32), jax.ShapeDtypeStruct((n, d), BF16)],
        compiler_params=_params("parallel"),
    )(x2, o_da, o_rw, wa, wb, g)


def _topk_rows(s, k, payload=None):
    rows = s.shape[0]
    iota = lax.broadcasted_iota(jnp.int32, s.shape, 0).astype(F32)
    vals, sel = [], []
    for _ in range(k):
        m = jnp.max(s, axis=0, keepdims=True)
        am = jnp.min(jnp.where(s == m, iota, float(rows)), axis=0, keepdims=True)
        hit = iota == am
        vals.append(m)
        sel.append(am if payload is None else jnp.sum(jnp.where(hit, payload, 0.0), axis=0, keepdims=True))
        s = jnp.where(hit, -jnp.inf, s)
    return vals, sel


def _stack_rows(rows_list):
    k = len(rows_list)
    iota = lax.broadcasted_iota(jnp.int32, (k, rows_list[0].shape[1]), 0)
    out = jnp.zeros(iota.shape, rows_list[0].dtype)
    for j, r in enumerate(rows_list):
        out = jnp.where(iota == j, r, out)
    return out


def _peertopk_kernel(u_ref, wq_ref, keys_ref, idx_ref, gate_ref):
    u = u_ref[...]
    half = N_KEYS
    idx_rows, gate_rows = [], []
    for h in range(PEER_HEADS):
        tops = []
        for p in range(2):
            hp = h * 2 + p
            q_t = lax.dot_general(wq_ref[hp * half:(hp + 1) * half, :], u, NT, preferred_element_type=F32)
            s_t = jnp.dot(keys_ref[hp], q_t.astype(BF16), preferred_element_type=F32)
            tops.append(_topk_rows(s_t, PEER_TOPK))
        (v1, i1), (v2, i2) = tops
        v2s = _stack_rows(v2)
        i2s = _stack_rows(i2)
        cand = jnp.concatenate([v1[i] + v2s for i in range(PEER_TOPK)], axis=0)
        cidx = jnp.concatenate([i1[i] * float(N_KEYS) + i2s for i in range(PEER_TOPK)], axis=0)
        best, idx = _topk_rows(cand, PEER_TOPK, payload=cidx)
        e = [jnp.exp(b - best[0]) for b in best]
        den = functools.reduce(lambda a, b: a + b, e)
        idx_rows.append(_stack_rows(idx))
        gate_rows.append(_stack_rows([x / den for x in e]))
    idx_ref[...] = jnp.concatenate(idx_rows, axis=0).T.astype(jnp.int32)
    gate_ref[...] = jnp.concatenate(gate_rows, axis=0).T


def _peertopk(u_bf, wq_t, keys, tt):
    n, d = u_bf.shape
    hk = PEER_HEADS * PEER_TOPK
    row = lambda i: (i, 0)
    return pl.pallas_call(
        _peertopk_kernel,
        grid=(n // tt,),
        in_specs=[pl.BlockSpec((tt, d), row), pl.BlockSpec(wq_t.shape, lambda i: (0, 0)),
                  pl.BlockSpec(keys.shape, lambda i: (0, 0, 0))],
        out_specs=[pl.BlockSpec((tt, hk), row), pl.BlockSpec((tt, hk), row)],
        out_shape=[jax.ShapeDtypeStruct((n, hk), jnp.int32), jax.ShapeDtypeStruct((n, hk), F32)],
        compiler_params=_params("parallel"),
    )(u_bf, wq_t, keys)


SC_CORES = 2
SC_SUBCORES = 16
GATHER_ROWS = 16
GATHER_SLOTS = 6
GATHER_INDEX_WINDOW = 1024


def _gather_rows(u_tab, v_tab, idx_flat):
    p = idx_flat.shape[0]
    w = u_tab.shape[1]
    workers = SC_CORES * SC_SUBCORES
    per_worker = p // workers
    iw = min(GATHER_INDEX_WINDOW, per_worker)
    r = GATHER_ROWS
    ns = GATHER_SLOTS
    lag = ns - 1
    nsub = iw // r
    assert p % workers == 0 and per_worker % iw == 0 and iw % r == 0 and nsub >= ns
    mesh = plsc.VectorSubcoreMesh(core_axis_name="core", subcore_axis_name="subcore")
    out = jax.ShapeDtypeStruct((p, w), u_tab.dtype)
    buf = pltpu.VMEM((r, w), u_tab.dtype)
    sem = pltpu.SemaphoreType.DMA

    @pl.kernel(out_type=(out, out), mesh=mesh,
               scratch_types=[pltpu.VMEM((iw,), jnp.int32)] + [buf] * (2 * ns) + [sem] * (4 * ns))
    def gather(u_hbm, v_hbm, i_hbm, ou_hbm, ov_hbm, idx_v, *scratch):
        ubuf, vbuf = scratch[0:ns], scratch[ns:2 * ns]
        sems = scratch[2 * ns:]
        gsu, gsv, wsu, wsv = sems[0:ns], sems[ns:2 * ns], sems[2 * ns:3 * ns], sems[3 * ns:4 * ns]
        wid = lax.axis_index("core") * SC_SUBCORES + lax.axis_index("subcore")

        @pl.loop(0, per_worker // iw)
        def _(o):
            start = wid * per_worker + o * iw
            pltpu.sync_copy(i_hbm.at[pl.ds(start, iw)], idx_v)

            def gathers(j):
                s = j % ns
                ids = idx_v.at[pl.ds(j * r, r)]
                return (pltpu.make_async_copy(u_hbm.at[ids], ubuf[s], gsu[s]),
                        pltpu.make_async_copy(v_hbm.at[ids], vbuf[s], gsv[s]))

            def writes(j):
                s = j % ns
                dst = pl.ds(start + j * r, r)
                return (pltpu.make_async_copy(ubuf[s], ou_hbm.at[dst], wsu[s]),
                        pltpu.make_async_copy(vbuf[s], ov_hbm.at[dst], wsv[s]))

            for j in range(nsub + lag):
                if j < nsub:
                    if j >= ns:
                        for c in writes(j - ns):
                            c.wait()
                    for c in gathers(j):
                        c.start()
                if j >= lag:
                    for c in gathers(j - lag):
                        c.wait()
                    for c in writes(j - lag):
                        c.start()
            for j in range(nsub - ns, nsub):
                for c in writes(j):
                    c.wait()

    return gather(u_tab, v_tab, idx_flat)


def _peermix_kernel(h_ref, xa_ref, xb_ref, gate_ref, ug_ref, vg_ref, o_ref, *, tt):
    hk2 = 2 * PEER_HEADS * PEER_TOPK
    x2 = jnp.concatenate([xa_ref[...], xb_ref[...]], axis=0)
    lane = lax.broadcasted_iota(jnp.int32, (tt, hk2), 1)
    row = lax.broadcasted_iota(jnp.int32, (tt, hk2), 0)
    even = (lane % 2) == 0
    part = jnp.zeros((tt, hk2), F32)
    for t in range(tt):
        ub = pltpu.bitcast(ug_ref[t * (hk2 // 2):(t + 1) * (hk2 // 2), :], BF16)
        r = lax.dot_general(x2, ub, NT, preferred_element_type=F32)
        part = jnp.where(row == t, jnp.where(even, r[0:tt], r[tt:2 * tt]), part)
    hid = part + jnp.where(even, pltpu.roll(part, hk2 - 1, 1), pltpu.roll(part, 1, 1))
    w = gate_ref[...] * (0.5 * hid * (1.0 + lax.erf(hid * (2.0 ** -0.5))))
    w2 = jnp.concatenate([jnp.where(even, w, 0.0), jnp.where(even, 0.0, w)], axis=0).astype(BF16)
    half = o_ref.shape[1] // 2
    orow = lax.broadcasted_iota(jnp.int32, (tt, half), 0)
    out_a = jnp.zeros((tt, half), F32)
    out_b = jnp.zeros((tt, half), F32)
    for t in range(tt):
        vb = pltpu.bitcast(vg_ref[t * (hk2 // 2):(t + 1) * (hk2 // 2), :], BF16)
        o = jnp.dot(w2, vb, preferred_element_type=F32)
        out_a = jnp.where(orow == t, o[0:tt], out_a)
        out_b = jnp.where(orow == t, o[tt:2 * tt], out_b)
    o_ref[:, 0:half] = h_ref[:, 0:half] + out_a
    o_ref[:, half:2 * half] = h_ref[:, half:2 * half] + out_b


def _peermix(h1, xa, xb, gate2, ug, vg, tt):
    n, d = h1.shape
    hk = PEER_HEADS * PEER_TOPK
    row = lambda i: (i, 0)
    return pl.pallas_call(
        functools.partial(_peermix_kernel, tt=tt),
        grid=(n // tt,),
        in_specs=[pl.BlockSpec((tt, d), row), pl.BlockSpec((tt, d // 2), row), pl.BlockSpec((tt, d // 2), row),
                  pl.BlockSpec((tt, 2 * hk), row), pl.BlockSpec((tt * hk, d // 2), row),
                  pl.BlockSpec((tt * hk, d // 2), row)],
        out_specs=pl.BlockSpec((tt, d), row),
        out_shape=jax.ShapeDtypeStruct((n, d), F32),
        compiler_params=_params("parallel"),
    )(h1, xa, xb, gate2, ug, vg)


def _ple_kernel(h_ref, p_ref, g_ref, wg_ref, wp_ref, gf_ref, o_ref):
    h = h_ref[...]
    gate = jax.nn.sigmoid(jnp.dot(_rms(h, g_ref[...]).astype(BF16), wg_ref[...], preferred_element_type=F32))
    pp = jnp.dot(p_ref[...].astype(BF16), wp_ref[...], preferred_element_type=F32)
    o_ref[...] = _rms(h + gate * pp, gf_ref[...])


def _ple(h2, p2, g, wg, wp, gf, tm):
    n, d = h2.shape
    pd = p2.shape[1]
    row = lambda i: (i, 0)
    fix = lambda i: (0, 0)
    return pl.pallas_call(
        _ple_kernel,
        grid=(n // tm,),
        in_specs=[pl.BlockSpec((tm, d), row), pl.BlockSpec((tm, pd), row), pl.BlockSpec((1, d), fix),
                  pl.BlockSpec((d, d), fix), pl.BlockSpec((pd, d), fix), pl.BlockSpec((1, d), fix)],
        out_specs=pl.BlockSpec((tm, d), row),
        out_shape=jax.ShapeDtypeStruct((n, d), F32),
        compiler_params=_params("parallel"),
    )(h2, p2, g, wg, wp, gf)


def _rope_tables(positions):
    half = ROT_DIM // 2
    inv_freq = ROPE_THETA ** (-jnp.arange(half, dtype=F32) * 2.0 / ROT_DIM)
    ang = positions.astype(F32).reshape(-1, 1) * inv_freq
    lane = jnp.arange(LANES)
    d = lane % DA_HEAD_DIM
    cos = jnp.take(jnp.cos(ang), d % half, axis=1)
    sin = jnp.take(jnp.sin(ang), d % half, axis=1)
    c = jnp.where(d < ROT_DIM, cos, 1.0)
    s1 = jnp.where(d < half, -sin, 0.0)
    s2 = jnp.where((d >= half) & (d < ROT_DIM), sin, 0.0)
    return c, s1, s2


def _pack_rows(tab):
    d = tab.shape[1]
    t = tab.astype(BF16)
    pair = jnp.stack([t[:, :d // 2], t[:, d // 2:]], axis=-1)
    return lax.bitcast_convert_type(pair, jnp.uint32)


def _block_diag_ones(width, head):
    i = jnp.arange(width)
    return (i[:, None] // head == i[None, :] // head).astype(F32)


def _tiles(seq):
    return dict(tm=min(256, seq), tq=min(512, seq), units=min(4, seq // RW_UNIT), tt_topk=min(256, seq), tt_mix=16,
                peer_chunks=2 if seq % 4096 == 0 else 1)


def kernel(x, p, positions, norm_mix_g, w_in, lam_q1, lam_k1, lam_q2, lam_k2, da_subln_g, rw_mu, rw_w0, rw_w_up, rw_a0, rw_a_up, rw_g_up, rw_k_k, rw_k_a, rw_r_k, rw_ln_g, rw_ln_b, w_out, norm_ffn_g, peer_w_q, peer_sub_keys, peer_u, peer_v, norm_ple_g, ple_gate_w, ple_proj_w, norm_final_g):
    batch, seq, d = x.shape
    t = _tiles(seq)
    row = lambda a: a.reshape(1, -1)
    f32 = F32

    w_in_bf = w_in[0].astype(BF16)
    lam = (jnp.exp(jnp.sum(lam_q1[0].astype(f32) * lam_k1[0].astype(f32)))
           - jnp.exp(jnp.sum(lam_q2[0].astype(f32) * lam_k2[0].astype(f32))) + LAM_INIT).reshape(1, 1)
    width = rw_w0.shape[1]
    wup_pad = jnp.concatenate([rw_w_up[0], jnp.zeros((LANES - rw_w_up.shape[1], width), f32)], axis=0)
    aup_pad = jnp.concatenate([jnp.zeros((LANES - rw_a_up.shape[1], width), f32), rw_a_up[0]], axis=0)
    head_ones = _block_diag_ones(width, RW_HEAD)
    w_out_bf = w_out[0].astype(BF16)
    da_w = DA_HEADS * 2 * DA_HEAD_DIM
    keys = peer_sub_keys[0].reshape(PEER_HEADS * 2, N_KEYS, -1).astype(BF16)
    wq_t = peer_w_q[0].T.astype(BF16)
    u_tab = _pack_rows(peer_u[0])
    v_tab = _pack_rows(peer_v[0])
    wg_bf = ple_gate_w[0].astype(BF16)
    wp_bf = ple_proj_w[0].astype(BF16)

    outs = []
    for b in range(batch):
        xs = x[b]
        rc, rs1, rs2 = _rope_tables(positions[b])
        qkv, zrw = _inproj(xs, row(norm_mix_g[0]), w_in_bf, rc, rs1, rs2, t["tm"])
        o_da = _attention(qkv, lam, row(da_subln_g[0]), 1, seq, t["tq"])
        rw = _rwprep(zrw, row(rw_mu[0]), row(rw_w0[0]), wup_pad, row(rw_a0[0]), aup_pad, rw_g_up[0],
                     row(rw_k_k[0]), row(rw_k_a[0]), head_ones, seq, t["tm"])
        o_rw = _rwcore(*rw, row(rw_ln_g[0]), row(rw_ln_b[0]), row(rw_r_k[0]), 1, seq, t["units"])
        h1, u2 = _outproj(xs, o_da, o_rw, w_out_bf[:da_w], w_out_bf[da_w:], row(norm_ffn_g[0]), t["tm"])

        idx, gate = _peertopk(u2, wq_t, keys, t["tt_topk"])
        gate2 = jnp.repeat(gate, 2, axis=1)
        xa, xb = u2[:, :d // 2], u2[:, d // 2:]
        nc = seq // t["peer_chunks"]
        h2_parts = []
        for c in range(t["peer_chunks"]):
            sl = slice(c * nc, (c + 1) * nc)
            ug, vg = _gather_rows(u_tab, v_tab, idx[sl].reshape(-1))
            h2_parts.append(_peermix(h1[sl], xa[sl], xb[sl], gate2[sl], ug, vg, t["tt_mix"]))
        h2 = jnp.concatenate(h2_parts, axis=0) if len(h2_parts) > 1 else h2_parts[0]
        outs.append(_ple(h2, p[0, b], row(norm_ple_g[0]), wg_bf, wp_bf, row(norm_final_g), t["tm"]))
    return jnp.stack(outs, axis=0)
```

```python
import functools
import math

import jax
import jax.numpy as jnp
from jax import lax
from jax.experimental import pallas as pl
from jax.experimental.pallas import tpu as pltpu
from jax.experimental.pallas import tpu_sc as plsc

F32 = jnp.float32
BF16 = jnp.bfloat16

NORM_EPS = 1e-6
DA_HEADS = 4
DA_HEAD_DIM = 64
ROPE_THETA = 500000.0
ROT_DIM = DA_HEAD_DIM // 4
RW_HEAD = 64
RW_GN_EPS = 64e-5
PEER_HEADS = 8
N_KEYS = 128
PEER_TOPK = 16
LAM_INIT = 0.8 - 0.6 * math.exp(-0.3 * 0)

LANES = 128
VMEM_LIMIT = 56 * 1024 * 1024
RW_CHUNK = 64
RW_UNIT = 2 * RW_CHUNK

NT = (((1,), (1,)), ((), ()))
TN = (((0,), (0,)), ((), ()))
HI = lax.Precision.HIGHEST


def _mm(a, b):
    return jnp.dot(a.astype(BF16), b.astype(BF16), preferred_element_type=F32)


def _mm_nt(a, b):
    return lax.dot_general(a.astype(BF16), b.astype(BF16), NT, preferred_element_type=F32)


def _mm_tn(a, b):
    return lax.dot_general(a.astype(BF16), b.astype(BF16), TN, preferred_element_type=F32)


def _mm_hi(a, b):
    return jnp.dot(a, b, precision=HI, preferred_element_type=F32)


def _params(*sem):
    return pltpu.CompilerParams(dimension_semantics=sem, vmem_limit_bytes=VMEM_LIMIT)


def _rms(x, g):
    return x * lax.rsqrt(jnp.mean(x * x, axis=-1, keepdims=True) + NORM_EPS) * g


def _inproj_kernel(x_ref, g_ref, w_ref, c_ref, s1_ref, s2_ref, qkv_ref, zrw_ref, *, n_qk, n_da):
    u = _rms(x_ref[...], g_ref[...]).astype(BF16)
    z = jnp.dot(u, w_ref[...], preferred_element_type=F32)
    c, s1, s2 = c_ref[...], s1_ref[...], s2_ref[...]
    half = ROT_DIM // 2
    for blk in range(n_da // LANES):
        t = z[:, blk * LANES:(blk + 1) * LANES]
        if blk < 2 * n_qk // LANES:
            t = t * c + pltpu.roll(t, LANES - half, 1) * s1 + pltpu.roll(t, half, 1) * s2
        if blk < n_qk // LANES:
            t = t * (DA_HEAD_DIM ** -0.5 * math.log2(math.e))
        qkv_ref[:, blk * LANES:(blk + 1) * LANES] = t.astype(BF16)
    zrw_ref[...] = z[:, n_da:]


def _inproj(x2, g, w_in_bf, rc, rs1, rs2, tm):
    n, d = x2.shape
    n_in = w_in_bf.shape[1]
    n_qk = DA_HEADS * 2 * DA_HEAD_DIM
    n_da = 3 * n_qk
    row = lambda i: (i, 0)
    fix = lambda i: (0, 0)
    return pl.pallas_call(
        functools.partial(_inproj_kernel, n_qk=n_qk, n_da=n_da),
        grid=(n // tm,),
        in_specs=[pl.BlockSpec((tm, d), row), pl.BlockSpec((1, d), fix), pl.BlockSpec((d, n_in), fix),
                  pl.BlockSpec((tm, LANES), row), pl.BlockSpec((tm, LANES), row), pl.BlockSpec((tm, LANES), row)],
        out_specs=[pl.BlockSpec((tm, n_da), row), pl.BlockSpec((tm, n_in - n_da), row)],
        out_shape=[jax.ShapeDtypeStruct((n, n_da), BF16), jax.ShapeDtypeStruct((n, n_in - n_da), F32)],
        compiler_params=_params("parallel"),
    )(x2, g, w_in_bf, rc, rs1, rs2)


def _attn_kernel(lam_ref, q_ref, k_ref, v_ref, sg_ref, o_ref, m_ref, acc_ref, *, tq):
    i = pl.program_id(2)
    q = q_ref[...]
    lane = lax.broadcasted_iota(jnp.int32, q.shape, 1)
    zero = jnp.zeros_like(q)
    qs = (jnp.where(lane < DA_HEAD_DIM, q, zero), jnp.where(lane >= DA_HEAD_DIM, q, zero))
    m_ref[...] = jnp.full(m_ref.shape, -jnp.inf, F32)
    acc_ref[...] = jnp.zeros(acc_ref.shape, F32)
    ones = jnp.ones((tq, LANES), BF16)

    def block(j, masked):
        kj = k_ref[pl.ds(pl.multiple_of(j * tq, tq), tq), :]
        vj = jnp.concatenate([v_ref[pl.ds(pl.multiple_of(j * tq, tq), tq), :], ones], axis=1)
        for c in range(2):
            s = lax.dot_general(qs[c], kj, NT, preferred_element_type=F32)
            if masked:
                r_id = lax.broadcasted_iota(jnp.int32, s.shape, 0)
                c_id = lax.broadcasted_iota(jnp.int32, s.shape, 1)
                s = jnp.where(c_id <= r_id, s, -jnp.inf)
            m_old = m_ref[c]
            m_new = jnp.maximum(m_old, jnp.max(s, axis=-1, keepdims=True))
            alpha = jnp.exp2(m_old - m_new)
            p = jnp.exp2(s - jnp.tile(m_new, (1, tq // LANES)))
            pv = jnp.dot(p.astype(BF16), vj, preferred_element_type=F32)
            acc_ref[c] = jnp.tile(alpha, (1, 2)) * acc_ref[c] + pv
            m_ref[c] = m_new

    def body(j, carry):
        block(j, False)
        return carry

    lax.fori_loop(0, i, body, 0)
    block(i, True)
    lam = lam_ref[0, 0]
    a0, a1 = acc_ref[0], acc_ref[1]
    o = a0[:, :LANES] / a0[:, LANES:] - lam * (a1[:, :LANES] / a1[:, LANES:])
    o = o * lax.rsqrt(jnp.mean(o * o, axis=-1, keepdims=True) + NORM_EPS) * sg_ref[...] * (1.0 - LAM_INIT)
    o_ref[...] = o.astype(o_ref.dtype)


def _attention(qkv, lam, subln_g, batch, seq, tq):
    n = qkv.shape[0]
    nq = seq // tq
    h = DA_HEADS
    return pl.pallas_call(
        functools.partial(_attn_kernel, tq=tq),
        grid=(batch, h, nq),
        in_specs=[pl.BlockSpec(memory_space=pltpu.SMEM),
                  pl.BlockSpec((tq, LANES), lambda b, hh, i: (b * nq + i, hh)),
                  pl.BlockSpec((seq, LANES), lambda b, hh, i: (b, h + hh)),
                  pl.BlockSpec((seq, LANES), lambda b, hh, i: (b, 2 * h + hh)),
                  pl.BlockSpec((1, LANES), lambda b, hh, i: (0, 0))],
        out_specs=pl.BlockSpec((tq, LANES), lambda b, hh, i: (b * nq + i, hh)),
        out_shape=jax.ShapeDtypeStruct((n, h * LANES), BF16),
        scratch_shapes=[pltpu.VMEM((2, tq, LANES), F32), pltpu.VMEM((2, tq, 2 * LANES), F32)],
        compiler_params=_params("parallel", "parallel", "arbitrary"),
    )(lam, qkv, qkv, qkv, subln_g)


def _rwprep_kernel(z_ref, zp_ref, mu_ref, w0_ref, wup_ref, a0_ref, aup_ref, gup_ref, kk_ref, ka_ref, bd_ref,
                   r_o, ld_o, k_o, v_o, kk_o, b_o, g_o, *, tiles_per_seq, width):
    i = pl.program_id(0)
    z = z_ref[...]
    last = zp_ref[7:8, :]
    first = jnp.where(i % tiles_per_seq == 0, jnp.zeros_like(last), last)
    row = lax.broadcasted_iota(jnp.int32, z.shape, 0)
    prev = jnp.where(row == 0, first, pltpu.roll(z, 1, 0))
    zs = z + (prev - z) * mu_ref[...]
    r = zs[:, 0:width]
    k = zs[:, width:2 * width]
    v = zs[:, 2 * width:3 * width]
    xwa = zs[:, 3 * width:3 * width + LANES]
    xg = zs[:, 3 * width + LANES:3 * width + 2 * LANES]
    w = -jax.nn.softplus(-(w0_ref[...] + _mm_hi(jnp.tanh(xwa), wup_ref[...]))) - 0.5
    a = jax.nn.sigmoid(a0_ref[...] + _mm_hi(xwa, aup_ref[...]))
    g = _mm_hi(jax.nn.sigmoid(xg), gup_ref[...])
    kk = k * kk_ref[...]
    kk = kk / jnp.maximum(jnp.sqrt(_mm_hi(kk * kk, bd_ref[...])), 1e-12)
    r_o[...] = r
    ld_o[...] = -jnp.exp(w)
    k_o[...] = k * (1.0 + (a - 1.0) * ka_ref[...])
    v_o[...] = v
    kk_o[...] = kk
    b_o[...] = kk * a
    g_o[...] = g


def _rwprep(zrw, mu, w0, wup_pad, a0, aup_pad, gup, k_k, k_a, bd, seq, tm):
    n, zin = zrw.shape
    width = w0.shape[1]
    row = lambda i: (i, 0)
    fix = lambda i: (0, 0)
    prev = lambda i: (jnp.maximum(i * (tm // 8) - 1, 0), 0)
    out = jax.ShapeDtypeStruct((n, width), F32)
    return pl.pallas_call(
        functools.partial(_rwprep_kernel, tiles_per_seq=seq // tm, width=width),
        grid=(n // tm,),
        in_specs=[pl.BlockSpec((tm, zin), row), pl.BlockSpec((8, zin), prev), pl.BlockSpec((1, zin), fix),
                  pl.BlockSpec((1, width), fix), pl.BlockSpec((LANES, width), fix),
                  pl.BlockSpec((1, width), fix), pl.BlockSpec((LANES, width), fix), pl.BlockSpec((LANES, width), fix),
                  pl.BlockSpec((1, width), fix), pl.BlockSpec((1, width), fix), pl.BlockSpec((width, width), fix)],
        out_specs=[pl.BlockSpec((tm, width), row)] * 7,
        out_shape=[out] * 7,
        compiler_params=_params("parallel"),
    )(zrw, zrw, mu, w0, wup_pad, a0, aup_pad, gup, k_k, k_a, bd)


def _rwcore_kernel(r_ref, ld_ref, k_ref, v_ref, kk_ref, b_ref, g_ref, lng_ref, lnb_ref, rk_ref, o_ref, s_ref, *, units):
    U, C, HD = RW_UNIT, RW_CHUNK, RW_HEAD

    @pl.when(pl.program_id(2) == 0)
    def _():
        s_ref[...] = jnp.zeros(s_ref.shape, F32)

    ri = lax.broadcasted_iota(jnp.int32, (U, U), 0)
    ci = lax.broadcasted_iota(jnp.int32, (U, U), 1)
    same = (ri // C) == (ci // C)
    tri_s = same & (ci < ri)
    tri_i = same & (ci <= ri)
    eye = (ri == ci).astype(F32)
    cum_w = tri_i.astype(F32)
    head_avg = same.astype(F32) * (1.0 / HD)
    head_sum = same.astype(F32)
    lane = lax.broadcasted_iota(jnp.int32, (U, U), 1)
    rowi = lax.broadcasted_iota(jnp.int32, (U, U), 0)
    hmask = (lane < HD, lane >= HD)
    cmask = (rowi < C, rowi >= C)
    zero = jnp.zeros((U, U), F32)

    local = []
    for u in range(units):
        sl = pl.ds(u * U, U)
        r, ld, k, v, kk, b = r_ref[sl, :], ld_ref[sl, :], k_ref[sl, :], v_ref[sl, :], kk_ref[sl, :], b_ref[sl, :]
        cum = _mm_hi(cum_w, ld)
        gam = jnp.exp(cum)
        ginv = jnp.exp(-cum)
        at = -kk * jnp.exp(cum - ld)
        bt = b * ginv
        kt = k * ginv
        rt = r * gam
        bk = jnp.concatenate([bt, kt], axis=0)
        ap = zero
        uu = zero
        rp = zero
        yp = zero
        for h in range(2):
            ar = jnp.concatenate([jnp.where(hmask[h], at, zero), jnp.where(hmask[h], rt, zero)], axis=0)
            m = _mm_nt(ar, bk)
            mab = jnp.where(tri_s, m[0:U, 0:U], zero)
            mak = jnp.where(tri_s, m[0:U, U:2 * U], zero)
            mrb = jnp.where(tri_i, m[U:2 * U, 0:U], zero)
            mrk = jnp.where(tri_i, m[U:2 * U, U:2 * U], zero)
            t = eye + mab
            p = mab
            for _ in range(int(math.log2(C)) - 1):
                p = _mm(p, p)
                t = t + _mm(t, p)
            aph = _mm(t, at)
            uh = _mm(t, _mm(mak, v))
            rph = _mm(mrb, aph)
            yph = _mm(mrb, uh) + _mm(mrk, v)
            ap = jnp.where(hmask[h], aph, ap)
            uu = jnp.where(hmask[h], uh, uu)
            rp = jnp.where(hmask[h], rph, rp)
            yp = jnp.where(hmask[h], yph, yp)
        rp = rp + rt
        gs, hs = [], []
        for c in range(2):
            gl = gam[(c + 1) * C - 1:(c + 1) * C, :]
            apc = jnp.where(cmask[c], ap, zero)
            uvc = jnp.concatenate([jnp.where(cmask[c], uu, zero), jnp.where(cmask[c], v, zero)], axis=0)
            gs.append(jnp.where(same, eye + _mm_tn(apc, bt), zero) * gl)
            hs.append(jnp.where(same, _mm_tn(uvc, bk), zero) * gl)
        local.append((rp, yp, gs, hs))

    s = s_ref[...]
    for u in range(units):
        rp, yp, gs, hs = local[u]
        y0 = lax.dot_general(rp, s, NT, precision=HI, preferred_element_type=F32)
        s = _mm_hi(s, gs[0]) + hs[0]
        y1 = lax.dot_general(rp, s, NT, precision=HI, preferred_element_type=F32)
        s = _mm_hi(s, gs[1]) + hs[1]
        y = jnp.where(cmask[0], y0, y1) + yp
        sl = pl.ds(u * U, U)
        r, k, v = r_ref[sl, :], k_ref[sl, :], v_ref[sl, :]
        mean = _mm_hi(y, head_avg)
        yc = y - mean
        var = _mm_hi(yc * yc, head_avg)
        yn = yc * lax.rsqrt(var + RW_GN_EPS) * lng_ref[...] + lnb_ref[...]
        yn = yn + _mm_hi(r * k * rk_ref[...], head_sum) * v
        o_ref[sl, :] = (yn * g_ref[sl, :]).astype(o_ref.dtype)
    s_ref[...] = s


def _rwcore(r, ld, k, v, kk, b, g, ln_g, ln_b, r_k, batch, seq, units):
    n, width = r.shape
    rows = units * RW_UNIT
    steps = seq // rows
    blk = pl.BlockSpec((rows, LANES), lambda bb, hp, i: (bb * steps + i, hp))
    vec = pl.BlockSpec((1, LANES), lambda bb, hp, i: (0, hp))
    return pl.pallas_call(
        functools.partial(_rwcore_kernel, units=units),
        grid=(batch, width // LANES, steps),
        in_specs=[blk] * 7 + [vec] * 3,
        out_specs=blk,
        out_shape=jax.ShapeDtypeStruct((n, width), BF16),
        scratch_shapes=[pltpu.VMEM((RW_UNIT, RW_UNIT), F32)],
        compiler_params=_params("parallel", "parallel", "arbitrary"),
    )(r, ld, k, v, kk, b, g, ln_g, ln_b, r_k)


def _outproj_kernel(x_ref, oda_ref, orw_ref, wa_ref, wb_ref, g_ref, h_ref, u_ref):
    h = (x_ref[...] + jnp.dot(oda_ref[...], wa_ref[...], preferred_element_type=F32)
         + jnp.dot(orw_ref[...], wb_ref[...], preferred_element_type=F32))
    h_ref[...] = h
    u_ref[...] = _rms(h, g_ref[...]).astype(BF16)


def _outproj(x2, o_da, o_rw, wa, wb, g, tm):
    n, d = x2.shape
    da = o_da.shape[1]
    rw = o_rw.shape[1]
    row = lambda i: (i, 0)
    fix = lambda i: (0, 0)
    return pl.pallas_call(
        _outproj_kernel,
        grid=(n // tm,),
        in_specs=[pl.BlockSpec((tm, d), row), pl.BlockSpec((tm, da), row), pl.BlockSpec((tm, rw), row),
                  pl.BlockSpec((da, d), fix), pl.BlockSpec((rw, d), fix), pl.BlockSpec((1, d), fix)],
        out_specs=[pl.BlockSpec((tm, d), row), pl.BlockSpec((tm, d), row)],
        out_shape=[jax.ShapeDtypeStruct((n, d), F32), jax.ShapeDtypeStruct((n, d), BF16)],
        compiler_params=_params("parallel"),
    )(x2, o_da, o_rw, wa, wb, g)


def _topk_rows(s, k, payload=None):
    rows = s.shape[0]
    iota = lax.broadcasted_iota(jnp.int32, s.shape, 0).astype(F32)
    vals, sel = [], []
    for _ in range(k):
        m = jnp.max(s, axis=0, keepdims=True)
        am = jnp.min(jnp.where(s == m, iota, float(rows)), axis=0, keepdims=True)
        hit = iota == am
        vals.append(m)
        sel.append(am if payload is None else jnp.sum(jnp.where(hit, payload, 0.0), axis=0, keepdims=True))
        s = jnp.where(hit, -jnp.inf, s)
    return vals, sel


def _stack_rows(rows_list):
    k = len(rows_list)
    iota = lax.broadcasted_iota(jnp.int32, (k, rows_list[0].shape[1]), 0)
    out = jnp.zeros(iota.shape, rows_list[0].dtype)
    for j, r in enumerate(rows_list):
        out = jnp.where(iota == j, r, out)
    return out


def _peertopk_kernel(u_ref, wq_ref, keys_ref, idx_ref, gate_ref):
    u = u_ref[...]
    half = N_KEYS
    idx_rows, gate_rows = [], []
    for h in range(PEER_HEADS):
        tops = []
        for p in range(2):
            hp = h * 2 + p
            q_t = lax.dot_general(wq_ref[hp * half:(hp + 1) * half, :], u, NT, preferred_element_type=F32)
            s_t = jnp.dot(keys_ref[hp], q_t.astype(BF16), preferred_element_type=F32)
            tops.append(_topk_rows(s_t, PEER_TOPK))
        (v1, i1), (v2, i2) = tops
        v2s = _stack_rows(v2)
        i2s = _stack_rows(i2)
        cand = jnp.concatenate([v1[i] + v2s for i in range(PEER_TOPK)], axis=0)
        cidx = jnp.concatenate([i1[i] * float(N_KEYS) + i2s for i in range(PEER_TOPK)], axis=0)
        best, idx = _topk_rows(cand, PEER_TOPK, payload=cidx)
        e = [jnp.exp(b - best[0]) for b in best]
        den = functools.reduce(lambda a, b: a + b, e)
        idx_rows.append(_stack_rows(idx))
        gate_rows.append(_stack_rows([x / den for x in e]))
    idx_ref[...] = jnp.concatenate(idx_rows, axis=0).T.astype(jnp.int32)
    gate_ref[...] = jnp.concatenate(gate_rows, axis=0).T


def _peertopk(u_bf, wq_t, keys, tt):
    n, d = u_bf.shape
    hk = PEER_HEADS * PEER_TOPK
    row = lambda i: (i, 0)
    return pl.pallas_call(
        _peertopk_kernel,
        grid=(n // tt,),
        in_specs=[pl.BlockSpec((tt, d), row), pl.BlockSpec(wq_t.shape, lambda i: (0, 0)),
                  pl.BlockSpec(keys.shape, lambda i: (0, 0, 0))],
        out_specs=[pl.BlockSpec((tt, hk), row), pl.BlockSpec((tt, hk), row)],
        out_shape=[jax.ShapeDtypeStruct((n, hk), jnp.int32), jax.ShapeDtypeStruct((n, hk), F32)],
        compiler_params=_params("parallel"),
    )(u_bf, wq_t, keys)


SC_CORES = 2
SC_SUBCORES = 16
SC_LANES = 16
GATHER_ROWS = 32
GATHER_SLOTS = 6
GATHER_INDEX_WINDOW = 1024
HK = PEER_HEADS * PEER_TOPK
SUM_ROWS = 16
SUM_TOKENS = 8


def _sc_mesh():
    return plsc.VectorSubcoreMesh(core_axis_name="core", subcore_axis_name="subcore")


def _sc_worker():
    return lax.axis_index("core") * SC_SUBCORES + lax.axis_index("subcore")


def _gather_rows(tab, idx_flat):
    p = idx_flat.shape[0]
    w = tab.shape[1]
    workers = SC_CORES * SC_SUBCORES
    per_worker = p // workers
    iw = min(GATHER_INDEX_WINDOW, per_worker)
    r = GATHER_ROWS
    ns = GATHER_SLOTS
    lag = ns - 1
    nsub = iw // r
    assert p % workers == 0 and per_worker % iw == 0 and iw % r == 0 and nsub >= ns
    buf = pltpu.VMEM((r, w), tab.dtype)
    sem = pltpu.SemaphoreType.DMA

    @pl.kernel(out_type=jax.ShapeDtypeStruct((p, w), tab.dtype), mesh=_sc_mesh(),
               scratch_types=[pltpu.VMEM((iw,), jnp.int32)] + [buf] * ns + [sem] * (2 * ns))
    def gather(t_hbm, i_hbm, o_hbm, idx_v, *scratch):
        bufs, gsem, wsem = scratch[0:ns], scratch[ns:2 * ns], scratch[2 * ns:3 * ns]
        wid = _sc_worker()

        @pl.loop(0, per_worker // iw)
        def _(o):
            start = wid * per_worker + o * iw
            pltpu.sync_copy(i_hbm.at[pl.ds(start, iw)], idx_v)

            def gather_of(j):
                s = j % ns
                return pltpu.make_async_copy(t_hbm.at[idx_v.at[pl.ds(j * r, r)]], bufs[s], gsem[s])

            def write_of(j):
                s = j % ns
                return pltpu.make_async_copy(bufs[s], o_hbm.at[pl.ds(start + j * r, r)], wsem[s])

            for j in range(nsub + lag):
                if j < nsub:
                    if j >= ns:
                        write_of(j - ns).wait()
                    gather_of(j).start()
                if j >= lag:
                    gather_of(j - lag).wait()
                    write_of(j - lag).start()
            for j in range(nsub - ns, nsub):
                write_of(j).wait()

    return gather(tab, idx_flat)


def _weighted_row_sum(tab, idx_flat, wrep):
    p = idx_flat.shape[0]
    w = tab.shape[1]
    t_total = p // HK
    workers = SC_CORES * SC_SUBCORES
    tpw = t_total // workers
    g = min(SUM_TOKENS, tpw)
    r = SUM_ROWS
    ns = HK // r
    ln = SC_LANES
    per_row = LANES // ln
    assert t_total % workers == 0 and tpw % g == 0 and g % 2 == 0 and w % ln == 0
    buf = pltpu.VMEM((r, w), tab.dtype)
    acc = pltpu.VMEM((2 * w,), F32)
    sem = pltpu.SemaphoreType.DMA

    @pl.kernel(out_type=jax.ShapeDtypeStruct((t_total, 2 * w), F32), mesh=_sc_mesh(),
               scratch_types=[pltpu.VMEM((g * HK,), jnp.int32), pltpu.VMEM((g * HK // per_row, LANES), F32), acc, acc]
               + [buf] * ns + [sem] * (ns + 2),
               compiler_params=pltpu.CompilerParams(needs_layout_passes=False))
    def wsum(t_hbm, i_hbm, w_hbm, o_hbm, idx_v, w_v, acc0, acc1, *scratch):
        bufs, gsem, osem = scratch[0:ns], scratch[ns:2 * ns], scratch[2 * ns:2 * ns + 2]
        accs = (acc0, acc1)
        wid = _sc_worker()
        zero = jnp.zeros((ln,), F32)

        @pl.loop(0, tpw // g)
        def _(win):
            tok0 = wid * tpw + win * g
            pltpu.sync_copy(i_hbm.at[pl.ds(tok0 * HK, g * HK)], idx_v)
            pltpu.sync_copy(w_hbm.at[pl.ds(tok0 * (HK // per_row), g * HK // per_row)], w_v)

            def gather_of(tl, s):
                return pltpu.make_async_copy(t_hbm.at[idx_v.at[pl.ds(tl * HK + s * r, r)]], bufs[s], gsem[s])

            def out_of(tl, par):
                return pltpu.make_async_copy(accs[par], o_hbm.at[tok0 + tl], osem[par])

            for s in range(ns):
                gather_of(0, s).start()

            @pl.loop(0, g // 2)
            def _(tp):
                for par in range(2):
                    tl = tp * 2 + par
                    ob = accs[par]

                    @pl.when(tl >= 2)
                    def _():
                        out_of(tl - 2, par).wait()

                    for c in range(2 * w // ln):
                        ob[pl.ds(c * ln, ln)] = zero
                    for s in range(ns):
                        gather_of(tl, s).wait()
                        wrow = tl * (HK // per_row) + s * (r // per_row)
                        wk = [w_v[wrow + q // per_row, pl.ds((q % per_row) * ln, ln)] for q in range(r)]
                        rows = bufs[s]

                        def fold(c, carry):
                            col = pl.multiple_of(c * ln, ln)
                            lo_acc, hi_acc = zero, zero
                            for q in range(r):
                                word = rows[q, pl.ds(col, ln)]
                                lo = plsc.bitcast(lax.shift_left(word, jnp.uint32(16)), F32)
                                hi = plsc.bitcast(word & jnp.uint32(0xFFFF0000), F32)
                                lo_acc = lo_acc + wk[q] * lo
                                hi_acc = hi_acc + wk[q] * hi
                            plsc.addupdate(ob.at[pl.ds(col, ln)], lo_acc)
                            plsc.addupdate(ob.at[pl.ds(w + col, ln)], hi_acc)
                            return carry

                        lax.fori_loop(0, w // ln, fold, 0)

                        @pl.when(tl + 1 < g)
                        def _():
                            gather_of(tl + 1, s).start()

                    out_of(tl, par).start()

            for par in range(2):
                out_of(g - 2 + par, par).wait()

    return wsum(tab, idx_flat, wrep)


def _peerw_kernel(xa_ref, xb_ref, gate_ref, ug_ref, rep_ref, o_ref, *, tt):
    hk2 = 2 * HK
    x2 = jnp.concatenate([xa_ref[...], xb_ref[...]], axis=0)
    lane = lax.broadcasted_iota(jnp.int32, (tt, hk2), 1)
    row = lax.broadcasted_iota(jnp.int32, (tt, hk2), 0)
    even = (lane % 2) == 0
    part = jnp.zeros((tt, hk2), F32)
    for t in range(tt):
        ub = pltpu.bitcast(ug_ref[t * HK:(t + 1) * HK, :], BF16)
        r = lax.dot_general(x2, ub, NT, preferred_element_type=F32)
        part = jnp.where(row == t, jnp.where(even, r[0:tt], r[tt:2 * tt]), part)
    hid = part + jnp.where(even, pltpu.roll(part, hk2 - 1, 1), pltpu.roll(part, 1, 1))
    w = gate_ref[...] * (0.5 * hid * (1.0 + lax.erf(hid * (2.0 ** -0.5))))
    o_ref[...] = jnp.dot(w.astype(BF16), rep_ref[...], preferred_element_type=F32)


def _peerw(xa, xb, gate2, ug, rep, tt):
    n, dh = xa.shape
    row = lambda i: (i, 0)
    return pl.pallas_call(
        functools.partial(_peerw_kernel, tt=tt),
        grid=(n // tt,),
        in_specs=[pl.BlockSpec((tt, dh), row), pl.BlockSpec((tt, dh), row), pl.BlockSpec((tt, 2 * HK), row),
                  pl.BlockSpec((tt * HK, dh), row), pl.BlockSpec(rep.shape, lambda i: (0, 0))],
        out_specs=pl.BlockSpec((tt, HK * SC_LANES), row),
        out_shape=jax.ShapeDtypeStruct((n, HK * SC_LANES), F32),
        compiler_params=_params("parallel"),
    )(xa, xb, gate2, ug, rep)


def _repeat_matrix():
    src = jnp.arange(2 * HK)[:, None]
    dst = jnp.arange(HK * SC_LANES)[None, :]
    return (src == 2 * (dst // SC_LANES)).astype(BF16)


def _ple_kernel(h_ref, f_ref, p_ref, g_ref, wg_ref, wp_ref, gf_ref, o_ref):
    h = h_ref[...] + f_ref[...]
    gate = jax.nn.sigmoid(jnp.dot(_rms(h, g_ref[...]).astype(BF16), wg_ref[...], preferred_element_type=F32))
    pp = jnp.dot(p_ref[...].astype(BF16), wp_ref[...], preferred_element_type=F32)
    o_ref[...] = _rms(h + gate * pp, gf_ref[...])


def _ple(h1, ffn, p2, g, wg, wp, gf, tm):
    n, d = h1.shape
    pd = p2.shape[1]
    row = lambda i: (i, 0)
    fix = lambda i: (0, 0)
    return pl.pallas_call(
        _ple_kernel,
        grid=(n // tm,),
        in_specs=[pl.BlockSpec((tm, d), row), pl.BlockSpec((tm, d), row), pl.BlockSpec((tm, pd), row),
                  pl.BlockSpec((1, d), fix), pl.BlockSpec((d, d), fix), pl.BlockSpec((pd, d), fix),
                  pl.BlockSpec((1, d), fix)],
        out_specs=pl.BlockSpec((tm, d), row),
        out_shape=jax.ShapeDtypeStruct((n, d), F32),
        compiler_params=_params("parallel"),
    )(h1, ffn, p2, g, wg, wp, gf)


def _rope_tables(positions):
    half = ROT_DIM // 2
    inv_freq = ROPE_THETA ** (-jnp.arange(half, dtype=F32) * 2.0 / ROT_DIM)
    ang = positions.astype(F32).reshape(-1, 1) * inv_freq
    lane = jnp.arange(LANES)
    d = lane % DA_HEAD_DIM
    cos = jnp.take(jnp.cos(ang), d % half, axis=1)
    sin = jnp.take(jnp.sin(ang), d % half, axis=1)
    c = jnp.where(d < ROT_DIM, cos, 1.0)
    s1 = jnp.where(d < half, -sin, 0.0)
    s2 = jnp.where((d >= half) & (d < ROT_DIM), sin, 0.0)
    return c, s1, s2


def _pack_rows(tab):
    d = tab.shape[1]
    t = tab.astype(BF16)
    pair = jnp.stack([t[:, :d // 2], t[:, d // 2:]], axis=-1)
    return lax.bitcast_convert_type(pair, jnp.uint32)


def _block_diag_ones(width, head):
    i = jnp.arange(width)
    return (i[:, None] // head == i[None, :] // head).astype(F32)


def _tiles(seq):
    return dict(tm=min(256, seq), tq=min(512, seq), units=min(4, seq // RW_UNIT), tt_topk=min(256, seq), tt_mix=16,
                peer_chunks=2 if seq % 4096 == 0 else 1)


def kernel(x, p, positions, norm_mix_g, w_in, lam_q1, lam_k1, lam_q2, lam_k2, da_subln_g, rw_mu, rw_w0, rw_w_up, rw_a0, rw_a_up, rw_g_up, rw_k_k, rw_k_a, rw_r_k, rw_ln_g, rw_ln_b, w_out, norm_ffn_g, peer_w_q, peer_sub_keys, peer_u, peer_v, norm_ple_g, ple_gate_w, ple_proj_w, norm_final_g):
    batch, seq, d = x.shape
    t = _tiles(seq)
    row = lambda a: a.reshape(1, -1)
    f32 = F32

    w_in_bf = w_in[0].astype(BF16)
    lam = (jnp.exp(jnp.sum(lam_q1[0].astype(f32) * lam_k1[0].astype(f32)))
           - jnp.exp(jnp.sum(lam_q2[0].astype(f32) * lam_k2[0].astype(f32))) + LAM_INIT).reshape(1, 1)
    width = rw_w0.shape[1]
    wup_pad = jnp.concatenate([rw_w_up[0], jnp.zeros((LANES - rw_w_up.shape[1], width), f32)], axis=0)
    aup_pad = jnp.concatenate([jnp.zeros((LANES - rw_a_up.shape[1], width), f32), rw_a_up[0]], axis=0)
    head_ones = _block_diag_ones(width, RW_HEAD)
    w_out_bf = w_out[0].astype(BF16)
    da_w = DA_HEADS * 2 * DA_HEAD_DIM
    keys = peer_sub_keys[0].reshape(PEER_HEADS * 2, N_KEYS, -1).astype(BF16)
    wq_t = peer_w_q[0].T.astype(BF16)
    u_tab = _pack_rows(peer_u[0])
    v_tab = _pack_rows(peer_v[0])
    wg_bf = ple_gate_w[0].astype(BF16)
    wp_bf = ple_proj_w[0].astype(BF16)
    rep = _repeat_matrix()

    outs = []
    for b in range(batch):
        xs = x[b]
        rc, rs1, rs2 = _rope_tables(positions[b])
        qkv, zrw = _inproj(xs, row(norm_mix_g[0]), w_in_bf, rc, rs1, rs2, t["tm"])
        o_da = _attention(qkv, lam, row(da_subln_g[0]), 1, seq, t["tq"])
        rw = _rwprep(zrw, row(rw_mu[0]), row(rw_w0[0]), wup_pad, row(rw_a0[0]), aup_pad, rw_g_up[0],
                     row(rw_k_k[0]), row(rw_k_a[0]), head_ones, seq, t["tm"])
        o_rw = _rwcore(*rw, row(rw_ln_g[0]), row(rw_ln_b[0]), row(rw_r_k[0]), 1, seq, t["units"])
        h1, u2 = _outproj(xs, o_da, o_rw, w_out_bf[:da_w], w_out_bf[da_w:], row(norm_ffn_g[0]), t["tm"])

        idx, gate = _peertopk(u2, wq_t, keys, t["tt_topk"])
        gate2 = jnp.repeat(gate, 2, axis=1)
        xa, xb = u2[:, :d // 2], u2[:, d // 2:]
        nc = seq // t["peer_chunks"]
        ffn_parts = []
        for c in range(t["peer_chunks"]):
            sl = slice(c * nc, (c + 1) * nc)
            idx_c = idx[sl].reshape(-1)
            ug = _gather_rows(u_tab, idx_c)
            wrep = _peerw(xa[sl], xb[sl], gate2[sl], ug, rep, t["tt_mix"])
            ffn_parts.append(_weighted_row_sum(v_tab, idx_c, wrep.reshape(-1, LANES)))
        ffn = jnp.concatenate(ffn_parts, axis=0) if len(ffn_parts) > 1 else ffn_parts[0]
        outs.append(_ple(h1, ffn, p[0, b], row(norm_ple_g[0]), wg_bf, wp_bf, row(norm_final_g), t["tm"]))
    return jnp.stack(outs, axis=0)
```

```python
import functools
import math

import jax
import jax.numpy as jnp
from jax import lax
from jax.experimental import pallas as pl
from jax.experimental.pallas import tpu as pltpu
from jax.experimental.pallas import tpu_sc as plsc

F32 = jnp.float32
BF16 = jnp.bfloat16

NORM_EPS = 1e-6
DA_HEADS = 4
DA_HEAD_DIM = 64
ROPE_THETA = 500000.0
ROT_DIM = DA_HEAD_DIM // 4
RW_HEAD = 64
RW_GN_EPS = 64e-5
PEER_HEADS = 8
N_KEYS = 128
PEER_TOPK = 16
LAM_INIT = 0.8 - 0.6 * math.exp(-0.3 * 0)

LANES = 128
VMEM_LIMIT = 56 * 1024 * 1024
RW_CHUNK = 64
RW_UNIT = 2 * RW_CHUNK

NT = (((1,), (1,)), ((), ()))
TN = (((0,), (0,)), ((), ()))
HI = lax.Precision.HIGHEST


def _mm(a, b):
    return jnp.dot(a.astype(BF16), b.astype(BF16), preferred_element_type=F32)


def _mm_nt(a, b):
    return lax.dot_general(a.astype(BF16), b.astype(BF16), NT, preferred_element_type=F32)


def _mm_tn(a, b):
    return lax.dot_general(a.astype(BF16), b.astype(BF16), TN, preferred_element_type=F32)


def _mm_hi(a, b):
    return jnp.dot(a, b, precision=HI, preferred_element_type=F32)


def _params(*sem):
    return pltpu.CompilerParams(dimension_semantics=sem, vmem_limit_bytes=VMEM_LIMIT)


def _rms(x, g):
    return x * lax.rsqrt(jnp.mean(x * x, axis=-1, keepdims=True) + NORM_EPS) * g


def _inproj_kernel(x_ref, g_ref, w_ref, c_ref, s1_ref, s2_ref, qkv_ref, zrw_ref, *, n_qk, n_da):
    u = _rms(x_ref[...], g_ref[...]).astype(BF16)
    z = jnp.dot(u, w_ref[...], preferred_element_type=F32)
    c, s1, s2 = c_ref[...], s1_ref[...], s2_ref[...]
    half = ROT_DIM // 2
    for blk in range(n_da // LANES):
        t = z[:, blk * LANES:(blk + 1) * LANES]
        if blk < 2 * n_qk // LANES:
            t = t * c + pltpu.roll(t, LANES - half, 1) * s1 + pltpu.roll(t, half, 1) * s2
        if blk < n_qk // LANES:
            t = t * (DA_HEAD_DIM ** -0.5 * math.log2(math.e))
        qkv_ref[:, blk * LANES:(blk + 1) * LANES] = t.astype(BF16)
    zrw_ref[...] = z[:, n_da:]


def _inproj(x2, g, w_in_bf, rc, rs1, rs2, tm):
    n, d = x2.shape
    n_in = w_in_bf.shape[1]
    n_qk = DA_HEADS * 2 * DA_HEAD_DIM
    n_da = 3 * n_qk
    row = lambda i: (i, 0)
    fix = lambda i: (0, 0)
    return pl.pallas_call(
        functools.partial(_inproj_kernel, n_qk=n_qk, n_da=n_da),
        grid=(n // tm,),
        in_specs=[pl.BlockSpec((tm, d), row), pl.BlockSpec((1, d), fix), pl.BlockSpec((d, n_in), fix),
                  pl.BlockSpec((tm, LANES), row), pl.BlockSpec((tm, LANES), row), pl.BlockSpec((tm, LANES), row)],
        out_specs=[pl.BlockSpec((tm, n_da), row), pl.BlockSpec((tm, n_in - n_da), row)],
        out_shape=[jax.ShapeDtypeStruct((n, n_da), BF16), jax.ShapeDtypeStruct((n, n_in - n_da), F32)],
        compiler_params=_params("parallel"),
    )(x2, g, w_in_bf, rc, rs1, rs2)


def _attn_kernel(lam_ref, q_ref, k_ref, v_ref, sg_ref, o_ref, m_ref, acc_ref, *, tq):
    i = pl.program_id(2)
    q = q_ref[...]
    lane = lax.broadcasted_iota(jnp.int32, q.shape, 1)
    zero = jnp.zeros_like(q)
    qs = (jnp.where(lane < DA_HEAD_DIM, q, zero), jnp.where(lane >= DA_HEAD_DIM, q, zero))
    m_ref[...] = jnp.full(m_ref.shape, -jnp.inf, F32)
    acc_ref[...] = jnp.zeros(acc_ref.shape, F32)
    ones = jnp.ones((tq, LANES), BF16)

    def block(j, masked):
        kj = k_ref[pl.ds(pl.multiple_of(j * tq, tq), tq), :]
        vj = jnp.concatenate([v_ref[pl.ds(pl.multiple_of(j * tq, tq), tq), :], ones], axis=1)
        for c in range(2):
            s = lax.dot_general(qs[c], kj, NT, preferred_element_type=F32)
            if masked:
                r_id = lax.broadcasted_iota(jnp.int32, s.shape, 0)
                c_id = lax.broadcasted_iota(jnp.int32, s.shape, 1)
                s = jnp.where(c_id <= r_id, s, -jnp.inf)
            m_old = m_ref[c]
            m_new = jnp.maximum(m_old, jnp.max(s, axis=-1, keepdims=True))
            alpha = jnp.exp2(m_old - m_new)
            p = jnp.exp2(s - jnp.tile(m_new, (1, tq // LANES)))
            pv = jnp.dot(p.astype(BF16), vj, preferred_element_type=F32)
            acc_ref[c] = jnp.tile(alpha, (1, 2)) * acc_ref[c] + pv
            m_ref[c] = m_new

    def body(j, carry):
        block(j, False)
        return carry

    lax.fori_loop(0, i, body, 0)
    block(i, True)
    lam = lam_ref[0, 0]
    a0, a1 = acc_ref[0], acc_ref[1]
    o = a0[:, :LANES] / a0[:, LANES:] - lam * (a1[:, :LANES] / a1[:, LANES:])
    o = o * lax.rsqrt(jnp.mean(o * o, axis=-1, keepdims=True) + NORM_EPS) * sg_ref[...] * (1.0 - LAM_INIT)
    o_ref[...] = o.astype(o_ref.dtype)


def _attention(qkv, lam, subln_g, batch, seq, tq):
    n = qkv.shape[0]
    nq = seq // tq
    h = DA_HEADS
    return pl.pallas_call(
        functools.partial(_attn_kernel, tq=tq),
        grid=(batch, h, nq),
        in_specs=[pl.BlockSpec(memory_space=pltpu.SMEM),
                  pl.BlockSpec((tq, LANES), lambda b, hh, i: (b * nq + i, hh)),
                  pl.BlockSpec((seq, LANES), lambda b, hh, i: (b, h + hh)),
                  pl.BlockSpec((seq, LANES), lambda b, hh, i: (b, 2 * h + hh)),
                  pl.BlockSpec((1, LANES), lambda b, hh, i: (0, 0))],
        out_specs=pl.BlockSpec((tq, LANES), lambda b, hh, i: (b * nq + i, hh)),
        out_shape=jax.ShapeDtypeStruct((n, h * LANES), BF16),
        scratch_shapes=[pltpu.VMEM((2, tq, LANES), F32), pltpu.VMEM((2, tq, 2 * LANES), F32)],
        compiler_params=_params("parallel", "parallel", "arbitrary"),
    )(lam, qkv, qkv, qkv, subln_g)


def _rwprep_kernel(z_ref, zp_ref, mu_ref, w0_ref, wup_ref, a0_ref, aup_ref, gup_ref, kk_ref, ka_ref, bd_ref,
                   r_o, ld_o, k_o, v_o, kk_o, b_o, g_o, *, tiles_per_seq, width):
    i = pl.program_id(0)
    z = z_ref[...]
    last = zp_ref[7:8, :]
    first = jnp.where(i % tiles_per_seq == 0, jnp.zeros_like(last), last)
    row = lax.broadcasted_iota(jnp.int32, z.shape, 0)
    prev = jnp.where(row == 0, first, pltpu.roll(z, 1, 0))
    zs = z + (prev - z) * mu_ref[...]
    r = zs[:, 0:width]
    k = zs[:, width:2 * width]
    v = zs[:, 2 * width:3 * width]
    xwa = zs[:, 3 * width:3 * width + LANES]
    xg = zs[:, 3 * width + LANES:3 * width + 2 * LANES]
    w = -jax.nn.softplus(-(w0_ref[...] + _mm_hi(jnp.tanh(xwa), wup_ref[...]))) - 0.5
    a = jax.nn.sigmoid(a0_ref[...] + _mm_hi(xwa, aup_ref[...]))
    g = _mm_hi(jax.nn.sigmoid(xg), gup_ref[...])
    kk = k * kk_ref[...]
    kk = kk / jnp.maximum(jnp.sqrt(_mm_hi(kk * kk, bd_ref[...])), 1e-12)
    r_o[...] = r
    ld_o[...] = -jnp.exp(w)
    k_o[...] = k * (1.0 + (a - 1.0) * ka_ref[...])
    v_o[...] = v
    kk_o[...] = kk
    b_o[...] = kk * a
    g_o[...] = g


def _rwprep(zrw, mu, w0, wup_pad, a0, aup_pad, gup, k_k, k_a, bd, seq, tm):
    n, zin = zrw.shape
    width = w0.shape[1]
    row = lambda i: (i, 0)
    fix = lambda i: (0, 0)
    prev = lambda i: (jnp.maximum(i * (tm // 8) - 1, 0), 0)
    out = jax.ShapeDtypeStruct((n, width), F32)
    return pl.pallas_call(
        functools.partial(_rwprep_kernel, tiles_per_seq=seq // tm, width=width),
        grid=(n // tm,),
        in_specs=[pl.BlockSpec((tm, zin), row), pl.BlockSpec((8, zin), prev), pl.BlockSpec((1, zin), fix),
                  pl.BlockSpec((1, width), fix), pl.BlockSpec((LANES, width), fix),
                  pl.BlockSpec((1, width), fix), pl.BlockSpec((LANES, width), fix), pl.BlockSpec((LANES, width), fix),
                  pl.BlockSpec((1, width), fix), pl.BlockSpec((1, width), fix), pl.BlockSpec((width, width), fix)],
        out_specs=[pl.BlockSpec((tm, width), row)] * 7,
        out_shape=[out] * 7,
        compiler_params=_params("parallel"),
    )(zrw, zrw, mu, w0, wup_pad, a0, aup_pad, gup, k_k, k_a, bd)


def _rwcore_kernel(r_ref, ld_ref, k_ref, v_ref, kk_ref, b_ref, g_ref, lng_ref, lnb_ref, rk_ref, o_ref, s_ref, *, units):
    U, C, HD = RW_UNIT, RW_CHUNK, RW_HEAD

    @pl.when(pl.program_id(2) == 0)
    def _():
        s_ref[...] = jnp.zeros(s_ref.shape, F32)

    ri = lax.broadcasted_iota(jnp.int32, (U, U), 0)
    ci = lax.broadcasted_iota(jnp.int32, (U, U), 1)
    same = (ri // C) == (ci // C)
    tri_s = same & (ci < ri)
    tri_i = same & (ci <= ri)
    eye = (ri == ci).astype(F32)
    cum_w = tri_i.astype(F32)
    head_avg = same.astype(F32) * (1.0 / HD)
    head_sum = same.astype(F32)
    lane = lax.broadcasted_iota(jnp.int32, (U, U), 1)
    rowi = lax.broadcasted_iota(jnp.int32, (U, U), 0)
    hmask = (lane < HD, lane >= HD)
    cmask = (rowi < C, rowi >= C)
    zero = jnp.zeros((U, U), F32)

    local = []
    for u in range(units):
        sl = pl.ds(u * U, U)
        r, ld, k, v, kk, b = r_ref[sl, :], ld_ref[sl, :], k_ref[sl, :], v_ref[sl, :], kk_ref[sl, :], b_ref[sl, :]
        cum = _mm_hi(cum_w, ld)
        gam = jnp.exp(cum)
        ginv = jnp.exp(-cum)
        at = -kk * jnp.exp(cum - ld)
        bt = b * ginv
        kt = k * ginv
        rt = r * gam
        bk = jnp.concatenate([bt, kt], axis=0)
        ap = zero
        uu = zero
        rp = zero
        yp = zero
        for h in range(2):
            ar = jnp.concatenate([jnp.where(hmask[h], at, zero), jnp.where(hmask[h], rt, zero)], axis=0)
            m = _mm_nt(ar, bk)
            mab = jnp.where(tri_s, m[0:U, 0:U], zero)
            mak = jnp.where(tri_s, m[0:U, U:2 * U], zero)
            mrb = jnp.where(tri_i, m[U:2 * U, 0:U], zero)
            mrk = jnp.where(tri_i, m[U:2 * U, U:2 * U], zero)
            t = eye + mab
            p = mab
            for _ in range(int(math.log2(C)) - 1):
                p = _mm(p, p)
                t = t + _mm(t, p)
            aph = _mm(t, at)
            uh = _mm(t, _mm(mak, v))
            rph = _mm(mrb, aph)
            yph = _mm(mrb, uh) + _mm(mrk, v)
            ap = jnp.where(hmask[h], aph, ap)
            uu = jnp.where(hmask[h], uh, uu)
            rp = jnp.where(hmask[h], rph, rp)
            yp = jnp.where(hmask[h], yph, yp)
        rp = rp + rt
        gs, hs = [], []
        for c in range(2):
            gl = gam[(c + 1) * C - 1:(c + 1) * C, :]
            apc = jnp.where(cmask[c], ap, zero)
            uvc = jnp.concatenate([jnp.where(cmask[c], uu, zero), jnp.where(cmask[c], v, zero)], axis=0)
            gs.append(jnp.where(same, eye + _mm_tn(apc, bt), zero) * gl)
            hs.append(jnp.where(same, _mm_tn(uvc, bk), zero) * gl)
        local.append((rp, yp, gs, hs))

    s = s_ref[...]
    for u in range(units):
        rp, yp, gs, hs = local[u]
        y0 = lax.dot_general(rp, s, NT, precision=HI, preferred_element_type=F32)
        s = _mm_hi(s, gs[0]) + hs[0]
        y1 = lax.dot_general(rp, s, NT, precision=HI, preferred_element_type=F32)
        s = _mm_hi(s, gs[1]) + hs[1]
        y = jnp.where(cmask[0], y0, y1) + yp
        sl = pl.ds(u * U, U)
        r, k, v = r_ref[sl, :], k_ref[sl, :], v_ref[sl, :]
        mean = _mm_hi(y, head_avg)
        yc = y - mean
        var = _mm_hi(yc * yc, head_avg)
        yn = yc * lax.rsqrt(var + RW_GN_EPS) * lng_ref[...] + lnb_ref[...]
        yn = yn + _mm_hi(r * k * rk_ref[...], head_sum) * v
        o_ref[sl, :] = (yn * g_ref[sl, :]).astype(o_ref.dtype)
    s_ref[...] = s


def _rwcore(r, ld, k, v, kk, b, g, ln_g, ln_b, r_k, batch, seq, units):
    n, width = r.shape
    rows = units * RW_UNIT
    steps = seq // rows
    blk = pl.BlockSpec((rows, LANES), lambda bb, hp, i: (bb * steps + i, hp))
    vec = pl.BlockSpec((1, LANES), lambda bb, hp, i: (0, hp))
    return pl.pallas_call(
        functools.partial(_rwcore_kernel, units=units),
        grid=(batch, width // LANES, steps),
        in_specs=[blk] * 7 + [vec] * 3,
        out_specs=blk,
        out_shape=jax.ShapeDtypeStruct((n, width), BF16),
        scratch_shapes=[pltpu.VMEM((RW_UNIT, RW_UNIT), F32)],
        compiler_params=_params("parallel", "parallel", "arbitrary"),
    )(r, ld, k, v, kk, b, g, ln_g, ln_b, r_k)


def _outproj_kernel(x_ref, oda_ref, orw_ref, wa_ref, wb_ref, g_ref, h_ref, u_ref):
    h = (x_ref[...] + jnp.dot(oda_ref[...], wa_ref[...], preferred_element_type=F32)
         + jnp.dot(orw_ref[...], wb_ref[...], preferred_element_type=F32))
    h_ref[...] = h
    u_ref[...] = _rms(h, g_ref[...]).astype(BF16)


def _outproj(x2, o_da, o_rw, wa, wb, g, tm):
    n, d = x2.shape
    da = o_da.shape[1]
    rw = o_rw.shape[1]
    row = lambda i: (i, 0)
    fix = lambda i: (0, 0)
    return pl.pallas_call(
        _outproj_kernel,
        grid=(n // tm,),
        in_specs=[pl.BlockSpec((tm, d), row), pl.BlockSpec((tm, da), row), pl.BlockSpec((tm, rw), row),
                  pl.BlockSpec((da, d), fix), pl.BlockSpec((rw, d), fix), pl.BlockSpec((1, d), fix)],
        out_specs=[pl.BlockSpec((tm, d), row), pl.BlockSpec((tm, d), row)],
        out_shape=[jax.ShapeDtypeStruct((n, d), F32), jax.ShapeDtypeStruct((n, d), BF16)],
        compiler_params=_params("parallel"),
    )(x2, o_da, o_rw, wa, wb, g)


def _topk_rows(s, k, payload=None):
    rows = s.shape[0]
    iota = lax.broadcasted_iota(jnp.int32, s.shape, 0).astype(F32)
    vals, sel = [], []
    for _ in range(k):
        m = jnp.max(s, axis=0, keepdims=True)
        am = jnp.min(jnp.where(s == m, iota, float(rows)), axis=0, keepdims=True)
        hit = iota == am
        vals.append(m)
        sel.append(am if payload is None else jnp.sum(jnp.where(hit, payload, 0.0), axis=0, keepdims=True))
        s = jnp.where(hit, -jnp.inf, s)
    return vals, sel


def _stack_rows(rows_list):
    k = len(rows_list)
    iota = lax.broadcasted_iota(jnp.int32, (k, rows_list[0].shape[1]), 0)
    out = jnp.zeros(iota.shape, rows_list[0].dtype)
    for j, r in enumerate(rows_list):
        out = jnp.where(iota == j, r, out)
    return out


def _peertopk_kernel(u_ref, wq_ref, keys_ref, idx_ref, gate_ref):
    u = u_ref[...]
    half = N_KEYS
    idx_rows, gate_rows = [], []
    for h in range(PEER_HEADS):
        tops = []
        for p in range(2):
            hp = h * 2 + p
            q_t = lax.dot_general(wq_ref[hp * half:(hp + 1) * half, :], u, NT, preferred_element_type=F32)
            s_t = jnp.dot(keys_ref[hp], q_t.astype(BF16), preferred_element_type=F32)
            tops.append(_topk_rows(s_t, PEER_TOPK))
        (v1, i1), (v2, i2) = tops
        v2s = _stack_rows(v2)
        i2s = _stack_rows(i2)
        cand = jnp.concatenate([v1[i] + v2s for i in range(PEER_TOPK)], axis=0)
        cidx = jnp.concatenate([i1[i] * float(N_KEYS) + i2s for i in range(PEER_TOPK)], axis=0)
        best, idx = _topk_rows(cand, PEER_TOPK, payload=cidx)
        e = [jnp.exp(b - best[0]) for b in best]
        den = functools.reduce(lambda a, b: a + b, e)
        idx_rows.append(_stack_rows(idx))
        gate_rows.append(_stack_rows([x / den for x in e]))
    idx_ref[...] = jnp.concatenate(idx_rows, axis=0).T.astype(jnp.int32)
    gate_ref[...] = jnp.concatenate(gate_rows, axis=0).T


def _peertopk(u_bf, wq_t, keys, tt):
    n, d = u_bf.shape
    hk = PEER_HEADS * PEER_TOPK
    row = lambda i: (i, 0)
    return pl.pallas_call(
        _peertopk_kernel,
        grid=(n // tt,),
        in_specs=[pl.BlockSpec((tt, d), row), pl.BlockSpec(wq_t.shape, lambda i: (0, 0)),
                  pl.BlockSpec(keys.shape, lambda i: (0, 0, 0))],
        out_specs=[pl.BlockSpec((tt, hk), row), pl.BlockSpec((tt, hk), row)],
        out_shape=[jax.ShapeDtypeStruct((n, hk), jnp.int32), jax.ShapeDtypeStruct((n, hk), F32)],
        compiler_params=_params("parallel"),
    )(u_bf, wq_t, keys)


SC_CORES = 2
SC_SUBCORES = 16
SC_LANES = 16
GATHER_ROWS = 32
GATHER_SLOTS = 6
GATHER_INDEX_WINDOW = 1024
HK = PEER_HEADS * PEER_TOPK
SUM_ROWS = 16
SUM_TOKENS = 8


def _sc_mesh():
    return plsc.VectorSubcoreMesh(core_axis_name="core", subcore_axis_name="subcore")


def _sc_worker():
    return lax.axis_index("core") * SC_SUBCORES + lax.axis_index("subcore")


def _gather_rows(tab, idx_flat):
    p = idx_flat.shape[0]
    w = tab.shape[1]
    workers = SC_CORES * SC_SUBCORES
    per_worker = p // workers
    iw = min(GATHER_INDEX_WINDOW, per_worker)
    r = GATHER_ROWS
    ns = GATHER_SLOTS
    lag = ns - 1
    nsub = iw // r
    assert p % workers == 0 and per_worker % iw == 0 and iw % r == 0 and nsub >= ns
    buf = pltpu.VMEM((r, w), tab.dtype)
    sem = pltpu.SemaphoreType.DMA

    @pl.kernel(out_type=jax.ShapeDtypeStruct((p, w), tab.dtype), mesh=_sc_mesh(),
               scratch_types=[pltpu.VMEM((iw,), jnp.int32)] + [buf] * ns + [sem] * (2 * ns))
    def gather(t_hbm, i_hbm, o_hbm, idx_v, *scratch):
        bufs, gsem, wsem = scratch[0:ns], scratch[ns:2 * ns], scratch[2 * ns:3 * ns]
        wid = _sc_worker()

        @pl.loop(0, per_worker // iw)
        def _(o):
            start = wid * per_worker + o * iw
            pltpu.sync_copy(i_hbm.at[pl.ds(start, iw)], idx_v)

            def gather_of(j):
                s = j % ns
                return pltpu.make_async_copy(t_hbm.at[idx_v.at[pl.ds(j * r, r)]], bufs[s], gsem[s])

            def write_of(j):
                s = j % ns
                return pltpu.make_async_copy(bufs[s], o_hbm.at[pl.ds(start + j * r, r)], wsem[s])

            for j in range(nsub + lag):
                if j < nsub:
                    if j >= ns:
                        write_of(j - ns).wait()
                    gather_of(j).start()
                if j >= lag:
                    gather_of(j - lag).wait()
                    write_of(j - lag).start()
            for j in range(nsub - ns, nsub):
                write_of(j).wait()

    return gather(tab, idx_flat)


def _weighted_row_sum(tab, idx_flat, wrep):
    p = idx_flat.shape[0]
    w = tab.shape[1]
    t_total = p // HK
    workers = SC_CORES * SC_SUBCORES
    tpw = t_total // workers
    g = min(SUM_TOKENS, tpw)
    r = SUM_ROWS
    ns = HK // r
    ln = SC_LANES
    per_row = LANES // ln
    assert t_total % workers == 0 and tpw % g == 0 and g % 2 == 0 and w % ln == 0
    buf = pltpu.VMEM((r, w), tab.dtype)
    acc = pltpu.VMEM((2 * w,), F32)
    sem = pltpu.SemaphoreType.DMA

    @pl.kernel(out_type=jax.ShapeDtypeStruct((t_total, 2 * w), F32), mesh=_sc_mesh(),
               scratch_types=[pltpu.VMEM((g * HK,), jnp.int32), pltpu.VMEM((g * HK // per_row, LANES), F32), acc, acc]
               + [buf] * ns + [sem] * (ns + 2),
               compiler_params=pltpu.CompilerParams(needs_layout_passes=False))
    def wsum(t_hbm, i_hbm, w_hbm, o_hbm, idx_v, w_v, acc0, acc1, *scratch):
        bufs, gsem, osem = scratch[0:ns], scratch[ns:2 * ns], scratch[2 * ns:2 * ns + 2]
        accs = (acc0, acc1)
        wid = _sc_worker()
        zero = jnp.zeros((ln,), F32)

        @pl.loop(0, tpw // g)
        def _(win):
            tok0 = wid * tpw + win * g
            pltpu.sync_copy(i_hbm.at[pl.ds(tok0 * HK, g * HK)], idx_v)
            pltpu.sync_copy(w_hbm.at[pl.ds(tok0 * (HK // per_row), g * HK // per_row)], w_v)

            def gather_of(tl, s):
                return pltpu.make_async_copy(t_hbm.at[idx_v.at[pl.ds(tl * HK + s * r, r)]], bufs[s], gsem[s])

            def out_of(tl, par):
                return pltpu.make_async_copy(accs[par], o_hbm.at[tok0 + tl], osem[par])

            for s in range(ns):
                gather_of(0, s).start()

            @pl.loop(0, g // 2)
            def _(tp):
                for par in range(2):
                    tl = tp * 2 + par
                    ob = accs[par]

                    @pl.when(tl >= 2)
                    def _():
                        out_of(tl - 2, par).wait()

                    for c in range(2 * w // ln):
                        ob[pl.ds(c * ln, ln)] = zero
                    for s in range(ns):
                        gather_of(tl, s).wait()
                        wrow = tl * (HK // per_row) + s * (r // per_row)
                        wk = [w_v[wrow + q // per_row, pl.ds((q % per_row) * ln, ln)] for q in range(r)]
                        rows = bufs[s]

                        def fold(c, carry):
                            col = pl.multiple_of(c * ln, ln)
                            lo_acc, hi_acc = zero, zero
                            for q in range(r):
                                word = rows[q, pl.ds(col, ln)]
                                lo = plsc.bitcast(lax.shift_left(word, jnp.uint32(16)), F32)
                                hi = plsc.bitcast(word & jnp.uint32(0xFFFF0000), F32)
                                lo_acc = lo_acc + wk[q] * lo
                                hi_acc = hi_acc + wk[q] * hi
                            plsc.addupdate(ob.at[pl.ds(col, ln)], lo_acc)
                            plsc.addupdate(ob.at[pl.ds(w + col, ln)], hi_acc)
                            return carry

                        lax.fori_loop(0, w // ln, fold, 0)

                        @pl.when(tl + 1 < g)
                        def _():
                            gather_of(tl + 1, s).start()

                    out_of(tl, par).start()

            for par in range(2):
                out_of(g - 2 + par, par).wait()

    return wsum(tab, idx_flat, wrep)


def _peerw_kernel(xa_ref, xb_ref, gate_ref, ug_ref, rep_ref, o_ref, *, tt):
    hk2 = 2 * HK
    x2 = jnp.concatenate([xa_ref[...], xb_ref[...]], axis=0)
    lane = lax.broadcasted_iota(jnp.int32, (tt, hk2), 1)
    row = lax.broadcasted_iota(jnp.int32, (tt, hk2), 0)
    even = (lane % 2) == 0
    part = jnp.zeros((tt, hk2), F32)
    for t in range(tt):
        ub = pltpu.bitcast(ug_ref[t * HK:(t + 1) * HK, :], BF16)
        r = lax.dot_general(x2, ub, NT, preferred_element_type=F32)
        part = jnp.where(row == t, jnp.where(even, r[0:tt], r[tt:2 * tt]), part)
    hid = part + jnp.where(even, pltpu.roll(part, hk2 - 1, 1), pltpu.roll(part, 1, 1))
    w = gate_ref[...] * (0.5 * hid * (1.0 + lax.erf(hid * (2.0 ** -0.5))))
    o_ref[...] = jnp.dot(w.astype(BF16), rep_ref[...], preferred_element_type=F32)


def _peerw(xa, xb, gate2, ug, rep, tt):
    n, dh = xa.shape
    row = lambda i: (i, 0)
    return pl.pallas_call(
        functools.partial(_peerw_kernel, tt=tt),
        grid=(n // tt,),
        in_specs=[pl.BlockSpec((tt, dh), row), pl.BlockSpec((tt, dh), row), pl.BlockSpec((tt, 2 * HK), row),
                  pl.BlockSpec((tt * HK, dh), row), pl.BlockSpec(rep.shape, lambda i: (0, 0))],
        out_specs=pl.BlockSpec((tt, HK * SC_LANES), row),
        out_shape=jax.ShapeDtypeStruct((n, HK * SC_LANES), F32),
        compiler_params=_params("parallel"),
    )(xa, xb, gate2, ug, rep)


def _repeat_matrix():
    src = jnp.arange(2 * HK)[:, None]
    dst = jnp.arange(HK * SC_LANES)[None, :]
    return (src == 2 * (dst // SC_LANES)).astype(BF16)


def _ple_kernel(h_ref, f_ref, p_ref, g_ref, wg_ref, wp_ref, gf_ref, o_ref):
    h = h_ref[...] + f_ref[...]
    gate = jax.nn.sigmoid(jnp.dot(_rms(h, g_ref[...]).astype(BF16), wg_ref[...], preferred_element_type=F32))
    pp = jnp.dot(p_ref[...].astype(BF16), wp_ref[...], preferred_element_type=F32)
    o_ref[...] = _rms(h + gate * pp, gf_ref[...])


def _ple(h1, ffn, p2, g, wg, wp, gf, tm):
    n, d = h1.shape
    pd = p2.shape[1]
    row = lambda i: (i, 0)
    fix = lambda i: (0, 0)
    return pl.pallas_call(
        _ple_kernel,
        grid=(n // tm,),
        in_specs=[pl.BlockSpec((tm, d), row), pl.BlockSpec((tm, d), row), pl.BlockSpec((tm, pd), row),
                  pl.BlockSpec((1, d), fix), pl.BlockSpec((d, d), fix), pl.BlockSpec((pd, d), fix),
                  pl.BlockSpec((1, d), fix)],
        out_specs=pl.BlockSpec((tm, d), row),
        out_shape=jax.ShapeDtypeStruct((n, d), F32),
        compiler_params=_params("parallel"),
    )(h1, ffn, p2, g, wg, wp, gf)


def _rope_tables(positions):
    half = ROT_DIM // 2
    inv_freq = ROPE_THETA ** (-jnp.arange(half, dtype=F32) * 2.0 / ROT_DIM)
    ang = positions.astype(F32).reshape(-1, 1) * inv_freq
    lane = jnp.arange(LANES)
    d = lane % DA_HEAD_DIM
    cos = jnp.take(jnp.cos(ang), d % half, axis=1)
    sin = jnp.take(jnp.sin(ang), d % half, axis=1)
    c = jnp.where(d < ROT_DIM, cos, 1.0)
    s1 = jnp.where(d < half, -sin, 0.0)
    s2 = jnp.where((d >= half) & (d < ROT_DIM), sin, 0.0)
    return c, s1, s2


def _pack_rows(tab):
    d = tab.shape[1]
    t = tab.astype(BF16)
    pair = jnp.stack([t[:, :d // 2], t[:, d // 2:]], axis=-1)
    return lax.bitcast_convert_type(pair, jnp.uint32)


def _block_diag_ones(width, head):
    i = jnp.arange(width)
    return (i[:, None] // head == i[None, :] // head).astype(F32)


def _tiles(seq):
    return dict(tm=min(256, seq), tq=min(512, seq), units=min(4, seq // RW_UNIT), tt_topk=min(256, seq), tt_mix=16,
                peer_chunks=2 if seq % 4096 == 0 else 1)


def kernel(x, p, positions, norm_mix_g, w_in, lam_q1, lam_k1, lam_q2, lam_k2, da_subln_g, rw_mu, rw_w0, rw_w_up, rw_a0, rw_a_up, rw_g_up, rw_k_k, rw_k_a, rw_r_k, rw_ln_g, rw_ln_b, w_out, norm_ffn_g, peer_w_q, peer_sub_keys, peer_u, peer_v, norm_ple_g, ple_gate_w, ple_proj_w, norm_final_g):
    batch, seq, d = x.shape
    t = _tiles(seq)
    row = lambda a: a.reshape(1, -1)
    f32 = F32

    w_in_bf = w_in[0].astype(BF16)
    lam = (jnp.exp(jnp.sum(lam_q1[0].astype(f32) * lam_k1[0].astype(f32)))
           - jnp.exp(jnp.sum(lam_q2[0].astype(f32) * lam_k2[0].astype(f32))) + LAM_INIT).reshape(1, 1)
    width = rw_w0.shape[1]
    wup_pad = jnp.concatenate([rw_w_up[0], jnp.zeros((LANES - rw_w_up.shape[1], width), f32)], axis=0)
    aup_pad = jnp.concatenate([jnp.zeros((LANES - rw_a_up.shape[1], width), f32), rw_a_up[0]], axis=0)
    head_ones = _block_diag_ones(width, RW_HEAD)
    w_out_bf = w_out[0].astype(BF16)
    da_w = DA_HEADS * 2 * DA_HEAD_DIM
    keys = peer_sub_keys[0].reshape(PEER_HEADS * 2, N_KEYS, -1).astype(BF16)
    wq_t = peer_w_q[0].T.astype(BF16)
    u_tab = _pack_rows(peer_u[0])
    v_tab = _pack_rows(peer_v[0])
    wg_bf = ple_gate_w[0].astype(BF16)
    wp_bf = ple_proj_w[0].astype(BF16)
    rep = _repeat_matrix()

    nc = seq // t["peer_chunks"]
    chunks = [slice(c * nc, (c + 1) * nc) for c in range(t["peer_chunks"])]

    def mixers(xs, pos):
        rc, rs1, rs2 = _rope_tables(pos)
        qkv, zrw = _inproj(xs, row(norm_mix_g[0]), w_in_bf, rc, rs1, rs2, t["tm"])
        o_da = _attention(qkv, lam, row(da_subln_g[0]), 1, seq, t["tq"])
        rw = _rwprep(zrw, row(rw_mu[0]), row(rw_w0[0]), wup_pad, row(rw_a0[0]), aup_pad, rw_g_up[0],
                     row(rw_k_k[0]), row(rw_k_a[0]), head_ones, seq, t["tm"])
        o_rw = _rwcore(*rw, row(rw_ln_g[0]), row(rw_ln_b[0]), row(rw_r_k[0]), 1, seq, t["units"])
        h1, u2 = _outproj(xs, o_da, o_rw, w_out_bf[:da_w], w_out_bf[da_w:], row(norm_ffn_g[0]), t["tm"])
        idx, gate = _peertopk(u2, wq_t, keys, t["tt_topk"])
        idx_c = [idx[sl].reshape(-1) for sl in chunks]
        ug = [_gather_rows(u_tab, i) for i in idx_c]
        return dict(h1=h1, u2=u2, gate2=jnp.repeat(gate, 2, axis=1), idx_c=idx_c, ug=ug)

    def weights(s):
        xa, xb = s["u2"][:, :d // 2], s["u2"][:, d // 2:]
        return [_peerw(xa[sl], xb[sl], s["gate2"][sl], ug, rep, t["tt_mix"]) for sl, ug in zip(chunks, s["ug"])]

    def finish(s, wrep, pb):
        ffn = [_weighted_row_sum(v_tab, i, w.reshape(-1, LANES)) for i, w in zip(s["idx_c"], wrep)]
        ffn = jnp.concatenate(ffn, axis=0) if len(ffn) > 1 else ffn[0]
        return _ple(s["h1"], ffn, pb, row(norm_ple_g[0]), wg_bf, wp_bf, row(norm_final_g), t["tm"])

    outs = []
    prev = None
    xs = x[0]
    for b in range(batch):
        cur = mixers(xs, positions[b])
        if prev is not None:
            prev["ug"], cur["idx_c"] = lax.optimization_barrier((prev["ug"], cur["idx_c"]))
            wrep = weights(prev)
            if b + 1 < batch:
                wrep, xs = lax.optimization_barrier((wrep, x[b + 1]))
            outs.append(finish(prev, wrep, p[0, b - 1]))
        elif b + 1 < batch:
            xs = x[b + 1]
        prev = cur
    outs.append(finish(prev, weights(prev), p[0, batch - 1]))
    return jnp.stack(outs, axis=0)
```

```python
import functools
import math

import jax
import jax.numpy as jnp
from jax import lax
from jax.experimental import pallas as pl
from jax.experimental.pallas import tpu as pltpu
from jax.experimental.pallas import tpu_sc as plsc

F32 = jnp.float32
BF16 = jnp.bfloat16

NORM_EPS = 1e-6
DA_HEADS = 4
DA_HEAD_DIM = 64
ROPE_THETA = 500000.0
ROT_DIM = DA_HEAD_DIM // 4
RW_HEAD = 64
RW_GN_EPS = 64e-5
PEER_HEADS = 8
N_KEYS = 128
PEER_TOPK = 16
LAM_INIT = 0.8 - 0.6 * math.exp(-0.3 * 0)

LANES = 128
VMEM_LIMIT = 56 * 1024 * 1024
RW_CHUNK = 64
RW_UNIT = 2 * RW_CHUNK

NT = (((1,), (1,)), ((), ()))
TN = (((0,), (0,)), ((), ()))
HI = lax.Precision.HIGHEST


def _mm(a, b):
    return jnp.dot(a.astype(BF16), b.astype(BF16), preferred_element_type=F32)


def _mm_nt(a, b):
    return lax.dot_general(a.astype(BF16), b.astype(BF16), NT, preferred_element_type=F32)


def _mm_tn(a, b):
    return lax.dot_general(a.astype(BF16), b.astype(BF16), TN, preferred_element_type=F32)


def _mm_hi(a, b):
    return jnp.dot(a, b, precision=HI, preferred_element_type=F32)


def _pieces(a, n):
    out = []
    for _ in range(n):
        piece = a.astype(BF16)
        out.append(piece)
        a = a - piece.astype(F32)
    return out


def _mm_split(a, b, dims, a_pieces, b_pieces):
    ap, bp = _pieces(a, a_pieces), _pieces(b, b_pieces)
    terms = [lax.dot_general(x, y, dims, preferred_element_type=F32)
             for i, x in enumerate(ap) for j, y in enumerate(bp) if i + j < max(a_pieces, b_pieces)]
    return functools.reduce(lambda u, v: u + v, terms)


NN = (((1,), (0,)), ((), ()))


def _params(*sem):
    return pltpu.CompilerParams(dimension_semantics=sem, vmem_limit_bytes=VMEM_LIMIT)


def _rms(x, g):
    return x * lax.rsqrt(jnp.mean(x * x, axis=-1, keepdims=True) + NORM_EPS) * g


def _inproj_kernel(x_ref, g_ref, w_ref, c_ref, s1_ref, s2_ref, qkv_ref, zrw_ref, *, n_qk, n_da):
    u = _rms(x_ref[...], g_ref[...]).astype(BF16)
    z = jnp.dot(u, w_ref[...], preferred_element_type=F32)
    c, s1, s2 = c_ref[...], s1_ref[...], s2_ref[...]
    half = ROT_DIM // 2
    for blk in range(n_da // LANES):
        t = z[:, blk * LANES:(blk + 1) * LANES]
        if blk < 2 * n_qk // LANES:
            t = t * c + pltpu.roll(t, LANES - half, 1) * s1 + pltpu.roll(t, half, 1) * s2
        if blk < n_qk // LANES:
            t = t * (DA_HEAD_DIM ** -0.5 * math.log2(math.e))
        qkv_ref[:, blk * LANES:(blk + 1) * LANES] = t.astype(BF16)
    zrw_ref[...] = z[:, n_da:]


def _inproj(x2, g, w_in_bf, rc, rs1, rs2, tm):
    n, d = x2.shape
    n_in = w_in_bf.shape[1]
    n_qk = DA_HEADS * 2 * DA_HEAD_DIM
    n_da = 3 * n_qk
    row = lambda i: (i, 0)
    fix = lambda i: (0, 0)
    return pl.pallas_call(
        functools.partial(_inproj_kernel, n_qk=n_qk, n_da=n_da),
        grid=(n // tm,),
        in_specs=[pl.BlockSpec((tm, d), row), pl.BlockSpec((1, d), fix), pl.BlockSpec((d, n_in), fix),
                  pl.BlockSpec((tm, LANES), row), pl.BlockSpec((tm, LANES), row), pl.BlockSpec((tm, LANES), row)],
        out_specs=[pl.BlockSpec((tm, n_da), row), pl.BlockSpec((tm, n_in - n_da), row)],
        out_shape=[jax.ShapeDtypeStruct((n, n_da), BF16), jax.ShapeDtypeStruct((n, n_in - n_da), F32)],
        compiler_params=_params("parallel"),
    )(x2, g, w_in_bf, rc, rs1, rs2)


def _attn_kernel(lam_ref, q_ref, k_ref, v_ref, sg_ref, o_ref, m_ref, acc_ref, *, tq):
    i = pl.program_id(2)
    q = q_ref[...]
    lane = lax.broadcasted_iota(jnp.int32, q.shape, 1)
    zero = jnp.zeros_like(q)
    qs = (jnp.where(lane < DA_HEAD_DIM, q, zero), jnp.where(lane >= DA_HEAD_DIM, q, zero))
    m_ref[...] = jnp.full(m_ref.shape, -jnp.inf, F32)
    acc_ref[...] = jnp.zeros(acc_ref.shape, F32)
    ones = jnp.ones((tq, LANES), BF16)

    def block(j, masked):
        kj = k_ref[pl.ds(pl.multiple_of(j * tq, tq), tq), :]
        vj = jnp.concatenate([v_ref[pl.ds(pl.multiple_of(j * tq, tq), tq), :], ones], axis=1)
        for c in range(2):
            s = lax.dot_general(qs[c], kj, NT, preferred_element_type=F32)
            if masked:
                r_id = lax.broadcasted_iota(jnp.int32, s.shape, 0)
                c_id = lax.broadcasted_iota(jnp.int32, s.shape, 1)
                s = jnp.where(c_id <= r_id, s, -jnp.inf)
            m_old = m_ref[c]
            m_new = jnp.maximum(m_old, jnp.max(s, axis=-1, keepdims=True))
            alpha = jnp.exp2(m_old - m_new)
            p = jnp.exp2(s - jnp.tile(m_new, (1, tq // LANES)))
            pv = jnp.dot(p.astype(BF16), vj, preferred_element_type=F32)
            acc_ref[c] = jnp.tile(alpha, (1, 2)) * acc_ref[c] + pv
            m_ref[c] = m_new

    def body(j, carry):
        block(j, False)
        return carry

    lax.fori_loop(0, i, body, 0)
    block(i, True)
    lam = lam_ref[0, 0]
    a0, a1 = acc_ref[0], acc_ref[1]
    o = a0[:, :LANES] / a0[:, LANES:] - lam * (a1[:, :LANES] / a1[:, LANES:])
    o = o * lax.rsqrt(jnp.mean(o * o, axis=-1, keepdims=True) + NORM_EPS) * sg_ref[...] * (1.0 - LAM_INIT)
    o_ref[...] = o.astype(o_ref.dtype)


def _attention(qkv, lam, subln_g, batch, seq, tq):
    n = qkv.shape[0]
    nq = seq // tq
    h = DA_HEADS
    return pl.pallas_call(
        functools.partial(_attn_kernel, tq=tq),
        grid=(batch, h, nq),
        in_specs=[pl.BlockSpec(memory_space=pltpu.SMEM),
                  pl.BlockSpec((tq, LANES), lambda b, hh, i: (b * nq + i, hh)),
                  pl.BlockSpec((seq, LANES), lambda b, hh, i: (b, h + hh)),
                  pl.BlockSpec((seq, LANES), lambda b, hh, i: (b, 2 * h + hh)),
                  pl.BlockSpec((1, LANES), lambda b, hh, i: (0, 0))],
        out_specs=pl.BlockSpec((tq, LANES), lambda b, hh, i: (b * nq + i, hh)),
        out_shape=jax.ShapeDtypeStruct((n, h * LANES), BF16),
        scratch_shapes=[pltpu.VMEM((2, tq, LANES), F32), pltpu.VMEM((2, tq, 2 * LANES), F32)],
        compiler_params=_params("parallel", "parallel", "arbitrary"),
    )(lam, qkv, qkv, qkv, subln_g)


def _rwprep_kernel(z_ref, zp_ref, mu_ref, w0_ref, wup_ref, a0_ref, aup_ref, gup_ref, kk_ref, ka_ref, bd_ref,
                   r_o, ld_o, k_o, v_o, kk_o, b_o, g_o, *, tiles_per_seq, width):
    i = pl.program_id(0)
    z = z_ref[...]
    last = zp_ref[7:8, :]
    first = jnp.where(i % tiles_per_seq == 0, jnp.zeros_like(last), last)
    row = lax.broadcasted_iota(jnp.int32, z.shape, 0)
    prev = jnp.where(row == 0, first, pltpu.roll(z, 1, 0))
    zs = z + (prev - z) * mu_ref[...]
    r = zs[:, 0:width]
    k = zs[:, width:2 * width]
    v = zs[:, 2 * width:3 * width]
    xwa = zs[:, 3 * width:3 * width + LANES]
    xg = zs[:, 3 * width + LANES:3 * width + 2 * LANES]
    w = -jax.nn.softplus(-(w0_ref[...] + _mm_hi(jnp.tanh(xwa), wup_ref[...]))) - 0.5
    a = jax.nn.sigmoid(a0_ref[...] + _mm_hi(xwa, aup_ref[...]))
    g = _mm_hi(jax.nn.sigmoid(xg), gup_ref[...])
    kk = k * kk_ref[...]
    kk = kk / jnp.maximum(jnp.sqrt(_mm_hi(kk * kk, bd_ref[...])), 1e-12)
    r_o[...] = r
    ld_o[...] = -jnp.exp(w)
    k_o[...] = k * (1.0 + (a - 1.0) * ka_ref[...])
    v_o[...] = v
    kk_o[...] = kk
    b_o[...] = kk * a
    g_o[...] = g


def _rwprep(zrw, mu, w0, wup_pad, a0, aup_pad, gup, k_k, k_a, bd, seq, tm):
    n, zin = zrw.shape
    width = w0.shape[1]
    row = lambda i: (i, 0)
    fix = lambda i: (0, 0)
    prev = lambda i: (jnp.maximum(i * (tm // 8) - 1, 0), 0)
    out = jax.ShapeDtypeStruct((n, width), F32)
    return pl.pallas_call(
        functools.partial(_rwprep_kernel, tiles_per_seq=seq // tm, width=width),
        grid=(n // tm,),
        in_specs=[pl.BlockSpec((tm, zin), row), pl.BlockSpec((8, zin), prev), pl.BlockSpec((1, zin), fix),
                  pl.BlockSpec((1, width), fix), pl.BlockSpec((LANES, width), fix),
                  pl.BlockSpec((1, width), fix), pl.BlockSpec((LANES, width), fix), pl.BlockSpec((LANES, width), fix),
                  pl.BlockSpec((1, width), fix), pl.BlockSpec((1, width), fix), pl.BlockSpec((width, width), fix)],
        out_specs=[pl.BlockSpec((tm, width), row)] * 7,
        out_shape=[out] * 7,
        compiler_params=_params("parallel"),
    )(zrw, zrw, mu, w0, wup_pad, a0, aup_pad, gup, k_k, k_a, bd)


def _rwcore_kernel(r_ref, ld_ref, k_ref, v_ref, kk_ref, b_ref, g_ref, lng_ref, lnb_ref, rk_ref, o_ref, s_ref, *, units):
    U, C, HD = RW_UNIT, RW_CHUNK, RW_HEAD

    @pl.when(pl.program_id(2) == 0)
    def _():
        s_ref[...] = jnp.zeros(s_ref.shape, F32)

    ri = lax.broadcasted_iota(jnp.int32, (U, U), 0)
    ci = lax.broadcasted_iota(jnp.int32, (U, U), 1)
    same = (ri // C) == (ci // C)
    tri_s = same & (ci < ri)
    tri_i = same & (ci <= ri)
    eye = (ri == ci).astype(F32)
    cum_w = tri_i.astype(F32)
    head_avg = same.astype(F32) * (1.0 / HD)
    head_sum = same.astype(F32)
    lane = lax.broadcasted_iota(jnp.int32, (U, U), 1)
    rowi = lax.broadcasted_iota(jnp.int32, (U, U), 0)
    hmask = (lane < HD, lane >= HD)
    cmask = (rowi < C, rowi >= C)
    zero = jnp.zeros((U, U), F32)

    local = []
    for u in range(units):
        sl = pl.ds(u * U, U)
        r, ld, k, v, kk, b = r_ref[sl, :], ld_ref[sl, :], k_ref[sl, :], v_ref[sl, :], kk_ref[sl, :], b_ref[sl, :]
        cum = _mm_split(cum_w, ld, NN, 1, 3)
        gam = jnp.exp(cum)
        ginv = jnp.exp(-cum)
        at = -kk * jnp.exp(cum - ld)
        bt = b * ginv
        kt = k * ginv
        rt = r * gam
        bk = jnp.concatenate([bt, kt], axis=0)
        ap = zero
        uu = zero
        rp = zero
        yp = zero
        for h in range(2):
            ar = jnp.concatenate([jnp.where(hmask[h], at, zero), jnp.where(hmask[h], rt, zero)], axis=0)
            m = _mm_nt(ar, bk)
            mab = jnp.where(tri_s, m[0:U, 0:U], zero)
            mak = jnp.where(tri_s, m[0:U, U:2 * U], zero)
            mrb = jnp.where(tri_i, m[U:2 * U, 0:U], zero)
            mrk = jnp.where(tri_i, m[U:2 * U, U:2 * U], zero)
            t = eye + mab
            p = mab
            for _ in range(int(math.log2(C)) - 1):
                p = _mm(p, p)
                t = t + _mm(t, p)
            aph = _mm(t, at)
            uh = _mm(t, _mm(mak, v))
            rph = _mm(mrb, aph)
            yph = _mm(mrb, uh) + _mm(mrk, v)
            ap = jnp.where(hmask[h], aph, ap)
            uu = jnp.where(hmask[h], uh, uu)
            rp = jnp.where(hmask[h], rph, rp)
            yp = jnp.where(hmask[h], yph, yp)
        rp = rp + rt
        gs, hs = [], []
        for c in range(2):
            gl = gam[(c + 1) * C - 1:(c + 1) * C, :]
            apc = jnp.where(cmask[c], ap, zero)
            uvc = jnp.concatenate([jnp.where(cmask[c], uu, zero), jnp.where(cmask[c], v, zero)], axis=0)
            gs.append(jnp.where(same, eye + _mm_tn(apc, bt), zero) * gl)
            hs.append(jnp.where(same, _mm_tn(uvc, bk), zero) * gl)
        local.append((rp, yp, gs, hs))

    s = s_ref[...]
    for u in range(units):
        rp, yp, gs, hs = local[u]
        y0 = _mm_split(rp, s, NT, 2, 2)
        s = _mm_split(s, gs[0], NN, 2, 2) + hs[0]
        y1 = _mm_split(rp, s, NT, 2, 2)
        s = _mm_split(s, gs[1], NN, 2, 2) + hs[1]
        y = jnp.where(cmask[0], y0, y1) + yp
        sl = pl.ds(u * U, U)
        r, k, v = r_ref[sl, :], k_ref[sl, :], v_ref[sl, :]
        mean = _mm_split(y, head_avg, NN, 2, 1)
        yc = y - mean
        var = _mm_split(yc * yc, head_avg, NN, 2, 1)
        yn = yc * lax.rsqrt(var + RW_GN_EPS) * lng_ref[...] + lnb_ref[...]
        yn = yn + _mm_split(r * k * rk_ref[...], head_sum, NN, 2, 1) * v
        o_ref[sl, :] = (yn * g_ref[sl, :]).astype(o_ref.dtype)
    s_ref[...] = s


def _rwcore(r, ld, k, v, kk, b, g, ln_g, ln_b, r_k, batch, seq, units):
    n, width = r.shape
    rows = units * RW_UNIT
    steps = seq // rows
    blk = pl.BlockSpec((rows, LANES), lambda bb, hp, i: (bb * steps + i, hp))
    vec = pl.BlockSpec((1, LANES), lambda bb, hp, i: (0, hp))
    return pl.pallas_call(
        functools.partial(_rwcore_kernel, units=units),
        grid=(batch, width // LANES, steps),
        in_specs=[blk] * 7 + [vec] * 3,
        out_specs=blk,
        out_shape=jax.ShapeDtypeStruct((n, width), BF16),
        scratch_shapes=[pltpu.VMEM((RW_UNIT, RW_UNIT), F32)],
        compiler_params=_params("parallel", "parallel", "arbitrary"),
    )(r, ld, k, v, kk, b, g, ln_g, ln_b, r_k)


def _outproj_kernel(x_ref, oda_ref, orw_ref, wa_ref, wb_ref, g_ref, h_ref, u_ref):
    h = (x_ref[...] + jnp.dot(oda_ref[...], wa_ref[...], preferred_element_type=F32)
         + jnp.dot(orw_ref[...], wb_ref[...], preferred_element_type=F32))
    h_ref[...] = h
    u_ref[...] = _rms(h, g_ref[...]).astype(BF16)


def _outproj(x2, o_da, o_rw, wa, wb, g, tm):
    n, d = x2.shape
    da = o_da.shape[1]
    rw = o_rw.shape[1]
    row = lambda i: (i, 0)
    fix = lambda i: (0, 0)
    return pl.pallas_call(
        _outproj_kernel,
        grid=(n // tm,),
        in_specs=[pl.BlockSpec((tm, d), row), pl.BlockSpec((tm, da), row), pl.BlockSpec((tm, rw), row),
                  pl.BlockSpec((da, d), fix), pl.BlockSpec((rw, d), fix), pl.BlockSpec((1, d), fix)],
        out_specs=[pl.BlockSpec((tm, d), row), pl.BlockSpec((tm, d), row)],
        out_shape=[jax.ShapeDtypeStruct((n, d), F32), jax.ShapeDtypeStruct((n, d), BF16)],
        compiler_params=_params("parallel"),
    )(x2, o_da, o_rw, wa, wb, g)


def _topk_rows(s, k, payload=None):
    rows = s.shape[0]
    iota = lax.broadcasted_iota(jnp.int32, s.shape, 0).astype(F32)
    vals, sel = [], []
    for _ in range(k):
        m = jnp.max(s, axis=0, keepdims=True)
        am = jnp.min(jnp.where(s == m, iota, float(rows)), axis=0, keepdims=True)
        hit = iota == am
        vals.append(m)
        sel.append(am if payload is None else jnp.sum(jnp.where(hit, payload, 0.0), axis=0, keepdims=True))
        s = jnp.where(hit, -jnp.inf, s)
    return vals, sel


def _stack_rows(rows_list):
    k = len(rows_list)
    iota = lax.broadcasted_iota(jnp.int32, (k, rows_list[0].shape[1]), 0)
    out = jnp.zeros(iota.shape, rows_list[0].dtype)
    for j, r in enumerate(rows_list):
        out = jnp.where(iota == j, r, out)
    return out


def _peertopk_kernel(u_ref, wq_ref, keys_ref, idx_ref, gate_ref):
    u = u_ref[...]
    half = N_KEYS
    idx_rows, gate_rows = [], []
    for h in range(PEER_HEADS):
        tops = []
        for p in range(2):
            hp = h * 2 + p
            q_t = lax.dot_general(wq_ref[hp * half:(hp + 1) * half, :], u, NT, preferred_element_type=F32)
            s_t = jnp.dot(keys_ref[hp], q_t.astype(BF16), preferred_element_type=F32)
            tops.append(_topk_rows(s_t, PEER_TOPK))
        (v1, i1), (v2, i2) = tops
        pairs = [(i, j) for i in range(PEER_TOPK) for j in range(PEER_TOPK) if (i + 1) * (j + 1) <= PEER_TOPK]
        pad = -len(pairs) % 8
        cand = _stack_rows([v1[i] + v2[j] for i, j in pairs] + [jnp.full_like(v1[0], -jnp.inf)] * pad)
        cidx = _stack_rows([i1[i] * float(N_KEYS) + i2[j] for i, j in pairs] + [jnp.zeros_like(i1[0])] * pad)
        best, idx = _topk_rows(cand, PEER_TOPK, payload=cidx)
        e = [jnp.exp(b - best[0]) for b in best]
        den = functools.reduce(lambda a, b: a + b, e)
        idx_rows.append(_stack_rows(idx))
        gate_rows.append(_stack_rows([x / den for x in e]))
    idx_ref[...] = jnp.concatenate(idx_rows, axis=0).T.astype(jnp.int32)
    gate_ref[...] = jnp.concatenate(gate_rows, axis=0).T


def _peertopk(u_bf, wq_t, keys, tt):
    n, d = u_bf.shape
    hk = PEER_HEADS * PEER_TOPK
    row = lambda i: (i, 0)
    return pl.pallas_call(
        _peertopk_kernel,
        grid=(n // tt,),
        in_specs=[pl.BlockSpec((tt, d), row), pl.BlockSpec(wq_t.shape, lambda i: (0, 0)),
                  pl.BlockSpec(keys.shape, lambda i: (0, 0, 0))],
        out_specs=[pl.BlockSpec((tt, hk), row), pl.BlockSpec((tt, hk), row)],
        out_shape=[jax.ShapeDtypeStruct((n, hk), jnp.int32), jax.ShapeDtypeStruct((n, hk), F32)],
        compiler_params=_params("parallel"),
    )(u_bf, wq_t, keys)


SC_CORES = 2
SC_SUBCORES = 16
SC_LANES = 16
GATHER_ROWS = 32
GATHER_SLOTS = 6
GATHER_INDEX_WINDOW = 1024
HK = PEER_HEADS * PEER_TOPK
SUM_ROWS = 16
SUM_TOKENS = 8
SUM_UNROLL = 2


def _sc_mesh():
    return plsc.VectorSubcoreMesh(core_axis_name="core", subcore_axis_name="subcore")


def _sc_worker():
    return lax.axis_index("core") * SC_SUBCORES + lax.axis_index("subcore")


def _gather_rows(tab, idx_flat):
    p = idx_flat.shape[0]
    w = tab.shape[1]
    workers = SC_CORES * SC_SUBCORES
    per_worker = p // workers
    iw = min(GATHER_INDEX_WINDOW, per_worker)
    r = GATHER_ROWS
    ns = GATHER_SLOTS
    lag = ns - 1
    nsub = iw // r
    assert p % workers == 0 and per_worker % iw == 0 and iw % r == 0 and nsub >= ns
    buf = pltpu.VMEM((r, w), tab.dtype)
    sem = pltpu.SemaphoreType.DMA

    @pl.kernel(out_type=jax.ShapeDtypeStruct((p, w), tab.dtype), mesh=_sc_mesh(),
               scratch_types=[pltpu.VMEM((iw,), jnp.int32)] + [buf] * ns + [sem] * (2 * ns))
    def gather(t_hbm, i_hbm, o_hbm, idx_v, *scratch):
        bufs, gsem, wsem = scratch[0:ns], scratch[ns:2 * ns], scratch[2 * ns:3 * ns]
        wid = _sc_worker()

        @pl.loop(0, per_worker // iw)
        def _(o):
            start = wid * per_worker + o * iw
            pltpu.sync_copy(i_hbm.at[pl.ds(start, iw)], idx_v)

            def gather_of(j):
                s = j % ns
                return pltpu.make_async_copy(t_hbm.at[idx_v.at[pl.ds(j * r, r)]], bufs[s], gsem[s])

            def write_of(j):
                s = j % ns
                return pltpu.make_async_copy(bufs[s], o_hbm.at[pl.ds(start + j * r, r)], wsem[s])

            for j in range(nsub + lag):
                if j < nsub:
                    if j >= ns:
                        write_of(j - ns).wait()
                    gather_of(j).start()
                if j >= lag:
                    gather_of(j - lag).wait()
                    write_of(j - lag).start()
            for j in range(nsub - ns, nsub):
                write_of(j).wait()

    return gather(tab, idx_flat)


def _weighted_row_sum(tab, idx_flat, wrep):
    p = idx_flat.shape[0]
    w = tab.shape[1]
    t_total = p // HK
    workers = SC_CORES * SC_SUBCORES
    tpw = t_total // workers
    g = min(SUM_TOKENS, tpw)
    r = SUM_ROWS
    ns = HK // r
    ln = SC_LANES
    per_row = LANES // ln
    assert t_total % workers == 0 and tpw % g == 0 and g % 2 == 0 and w % ln == 0
    buf = pltpu.VMEM((r, w), tab.dtype)
    acc = pltpu.VMEM((2 * w,), F32)
    sem = pltpu.SemaphoreType.DMA

    @pl.kernel(out_type=jax.ShapeDtypeStruct((t_total, 2 * w), F32), mesh=_sc_mesh(),
               scratch_types=[pltpu.VMEM((g * HK,), jnp.int32), pltpu.VMEM((g * HK // per_row, LANES), F32), acc, acc]
               + [buf] * ns + [sem] * (ns + 2),
               compiler_params=pltpu.CompilerParams(needs_layout_passes=False))
    def wsum(t_hbm, i_hbm, w_hbm, o_hbm, idx_v, w_v, acc0, acc1, *scratch):
        bufs, gsem, osem = scratch[0:ns], scratch[ns:2 * ns], scratch[2 * ns:2 * ns + 2]
        accs = (acc0, acc1)
        wid = _sc_worker()
        zero = jnp.zeros((ln,), F32)

        @pl.loop(0, tpw // g)
        def _(win):
            tok0 = wid * tpw + win * g
            pltpu.sync_copy(i_hbm.at[pl.ds(tok0 * HK, g * HK)], idx_v)
            pltpu.sync_copy(w_hbm.at[pl.ds(tok0 * (HK // per_row), g * HK // per_row)], w_v)

            def gather_of(tl, s):
                return pltpu.make_async_copy(t_hbm.at[idx_v.at[pl.ds(tl * HK + s * r, r)]], bufs[s], gsem[s])

            def out_of(tl, par):
                return pltpu.make_async_copy(accs[par], o_hbm.at[tok0 + tl], osem[par])

            for s in range(ns):
                gather_of(0, s).start()

            @pl.loop(0, g // 2)
            def _(tp):
                for par in range(2):
                    tl = tp * 2 + par
                    ob = accs[par]

                    @pl.when(tl >= 2)
                    def _():
                        out_of(tl - 2, par).wait()

                    for c in range(2 * w // ln):
                        ob[pl.ds(c * ln, ln)] = zero
                    for s in range(ns):
                        gather_of(tl, s).wait()
                        wrow = tl * (HK // per_row) + s * (r // per_row)
                        wk = [w_v[wrow + q // per_row, pl.ds((q % per_row) * ln, ln)] for q in range(r)]
                        rows = bufs[s]

                        def fold(c, carry):
                            for half in range(SUM_UNROLL):
                                col = pl.multiple_of((c * SUM_UNROLL + half) * ln, ln)
                                lo_acc = hi_acc = None
                                for q in range(r):
                                    word = rows[q, pl.ds(col, ln)]
                                    lo = wk[q] * plsc.bitcast(lax.shift_left(word, jnp.uint32(16)), F32)
                                    hi = wk[q] * plsc.bitcast(word & jnp.uint32(0xFFFF0000), F32)
                                    lo_acc = lo if lo_acc is None else lo_acc + lo
                                    hi_acc = hi if hi_acc is None else hi_acc + hi
                                plsc.addupdate(ob.at[pl.ds(col, ln)], lo_acc)
                                plsc.addupdate(ob.at[pl.ds(w + col, ln)], hi_acc)
                            return carry

                        lax.fori_loop(0, w // (ln * SUM_UNROLL), fold, 0)

                        @pl.when(tl + 1 < g)
                        def _():
                            gather_of(tl + 1, s).start()

                    out_of(tl, par).start()

            for par in range(2):
                out_of(g - 2 + par, par).wait()

    return wsum(tab, idx_flat, wrep)


def _peerw_kernel(xa_ref, xb_ref, gate_ref, ug_ref, rep_ref, o_ref, *, tt):
    hk2 = 2 * HK
    x2 = jnp.concatenate([xa_ref[...], xb_ref[...]], axis=0)
    lane = lax.broadcasted_iota(jnp.int32, (tt, hk2), 1)
    row = lax.broadcasted_iota(jnp.int32, (tt, hk2), 0)
    even = (lane % 2) == 0
    part = jnp.zeros((tt, hk2), F32)
    for t in range(tt):
        ub = pltpu.bitcast(ug_ref[t * HK:(t + 1) * HK, :], BF16)
        r = lax.dot_general(x2, ub, NT, preferred_element_type=F32)
        part = jnp.where(row == t, jnp.where(even, r[0:tt], r[tt:2 * tt]), part)
    hid = part + jnp.where(even, pltpu.roll(part, hk2 - 1, 1), pltpu.roll(part, 1, 1))
    w = gate_ref[...] * (0.5 * hid * (1.0 + lax.erf(hid * (2.0 ** -0.5))))
    o_ref[...] = jnp.dot(w.astype(BF16), rep_ref[...], preferred_element_type=F32)


def _peerw(xa, xb, gate2, ug, rep, tt):
    n, dh = xa.shape
    row = lambda i: (i, 0)
    return pl.pallas_call(
        functools.partial(_peerw_kernel, tt=tt),
        grid=(n // tt,),
        in_specs=[pl.BlockSpec((tt, dh), row), pl.BlockSpec((tt, dh), row), pl.BlockSpec((tt, 2 * HK), row),
                  pl.BlockSpec((tt * HK, dh), row), pl.BlockSpec(rep.shape, lambda i: (0, 0))],
        out_specs=pl.BlockSpec((tt, HK * SC_LANES), row),
        out_shape=jax.ShapeDtypeStruct((n, HK * SC_LANES), F32),
        compiler_params=_params("parallel"),
    )(xa, xb, gate2, ug, rep)


def _repeat_matrix():
    src = jnp.arange(2 * HK)[:, None]
    dst = jnp.arange(HK * SC_LANES)[None, :]
    return (src == 2 * (dst // SC_LANES)).astype(BF16)


def _ple_kernel(h_ref, f_ref, p_ref, g_ref, wg_ref, wp_ref, gf_ref, o_ref):
    h = h_ref[...] + f_ref[...]
    gate = jax.nn.sigmoid(jnp.dot(_rms(h, g_ref[...]).astype(BF16), wg_ref[...], preferred_element_type=F32))
    pp = jnp.dot(p_ref[...].astype(BF16), wp_ref[...], preferred_element_type=F32)
    o_ref[...] = _rms(h + gate * pp, gf_ref[...])


def _ple(h1, ffn, p2, g, wg, wp, gf, tm):
    n, d = h1.shape
    pd = p2.shape[1]
    row = lambda i: (i, 0)
    fix = lambda i: (0, 0)
    return pl.pallas_call(
        _ple_kernel,
        grid=(n // tm,),
        in_specs=[pl.BlockSpec((tm, d), row), pl.BlockSpec((tm, d), row), pl.BlockSpec((tm, pd), row),
                  pl.BlockSpec((1, d), fix), pl.BlockSpec((d, d), fix), pl.BlockSpec((pd, d), fix),
                  pl.BlockSpec((1, d), fix)],
        out_specs=pl.BlockSpec((tm, d), row),
        out_shape=jax.ShapeDtypeStruct((n, d), F32),
        compiler_params=_params("parallel"),
    )(h1, ffn, p2, g, wg, wp, gf)


def _rope_tables(positions):
    half = ROT_DIM // 2
    inv_freq = ROPE_THETA ** (-jnp.arange(half, dtype=F32) * 2.0 / ROT_DIM)
    ang = positions.astype(F32).reshape(-1, 1) * inv_freq
    lane = jnp.arange(LANES)
    d = lane % DA_HEAD_DIM
    cos = jnp.take(jnp.cos(ang), d % half, axis=1)
    sin = jnp.take(jnp.sin(ang), d % half, axis=1)
    c = jnp.where(d < ROT_DIM, cos, 1.0)
    s1 = jnp.where(d < half, -sin, 0.0)
    s2 = jnp.where((d >= half) & (d < ROT_DIM), sin, 0.0)
    return c, s1, s2


def _pack_rows(tab):
    d = tab.shape[1]
    t = tab.astype(BF16)
    pair = jnp.stack([t[:, :d // 2], t[:, d // 2:]], axis=-1)
    return lax.bitcast_convert_type(pair, jnp.uint32)


def _block_diag_ones(width, head):
    i = jnp.arange(width)
    return (i[:, None] // head == i[None, :] // head).astype(F32)


def _tiles(seq):
    return dict(tm=min(256, seq), tq=min(512, seq), units=min(4, seq // RW_UNIT), tt_topk=min(256, seq), tt_mix=16,
                peer_chunks=2 if seq % 4096 == 0 else 1)


def kernel(x, p, positions, norm_mix_g, w_in, lam_q1, lam_k1, lam_q2, lam_k2, da_subln_g, rw_mu, rw_w0, rw_w_up, rw_a0, rw_a_up, rw_g_up, rw_k_k, rw_k_a, rw_r_k, rw_ln_g, rw_ln_b, w_out, norm_ffn_g, peer_w_q, peer_sub_keys, peer_u, peer_v, norm_ple_g, ple_gate_w, ple_proj_w, norm_final_g):
    batch, seq, d = x.shape
    t = _tiles(seq)
    row = lambda a: a.reshape(1, -1)
    f32 = F32

    w_in_bf = w_in[0].astype(BF16)
    lam = (jnp.exp(jnp.sum(lam_q1[0].astype(f32) * lam_k1[0].astype(f32)))
           - jnp.exp(jnp.sum(lam_q2[0].astype(f32) * lam_k2[0].astype(f32))) + LAM_INIT).reshape(1, 1)
    width = rw_w0.shape[1]
    wup_pad = jnp.concatenate([rw_w_up[0], jnp.zeros((LANES - rw_w_up.shape[1], width), f32)], axis=0)
    aup_pad = jnp.concatenate([jnp.zeros((LANES - rw_a_up.shape[1], width), f32), rw_a_up[0]], axis=0)
    head_ones = _block_diag_ones(width, RW_HEAD)
    w_out_bf = w_out[0].astype(BF16)
    da_w = DA_HEADS * 2 * DA_HEAD_DIM
    keys = peer_sub_keys[0].reshape(PEER_HEADS * 2, N_KEYS, -1).astype(BF16)
    wq_t = peer_w_q[0].T.astype(BF16)
    u_tab = _pack_rows(peer_u[0])
    v_tab = _pack_rows(peer_v[0])
    wg_bf = ple_gate_w[0].astype(BF16)
    wp_bf = ple_proj_w[0].astype(BF16)
    rep = _repeat_matrix()

    nc = seq // t["peer_chunks"]
    chunks = [slice(c * nc, (c + 1) * nc) for c in range(t["peer_chunks"])]

    outs = []
    for b in range(batch):
        xs = x[b]
        rc, rs1, rs2 = _rope_tables(positions[b])
        qkv, zrw = _inproj(xs, row(norm_mix_g[0]), w_in_bf, rc, rs1, rs2, t["tm"])
        o_da = _attention(qkv, lam, row(da_subln_g[0]), 1, seq, t["tq"])
        rw = _rwprep(zrw, row(rw_mu[0]), row(rw_w0[0]), wup_pad, row(rw_a0[0]), aup_pad, rw_g_up[0],
                     row(rw_k_k[0]), row(rw_k_a[0]), head_ones, seq, t["tm"])
        o_rw = _rwcore(*rw, row(rw_ln_g[0]), row(rw_ln_b[0]), row(rw_r_k[0]), 1, seq, t["units"])
        h1, u2 = _outproj(xs, o_da, o_rw, w_out_bf[:da_w], w_out_bf[da_w:], row(norm_ffn_g[0]), t["tm"])
        idx, gate = _peertopk(u2, wq_t, keys, t["tt_topk"])
        gate2 = jnp.repeat(gate, 2, axis=1)
        xa, xb = u2[:, :d // 2], u2[:, d // 2:]
        ffn = []
        for sl in chunks:
            idx_c = idx[sl].reshape(-1)
            ug = _gather_rows(u_tab, idx_c)
            wrep = _peerw(xa[sl], xb[sl], gate2[sl], ug, rep, t["tt_mix"])
            ffn.append(_weighted_row_sum(v_tab, idx_c, wrep.reshape(-1, LANES)))
        ffn = jnp.concatenate(ffn, axis=0) if len(ffn) > 1 else ffn[0]
        outs.append(_ple(h1, ffn, p[0, b], row(norm_ple_g[0]), wg_bf, wp_bf, row(norm_final_g), t["tm"]))
    return jnp.stack(outs, axis=0)
```

```python
import functools
import math

import jax
import jax.numpy as jnp
from jax import lax
from jax.experimental import pallas as pl
from jax.experimental.pallas import tpu as pltpu
from jax.experimental.pallas import tpu_sc as plsc

F32 = jnp.float32
BF16 = jnp.bfloat16

NORM_EPS = 1e-6
DA_HEADS = 4
DA_HEAD_DIM = 64
ROPE_THETA = 500000.0
ROT_DIM = DA_HEAD_DIM // 4
RW_HEAD = 64
RW_GN_EPS = 64e-5
PEER_HEADS = 8
N_KEYS = 128
PEER_TOPK = 16
LAM_INIT = 0.8 - 0.6 * math.exp(-0.3 * 0)

LANES = 128
VMEM_LIMIT = 56 * 1024 * 1024
RW_CHUNK = 64
RW_UNIT = 2 * RW_CHUNK

NT = (((1,), (1,)), ((), ()))
TN = (((0,), (0,)), ((), ()))
HI = lax.Precision.HIGHEST


def _mm(a, b):
    return jnp.dot(a.astype(BF16), b.astype(BF16), preferred_element_type=F32)


def _mm_nt(a, b):
    return lax.dot_general(a.astype(BF16), b.astype(BF16), NT, preferred_element_type=F32)


def _mm_tn(a, b):
    return lax.dot_general(a.astype(BF16), b.astype(BF16), TN, preferred_element_type=F32)


def _mm_hi(a, b):
    return jnp.dot(a, b, precision=HI, preferred_element_type=F32)


def _pieces(a, n):
    out = []
    for _ in range(n):
        piece = a.astype(BF16)
        out.append(piece)
        a = a - piece.astype(F32)
    return out


def _mm_split(a, b, dims, a_pieces, b_pieces):
    ap, bp = _pieces(a, a_pieces), _pieces(b, b_pieces)
    terms = [lax.dot_general(x, y, dims, preferred_element_type=F32)
             for i, x in enumerate(ap) for j, y in enumerate(bp) if i + j < max(a_pieces, b_pieces)]
    return functools.reduce(lambda u, v: u + v, terms)


NN = (((1,), (0,)), ((), ()))


def _params(*sem):
    return pltpu.CompilerParams(dimension_semantics=sem, vmem_limit_bytes=VMEM_LIMIT)


def _rms(x, g):
    return x * lax.rsqrt(jnp.mean(x * x, axis=-1, keepdims=True) + NORM_EPS) * g


def _inproj_kernel(x_ref, g_ref, w_ref, c_ref, s1_ref, s2_ref, qkv_ref, zrw_ref, *, n_qk, n_da):
    u = _rms(x_ref[...], g_ref[...]).astype(BF16)
    z = jnp.dot(u, w_ref[...], preferred_element_type=F32)
    c, s1, s2 = c_ref[...], s1_ref[...], s2_ref[...]
    half = ROT_DIM // 2
    for blk in range(n_da // LANES):
        t = z[:, blk * LANES:(blk + 1) * LANES]
        if blk < 2 * n_qk // LANES:
            t = t * c + pltpu.roll(t, LANES - half, 1) * s1 + pltpu.roll(t, half, 1) * s2
        if blk < n_qk // LANES:
            t = t * (DA_HEAD_DIM ** -0.5 * math.log2(math.e))
        qkv_ref[:, blk * LANES:(blk + 1) * LANES] = t.astype(BF16)
    zrw_ref[...] = z[:, n_da:]


def _inproj(x2, g, w_in_bf, rc, rs1, rs2, tm):
    n, d = x2.shape
    n_in = w_in_bf.shape[1]
    n_qk = DA_HEADS * 2 * DA_HEAD_DIM
    n_da = 3 * n_qk
    row = lambda i: (i, 0)
    fix = lambda i: (0, 0)
    return pl.pallas_call(
        functools.partial(_inproj_kernel, n_qk=n_qk, n_da=n_da),
        grid=(n // tm,),
        in_specs=[pl.BlockSpec((tm, d), row), pl.BlockSpec((1, d), fix), pl.BlockSpec((d, n_in), fix),
                  pl.BlockSpec((tm, LANES), row), pl.BlockSpec((tm, LANES), row), pl.BlockSpec((tm, LANES), row)],
        out_specs=[pl.BlockSpec((tm, n_da), row), pl.BlockSpec((tm, n_in - n_da), row)],
        out_shape=[jax.ShapeDtypeStruct((n, n_da), BF16), jax.ShapeDtypeStruct((n, n_in - n_da), F32)],
        compiler_params=_params("parallel"),
    )(x2, g, w_in_bf, rc, rs1, rs2)


def _attn_kernel(lam_ref, q_ref, k_ref, v_ref, sg_ref, o_ref, m_ref, acc_ref, *, tq):
    i = pl.program_id(2)
    q = q_ref[...]
    lane = lax.broadcasted_iota(jnp.int32, q.shape, 1)
    zero = jnp.zeros_like(q)
    qs = (jnp.where(lane < DA_HEAD_DIM, q, zero), jnp.where(lane >= DA_HEAD_DIM, q, zero))
    m_ref[...] = jnp.full(m_ref.shape, -jnp.inf, F32)
    acc_ref[...] = jnp.zeros(acc_ref.shape, F32)
    ones = jnp.ones((tq, LANES), BF16)

    def block(j, masked):
        kj = k_ref[pl.ds(pl.multiple_of(j * tq, tq), tq), :]
        vj = jnp.concatenate([v_ref[pl.ds(pl.multiple_of(j * tq, tq), tq), :], ones], axis=1)
        for c in range(2):
            s = lax.dot_general(qs[c], kj, NT, preferred_element_type=F32)
            if masked:
                r_id = lax.broadcasted_iota(jnp.int32, s.shape, 0)
                c_id = lax.broadcasted_iota(jnp.int32, s.shape, 1)
                s = jnp.where(c_id <= r_id, s, -jnp.inf)
            m_old = m_ref[c]
            m_new = jnp.maximum(m_old, jnp.max(s, axis=-1, keepdims=True))
            alpha = jnp.exp2(m_old - m_new)
            p = jnp.exp2(s - jnp.tile(m_new, (1, tq // LANES)))
            pv = jnp.dot(p.astype(BF16), vj, preferred_element_type=F32)
            acc_ref[c] = jnp.tile(alpha, (1, 2)) * acc_ref[c] + pv
            m_ref[c] = m_new

    def body(j, carry):
        block(j, False)
        return carry

    lax.fori_loop(0, i, body, 0)
    block(i, True)
    lam = lam_ref[0, 0]
    a0, a1 = acc_ref[0], acc_ref[1]
    o = a0[:, :LANES] / a0[:, LANES:] - lam * (a1[:, :LANES] / a1[:, LANES:])
    o = o * lax.rsqrt(jnp.mean(o * o, axis=-1, keepdims=True) + NORM_EPS) * sg_ref[...] * (1.0 - LAM_INIT)
    o_ref[...] = o.astype(o_ref.dtype)


def _attention(qkv, lam, subln_g, batch, seq, tq):
    n = qkv.shape[0]
    nq = seq // tq
    h = DA_HEADS
    return pl.pallas_call(
        functools.partial(_attn_kernel, tq=tq),
        grid=(batch, h, nq),
        in_specs=[pl.BlockSpec(memory_space=pltpu.SMEM),
                  pl.BlockSpec((tq, LANES), lambda b, hh, i: (b * nq + i, hh)),
                  pl.BlockSpec((seq, LANES), lambda b, hh, i: (b, h + hh)),
                  pl.BlockSpec((seq, LANES), lambda b, hh, i: (b, 2 * h + hh)),
                  pl.BlockSpec((1, LANES), lambda b, hh, i: (0, 0))],
        out_specs=pl.BlockSpec((tq, LANES), lambda b, hh, i: (b * nq + i, hh)),
        out_shape=jax.ShapeDtypeStruct((n, h * LANES), BF16),
        scratch_shapes=[pltpu.VMEM((2, tq, LANES), F32), pltpu.VMEM((2, tq, 2 * LANES), F32)],
        compiler_params=_params("parallel", "parallel", "arbitrary"),
    )(lam, qkv, qkv, qkv, subln_g)


def _rwprep_kernel(z_ref, zp_ref, mu_ref, w0_ref, wup_ref, a0_ref, aup_ref, gup_ref, kk_ref, ka_ref, bd_ref,
                   r_o, ld_o, k_o, v_o, kk_o, b_o, g_o, *, tiles_per_seq, width):
    i = pl.program_id(0)
    z = z_ref[...]
    last = zp_ref[7:8, :]
    first = jnp.where(i % tiles_per_seq == 0, jnp.zeros_like(last), last)
    row = lax.broadcasted_iota(jnp.int32, z.shape, 0)
    prev = jnp.where(row == 0, first, pltpu.roll(z, 1, 0))
    zs = z + (prev - z) * mu_ref[...]
    r = zs[:, 0:width]
    k = zs[:, width:2 * width]
    v = zs[:, 2 * width:3 * width]
    xwa = zs[:, 3 * width:3 * width + LANES]
    xg = zs[:, 3 * width + LANES:3 * width + 2 * LANES]
    w = -jax.nn.softplus(-(w0_ref[...] + _mm_hi(jnp.tanh(xwa), wup_ref[...]))) - 0.5
    a = jax.nn.sigmoid(a0_ref[...] + _mm_hi(xwa, aup_ref[...]))
    g = _mm_hi(jax.nn.sigmoid(xg), gup_ref[...])
    kk = k * kk_ref[...]
    kk = kk / jnp.maximum(jnp.sqrt(_mm_hi(kk * kk, bd_ref[...])), 1e-12)
    r_o[...] = r
    ld_o[...] = -jnp.exp(w)
    k_o[...] = k * (1.0 + (a - 1.0) * ka_ref[...])
    v_o[...] = v
    kk_o[...] = kk
    b_o[...] = kk * a
    g_o[...] = g


def _rwprep(zrw, mu, w0, wup_pad, a0, aup_pad, gup, k_k, k_a, bd, seq, tm):
    n, zin = zrw.shape
    width = w0.shape[1]
    row = lambda i: (i, 0)
    fix = lambda i: (0, 0)
    prev = lambda i: (jnp.maximum(i * (tm // 8) - 1, 0), 0)
    out = jax.ShapeDtypeStruct((n, width), F32)
    return pl.pallas_call(
        functools.partial(_rwprep_kernel, tiles_per_seq=seq // tm, width=width),
        grid=(n // tm,),
        in_specs=[pl.BlockSpec((tm, zin), row), pl.BlockSpec((8, zin), prev), pl.BlockSpec((1, zin), fix),
                  pl.BlockSpec((1, width), fix), pl.BlockSpec((LANES, width), fix),
                  pl.BlockSpec((1, width), fix), pl.BlockSpec((LANES, width), fix), pl.BlockSpec((LANES, width), fix),
                  pl.BlockSpec((1, width), fix), pl.BlockSpec((1, width), fix), pl.BlockSpec((width, width), fix)],
        out_specs=[pl.BlockSpec((tm, width), row)] * 7,
        out_shape=[out] * 7,
        compiler_params=_params("parallel"),
    )(zrw, zrw, mu, w0, wup_pad, a0, aup_pad, gup, k_k, k_a, bd)


def _rwcore_kernel(r_ref, ld_ref, k_ref, v_ref, kk_ref, b_ref, g_ref, lng_ref, lnb_ref, rk_ref, o_ref, s_ref, *,
                   units, groups):
    U, C, HD = RW_UNIT, RW_CHUNK, RW_HEAD

    @pl.when(pl.program_id(2) == 0)
    def _():
        s_ref[...] = jnp.zeros(s_ref.shape, F32)

    ri = lax.broadcasted_iota(jnp.int32, (U, U), 0)
    ci = lax.broadcasted_iota(jnp.int32, (U, U), 1)
    same = (ri // C) == (ci // C)
    tri_s = same & (ci < ri)
    tri_i = same & (ci <= ri)
    eye = (ri == ci).astype(F32)
    cum_w = tri_i.astype(F32)
    head_avg = same.astype(F32) * (1.0 / HD)
    head_sum = same.astype(F32)
    hmask = (ci < HD, ci >= HD)
    cmask = (ri < C, ri >= C)
    zero = jnp.zeros((U, U), F32)
    cells = [(u, q) for u in range(units) for q in range(groups)]

    def blk(ref, cell):
        u, q = cell
        return ref[pl.ds(u * U, U), q * LANES:(q + 1) * LANES]

    def vec(ref, q):
        return ref[:, q * LANES:(q + 1) * LANES]

    cell_v, cell_at, cell_bt, cell_rt, cell_bk, cell_gam = {}, {}, {}, {}, {}, {}
    for cell in cells:
        ld = blk(ld_ref, cell)
        cum = _mm_split(cum_w, ld, NN, 1, 3)
        gam = jnp.exp(cum)
        ginv = jnp.exp(-cum)
        bt = blk(b_ref, cell) * ginv
        cell_v[cell] = blk(v_ref, cell)
        cell_at[cell] = -blk(kk_ref, cell) * jnp.exp(cum - ld)
        cell_bt[cell] = bt
        cell_rt[cell] = blk(r_ref, cell) * gam
        cell_bk[cell] = jnp.concatenate([bt, blk(k_ref, cell) * ginv], axis=0)
        cell_gam[cell] = gam

    chains = [(cell, h) for cell in cells for h in range(2)]
    mab, mak, mrb, mrk = {}, {}, {}, {}
    for ch in chains:
        cell, h = ch
        ar = jnp.concatenate([jnp.where(hmask[h], cell_at[cell], zero), jnp.where(hmask[h], cell_rt[cell], zero)], axis=0)
        m = _mm_nt(ar, cell_bk[cell])
        mab[ch] = jnp.where(tri_s, m[0:U, 0:U], zero)
        mak[ch] = jnp.where(tri_s, m[0:U, U:2 * U], zero)
        mrb[ch] = jnp.where(tri_i, m[U:2 * U, 0:U], zero)
        mrk[ch] = jnp.where(tri_i, m[U:2 * U, U:2 * U], zero)

    tm = {ch: eye + mab[ch] for ch in chains}
    pw = dict(mab)
    for _ in range(int(math.log2(C)) - 1):
        pw = {ch: _mm(pw[ch], pw[ch]) for ch in chains}
        tm = {ch: tm[ch] + _mm(tm[ch], pw[ch]) for ch in chains}
    aph = {ch: _mm(tm[ch], cell_at[ch[0]]) for ch in chains}
    mv = {ch: _mm(mak[ch], cell_v[ch[0]]) for ch in chains}
    uh = {ch: _mm(tm[ch], mv[ch]) for ch in chains}
    rph = {ch: _mm(mrb[ch], aph[ch]) for ch in chains}
    yph = {ch: _mm(mrb[ch], uh[ch]) + _mm(mrk[ch], cell_v[ch[0]]) for ch in chains}

    def both_heads(d, cell):
        return jnp.where(hmask[0], d[(cell, 0)], d[(cell, 1)])

    rp, yp, gs, hs = {}, {}, {}, {}
    for cell in cells:
        ap, uu = both_heads(aph, cell), both_heads(uh, cell)
        rp[cell] = both_heads(rph, cell) + cell_rt[cell]
        yp[cell] = both_heads(yph, cell)
        for c in range(2):
            gl = cell_gam[cell][(c + 1) * C - 1:(c + 1) * C, :]
            apc = jnp.where(cmask[c], ap, zero)
            uvc = jnp.concatenate([jnp.where(cmask[c], uu, zero), jnp.where(cmask[c], cell_v[cell], zero)], axis=0)
            gs[cell, c] = jnp.where(same, eye + _mm_tn(apc, cell_bt[cell]), zero) * gl
            hs[cell, c] = jnp.where(same, _mm_tn(uvc, cell_bk[cell]), zero) * gl

    s = [s_ref[q] for q in range(groups)]
    ys = {}
    for u in range(units):
        for q in range(groups):
            cell = (u, q)
            y0 = _mm_split(rp[cell], s[q], NT, 2, 2)
            s[q] = _mm_split(s[q], gs[cell, 0], NN, 2, 2) + hs[cell, 0]
            y1 = _mm_split(rp[cell], s[q], NT, 2, 2)
            s[q] = _mm_split(s[q], gs[cell, 1], NN, 2, 2) + hs[cell, 1]
            ys[cell] = jnp.where(cmask[0], y0, y1) + yp[cell]
    for q in range(groups):
        s_ref[q] = s[q]

    for cell in cells:
        u, q = cell
        y = ys[cell]
        mean = _mm_split(y, head_avg, NN, 2, 1)
        yc = y - mean
        var = _mm_split(yc * yc, head_avg, NN, 2, 1)
        yn = yc * lax.rsqrt(var + RW_GN_EPS) * vec(lng_ref, q) + vec(lnb_ref, q)
        bonus = _mm_split(blk(r_ref, cell) * blk(k_ref, cell) * vec(rk_ref, q), head_sum, NN, 2, 1)
        yn = yn + bonus * cell_v[cell]
        o_ref[pl.ds(u * U, U), q * LANES:(q + 1) * LANES] = (yn * blk(g_ref, cell)).astype(o_ref.dtype)


def _rwcore(r, ld, k, v, kk, b, g, ln_g, ln_b, r_k, batch, seq, units, groups):
    n, width = r.shape
    rows = units * RW_UNIT
    steps = seq // rows
    lanes = groups * LANES
    blk = pl.BlockSpec((rows, lanes), lambda bb, hp, i: (bb * steps + i, hp))
    vec = pl.BlockSpec((1, lanes), lambda bb, hp, i: (0, hp))
    return pl.pallas_call(
        functools.partial(_rwcore_kernel, units=units, groups=groups),
        grid=(batch, width // lanes, steps),
        in_specs=[blk] * 7 + [vec] * 3,
        out_specs=blk,
        out_shape=jax.ShapeDtypeStruct((n, width), BF16),
        scratch_shapes=[pltpu.VMEM((groups, RW_UNIT, RW_UNIT), F32)],
        compiler_params=_params("parallel", "parallel", "arbitrary"),
    )(r, ld, k, v, kk, b, g, ln_g, ln_b, r_k)


def _outproj_kernel(x_ref, oda_ref, orw_ref, wa_ref, wb_ref, g_ref, h_ref, u_ref):
    h = (x_ref[...] + jnp.dot(oda_ref[...], wa_ref[...], preferred_element_type=F32)
         + jnp.dot(orw_ref[...], wb_ref[...], preferred_element_type=F32))
    h_ref[...] = h
    u_ref[...] = _rms(h, g_ref[...]).astype(BF16)


def _outproj(x2, o_da, o_rw, wa, wb, g, tm):
    n, d = x2.shape
    da = o_da.shape[1]
    rw = o_rw.shape[1]
    row = lambda i: (i, 0)
    fix = lambda i: (0, 0)
    return pl.pallas_call(
        _outproj_kernel,
        grid=(n // tm,),
        in_specs=[pl.BlockSpec((tm, d), row), pl.BlockSpec((tm, da), row), pl.BlockSpec((tm, rw), row),
                  pl.BlockSpec((da, d), fix), pl.BlockSpec((rw, d), fix), pl.BlockSpec((1, d), fix)],
        out_specs=[pl.BlockSpec((tm, d), row), pl.BlockSpec((tm, d), row)],
        out_shape=[jax.ShapeDtypeStruct((n, d), F32), jax.ShapeDtypeStruct((n, d), BF16)],
        compiler_params=_params("parallel"),
    )(x2, o_da, o_rw, wa, wb, g)


def _topk_rows(s, k, payload=None):
    rows = s.shape[0]
    iota = lax.broadcasted_iota(jnp.int32, s.shape, 0).astype(F32)
    vals, sel = [], []
    for _ in range(k):
        m = jnp.max(s, axis=0, keepdims=True)
        am = jnp.min(jnp.where(s == m, iota, float(rows)), axis=0, keepdims=True)
        hit = iota == am
        vals.append(m)
        sel.append(am if payload is None else jnp.sum(jnp.where(hit, payload, 0.0), axis=0, keepdims=True))
        s = jnp.where(hit, -jnp.inf, s)
    return vals, sel


def _stack_rows(rows_list):
    k = len(rows_list)
    iota = lax.broadcasted_iota(jnp.int32, (k, rows_list[0].shape[1]), 0)
    out = jnp.zeros(iota.shape, rows_list[0].dtype)
    for j, r in enumerate(rows_list):
        out = jnp.where(iota == j, r, out)
    return out


def _peertopk_kernel(u_ref, wq_ref, keys_ref, idx_ref, gate_ref):
    u = u_ref[...]
    half = N_KEYS
    idx_rows, gate_rows = [], []
    for h in range(PEER_HEADS):
        tops = []
        for p in range(2):
            hp = h * 2 + p
            q_t = lax.dot_general(wq_ref[hp * half:(hp + 1) * half, :], u, NT, preferred_element_type=F32)
            s_t = jnp.dot(keys_ref[hp], q_t.astype(BF16), preferred_element_type=F32)
            tops.append(_topk_rows(s_t, PEER_TOPK))
        (v1, i1), (v2, i2) = tops
        pairs = [(i, j) for i in range(PEER_TOPK) for j in range(PEER_TOPK) if (i + 1) * (j + 1) <= PEER_TOPK]
        pad = -len(pairs) % 8
        cand = _stack_rows([v1[i] + v2[j] for i, j in pairs] + [jnp.full_like(v1[0], -jnp.inf)] * pad)
        cidx = _stack_rows([i1[i] * float(N_KEYS) + i2[j] for i, j in pairs] + [jnp.zeros_like(i1[0])] * pad)
        best, idx = _topk_rows(cand, PEER_TOPK, payload=cidx)
        e = [jnp.exp(b - best[0]) for b in best]
        den = functools.reduce(lambda a, b: a + b, e)
        idx_rows.append(_stack_rows(idx))
        gate_rows.append(_stack_rows([x / den for x in e]))
    idx_ref[...] = jnp.concatenate(idx_rows, axis=0).T.astype(jnp.int32)
    gate_ref[...] = jnp.concatenate(gate_rows, axis=0).T


def _peertopk(u_bf, wq_t, keys, tt):
    n, d = u_bf.shape
    hk = PEER_HEADS * PEER_TOPK
    row = lambda i: (i, 0)
    return pl.pallas_call(
        _peertopk_kernel,
        grid=(n // tt,),
        in_specs=[pl.BlockSpec((tt, d), row), pl.BlockSpec(wq_t.shape, lambda i: (0, 0)),
                  pl.BlockSpec(keys.shape, lambda i: (0, 0, 0))],
        out_specs=[pl.BlockSpec((tt, hk), row), pl.BlockSpec((tt, hk), row)],
        out_shape=[jax.ShapeDtypeStruct((n, hk), jnp.int32), jax.ShapeDtypeStruct((n, hk), F32)],
        compiler_params=_params("parallel"),
    )(u_bf, wq_t, keys)


SC_CORES = 2
SC_SUBCORES = 16
SC_LANES = 16
GATHER_ROWS = 32
GATHER_SLOTS = 6
GATHER_INDEX_WINDOW = 1024
HK = PEER_HEADS * PEER_TOPK
SUM_ROWS = 16
SUM_TOKENS = 8
SUM_UNROLL = 2


def _sc_mesh():
    return plsc.VectorSubcoreMesh(core_axis_name="core", subcore_axis_name="subcore")


def _sc_worker():
    return lax.axis_index("core") * SC_SUBCORES + lax.axis_index("subcore")


def _gather_rows(tab, idx_flat):
    p = idx_flat.shape[0]
    w = tab.shape[1]
    workers = SC_CORES * SC_SUBCORES
    per_worker = p // workers
    iw = min(GATHER_INDEX_WINDOW, per_worker)
    r = GATHER_ROWS
    ns = GATHER_SLOTS
    lag = ns - 1
    nsub = iw // r
    assert p % workers == 0 and per_worker % iw == 0 and iw % r == 0 and nsub >= ns
    buf = pltpu.VMEM((r, w), tab.dtype)
    sem = pltpu.SemaphoreType.DMA

    @pl.kernel(out_type=jax.ShapeDtypeStruct((p, w), tab.dtype), mesh=_sc_mesh(),
               scratch_types=[pltpu.VMEM((iw,), jnp.int32)] + [buf] * ns + [sem] * (2 * ns))
    def gather(t_hbm, i_hbm, o_hbm, idx_v, *scratch):
        bufs, gsem, wsem = scratch[0:ns], scratch[ns:2 * ns], scratch[2 * ns:3 * ns]
        wid = _sc_worker()

        @pl.loop(0, per_worker // iw)
        def _(o):
            start = wid * per_worker + o * iw
            pltpu.sync_copy(i_hbm.at[pl.ds(start, iw)], idx_v)

            def gather_of(j):
                s = j % ns
                return pltpu.make_async_copy(t_hbm.at[idx_v.at[pl.ds(j * r, r)]], bufs[s], gsem[s])

            def write_of(j):
                s = j % ns
                return pltpu.make_async_copy(bufs[s], o_hbm.at[pl.ds(start + j * r, r)], wsem[s])

            for j in range(nsub + lag):
                if j < nsub:
                    if j >= ns:
                        write_of(j - ns).wait()
                    gather_of(j).start()
                if j >= lag:
                    gather_of(j - lag).wait()
                    write_of(j - lag).start()
            for j in range(nsub - ns, nsub):
                write_of(j).wait()

    return gather(tab, idx_flat)


def _weighted_row_sum(tab, idx_flat, wrep):
    p = idx_flat.shape[0]
    w = tab.shape[1]
    t_total = p // HK
    workers = SC_CORES * SC_SUBCORES
    tpw = t_total // workers
    g = min(SUM_TOKENS, tpw)
    r = SUM_ROWS
    ns = HK // r
    ln = SC_LANES
    per_row = LANES // ln
    assert t_total % workers == 0 and tpw % g == 0 and g % 2 == 0 and w % ln == 0
    buf = pltpu.VMEM((r, w), tab.dtype)
    acc = pltpu.VMEM((2 * w,), F32)
    sem = pltpu.SemaphoreType.DMA

    @pl.kernel(out_type=jax.ShapeDtypeStruct((t_total, 2 * w), F32), mesh=_sc_mesh(),
               scratch_types=[pltpu.VMEM((g * HK,), jnp.int32), pltpu.VMEM((g * HK // per_row, LANES), F32), acc, acc]
               + [buf] * ns + [sem] * (ns + 2),
               compiler_params=pltpu.CompilerParams(needs_layout_passes=False))
    def wsum(t_hbm, i_hbm, w_hbm, o_hbm, idx_v, w_v, acc0, acc1, *scratch):
        bufs, gsem, osem = scratch[0:ns], scratch[ns:2 * ns], scratch[2 * ns:2 * ns + 2]
        accs = (acc0, acc1)
        wid = _sc_worker()
        zero = jnp.zeros((ln,), F32)

        @pl.loop(0, tpw // g)
        def _(win):
            tok0 = wid * tpw + win * g
            pltpu.sync_copy(i_hbm.at[pl.ds(tok0 * HK, g * HK)], idx_v)
            pltpu.sync_copy(w_hbm.at[pl.ds(tok0 * (HK // per_row), g * HK // per_row)], w_v)

            def gather_of(tl, s):
                return pltpu.make_async_copy(t_hbm.at[idx_v.at[pl.ds(tl * HK + s * r, r)]], bufs[s], gsem[s])

            def out_of(tl, par):
                return pltpu.make_async_copy(accs[par], o_hbm.at[tok0 + tl], osem[par])

            for s in range(ns):
                gather_of(0, s).start()

            @pl.loop(0, g // 2)
            def _(tp):
                for par in range(2):
                    tl = tp * 2 + par
                    ob = accs[par]

                    @pl.when(tl >= 2)
                    def _():
                        out_of(tl - 2, par).wait()

                    for c in range(2 * w // ln):
                        ob[pl.ds(c * ln, ln)] = zero
                    for s in range(ns):
                        gather_of(tl, s).wait()
                        wrow = tl * (HK // per_row) + s * (r // per_row)
                        wk = [w_v[wrow + q // per_row, pl.ds((q % per_row) * ln, ln)] for q in range(r)]
                        rows = bufs[s]

                        def fold(c, carry):
                            for half in range(SUM_UNROLL):
                                col = pl.multiple_of((c * SUM_UNROLL + half) * ln, ln)
                                lo_acc = hi_acc = None
                                for q in range(r):
                                    word = rows[q, pl.ds(col, ln)]
                                    lo = wk[q] * plsc.bitcast(lax.shift_left(word, jnp.uint32(16)), F32)
                                    hi = wk[q] * plsc.bitcast(word & jnp.uint32(0xFFFF0000), F32)
                                    lo_acc = lo if lo_acc is None else lo_acc + lo
                                    hi_acc = hi if hi_acc is None else hi_acc + hi
                                plsc.addupdate(ob.at[pl.ds(col, ln)], lo_acc)
                                plsc.addupdate(ob.at[pl.ds(w + col, ln)], hi_acc)
                            return carry

                        lax.fori_loop(0, w // (ln * SUM_UNROLL), fold, 0)

                        @pl.when(tl + 1 < g)
                        def _():
                            gather_of(tl + 1, s).start()

                    out_of(tl, par).start()

            for par in range(2):
                out_of(g - 2 + par, par).wait()

    return wsum(tab, idx_flat, wrep)


def _peerw_kernel(xa_ref, xb_ref, gate_ref, ug_ref, rep_ref, o_ref, *, tt):
    hk2 = 2 * HK
    x2 = jnp.concatenate([xa_ref[...], xb_ref[...]], axis=0)
    lane = lax.broadcasted_iota(jnp.int32, (tt, hk2), 1)
    row = lax.broadcasted_iota(jnp.int32, (tt, hk2), 0)
    even = (lane % 2) == 0
    part = jnp.zeros((tt, hk2), F32)
    for t in range(tt):
        ub = pltpu.bitcast(ug_ref[t * HK:(t + 1) * HK, :], BF16)
        r = lax.dot_general(x2, ub, NT, preferred_element_type=F32)
        part = jnp.where(row == t, jnp.where(even, r[0:tt], r[tt:2 * tt]), part)
    hid = part + jnp.where(even, pltpu.roll(part, hk2 - 1, 1), pltpu.roll(part, 1, 1))
    w = gate_ref[...] * (0.5 * hid * (1.0 + lax.erf(hid * (2.0 ** -0.5))))
    o_ref[...] = jnp.dot(w.astype(BF16), rep_ref[...], preferred_element_type=F32)


def _peerw(xa, xb, gate2, ug, rep, tt):
    n, dh = xa.shape
    row = lambda i: (i, 0)
    return pl.pallas_call(
        functools.partial(_peerw_kernel, tt=tt),
        grid=(n // tt,),
        in_specs=[pl.BlockSpec((tt, dh), row), pl.BlockSpec((tt, dh), row), pl.BlockSpec((tt, 2 * HK), row),
                  pl.BlockSpec((tt * HK, dh), row), pl.BlockSpec(rep.shape, lambda i: (0, 0))],
        out_specs=pl.BlockSpec((tt, HK * SC_LANES), row),
        out_shape=jax.ShapeDtypeStruct((n, HK * SC_LANES), F32),
        compiler_params=_params("parallel"),
    )(xa, xb, gate2, ug, rep)


def _repeat_matrix():
    src = jnp.arange(2 * HK)[:, None]
    dst = jnp.arange(HK * SC_LANES)[None, :]
    return (src == 2 * (dst // SC_LANES)).astype(BF16)


def _ple_kernel(h_ref, f_ref, p_ref, g_ref, wg_ref, wp_ref, gf_ref, o_ref):
    h = h_ref[...] + f_ref[...]
    gate = jax.nn.sigmoid(jnp.dot(_rms(h, g_ref[...]).astype(BF16), wg_ref[...], preferred_element_type=F32))
    pp = jnp.dot(p_ref[...].astype(BF16), wp_ref[...], preferred_element_type=F32)
    o_ref[...] = _rms(h + gate * pp, gf_ref[...])


def _ple(h1, ffn, p2, g, wg, wp, gf, tm):
    n, d = h1.shape
    pd = p2.shape[1]
    row = lambda i: (i, 0)
    fix = lambda i: (0, 0)
    return pl.pallas_call(
        _ple_kernel,
        grid=(n // tm,),
        in_specs=[pl.BlockSpec((tm, d), row), pl.BlockSpec((tm, d), row), pl.BlockSpec((tm, pd), row),
                  pl.BlockSpec((1, d), fix), pl.BlockSpec((d, d), fix), pl.BlockSpec((pd, d), fix),
                  pl.BlockSpec((1, d), fix)],
        out_specs=pl.BlockSpec((tm, d), row),
        out_shape=jax.ShapeDtypeStruct((n, d), F32),
        compiler_params=_params("parallel"),
    )(h1, ffn, p2, g, wg, wp, gf)


def _rope_tables(positions):
    half = ROT_DIM // 2
    inv_freq = ROPE_THETA ** (-jnp.arange(half, dtype=F32) * 2.0 / ROT_DIM)
    ang = positions.astype(F32).reshape(-1, 1) * inv_freq
    lane = jnp.arange(LANES)
    d = lane % DA_HEAD_DIM
    cos = jnp.take(jnp.cos(ang), d % half, axis=1)
    sin = jnp.take(jnp.sin(ang), d % half, axis=1)
    c = jnp.where(d < ROT_DIM, cos, 1.0)
    s1 = jnp.where(d < half, -sin, 0.0)
    s2 = jnp.where((d >= half) & (d < ROT_DIM), sin, 0.0)
    return c, s1, s2


def _pack_rows(tab):
    d = tab.shape[1]
    t = tab.astype(BF16)
    pair = jnp.stack([t[:, :d // 2], t[:, d // 2:]], axis=-1)
    return lax.bitcast_convert_type(pair, jnp.uint32)


def _block_diag_ones(width, head):
    i = jnp.arange(width)
    return (i[:, None] // head == i[None, :] // head).astype(F32)


def _tiles(seq):
    return dict(tm=min(256, seq), tq=min(512, seq), units=min(4, seq // RW_UNIT), groups=2, tt_topk=min(256, seq), tt_mix=16,
                peer_chunks=2 if seq % 4096 == 0 else 1)


def kernel(x, p, positions, norm_mix_g, w_in, lam_q1, lam_k1, lam_q2, lam_k2, da_subln_g, rw_mu, rw_w0, rw_w_up, rw_a0, rw_a_up, rw_g_up, rw_k_k, rw_k_a, rw_r_k, rw_ln_g, rw_ln_b, w_out, norm_ffn_g, peer_w_q, peer_sub_keys, peer_u, peer_v, norm_ple_g, ple_gate_w, ple_proj_w, norm_final_g):
    batch, seq, d = x.shape
    t = _tiles(seq)
    row = lambda a: a.reshape(1, -1)
    f32 = F32

    w_in_bf = w_in[0].astype(BF16)
    lam = (jnp.exp(jnp.sum(lam_q1[0].astype(f32) * lam_k1[0].astype(f32)))
           - jnp.exp(jnp.sum(lam_q2[0].astype(f32) * lam_k2[0].astype(f32))) + LAM_INIT).reshape(1, 1)
    width = rw_w0.shape[1]
    wup_pad = jnp.concatenate([rw_w_up[0], jnp.zeros((LANES - rw_w_up.shape[1], width), f32)], axis=0)
    aup_pad = jnp.concatenate([jnp.zeros((LANES - rw_a_up.shape[1], width), f32), rw_a_up[0]], axis=0)
    head_ones = _block_diag_ones(width, RW_HEAD)
    w_out_bf = w_out[0].astype(BF16)
    da_w = DA_HEADS * 2 * DA_HEAD_DIM
    keys = peer_sub_keys[0].reshape(PEER_HEADS * 2, N_KEYS, -1).astype(BF16)
    wq_t = peer_w_q[0].T.astype(BF16)
    u_tab = _pack_rows(peer_u[0])
    v_tab = _pack_rows(peer_v[0])
    wg_bf = ple_gate_w[0].astype(BF16)
    wp_bf = ple_proj_w[0].astype(BF16)
    rep = _repeat_matrix()

    nc = seq // t["peer_chunks"]
    chunks = [slice(c * nc, (c + 1) * nc) for c in range(t["peer_chunks"])]

    outs = []
    for b in range(batch):
        xs = x[b]
        rc, rs1, rs2 = _rope_tables(positions[b])
        qkv, zrw = _inproj(xs, row(norm_mix_g[0]), w_in_bf, rc, rs1, rs2, t["tm"])
        o_da = _attention(qkv, lam, row(da_subln_g[0]), 1, seq, t["tq"])
        rw = _rwprep(zrw, row(rw_mu[0]), row(rw_w0[0]), wup_pad, row(rw_a0[0]), aup_pad, rw_g_up[0],
                     row(rw_k_k[0]), row(rw_k_a[0]), head_ones, seq, t["tm"])
        o_rw = _rwcore(*rw, row(rw_ln_g[0]), row(rw_ln_b[0]), row(rw_r_k[0]), 1, seq, t["units"], t["groups"])
        h1, u2 = _outproj(xs, o_da, o_rw, w_out_bf[:da_w], w_out_bf[da_w:], row(norm_ffn_g[0]), t["tm"])
        idx, gate = _peertopk(u2, wq_t, keys, t["tt_topk"])
        gate2 = jnp.repeat(gate, 2, axis=1)
        xa, xb = u2[:, :d // 2], u2[:, d // 2:]
        ffn = []
        for sl in chunks:
            idx_c = idx[sl].reshape(-1)
            ug = _gather_rows(u_tab, idx_c)
            wrep = _peerw(xa[sl], xb[sl], gate2[sl], ug, rep, t["tt_mix"])
            ffn.append(_weighted_row_sum(v_tab, idx_c, wrep.reshape(-1, LANES)))
        ffn = jnp.concatenate(ffn, axis=0) if len(ffn) > 1 else ffn[0]
        outs.append(_ple(h1, ffn, p[0, b], row(norm_ple_g[0]), wg_bf, wp_bf, row(norm_final_g), t["tm"]))
    return jnp.stack(outs, axis=0)
```

```python
import functools
import math

import jax
import jax.numpy as jnp
from jax import lax
from jax.experimental import pallas as pl
from jax.experimental.pallas import tpu as pltpu
from jax.experimental.pallas import tpu_sc as plsc

F32 = jnp.float32
BF16 = jnp.bfloat16

NORM_EPS = 1e-6
DA_HEADS = 4
DA_HEAD_DIM = 64
ROPE_THETA = 500000.0
ROT_DIM = DA_HEAD_DIM // 4
RW_HEAD = 64
RW_GN_EPS = 64e-5
PEER_HEADS = 8
N_KEYS = 128
PEER_TOPK = 16
LAM_INIT = 0.8 - 0.6 * math.exp(-0.3 * 0)

LANES = 128
VMEM_LIMIT = 56 * 1024 * 1024
RW_CHUNK = 64
RW_UNIT = 2 * RW_CHUNK

NT = (((1,), (1,)), ((), ()))
TN = (((0,), (0,)), ((), ()))
HI = lax.Precision.HIGHEST


def _mm(a, b):
    return jnp.dot(a.astype(BF16), b.astype(BF16), preferred_element_type=F32)


def _mm_nt(a, b):
    return lax.dot_general(a.astype(BF16), b.astype(BF16), NT, preferred_element_type=F32)


def _mm_tn(a, b):
    return lax.dot_general(a.astype(BF16), b.astype(BF16), TN, preferred_element_type=F32)


def _mm_hi(a, b):
    return jnp.dot(a, b, precision=HI, preferred_element_type=F32)


def _pieces(a, n):
    out = []
    for _ in range(n):
        piece = a.astype(BF16)
        out.append(piece)
        a = a - piece.astype(F32)
    return out


def _mm_split(a, b, dims, a_pieces, b_pieces):
    ap, bp = _pieces(a, a_pieces), _pieces(b, b_pieces)
    terms = [lax.dot_general(x, y, dims, preferred_element_type=F32)
             for i, x in enumerate(ap) for j, y in enumerate(bp) if i + j < max(a_pieces, b_pieces)]
    return functools.reduce(lambda u, v: u + v, terms)


NN = (((1,), (0,)), ((), ()))


def _params(*sem):
    return pltpu.CompilerParams(dimension_semantics=sem, vmem_limit_bytes=VMEM_LIMIT)


def _rms(x, g):
    return x * lax.rsqrt(jnp.mean(x * x, axis=-1, keepdims=True) + NORM_EPS) * g


def _inproj_kernel(x_ref, g_ref, w_ref, c_ref, s1_ref, s2_ref, qkv_ref, zrw_ref, *, n_qk, n_da):
    u = _rms(x_ref[...], g_ref[...]).astype(BF16)
    z = jnp.dot(u, w_ref[...], preferred_element_type=F32)
    c, s1, s2 = c_ref[...], s1_ref[...], s2_ref[...]
    half = ROT_DIM // 2
    for blk in range(n_da // LANES):
        t = z[:, blk * LANES:(blk + 1) * LANES]
        if blk < 2 * n_qk // LANES:
            t = t * c + pltpu.roll(t, LANES - half, 1) * s1 + pltpu.roll(t, half, 1) * s2
        if blk < n_qk // LANES:
            t = t * (DA_HEAD_DIM ** -0.5 * math.log2(math.e))
        qkv_ref[:, blk * LANES:(blk + 1) * LANES] = t.astype(BF16)
    zrw_ref[...] = z[:, n_da:]


def _inproj(x2, g, w_in_bf, rc, rs1, rs2, tm):
    n, d = x2.shape
    n_in = w_in_bf.shape[1]
    n_qk = DA_HEADS * 2 * DA_HEAD_DIM
    n_da = 3 * n_qk
    row = lambda i: (i, 0)
    fix = lambda i: (0, 0)
    return pl.pallas_call(
        functools.partial(_inproj_kernel, n_qk=n_qk, n_da=n_da),
        grid=(n // tm,),
        in_specs=[pl.BlockSpec((tm, d), row), pl.BlockSpec((1, d), fix), pl.BlockSpec((d, n_in), fix),
                  pl.BlockSpec((tm, LANES), row), pl.BlockSpec((tm, LANES), row), pl.BlockSpec((tm, LANES), row)],
        out_specs=[pl.BlockSpec((tm, n_da), row), pl.BlockSpec((tm, n_in - n_da), row)],
        out_shape=[jax.ShapeDtypeStruct((n, n_da), BF16), jax.ShapeDtypeStruct((n, n_in - n_da), F32)],
        compiler_params=_params("parallel"),
    )(x2, g, w_in_bf, rc, rs1, rs2)


def _attn_kernel(lam_ref, q_ref, k_ref, v_ref, sg_ref, o_ref, m_ref, acc_ref, *, tq):
    i = pl.program_id(2)
    q = q_ref[...]
    lane = lax.broadcasted_iota(jnp.int32, q.shape, 1)
    zero = jnp.zeros_like(q)
    qs = (jnp.where(lane < DA_HEAD_DIM, q, zero), jnp.where(lane >= DA_HEAD_DIM, q, zero))
    m_ref[...] = jnp.full(m_ref.shape, -jnp.inf, F32)
    acc_ref[...] = jnp.zeros(acc_ref.shape, F32)
    ones = jnp.ones((tq, LANES), BF16)

    def block(j, masked):
        kj = k_ref[pl.ds(pl.multiple_of(j * tq, tq), tq), :]
        vj = jnp.concatenate([v_ref[pl.ds(pl.multiple_of(j * tq, tq), tq), :], ones], axis=1)
        for c in range(2):
            s = lax.dot_general(qs[c], kj, NT, preferred_element_type=F32)
            if masked:
                r_id = lax.broadcasted_iota(jnp.int32, s.shape, 0)
                c_id = lax.broadcasted_iota(jnp.int32, s.shape, 1)
                s = jnp.where(c_id <= r_id, s, -jnp.inf)
            m_old = m_ref[c]
            m_new = jnp.maximum(m_old, jnp.max(s, axis=-1, keepdims=True))
            alpha = jnp.exp2(m_old - m_new)
            p = jnp.exp2(s - jnp.tile(m_new, (1, tq // LANES)))
            pv = jnp.dot(p.astype(BF16), vj, preferred_element_type=F32)
            acc_ref[c] = jnp.tile(alpha, (1, 2)) * acc_ref[c] + pv
            m_ref[c] = m_new

    def body(j, carry):
        block(j, False)
        return carry

    lax.fori_loop(0, i, body, 0)
    block(i, True)
    lam = lam_ref[0, 0]
    a0, a1 = acc_ref[0], acc_ref[1]
    o = a0[:, :LANES] / a0[:, LANES:] - lam * (a1[:, :LANES] / a1[:, LANES:])
    o = o * lax.rsqrt(jnp.mean(o * o, axis=-1, keepdims=True) + NORM_EPS) * sg_ref[...] * (1.0 - LAM_INIT)
    o_ref[...] = o.astype(o_ref.dtype)


def _attention(qkv, lam, subln_g, batch, seq, tq):
    n = qkv.shape[0]
    nq = seq // tq
    h = DA_HEADS
    return pl.pallas_call(
        functools.partial(_attn_kernel, tq=tq),
        grid=(batch, h, nq),
        in_specs=[pl.BlockSpec(memory_space=pltpu.SMEM),
                  pl.BlockSpec((tq, LANES), lambda b, hh, i: (b * nq + i, hh)),
                  pl.BlockSpec((seq, LANES), lambda b, hh, i: (b, h + hh)),
                  pl.BlockSpec((seq, LANES), lambda b, hh, i: (b, 2 * h + hh)),
                  pl.BlockSpec((1, LANES), lambda b, hh, i: (0, 0))],
        out_specs=pl.BlockSpec((tq, LANES), lambda b, hh, i: (b * nq + i, hh)),
        out_shape=jax.ShapeDtypeStruct((n, h * LANES), BF16),
        scratch_shapes=[pltpu.VMEM((2, tq, LANES), F32), pltpu.VMEM((2, tq, 2 * LANES), F32)],
        compiler_params=_params("parallel", "parallel", "arbitrary"),
    )(lam, qkv, qkv, qkv, subln_g)


def _rwprep_kernel(z_ref, zp_ref, mu_ref, w0_ref, wup_ref, a0_ref, aup_ref, gup_ref, kk_ref, ka_ref, bd_ref,
                   r_o, ld_o, k_o, v_o, kk_o, b_o, g_o, *, tiles_per_seq, width):
    i = pl.program_id(0)
    z = z_ref[...]
    last = zp_ref[7:8, :]
    first = jnp.where(i % tiles_per_seq == 0, jnp.zeros_like(last), last)
    row = lax.broadcasted_iota(jnp.int32, z.shape, 0)
    prev = jnp.where(row == 0, first, pltpu.roll(z, 1, 0))
    zs = z + (prev - z) * mu_ref[...]
    r = zs[:, 0:width]
    k = zs[:, width:2 * width]
    v = zs[:, 2 * width:3 * width]
    xwa = zs[:, 3 * width:3 * width + LANES]
    xg = zs[:, 3 * width + LANES:3 * width + 2 * LANES]
    w = -jax.nn.softplus(-(w0_ref[...] + _mm_hi(jnp.tanh(xwa), wup_ref[...]))) - 0.5
    a = jax.nn.sigmoid(a0_ref[...] + _mm_hi(xwa, aup_ref[...]))
    g = _mm_hi(jax.nn.sigmoid(xg), gup_ref[...])
    kk = k * kk_ref[...]
    kk = kk / jnp.maximum(jnp.sqrt(_mm_hi(kk * kk, bd_ref[...])), 1e-12)
    r_o[...] = r
    ld_o[...] = -jnp.exp(w)
    k_o[...] = k * (1.0 + (a - 1.0) * ka_ref[...])
    v_o[...] = v
    kk_o[...] = kk
    b_o[...] = kk * a
    g_o[...] = g


def _rwprep(zrw, mu, w0, wup_pad, a0, aup_pad, gup, k_k, k_a, bd, seq, tm):
    n, zin = zrw.shape
    width = w0.shape[1]
    row = lambda i: (i, 0)
    fix = lambda i: (0, 0)
    prev = lambda i: (jnp.maximum(i * (tm // 8) - 1, 0), 0)
    out = jax.ShapeDtypeStruct((n, width), F32)
    return pl.pallas_call(
        functools.partial(_rwprep_kernel, tiles_per_seq=seq // tm, width=width),
        grid=(n // tm,),
        in_specs=[pl.BlockSpec((tm, zin), row), pl.BlockSpec((8, zin), prev), pl.BlockSpec((1, zin), fix),
                  pl.BlockSpec((1, width), fix), pl.BlockSpec((LANES, width), fix),
                  pl.BlockSpec((1, width), fix), pl.BlockSpec((LANES, width), fix), pl.BlockSpec((LANES, width), fix),
                  pl.BlockSpec((1, width), fix), pl.BlockSpec((1, width), fix), pl.BlockSpec((width, width), fix)],
        out_specs=[pl.BlockSpec((tm, width), row)] * 7,
        out_shape=[out] * 7,
        compiler_params=_params("parallel"),
    )(zrw, zrw, mu, w0, wup_pad, a0, aup_pad, gup, k_k, k_a, bd)


def _rwcore_kernel(r_ref, ld_ref, k_ref, v_ref, kk_ref, b_ref, g_ref, lng_ref, lnb_ref, rk_ref, o_ref, s_ref, *,
                   units, groups):
    U, C, HD = RW_UNIT, RW_CHUNK, RW_HEAD

    @pl.when(pl.program_id(2) == 0)
    def _():
        s_ref[...] = jnp.zeros(s_ref.shape, F32)

    ri = lax.broadcasted_iota(jnp.int32, (U, U), 0)
    ci = lax.broadcasted_iota(jnp.int32, (U, U), 1)
    same = (ri // C) == (ci // C)
    tri_s = same & (ci < ri)
    tri_i = same & (ci <= ri)
    eye = (ri == ci).astype(F32)
    cum_w = tri_i.astype(F32)
    head_avg = same.astype(F32) * (1.0 / HD)
    head_sum = same.astype(F32)
    hmask = (ci < HD, ci >= HD)
    cmask = (ri < C, ri >= C)
    zero = jnp.zeros((U, U), F32)
    cells = [(u, q) for u in range(units) for q in range(groups)]

    def blk(ref, cell):
        u, q = cell
        return ref[pl.ds(u * U, U), q * LANES:(q + 1) * LANES]

    def vec(ref, q):
        return ref[:, q * LANES:(q + 1) * LANES]

    cell_v, cell_at, cell_bt, cell_rt, cell_bk, cell_gam = {}, {}, {}, {}, {}, {}
    for cell in cells:
        ld = blk(ld_ref, cell)
        cum = _mm_split(cum_w, ld, NN, 1, 3)
        gam = jnp.exp(cum)
        ginv = jnp.exp(-cum)
        bt = blk(b_ref, cell) * ginv
        cell_v[cell] = blk(v_ref, cell)
        cell_at[cell] = -blk(kk_ref, cell) * jnp.exp(cum - ld)
        cell_bt[cell] = bt
        cell_rt[cell] = blk(r_ref, cell) * gam
        cell_bk[cell] = jnp.concatenate([bt, blk(k_ref, cell) * ginv], axis=0)
        cell_gam[cell] = gam

    chains = [(cell, h) for cell in cells for h in range(2)]
    mab, mak, mrb, mrk = {}, {}, {}, {}
    for ch in chains:
        cell, h = ch
        ar = jnp.concatenate([jnp.where(hmask[h], cell_at[cell], zero), jnp.where(hmask[h], cell_rt[cell], zero)], axis=0)
        m = _mm_nt(ar, cell_bk[cell])
        mab[ch] = jnp.where(tri_s, m[0:U, 0:U], zero)
        mak[ch] = jnp.where(tri_s, m[0:U, U:2 * U], zero)
        mrb[ch] = jnp.where(tri_i, m[U:2 * U, 0:U], zero)
        mrk[ch] = jnp.where(tri_i, m[U:2 * U, U:2 * U], zero)

    tm = {ch: eye + mab[ch] for ch in chains}
    pw = dict(mab)
    for _ in range(int(math.log2(C)) - 1):
        pw = {ch: _mm(pw[ch], pw[ch]) for ch in chains}
        tm = {ch: tm[ch] + _mm(tm[ch], pw[ch]) for ch in chains}
    aph = {ch: _mm(tm[ch], cell_at[ch[0]]) for ch in chains}
    mv = {ch: _mm(mak[ch], cell_v[ch[0]]) for ch in chains}
    uh = {ch: _mm(tm[ch], mv[ch]) for ch in chains}
    rph = {ch: _mm(mrb[ch], aph[ch]) for ch in chains}
    yph = {ch: _mm(mrb[ch], uh[ch]) + _mm(mrk[ch], cell_v[ch[0]]) for ch in chains}

    def both_heads(d, cell):
        return jnp.where(hmask[0], d[(cell, 0)], d[(cell, 1)])

    rp, yp, gs, hs = {}, {}, {}, {}
    for cell in cells:
        ap, uu = both_heads(aph, cell), both_heads(uh, cell)
        rp[cell] = both_heads(rph, cell) + cell_rt[cell]
        yp[cell] = both_heads(yph, cell)
        for c in range(2):
            gl = cell_gam[cell][(c + 1) * C - 1:(c + 1) * C, :]
            apc = jnp.where(cmask[c], ap, zero)
            uvc = jnp.concatenate([jnp.where(cmask[c], uu, zero), jnp.where(cmask[c], cell_v[cell], zero)], axis=0)
            gs[cell, c] = jnp.where(same, eye + _mm_tn(apc, cell_bt[cell]), zero) * gl
            hs[cell, c] = jnp.where(same, _mm_tn(uvc, cell_bk[cell]), zero) * gl

    s = [s_ref[q] for q in range(groups)]
    ys = {}
    for u in range(units):
        for q in range(groups):
            cell = (u, q)
            y0 = _mm_split(rp[cell], s[q], NT, 2, 2)
            s[q] = _mm_split(s[q], gs[cell, 0], NN, 2, 2) + hs[cell, 0]
            y1 = _mm_split(rp[cell], s[q], NT, 2, 2)
            s[q] = _mm_split(s[q], gs[cell, 1], NN, 2, 2) + hs[cell, 1]
            ys[cell] = jnp.where(cmask[0], y0, y1) + yp[cell]
    for q in range(groups):
        s_ref[q] = s[q]

    for cell in cells:
        u, q = cell
        y = ys[cell]
        mean = _mm_split(y, head_avg, NN, 2, 1)
        yc = y - mean
        var = _mm_split(yc * yc, head_avg, NN, 2, 1)
        yn = yc * lax.rsqrt(var + RW_GN_EPS) * vec(lng_ref, q) + vec(lnb_ref, q)
        bonus = _mm_split(blk(r_ref, cell) * blk(k_ref, cell) * vec(rk_ref, q), head_sum, NN, 2, 1)
        yn = yn + bonus * cell_v[cell]
        o_ref[pl.ds(u * U, U), q * LANES:(q + 1) * LANES] = (yn * blk(g_ref, cell)).astype(o_ref.dtype)


def _rwcore(r, ld, k, v, kk, b, g, ln_g, ln_b, r_k, batch, seq, units, groups):
    n, width = r.shape
    rows = units * RW_UNIT
    steps = seq // rows
    lanes = groups * LANES
    blk = pl.BlockSpec((rows, lanes), lambda bb, hp, i: (bb * steps + i, hp))
    vec = pl.BlockSpec((1, lanes), lambda bb, hp, i: (0, hp))
    return pl.pallas_call(
        functools.partial(_rwcore_kernel, units=units, groups=groups),
        grid=(batch, width // lanes, steps),
        in_specs=[blk] * 7 + [vec] * 3,
        out_specs=blk,
        out_shape=jax.ShapeDtypeStruct((n, width), BF16),
        scratch_shapes=[pltpu.VMEM((groups, RW_UNIT, RW_UNIT), F32)],
        compiler_params=_params("parallel", "parallel", "arbitrary"),
    )(r, ld, k, v, kk, b, g, ln_g, ln_b, r_k)


def _outproj_kernel(x_ref, oda_ref, orw_ref, wa_ref, wb_ref, g_ref, h_ref, u_ref):
    h = (x_ref[...] + jnp.dot(oda_ref[...], wa_ref[...], preferred_element_type=F32)
         + jnp.dot(orw_ref[...], wb_ref[...], preferred_element_type=F32))
    h_ref[...] = h
    u_ref[...] = _rms(h, g_ref[...]).astype(BF16)


def _outproj(x2, o_da, o_rw, wa, wb, g, tm):
    n, d = x2.shape
    da = o_da.shape[1]
    rw = o_rw.shape[1]
    row = lambda i: (i, 0)
    fix = lambda i: (0, 0)
    return pl.pallas_call(
        _outproj_kernel,
        grid=(n // tm,),
        in_specs=[pl.BlockSpec((tm, d), row), pl.BlockSpec((tm, da), row), pl.BlockSpec((tm, rw), row),
                  pl.BlockSpec((da, d), fix), pl.BlockSpec((rw, d), fix), pl.BlockSpec((1, d), fix)],
        out_specs=[pl.BlockSpec((tm, d), row), pl.BlockSpec((tm, d), row)],
        out_shape=[jax.ShapeDtypeStruct((n, d), F32), jax.ShapeDtypeStruct((n, d), BF16)],
        compiler_params=_params("parallel"),
    )(x2, o_da, o_rw, wa, wb, g)


def _topk_rows(s, k, payload=None):
    rows = s.shape[0]
    iota = lax.broadcasted_iota(jnp.int32, s.shape, 0).astype(F32)
    vals, sel = [], []
    for _ in range(k):
        m = jnp.max(s, axis=0, keepdims=True)
        am = jnp.min(jnp.where(s == m, iota, float(rows)), axis=0, keepdims=True)
        hit = iota == am
        vals.append(m)
        sel.append(am if payload is None else jnp.sum(jnp.where(hit, payload, 0.0), axis=0, keepdims=True))
        s = jnp.where(hit, -jnp.inf, s)
    return vals, sel


def _stack_rows(rows_list):
    k = len(rows_list)
    iota = lax.broadcasted_iota(jnp.int32, (k, rows_list[0].shape[1]), 0)
    out = jnp.zeros(iota.shape, rows_list[0].dtype)
    for j, r in enumerate(rows_list):
        out = jnp.where(iota == j, r, out)
    return out


def _peertopk_kernel(u_ref, wq_ref, keys_ref, idx_ref, gate_ref):
    u = u_ref[...]
    half = N_KEYS
    idx_rows, gate_rows = [], []
    for h in range(PEER_HEADS):
        tops = []
        for p in range(2):
            hp = h * 2 + p
            q_t = lax.dot_general(wq_ref[hp * half:(hp + 1) * half, :], u, NT, preferred_element_type=F32)
            s_t = jnp.dot(keys_ref[hp], q_t.astype(BF16), preferred_element_type=F32)
            tops.append(_topk_rows(s_t, PEER_TOPK))
        (v1, i1), (v2, i2) = tops
        pairs = [(i, j) for i in range(PEER_TOPK) for j in range(PEER_TOPK) if (i + 1) * (j + 1) <= PEER_TOPK]
        pad = -len(pairs) % 8
        cand = _stack_rows([v1[i] + v2[j] for i, j in pairs] + [jnp.full_like(v1[0], -jnp.inf)] * pad)
        cidx = _stack_rows([i1[i] * float(N_KEYS) + i2[j] for i, j in pairs] + [jnp.zeros_like(i1[0])] * pad)
        best, idx = _topk_rows(cand, PEER_TOPK, payload=cidx)
        e = [jnp.exp(b - best[0]) for b in best]
        den = functools.reduce(lambda a, b: a + b, e)
        idx_rows.append(_stack_rows(idx))
        gate_rows.append(_stack_rows([x / den for x in e]))
    idx_ref[...] = jnp.concatenate(idx_rows, axis=0).T.astype(jnp.int32)
    gate_ref[...] = jnp.concatenate(gate_rows, axis=0).T


def _peertopk(u_bf, wq_t, keys, tt):
    n, d = u_bf.shape
    hk = PEER_HEADS * PEER_TOPK
    row = lambda i: (i, 0)
    return pl.pallas_call(
        _peertopk_kernel,
        grid=(n // tt,),
        in_specs=[pl.BlockSpec((tt, d), row), pl.BlockSpec(wq_t.shape, lambda i: (0, 0)),
                  pl.BlockSpec(keys.shape, lambda i: (0, 0, 0))],
        out_specs=[pl.BlockSpec((tt, hk), row), pl.BlockSpec((tt, hk), row)],
        out_shape=[jax.ShapeDtypeStruct((n, hk), jnp.int32), jax.ShapeDtypeStruct((n, hk), F32)],
        compiler_params=_params("parallel"),
    )(u_bf, wq_t, keys)


SC_CORES = 2
SC_SUBCORES = 16
SC_LANES = 16
GATHER_ROWS = 32
GATHER_SLOTS = 6
GATHER_INDEX_WINDOW = 1024
HK = PEER_HEADS * PEER_TOPK
SUM_ROWS = 16
SUM_TOKENS = 8
SUM_UNROLL = 2


def _sc_mesh():
    return plsc.VectorSubcoreMesh(core_axis_name="core", subcore_axis_name="subcore")


def _sc_worker():
    return lax.axis_index("core") * SC_SUBCORES + lax.axis_index("subcore")


def _gather_rows(tab, idx_flat):
    p = idx_flat.shape[0]
    w = tab.shape[1]
    workers = SC_CORES * SC_SUBCORES
    per_worker = p // workers
    iw = min(GATHER_INDEX_WINDOW, per_worker)
    r = GATHER_ROWS
    ns = GATHER_SLOTS
    lag = ns - 1
    nsub = iw // r
    assert p % workers == 0 and per_worker % iw == 0 and iw % r == 0 and nsub >= ns
    buf = pltpu.VMEM((r, w), tab.dtype)
    sem = pltpu.SemaphoreType.DMA

    @pl.kernel(out_type=jax.ShapeDtypeStruct((p, w), tab.dtype), mesh=_sc_mesh(),
               scratch_types=[pltpu.VMEM((iw,), jnp.int32)] + [buf] * ns + [sem] * (2 * ns))
    def gather(t_hbm, i_hbm, o_hbm, idx_v, *scratch):
        bufs, gsem, wsem = scratch[0:ns], scratch[ns:2 * ns], scratch[2 * ns:3 * ns]
        wid = _sc_worker()

        @pl.loop(0, per_worker // iw)
        def _(o):
            start = wid * per_worker + o * iw
            pltpu.sync_copy(i_hbm.at[pl.ds(start, iw)], idx_v)

            def gather_of(j):
                s = j % ns
                return pltpu.make_async_copy(t_hbm.at[idx_v.at[pl.ds(j * r, r)]], bufs[s], gsem[s])

            def write_of(j):
                s = j % ns
                return pltpu.make_async_copy(bufs[s], o_hbm.at[pl.ds(start + j * r, r)], wsem[s])

            for j in range(nsub + lag):
                if j < nsub:
                    if j >= ns:
                        write_of(j - ns).wait()
                    gather_of(j).start()
                if j >= lag:
                    gather_of(j - lag).wait()
                    write_of(j - lag).start()
            for j in range(nsub - ns, nsub):
                write_of(j).wait()

    return gather(tab, idx_flat)


def _weighted_row_sum(tab, idx_flat, wrep):
    p = idx_flat.shape[0]
    w = tab.shape[1]
    t_total = p // HK
    workers = SC_CORES * SC_SUBCORES
    tpw = t_total // workers
    g = min(SUM_TOKENS, tpw)
    r = SUM_ROWS
    ns = HK // r
    ln = SC_LANES
    per_row = LANES // ln
    assert t_total % workers == 0 and tpw % g == 0 and g % 2 == 0 and w % ln == 0
    buf = pltpu.VMEM((r, w), tab.dtype)
    acc = pltpu.VMEM((2 * w,), F32)
    sem = pltpu.SemaphoreType.DMA

    @pl.kernel(out_type=jax.ShapeDtypeStruct((t_total, 2 * w), F32), mesh=_sc_mesh(),
               scratch_types=[pltpu.VMEM((g * HK,), jnp.int32), pltpu.VMEM((g * HK // per_row, LANES), F32), acc, acc]
               + [buf] * ns + [sem] * (ns + 2),
               compiler_params=pltpu.CompilerParams(needs_layout_passes=False))
    def wsum(t_hbm, i_hbm, w_hbm, o_hbm, idx_v, w_v, acc0, acc1, *scratch):
        bufs, gsem, osem = scratch[0:ns], scratch[ns:2 * ns], scratch[2 * ns:2 * ns + 2]
        accs = (acc0, acc1)
        wid = _sc_worker()
        zero = jnp.zeros((ln,), F32)

        @pl.loop(0, tpw // g)
        def _(win):
            tok0 = wid * tpw + win * g
            pltpu.sync_copy(i_hbm.at[pl.ds(tok0 * HK, g * HK)], idx_v)
            pltpu.sync_copy(w_hbm.at[pl.ds(tok0 * (HK // per_row), g * HK // per_row)], w_v)

            def gather_of(tl, s):
                return pltpu.make_async_copy(t_hbm.at[idx_v.at[pl.ds(tl * HK + s * r, r)]], bufs[s], gsem[s])

            def out_of(tl, par):
                return pltpu.make_async_copy(accs[par], o_hbm.at[tok0 + tl], osem[par])

            for s in range(ns):
                gather_of(0, s).start()

            @pl.loop(0, g // 2)
            def _(tp):
                for par in range(2):
                    tl = tp * 2 + par
                    ob = accs[par]

                    @pl.when(tl >= 2)
                    def _():
                        out_of(tl - 2, par).wait()

                    for c in range(2 * w // ln):
                        ob[pl.ds(c * ln, ln)] = zero
                    for s in range(ns):
                        gather_of(tl, s).wait()
                        wrow = tl * (HK // per_row) + s * (r // per_row)
                        wk = []
                        for q in range(r):
                            bits = plsc.bitcast(w_v[wrow + q // per_row, pl.ds((q % per_row) * ln, ln)], jnp.uint32)
                            wk.append(plsc.bitcast(bits | lax.shift_right_logical(bits, jnp.uint32(16)), BF16))
                        rows = bufs[s]

                        def fold(c, carry):
                            for half in range(SUM_UNROLL):
                                col = pl.multiple_of((c * SUM_UNROLL + half) * ln, ln)
                                lo_acc = hi_acc = None
                                for q in range(0, r, 2):
                                    pa = plsc.bitcast(rows[q, pl.ds(col, ln)], BF16) * wk[q]
                                    pb = plsc.bitcast(rows[q + 1, pl.ds(col, ln)], BF16) * wk[q + 1]
                                    pair = plsc.bitcast(pa + pb, jnp.uint32)
                                    lo = plsc.bitcast(lax.shift_left(pair, jnp.uint32(16)), F32)
                                    hi = plsc.bitcast(pair & jnp.uint32(0xFFFF0000), F32)
                                    lo_acc = lo if lo_acc is None else lo_acc + lo
                                    hi_acc = hi if hi_acc is None else hi_acc + hi
                                plsc.addupdate(ob.at[pl.ds(col, ln)], lo_acc)
                                plsc.addupdate(ob.at[pl.ds(w + col, ln)], hi_acc)
                            return carry

                        lax.fori_loop(0, w // (ln * SUM_UNROLL), fold, 0)

                        @pl.when(tl + 1 < g)
                        def _():
                            gather_of(tl + 1, s).start()

                    out_of(tl, par).start()

            for par in range(2):
                out_of(g - 2 + par, par).wait()

    return wsum(tab, idx_flat, wrep)


def _peerw_kernel(xa_ref, xb_ref, gate_ref, ug_ref, rep_ref, o_ref, *, tt):
    hk2 = 2 * HK
    x2 = jnp.concatenate([xa_ref[...], xb_ref[...]], axis=0)
    lane = lax.broadcasted_iota(jnp.int32, (tt, hk2), 1)
    row = lax.broadcasted_iota(jnp.int32, (tt, hk2), 0)
    even = (lane % 2) == 0
    part = jnp.zeros((tt, hk2), F32)
    for t in range(tt):
        ub = pltpu.bitcast(ug_ref[t * HK:(t + 1) * HK, :], BF16)
        r = lax.dot_general(x2, ub, NT, preferred_element_type=F32)
        part = jnp.where(row == t, jnp.where(even, r[0:tt], r[tt:2 * tt]), part)
    hid = part + jnp.where(even, pltpu.roll(part, hk2 - 1, 1), pltpu.roll(part, 1, 1))
    w = gate_ref[...] * (0.5 * hid * (1.0 + lax.erf(hid * (2.0 ** -0.5))))
    o_ref[...] = jnp.dot(w.astype(BF16), rep_ref[...], preferred_element_type=F32)


def _peerw(xa, xb, gate2, ug, rep, tt):
    n, dh = xa.shape
    row = lambda i: (i, 0)
    return pl.pallas_call(
        functools.partial(_peerw_kernel, tt=tt),
        grid=(n // tt,),
        in_specs=[pl.BlockSpec((tt, dh), row), pl.BlockSpec((tt, dh), row), pl.BlockSpec((tt, 2 * HK), row),
                  pl.BlockSpec((tt * HK, dh), row), pl.BlockSpec(rep.shape, lambda i: (0, 0))],
        out_specs=pl.BlockSpec((tt, HK * SC_LANES), row),
        out_shape=jax.ShapeDtypeStruct((n, HK * SC_LANES), F32),
        compiler_params=_params("parallel"),
    )(xa, xb, gate2, ug, rep)


def _repeat_matrix():
    src = jnp.arange(2 * HK)[:, None]
    dst = jnp.arange(HK * SC_LANES)[None, :]
    return (src == 2 * (dst // SC_LANES)).astype(BF16)


def _ple_kernel(h_ref, f_ref, p_ref, g_ref, wg_ref, wp_ref, gf_ref, o_ref):
    h = h_ref[...] + f_ref[...]
    gate = jax.nn.sigmoid(jnp.dot(_rms(h, g_ref[...]).astype(BF16), wg_ref[...], preferred_element_type=F32))
    pp = jnp.dot(p_ref[...].astype(BF16), wp_ref[...], preferred_element_type=F32)
    o_ref[...] = _rms(h + gate * pp, gf_ref[...])


def _ple(h1, ffn, p2, g, wg, wp, gf, tm):
    n, d = h1.shape
    pd = p2.shape[1]
    row = lambda i: (i, 0)
    fix = lambda i: (0, 0)
    return pl.pallas_call(
        _ple_kernel,
        grid=(n // tm,),
        in_specs=[pl.BlockSpec((tm, d), row), pl.BlockSpec((tm, d), row), pl.BlockSpec((tm, pd), row),
                  pl.BlockSpec((1, d), fix), pl.BlockSpec((d, d), fix), pl.BlockSpec((pd, d), fix),
                  pl.BlockSpec((1, d), fix)],
        out_specs=pl.BlockSpec((tm, d), row),
        out_shape=jax.ShapeDtypeStruct((n, d), F32),
        compiler_params=_params("parallel"),
    )(h1, ffn, p2, g, wg, wp, gf)


def _rope_tables(positions):
    half = ROT_DIM // 2
    inv_freq = ROPE_THETA ** (-jnp.arange(half, dtype=F32) * 2.0 / ROT_DIM)
    ang = positions.astype(F32).reshape(-1, 1) * inv_freq
    lane = jnp.arange(LANES)
    d = lane % DA_HEAD_DIM
    cos = jnp.take(jnp.cos(ang), d % half, axis=1)
    sin = jnp.take(jnp.sin(ang), d % half, axis=1)
    c = jnp.where(d < ROT_DIM, cos, 1.0)
    s1 = jnp.where(d < half, -sin, 0.0)
    s2 = jnp.where((d >= half) & (d < ROT_DIM), sin, 0.0)
    return c, s1, s2


def _pack_rows(tab):
    d = tab.shape[1]
    t = tab.astype(BF16)
    pair = jnp.stack([t[:, :d // 2], t[:, d // 2:]], axis=-1)
    return lax.bitcast_convert_type(pair, jnp.uint32)


def _block_diag_ones(width, head):
    i = jnp.arange(width)
    return (i[:, None] // head == i[None, :] // head).astype(F32)


def _tiles(seq):
    return dict(tm=min(256, seq), tq=min(512, seq), units=min(4, seq // RW_UNIT), groups=2, tt_topk=min(256, seq), tt_mix=16,
                peer_chunks=2 if seq % 4096 == 0 else 1)


def kernel(x, p, positions, norm_mix_g, w_in, lam_q1, lam_k1, lam_q2, lam_k2, da_subln_g, rw_mu, rw_w0, rw_w_up, rw_a0, rw_a_up, rw_g_up, rw_k_k, rw_k_a, rw_r_k, rw_ln_g, rw_ln_b, w_out, norm_ffn_g, peer_w_q, peer_sub_keys, peer_u, peer_v, norm_ple_g, ple_gate_w, ple_proj_w, norm_final_g):
    batch, seq, d = x.shape
    t = _tiles(seq)
    row = lambda a: a.reshape(1, -1)
    f32 = F32

    w_in_bf = w_in[0].astype(BF16)
    lam = (jnp.exp(jnp.sum(lam_q1[0].astype(f32) * lam_k1[0].astype(f32)))
           - jnp.exp(jnp.sum(lam_q2[0].astype(f32) * lam_k2[0].astype(f32))) + LAM_INIT).reshape(1, 1)
    width = rw_w0.shape[1]
    wup_pad = jnp.concatenate([rw_w_up[0], jnp.zeros((LANES - rw_w_up.shape[1], width), f32)], axis=0)
    aup_pad = jnp.concatenate([jnp.zeros((LANES - rw_a_up.shape[1], width), f32), rw_a_up[0]], axis=0)
    head_ones = _block_diag_ones(width, RW_HEAD)
    w_out_bf = w_out[0].astype(BF16)
    da_w = DA_HEADS * 2 * DA_HEAD_DIM
    keys = peer_sub_keys[0].reshape(PEER_HEADS * 2, N_KEYS, -1).astype(BF16)
    wq_t = peer_w_q[0].T.astype(BF16)
    u_tab = _pack_rows(peer_u[0])
    v_tab = _pack_rows(peer_v[0])
    wg_bf = ple_gate_w[0].astype(BF16)
    wp_bf = ple_proj_w[0].astype(BF16)
    rep = _repeat_matrix()

    nc = seq // t["peer_chunks"]
    chunks = [slice(c * nc, (c + 1) * nc) for c in range(t["peer_chunks"])]

    outs = []
    for b in range(batch):
        xs = x[b]
        rc, rs1, rs2 = _rope_tables(positions[b])
        qkv, zrw = _inproj(xs, row(norm_mix_g[0]), w_in_bf, rc, rs1, rs2, t["tm"])
        o_da = _attention(qkv, lam, row(da_subln_g[0]), 1, seq, t["tq"])
        rw = _rwprep(zrw, row(rw_mu[0]), row(rw_w0[0]), wup_pad, row(rw_a0[0]), aup_pad, rw_g_up[0],
                     row(rw_k_k[0]), row(rw_k_a[0]), head_ones, seq, t["tm"])
        o_rw = _rwcore(*rw, row(rw_ln_g[0]), row(rw_ln_b[0]), row(rw_r_k[0]), 1, seq, t["units"], t["groups"])
        h1, u2 = _outproj(xs, o_da, o_rw, w_out_bf[:da_w], w_out_bf[da_w:], row(norm_ffn_g[0]), t["tm"])
        idx, gate = _peertopk(u2, wq_t, keys, t["tt_topk"])
        gate2 = jnp.repeat(gate, 2, axis=1)
        xa, xb = u2[:, :d // 2], u2[:, d // 2:]
        ffn = []
        for sl in chunks:
            idx_c = idx[sl].reshape(-1)
            ug = _gather_rows(u_tab, idx_c)
            wrep = _peerw(xa[sl], xb[sl], gate2[sl], ug, rep, t["tt_mix"])
            ffn.append(_weighted_row_sum(v_tab, idx_c, wrep.reshape(-1, LANES)))
        ffn = jnp.concatenate(ffn, axis=0) if len(ffn) > 1 else ffn[0]
        outs.append(_ple(h1, ffn, p[0, b], row(norm_ple_g[0]), wg_bf, wp_bf, row(norm_final_g), t["tm"]))
    return jnp.stack(outs, axis=0)
```

```python
import functools
import math

import jax
import jax.numpy as jnp
from jax import lax
from jax.experimental import pallas as pl
from jax.experimental.pallas import tpu as pltpu
from jax.experimental.pallas import tpu_sc as plsc

F32 = jnp.float32
BF16 = jnp.bfloat16

NORM_EPS = 1e-6
DA_HEADS = 4
DA_HEAD_DIM = 64
ROPE_THETA = 500000.0
ROT_DIM = DA_HEAD_DIM // 4
RW_HEAD = 64
RW_GN_EPS = 64e-5
PEER_HEADS = 8
N_KEYS = 128
PEER_TOPK = 16
LAM_INIT = 0.8 - 0.6 * math.exp(-0.3 * 0)

LANES = 128
VMEM_LIMIT = 56 * 1024 * 1024
RW_CHUNK = 64
RW_UNIT = 2 * RW_CHUNK

NT = (((1,), (1,)), ((), ()))
TN = (((0,), (0,)), ((), ()))
HI = lax.Precision.HIGHEST


def _mm(a, b):
    return jnp.dot(a.astype(BF16), b.astype(BF16), preferred_element_type=F32)


def _mm_nt(a, b):
    return lax.dot_general(a.astype(BF16), b.astype(BF16), NT, preferred_element_type=F32)


def _mm_tn(a, b):
    return lax.dot_general(a.astype(BF16), b.astype(BF16), TN, preferred_element_type=F32)


def _mm_hi(a, b):
    return jnp.dot(a, b, precision=HI, preferred_element_type=F32)


def _pieces(a, n):
    out = []
    for _ in range(n):
        piece = a.astype(BF16)
        out.append(piece)
        a = a - piece.astype(F32)
    return out


def _mm_split(a, b, dims, a_pieces, b_pieces):
    ap, bp = _pieces(a, a_pieces), _pieces(b, b_pieces)
    terms = [lax.dot_general(x, y, dims, preferred_element_type=F32)
             for i, x in enumerate(ap) for j, y in enumerate(bp) if i + j < max(a_pieces, b_pieces)]
    return functools.reduce(lambda u, v: u + v, terms)


NN = (((1,), (0,)), ((), ()))


def _params(*sem):
    return pltpu.CompilerParams(dimension_semantics=sem, vmem_limit_bytes=VMEM_LIMIT)


def _rms(x, g):
    return x * lax.rsqrt(jnp.mean(x * x, axis=-1, keepdims=True) + NORM_EPS) * g


def _inproj_kernel(x_ref, g_ref, w_ref, c_ref, s1_ref, s2_ref, qkv_ref, zrw_ref, *, n_qk, n_da):
    u = _rms(x_ref[...], g_ref[...]).astype(BF16)
    z = jnp.dot(u, w_ref[...], preferred_element_type=F32)
    c, s1, s2 = c_ref[...], s1_ref[...], s2_ref[...]
    half = ROT_DIM // 2
    for blk in range(n_da // LANES):
        t = z[:, blk * LANES:(blk + 1) * LANES]
        if blk < 2 * n_qk // LANES:
            t = t * c + pltpu.roll(t, LANES - half, 1) * s1 + pltpu.roll(t, half, 1) * s2
        if blk < n_qk // LANES:
            t = t * (DA_HEAD_DIM ** -0.5 * math.log2(math.e))
        qkv_ref[:, blk * LANES:(blk + 1) * LANES] = t.astype(BF16)
    zrw_ref[...] = z[:, n_da:]


def _inproj(x2, g, w_in_bf, rc, rs1, rs2, tm):
    n, d = x2.shape
    n_in = w_in_bf.shape[1]
    n_qk = DA_HEADS * 2 * DA_HEAD_DIM
    n_da = 3 * n_qk
    row = lambda i: (i, 0)
    fix = lambda i: (0, 0)
    return pl.pallas_call(
        functools.partial(_inproj_kernel, n_qk=n_qk, n_da=n_da),
        grid=(n // tm,),
        in_specs=[pl.BlockSpec((tm, d), row), pl.BlockSpec((1, d), fix), pl.BlockSpec((d, n_in), fix),
                  pl.BlockSpec((tm, LANES), row), pl.BlockSpec((tm, LANES), row), pl.BlockSpec((tm, LANES), row)],
        out_specs=[pl.BlockSpec((tm, n_da), row), pl.BlockSpec((tm, n_in - n_da), row)],
        out_shape=[jax.ShapeDtypeStruct((n, n_da), BF16), jax.ShapeDtypeStruct((n, n_in - n_da), F32)],
        compiler_params=_params("parallel"),
    )(x2, g, w_in_bf, rc, rs1, rs2)


def _attn_kernel(lam_ref, q_ref, k_ref, v_ref, sg_ref, o_ref, m_ref, acc_ref, *, tq):
    i = pl.program_id(2)
    q = q_ref[...]
    lane = lax.broadcasted_iota(jnp.int32, q.shape, 1)
    zero = jnp.zeros_like(q)
    qs = (jnp.where(lane < DA_HEAD_DIM, q, zero), jnp.where(lane >= DA_HEAD_DIM, q, zero))
    m_ref[...] = jnp.full(m_ref.shape, -jnp.inf, F32)
    acc_ref[...] = jnp.zeros(acc_ref.shape, F32)
    ones = jnp.ones((tq, LANES), BF16)

    def block(j, masked):
        kj = k_ref[pl.ds(pl.multiple_of(j * tq, tq), tq), :]
        vj = jnp.concatenate([v_ref[pl.ds(pl.multiple_of(j * tq, tq), tq), :], ones], axis=1)
        for c in range(2):
            s = lax.dot_general(qs[c], kj, NT, preferred_element_type=F32)
            if masked:
                r_id = lax.broadcasted_iota(jnp.int32, s.shape, 0)
                c_id = lax.broadcasted_iota(jnp.int32, s.shape, 1)
                s = jnp.where(c_id <= r_id, s, -jnp.inf)
            m_old = m_ref[c]
            m_new = jnp.maximum(m_old, jnp.max(s, axis=-1, keepdims=True))
            alpha = jnp.exp2(m_old - m_new)
            p = jnp.exp2(s - jnp.tile(m_new, (1, tq // LANES)))
            pv = jnp.dot(p.astype(BF16), vj, preferred_element_type=F32)
            acc_ref[c] = jnp.tile(alpha, (1, 2)) * acc_ref[c] + pv
            m_ref[c] = m_new

    def body(j, carry):
        block(j, False)
        return carry

    lax.fori_loop(0, i, body, 0)
    block(i, True)
    lam = lam_ref[0, 0]
    a0, a1 = acc_ref[0], acc_ref[1]
    o = a0[:, :LANES] / a0[:, LANES:] - lam * (a1[:, :LANES] / a1[:, LANES:])
    o = o * lax.rsqrt(jnp.mean(o * o, axis=-1, keepdims=True) + NORM_EPS) * sg_ref[...] * (1.0 - LAM_INIT)
    o_ref[...] = o.astype(o_ref.dtype)


def _attention(qkv, lam, subln_g, batch, seq, tq):
    n = qkv.shape[0]
    nq = seq // tq
    h = DA_HEADS
    return pl.pallas_call(
        functools.partial(_attn_kernel, tq=tq),
        grid=(batch, h, nq),
        in_specs=[pl.BlockSpec(memory_space=pltpu.SMEM),
                  pl.BlockSpec((tq, LANES), lambda b, hh, i: (b * nq + i, hh)),
                  pl.BlockSpec((seq, LANES), lambda b, hh, i: (b, h + hh)),
                  pl.BlockSpec((seq, LANES), lambda b, hh, i: (b, 2 * h + hh)),
                  pl.BlockSpec((1, LANES), lambda b, hh, i: (0, 0))],
        out_specs=pl.BlockSpec((tq, LANES), lambda b, hh, i: (b * nq + i, hh)),
        out_shape=jax.ShapeDtypeStruct((n, h * LANES), BF16),
        scratch_shapes=[pltpu.VMEM((2, tq, LANES), F32), pltpu.VMEM((2, tq, 2 * LANES), F32)],
        compiler_params=_params("parallel", "parallel", "arbitrary"),
    )(lam, qkv, qkv, qkv, subln_g)


def _rwprep_kernel(z_ref, zp_ref, mu_ref, w0_ref, wup_ref, a0_ref, aup_ref, gup_ref, kk_ref, ka_ref, bd_ref,
                   r_o, ld_o, k_o, v_o, kk_o, b_o, g_o, *, tiles_per_seq, width):
    i = pl.program_id(0)
    z = z_ref[...]
    last = zp_ref[7:8, :]
    first = jnp.where(i % tiles_per_seq == 0, jnp.zeros_like(last), last)
    row = lax.broadcasted_iota(jnp.int32, z.shape, 0)
    prev = jnp.where(row == 0, first, pltpu.roll(z, 1, 0))
    zs = z + (prev - z) * mu_ref[...]
    r = zs[:, 0:width]
    k = zs[:, width:2 * width]
    v = zs[:, 2 * width:3 * width]
    xwa = zs[:, 3 * width:3 * width + LANES]
    xg = zs[:, 3 * width + LANES:3 * width + 2 * LANES]
    w = -jax.nn.softplus(-(w0_ref[...] + _mm_hi(jnp.tanh(xwa), wup_ref[...]))) - 0.5
    a = jax.nn.sigmoid(a0_ref[...] + _mm_hi(xwa, aup_ref[...]))
    g = _mm_hi(jax.nn.sigmoid(xg), gup_ref[...])
    kk = k * kk_ref[...]
    kk = kk / jnp.maximum(jnp.sqrt(_mm_hi(kk * kk, bd_ref[...])), 1e-12)
    r_o[...] = r
    ld_o[...] = -jnp.exp(w)
    k_o[...] = k * (1.0 + (a - 1.0) * ka_ref[...])
    v_o[...] = v
    kk_o[...] = kk
    b_o[...] = kk * a
    g_o[...] = g


def _rwprep(zrw, mu, w0, wup_pad, a0, aup_pad, gup, k_k, k_a, bd, seq, tm):
    n, zin = zrw.shape
    width = w0.shape[1]
    row = lambda i: (i, 0)
    fix = lambda i: (0, 0)
    prev = lambda i: (jnp.maximum(i * (tm // 8) - 1, 0), 0)
    out = jax.ShapeDtypeStruct((n, width), F32)
    return pl.pallas_call(
        functools.partial(_rwprep_kernel, tiles_per_seq=seq // tm, width=width),
        grid=(n // tm,),
        in_specs=[pl.BlockSpec((tm, zin), row), pl.BlockSpec((8, zin), prev), pl.BlockSpec((1, zin), fix),
                  pl.BlockSpec((1, width), fix), pl.BlockSpec((LANES, width), fix),
                  pl.BlockSpec((1, width), fix), pl.BlockSpec((LANES, width), fix), pl.BlockSpec((LANES, width), fix),
                  pl.BlockSpec((1, width), fix), pl.BlockSpec((1, width), fix), pl.BlockSpec((width, width), fix)],
        out_specs=[pl.BlockSpec((tm, width), row)] * 7,
        out_shape=[out] * 7,
        compiler_params=_params("parallel"),
    )(zrw, zrw, mu, w0, wup_pad, a0, aup_pad, gup, k_k, k_a, bd)


def _rwcore_kernel(r_ref, ld_ref, k_ref, v_ref, kk_ref, b_ref, g_ref, lng_ref, lnb_ref, rk_ref, o_ref, s_ref, *,
                   units, groups):
    U, C, HD = RW_UNIT, RW_CHUNK, RW_HEAD

    @pl.when(pl.program_id(2) == 0)
    def _():
        s_ref[...] = jnp.zeros(s_ref.shape, F32)

    ri = lax.broadcasted_iota(jnp.int32, (U, U), 0)
    ci = lax.broadcasted_iota(jnp.int32, (U, U), 1)
    same = (ri // C) == (ci // C)
    tri_s = same & (ci < ri)
    tri_i = same & (ci <= ri)
    eye = (ri == ci).astype(F32)
    cum_w = tri_i.astype(F32)
    head_avg = same.astype(F32) * (1.0 / HD)
    head_sum = same.astype(F32)
    hmask = (ci < HD, ci >= HD)
    cmask = (ri < C, ri >= C)
    zero = jnp.zeros((U, U), F32)
    cells = [(u, q) for u in range(units) for q in range(groups)]

    def blk(ref, cell):
        u, q = cell
        return ref[pl.ds(u * U, U), q * LANES:(q + 1) * LANES]

    def vec(ref, q):
        return ref[:, q * LANES:(q + 1) * LANES]

    cell_v, cell_at, cell_bt, cell_rt, cell_bk, cell_gam = {}, {}, {}, {}, {}, {}
    for cell in cells:
        ld = blk(ld_ref, cell)
        cum = _mm_split(cum_w, ld, NN, 1, 3)
        gam = jnp.exp(cum)
        ginv = jnp.exp(-cum)
        bt = blk(b_ref, cell) * ginv
        cell_v[cell] = blk(v_ref, cell)
        cell_at[cell] = -blk(kk_ref, cell) * jnp.exp(cum - ld)
        cell_bt[cell] = bt
        cell_rt[cell] = blk(r_ref, cell) * gam
        cell_bk[cell] = jnp.concatenate([bt, blk(k_ref, cell) * ginv], axis=0)
        cell_gam[cell] = gam

    chains = [(cell, h) for cell in cells for h in range(2)]
    mab, mak, mrb, mrk = {}, {}, {}, {}
    for ch in chains:
        cell, h = ch
        ar = jnp.concatenate([jnp.where(hmask[h], cell_at[cell], zero), jnp.where(hmask[h], cell_rt[cell], zero)], axis=0)
        m = _mm_nt(ar, cell_bk[cell])
        mab[ch] = jnp.where(tri_s, m[0:U, 0:U], zero)
        mak[ch] = jnp.where(tri_s, m[0:U, U:2 * U], zero)
        mrb[ch] = jnp.where(tri_i, m[U:2 * U, 0:U], zero)
        mrk[ch] = jnp.where(tri_i, m[U:2 * U, U:2 * U], zero)

    tm = {ch: eye + mab[ch] for ch in chains}
    pw = dict(mab)
    for _ in range(int(math.log2(C)) - 1):
        pw = {ch: _mm(pw[ch], pw[ch]) for ch in chains}
        tm = {ch: tm[ch] + _mm(tm[ch], pw[ch]) for ch in chains}
    aph = {ch: _mm(tm[ch], cell_at[ch[0]]) for ch in chains}
    mv = {ch: _mm(mak[ch], cell_v[ch[0]]) for ch in chains}
    uh = {ch: _mm(tm[ch], mv[ch]) for ch in chains}
    rph = {ch: _mm(mrb[ch], aph[ch]) for ch in chains}
    yph = {ch: _mm(mrb[ch], uh[ch]) + _mm(mrk[ch], cell_v[ch[0]]) for ch in chains}

    def both_heads(d, cell):
        return jnp.where(hmask[0], d[(cell, 0)], d[(cell, 1)])

    rp, yp, gs, hs = {}, {}, {}, {}
    for cell in cells:
        ap, uu = both_heads(aph, cell), both_heads(uh, cell)
        rp[cell] = both_heads(rph, cell) + cell_rt[cell]
        yp[cell] = both_heads(yph, cell)
        for c in range(2):
            gl = cell_gam[cell][(c + 1) * C - 1:(c + 1) * C, :]
            apc = jnp.where(cmask[c], ap, zero)
            uvc = jnp.concatenate([jnp.where(cmask[c], uu, zero), jnp.where(cmask[c], cell_v[cell], zero)], axis=0)
            gs[cell, c] = jnp.where(same, eye + _mm_tn(apc, cell_bt[cell]), zero) * gl
            hs[cell, c] = jnp.where(same, _mm_tn(uvc, cell_bk[cell]), zero) * gl

    s = [s_ref[q] for q in range(groups)]
    ys = {}
    for u in range(units):
        for q in range(groups):
            cell = (u, q)
            y0 = _mm_split(rp[cell], s[q], NT, 2, 2)
            s[q] = _mm_split(s[q], gs[cell, 0], NN, 2, 2) + hs[cell, 0]
            y1 = _mm_split(rp[cell], s[q], NT, 2, 2)
            s[q] = _mm_split(s[q], gs[cell, 1], NN, 2, 2) + hs[cell, 1]
            ys[cell] = jnp.where(cmask[0], y0, y1) + yp[cell]
    for q in range(groups):
        s_ref[q] = s[q]

    for cell in cells:
        u, q = cell
        y = ys[cell]
        mean = _mm_split(y, head_avg, NN, 2, 1)
        yc = y - mean
        var = _mm_split(yc * yc, head_avg, NN, 2, 1)
        yn = yc * lax.rsqrt(var + RW_GN_EPS) * vec(lng_ref, q) + vec(lnb_ref, q)
        bonus = _mm_split(blk(r_ref, cell) * blk(k_ref, cell) * vec(rk_ref, q), head_sum, NN, 2, 1)
        yn = yn + bonus * cell_v[cell]
        o_ref[pl.ds(u * U, U), q * LANES:(q + 1) * LANES] = (yn * blk(g_ref, cell)).astype(o_ref.dtype)


def _rwcore(r, ld, k, v, kk, b, g, ln_g, ln_b, r_k, batch, seq, units, groups):
    n, width = r.shape
    rows = units * RW_UNIT
    steps = seq // rows
    lanes = groups * LANES
    blk = pl.BlockSpec((rows, lanes), lambda bb, hp, i: (bb * steps + i, hp))
    vec = pl.BlockSpec((1, lanes), lambda bb, hp, i: (0, hp))
    return pl.pallas_call(
        functools.partial(_rwcore_kernel, units=units, groups=groups),
        grid=(batch, width // lanes, steps),
        in_specs=[blk] * 7 + [vec] * 3,
        out_specs=blk,
        out_shape=jax.ShapeDtypeStruct((n, width), BF16),
        scratch_shapes=[pltpu.VMEM((groups, RW_UNIT, RW_UNIT), F32)],
        compiler_params=_params("parallel", "parallel", "arbitrary"),
    )(r, ld, k, v, kk, b, g, ln_g, ln_b, r_k)


def _outproj_kernel(x_ref, oda_ref, orw_ref, wa_ref, wb_ref, g_ref, h_ref, u_ref):
    h = (x_ref[...] + jnp.dot(oda_ref[...], wa_ref[...], preferred_element_type=F32)
         + jnp.dot(orw_ref[...], wb_ref[...], preferred_element_type=F32))
    h_ref[...] = h
    u_ref[...] = _rms(h, g_ref[...]).astype(BF16)


def _outproj(x2, o_da, o_rw, wa, wb, g, tm):
    n, d = x2.shape
    da = o_da.shape[1]
    rw = o_rw.shape[1]
    row = lambda i: (i, 0)
    fix = lambda i: (0, 0)
    return pl.pallas_call(
        _outproj_kernel,
        grid=(n // tm,),
        in_specs=[pl.BlockSpec((tm, d), row), pl.BlockSpec((tm, da), row), pl.BlockSpec((tm, rw), row),
                  pl.BlockSpec((da, d), fix), pl.BlockSpec((rw, d), fix), pl.BlockSpec((1, d), fix)],
        out_specs=[pl.BlockSpec((tm, d), row), pl.BlockSpec((tm, d), row)],
        out_shape=[jax.ShapeDtypeStruct((n, d), F32), jax.ShapeDtypeStruct((n, d), BF16)],
        compiler_params=_params("parallel"),
    )(x2, o_da, o_rw, wa, wb, g)


def _topk_rows(s, k, payload=None):
    rows = s.shape[0]
    iota = lax.broadcasted_iota(jnp.int32, s.shape, 0).astype(F32)
    vals, sel = [], []
    for _ in range(k):
        m = jnp.max(s, axis=0, keepdims=True)
        am = jnp.min(jnp.where(s == m, iota, float(rows)), axis=0, keepdims=True)
        hit = iota == am
        vals.append(m)
        sel.append(am if payload is None else jnp.sum(jnp.where(hit, payload, 0.0), axis=0, keepdims=True))
        s = jnp.where(hit, -jnp.inf, s)
    return vals, sel


def _stack_rows(rows_list):
    k = len(rows_list)
    iota = lax.broadcasted_iota(jnp.int32, (k, rows_list[0].shape[1]), 0)
    out = jnp.zeros(iota.shape, rows_list[0].dtype)
    for j, r in enumerate(rows_list):
        out = jnp.where(iota == j, r, out)
    return out


def _peertopk_kernel(u_ref, wq_ref, keys_ref, idx_ref, gate_ref):
    u = u_ref[...]
    half = N_KEYS
    idx_rows, gate_rows = [], []
    for h in range(PEER_HEADS):
        tops = []
        for p in range(2):
            hp = h * 2 + p
            q_t = lax.dot_general(wq_ref[hp * half:(hp + 1) * half, :], u, NT, preferred_element_type=F32)
            s_t = jnp.dot(keys_ref[hp], q_t.astype(BF16), preferred_element_type=F32)
            tops.append(_topk_rows(s_t, PEER_TOPK))
        (v1, i1), (v2, i2) = tops
        pairs = [(i, j) for i in range(PEER_TOPK) for j in range(PEER_TOPK) if (i + 1) * (j + 1) <= PEER_TOPK]
        pad = -len(pairs) % 8
        cand = _stack_rows([v1[i] + v2[j] for i, j in pairs] + [jnp.full_like(v1[0], -jnp.inf)] * pad)
        cidx = _stack_rows([i1[i] * float(N_KEYS) + i2[j] for i, j in pairs] + [jnp.zeros_like(i1[0])] * pad)
        best, idx = _topk_rows(cand, PEER_TOPK, payload=cidx)
        e = [jnp.exp(b - best[0]) for b in best]
        den = functools.reduce(lambda a, b: a + b, e)
        idx_rows.append(_stack_rows(idx))
        gate_rows.append(_stack_rows([x / den for x in e]))
    idx_ref[...] = jnp.concatenate(idx_rows, axis=0).T.astype(jnp.int32)
    gate_ref[...] = jnp.concatenate(gate_rows, axis=0).T


def _peertopk(u_bf, wq_t, keys, tt):
    n, d = u_bf.shape
    hk = PEER_HEADS * PEER_TOPK
    row = lambda i: (i, 0)
    return pl.pallas_call(
        _peertopk_kernel,
        grid=(n // tt,),
        in_specs=[pl.BlockSpec((tt, d), row), pl.BlockSpec(wq_t.shape, lambda i: (0, 0)),
                  pl.BlockSpec(keys.shape, lambda i: (0, 0, 0))],
        out_specs=[pl.BlockSpec((tt, hk), row), pl.BlockSpec((tt, hk), row)],
        out_shape=[jax.ShapeDtypeStruct((n, hk), jnp.int32), jax.ShapeDtypeStruct((n, hk), F32)],
        compiler_params=_params("parallel"),
    )(u_bf, wq_t, keys)


SC_CORES = 2
SC_SUBCORES = 16
SC_LANES = 16
HK = PEER_HEADS * PEER_TOPK
SUM_ROWS = 16
SUM_TOKENS = 8
SUM_UNROLL = 2


def _sc_mesh():
    return plsc.VectorSubcoreMesh(core_axis_name="core", subcore_axis_name="subcore")


def _sc_worker():
    return lax.axis_index("core") * SC_SUBCORES + lax.axis_index("subcore")


def _row_dots(tab, idx_flat, xw):
    p = idx_flat.shape[0]
    w = tab.shape[1]
    t_total = p // HK
    workers = SC_CORES * SC_SUBCORES
    tpw = t_total // workers
    g = min(SUM_TOKENS, tpw)
    r = SUM_ROWS
    ns = HK // r
    ln = SC_LANES
    per_row = LANES // ln
    out_rows = HK // per_row
    assert t_total % workers == 0 and tpw % g == 0 and g % 2 == 0 and w % (2 * ln) == 0 and r % per_row == 0
    buf = pltpu.VMEM((r, w), tab.dtype)
    res = pltpu.VMEM((out_rows, LANES), F32)
    sem = pltpu.SemaphoreType.DMA

    @pl.kernel(out_type=jax.ShapeDtypeStruct((t_total * out_rows, LANES), F32), mesh=_sc_mesh(),
               scratch_types=[pltpu.VMEM((g * HK,), jnp.int32), pltpu.VMEM((g, w), tab.dtype), res, res]
               + [buf] * ns + [sem] * (ns + 2),
               compiler_params=pltpu.CompilerParams(needs_layout_passes=False))
    def dots(t_hbm, i_hbm, x_hbm, o_hbm, idx_v, x_v, res0, res1, *scratch):
        bufs, gsem, osem = scratch[0:ns], scratch[ns:2 * ns], scratch[2 * ns:2 * ns + 2]
        ress = (res0, res1)
        wid = _sc_worker()
        zero = jnp.zeros((ln,), F32)

        @pl.loop(0, tpw // g)
        def _(win):
            tok0 = wid * tpw + win * g
            pltpu.sync_copy(i_hbm.at[pl.ds(tok0 * HK, g * HK)], idx_v)
            pltpu.sync_copy(x_hbm.at[pl.ds(tok0, g)], x_v)

            def gather_of(tl, s):
                return pltpu.make_async_copy(t_hbm.at[idx_v.at[pl.ds(tl * HK + s * r, r)]], bufs[s], gsem[s])

            def out_of(tl, par):
                return pltpu.make_async_copy(ress[par], o_hbm.at[pl.ds((tok0 + tl) * out_rows, out_rows)], osem[par])

            for s in range(ns):
                gather_of(0, s).start()

            @pl.loop(0, g // 2)
            def _(tp):
                for par in range(2):
                    tl = tp * 2 + par
                    ob = ress[par]

                    @pl.when(tl >= 2)
                    def _():
                        out_of(tl - 2, par).wait()

                    for s in range(ns):
                        gather_of(tl, s).wait()
                        rows = bufs[s]

                        def fold(c, accs):
                            col = pl.multiple_of(c * 2 * ln, 2 * ln)
                            xa = plsc.bitcast(x_v[tl, pl.ds(col, ln)], BF16)
                            xb = plsc.bitcast(x_v[tl, pl.ds(col + ln, ln)], BF16)
                            out = []
                            for q in range(r):
                                pa = plsc.bitcast(rows[q, pl.ds(col, ln)], BF16) * xa
                                pb = plsc.bitcast(rows[q, pl.ds(col + ln, ln)], BF16) * xb
                                pair = plsc.bitcast(pa + pb, jnp.uint32)
                                lo = plsc.bitcast(lax.shift_left(pair, jnp.uint32(16)), F32)
                                hi = plsc.bitcast(pair & jnp.uint32(0xFFFF0000), F32)
                                out.append(accs[q] + lo + hi)
                            return tuple(out)

                        accs = lax.fori_loop(0, w // (2 * ln), fold, (zero,) * r)
                        for q in range(r):
                            k = s * r + q
                            ob[k // per_row, pl.ds((k % per_row) * ln, ln)] = accs[q]

                        @pl.when(tl + 1 < g)
                        def _():
                            gather_of(tl + 1, s).start()

                    out_of(tl, par).start()

            for par in range(2):
                out_of(g - 2 + par, par).wait()

    return dots(tab, idx_flat, xw)


def _weighted_row_sum(tab, idx_flat, wrep):
    p = idx_flat.shape[0]
    w = tab.shape[1]
    t_total = p // HK
    workers = SC_CORES * SC_SUBCORES
    tpw = t_total // workers
    g = min(SUM_TOKENS, tpw)
    r = SUM_ROWS
    ns = HK // r
    ln = SC_LANES
    per_row = LANES // ln
    assert t_total % workers == 0 and tpw % g == 0 and g % 2 == 0 and w % ln == 0
    buf = pltpu.VMEM((r, w), tab.dtype)
    acc = pltpu.VMEM((2 * w,), F32)
    sem = pltpu.SemaphoreType.DMA

    @pl.kernel(out_type=jax.ShapeDtypeStruct((t_total, 2 * w), F32), mesh=_sc_mesh(),
               scratch_types=[pltpu.VMEM((g * HK,), jnp.int32), pltpu.VMEM((g * HK // per_row, LANES), F32), acc, acc]
               + [buf] * ns + [sem] * (ns + 2),
               compiler_params=pltpu.CompilerParams(needs_layout_passes=False))
    def wsum(t_hbm, i_hbm, w_hbm, o_hbm, idx_v, w_v, acc0, acc1, *scratch):
        bufs, gsem, osem = scratch[0:ns], scratch[ns:2 * ns], scratch[2 * ns:2 * ns + 2]
        accs = (acc0, acc1)
        wid = _sc_worker()
        zero = jnp.zeros((ln,), F32)

        @pl.loop(0, tpw // g)
        def _(win):
            tok0 = wid * tpw + win * g
            pltpu.sync_copy(i_hbm.at[pl.ds(tok0 * HK, g * HK)], idx_v)
            pltpu.sync_copy(w_hbm.at[pl.ds(tok0 * (HK // per_row), g * HK // per_row)], w_v)

            def gather_of(tl, s):
                return pltpu.make_async_copy(t_hbm.at[idx_v.at[pl.ds(tl * HK + s * r, r)]], bufs[s], gsem[s])

            def out_of(tl, par):
                return pltpu.make_async_copy(accs[par], o_hbm.at[tok0 + tl], osem[par])

            for s in range(ns):
                gather_of(0, s).start()

            @pl.loop(0, g // 2)
            def _(tp):
                for par in range(2):
                    tl = tp * 2 + par
                    ob = accs[par]

                    @pl.when(tl >= 2)
                    def _():
                        out_of(tl - 2, par).wait()

                    for c in range(2 * w // ln):
                        ob[pl.ds(c * ln, ln)] = zero
                    for s in range(ns):
                        gather_of(tl, s).wait()
                        wrow = tl * (HK // per_row) + s * (r // per_row)
                        wk = []
                        for q in range(r):
                            bits = plsc.bitcast(w_v[wrow + q // per_row, pl.ds((q % per_row) * ln, ln)], jnp.uint32)
                            wk.append(plsc.bitcast(bits | lax.shift_right_logical(bits, jnp.uint32(16)), BF16))
                        rows = bufs[s]

                        def fold(c, carry):
                            for half in range(SUM_UNROLL):
                                col = pl.multiple_of((c * SUM_UNROLL + half) * ln, ln)
                                lo_acc = hi_acc = None
                                for q in range(0, r, 2):
                                    pa = plsc.bitcast(rows[q, pl.ds(col, ln)], BF16) * wk[q]
                                    pb = plsc.bitcast(rows[q + 1, pl.ds(col, ln)], BF16) * wk[q + 1]
                                    pair = plsc.bitcast(pa + pb, jnp.uint32)
                                    lo = plsc.bitcast(lax.shift_left(pair, jnp.uint32(16)), F32)
                                    hi = plsc.bitcast(pair & jnp.uint32(0xFFFF0000), F32)
                                    lo_acc = lo if lo_acc is None else lo_acc + lo
                                    hi_acc = hi if hi_acc is None else hi_acc + hi
                                plsc.addupdate(ob.at[pl.ds(col, ln)], lo_acc)
                                plsc.addupdate(ob.at[pl.ds(w + col, ln)], hi_acc)
                            return carry

                        lax.fori_loop(0, w // (ln * SUM_UNROLL), fold, 0)

                        @pl.when(tl + 1 < g)
                        def _():
                            gather_of(tl + 1, s).start()

                    out_of(tl, par).start()

            for par in range(2):
                out_of(g - 2 + par, par).wait()

    return wsum(tab, idx_flat, wrep)


def _peerw_kernel(part_ref, gate_ref, fold_ref, rep_ref, o_ref):
    hid = _mm_split(part_ref[...], fold_ref[...], NN, 3, 1)
    w = gate_ref[...] * (0.5 * hid * (1.0 + lax.erf(hid * (2.0 ** -0.5))))
    o_ref[...] = jnp.dot(w.astype(BF16), rep_ref[...], preferred_element_type=F32)


def _peerw(part, gate, tt):
    n = gate.shape[0]
    wide = HK * SC_LANES
    lane = jnp.arange(wide)
    fold = (lane[:, None] // SC_LANES == jnp.arange(HK)[None, :]).astype(BF16)
    row = lambda i: (i, 0)
    fix = lambda i: (0, 0)
    return pl.pallas_call(
        _peerw_kernel,
        grid=(n // tt,),
        in_specs=[pl.BlockSpec((tt, wide), row), pl.BlockSpec((tt, HK), row), pl.BlockSpec((wide, HK), fix),
                  pl.BlockSpec((HK, wide), fix)],
        out_specs=pl.BlockSpec((tt, wide), row),
        out_shape=jax.ShapeDtypeStruct((n, wide), F32),
        compiler_params=_params("parallel"),
    )(part.reshape(n, wide), gate, fold, fold.T)


def _ple_kernel(h_ref, f_ref, p_ref, g_ref, wg_ref, wp_ref, gf_ref, o_ref):
    h = h_ref[...] + f_ref[...]
    gate = jax.nn.sigmoid(jnp.dot(_rms(h, g_ref[...]).astype(BF16), wg_ref[...], preferred_element_type=F32))
    pp = jnp.dot(p_ref[...].astype(BF16), wp_ref[...], preferred_element_type=F32)
    o_ref[...] = _rms(h + gate * pp, gf_ref[...])


def _ple(h1, ffn, p2, g, wg, wp, gf, tm):
    n, d = h1.shape
    pd = p2.shape[1]
    row = lambda i: (i, 0)
    fix = lambda i: (0, 0)
    return pl.pallas_call(
        _ple_kernel,
        grid=(n // tm,),
        in_specs=[pl.BlockSpec((tm, d), row), pl.BlockSpec((tm, d), row), pl.BlockSpec((tm, pd), row),
                  pl.BlockSpec((1, d), fix), pl.BlockSpec((d, d), fix), pl.BlockSpec((pd, d), fix),
                  pl.BlockSpec((1, d), fix)],
        out_specs=pl.BlockSpec((tm, d), row),
        out_shape=jax.ShapeDtypeStruct((n, d), F32),
        compiler_params=_params("parallel"),
    )(h1, ffn, p2, g, wg, wp, gf)


def _rope_tables(positions):
    half = ROT_DIM // 2
    inv_freq = ROPE_THETA ** (-jnp.arange(half, dtype=F32) * 2.0 / ROT_DIM)
    ang = positions.astype(F32).reshape(-1, 1) * inv_freq
    lane = jnp.arange(LANES)
    d = lane % DA_HEAD_DIM
    cos = jnp.take(jnp.cos(ang), d % half, axis=1)
    sin = jnp.take(jnp.sin(ang), d % half, axis=1)
    c = jnp.where(d < ROT_DIM, cos, 1.0)
    s1 = jnp.where(d < half, -sin, 0.0)
    s2 = jnp.where((d >= half) & (d < ROT_DIM), sin, 0.0)
    return c, s1, s2


def _pack_rows(tab):
    d = tab.shape[1]
    t = tab.astype(BF16)
    pair = jnp.stack([t[:, :d // 2], t[:, d // 2:]], axis=-1)
    return lax.bitcast_convert_type(pair, jnp.uint32)


def _block_diag_ones(width, head):
    i = jnp.arange(width)
    return (i[:, None] // head == i[None, :] // head).astype(F32)


def _tiles(seq):
    return dict(tm=min(256, seq), tq=min(512, seq), units=min(4, seq // RW_UNIT), groups=2, tt_topk=min(256, seq), tt_mix=min(256, seq),
                peer_chunks=2 if seq % 4096 == 0 else 1)


def kernel(x, p, positions, norm_mix_g, w_in, lam_q1, lam_k1, lam_q2, lam_k2, da_subln_g, rw_mu, rw_w0, rw_w_up, rw_a0, rw_a_up, rw_g_up, rw_k_k, rw_k_a, rw_r_k, rw_ln_g, rw_ln_b, w_out, norm_ffn_g, peer_w_q, peer_sub_keys, peer_u, peer_v, norm_ple_g, ple_gate_w, ple_proj_w, norm_final_g):
    batch, seq, d = x.shape
    t = _tiles(seq)
    row = lambda a: a.reshape(1, -1)
    f32 = F32

    w_in_bf = w_in[0].astype(BF16)
    lam = (jnp.exp(jnp.sum(lam_q1[0].astype(f32) * lam_k1[0].astype(f32)))
           - jnp.exp(jnp.sum(lam_q2[0].astype(f32) * lam_k2[0].astype(f32))) + LAM_INIT).reshape(1, 1)
    width = rw_w0.shape[1]
    wup_pad = jnp.concatenate([rw_w_up[0], jnp.zeros((LANES - rw_w_up.shape[1], width), f32)], axis=0)
    aup_pad = jnp.concatenate([jnp.zeros((LANES - rw_a_up.shape[1], width), f32), rw_a_up[0]], axis=0)
    head_ones = _block_diag_ones(width, RW_HEAD)
    w_out_bf = w_out[0].astype(BF16)
    da_w = DA_HEADS * 2 * DA_HEAD_DIM
    keys = peer_sub_keys[0].reshape(PEER_HEADS * 2, N_KEYS, -1).astype(BF16)
    wq_t = peer_w_q[0].T.astype(BF16)
    u_tab = _pack_rows(peer_u[0])
    v_tab = _pack_rows(peer_v[0])
    wg_bf = ple_gate_w[0].astype(BF16)
    wp_bf = ple_proj_w[0].astype(BF16)

    nc = seq // t["peer_chunks"]
    chunks = [slice(c * nc, (c + 1) * nc) for c in range(t["peer_chunks"])]

    outs = []
    for b in range(batch):
        xs = x[b]
        rc, rs1, rs2 = _rope_tables(positions[b])
        qkv, zrw = _inproj(xs, row(norm_mix_g[0]), w_in_bf, rc, rs1, rs2, t["tm"])
        o_da = _attention(qkv, lam, row(da_subln_g[0]), 1, seq, t["tq"])
        rw = _rwprep(zrw, row(rw_mu[0]), row(rw_w0[0]), wup_pad, row(rw_a0[0]), aup_pad, rw_g_up[0],
                     row(rw_k_k[0]), row(rw_k_a[0]), head_ones, seq, t["tm"])
        o_rw = _rwcore(*rw, row(rw_ln_g[0]), row(rw_ln_b[0]), row(rw_r_k[0]), 1, seq, t["units"], t["groups"])
        h1, u2 = _outproj(xs, o_da, o_rw, w_out_bf[:da_w], w_out_bf[da_w:], row(norm_ffn_g[0]), t["tm"])
        idx, gate = _peertopk(u2, wq_t, keys, t["tt_topk"])
        xw = _pack_rows(u2)
        ffn = []
        for sl in chunks:
            idx_c = idx[sl].reshape(-1)
            part = _row_dots(u_tab, idx_c, xw[sl])
            wrep = _peerw(part, gate[sl], t["tt_mix"])
            ffn.append(_weighted_row_sum(v_tab, idx_c, wrep.reshape(-1, LANES)))
        ffn = jnp.concatenate(ffn, axis=0) if len(ffn) > 1 else ffn[0]
        outs.append(_ple(h1, ffn, p[0, b], row(norm_ple_g[0]), wg_bf, wp_bf, row(norm_final_g), t["tm"]))
    return jnp.stack(outs, axis=0)
```

```python
import functools
import math

import jax
import jax.numpy as jnp
from jax import lax
from jax.experimental import pallas as pl
from jax.experimental.pallas import tpu as pltpu
from jax.experimental.pallas import tpu_sc as plsc

F32 = jnp.float32
BF16 = jnp.bfloat16

NORM_EPS = 1e-6
DA_HEADS = 4
DA_HEAD_DIM = 64
ROPE_THETA = 500000.0
ROT_DIM = DA_HEAD_DIM // 4
RW_HEAD = 64
RW_GN_EPS = 64e-5
PEER_HEADS = 8
N_KEYS = 128
PEER_TOPK = 16
LAM_INIT = 0.8 - 0.6 * math.exp(-0.3 * 0)

LANES = 128
VMEM_LIMIT = 56 * 1024 * 1024
RW_CHUNK = 64
RW_UNIT = 2 * RW_CHUNK

NT = (((1,), (1,)), ((), ()))
TN = (((0,), (0,)), ((), ()))
HI = lax.Precision.HIGHEST


def _mm(a, b):
    return jnp.dot(a.astype(BF16), b.astype(BF16), preferred_element_type=F32)


def _mm_nt(a, b):
    return lax.dot_general(a.astype(BF16), b.astype(BF16), NT, preferred_element_type=F32)


def _mm_tn(a, b):
    return lax.dot_general(a.astype(BF16), b.astype(BF16), TN, preferred_element_type=F32)


def _mm_hi(a, b):
    return jnp.dot(a, b, precision=HI, preferred_element_type=F32)


def _pieces(a, n):
    out = []
    for _ in range(n):
        piece = a.astype(BF16)
        out.append(piece)
        a = a - piece.astype(F32)
    return out


def _mm_split(a, b, dims, a_pieces, b_pieces):
    ap, bp = _pieces(a, a_pieces), _pieces(b, b_pieces)
    terms = [lax.dot_general(x, y, dims, preferred_element_type=F32)
             for i, x in enumerate(ap) for j, y in enumerate(bp) if i + j < max(a_pieces, b_pieces)]
    return functools.reduce(lambda u, v: u + v, terms)


NN = (((1,), (0,)), ((), ()))


def _params(*sem):
    return pltpu.CompilerParams(dimension_semantics=sem, vmem_limit_bytes=VMEM_LIMIT)


def _rms(x, g):
    return x * lax.rsqrt(jnp.mean(x * x, axis=-1, keepdims=True) + NORM_EPS) * g


def _inproj_kernel(x_ref, g_ref, w_ref, c_ref, s1_ref, s2_ref, qkv_ref, zrw_ref, *, n_qk, n_da):
    u = _rms(x_ref[...], g_ref[...]).astype(BF16)
    z = jnp.dot(u, w_ref[...], preferred_element_type=F32)
    c, s1, s2 = c_ref[...], s1_ref[...], s2_ref[...]
    half = ROT_DIM // 2
    for blk in range(n_da // LANES):
        t = z[:, blk * LANES:(blk + 1) * LANES]
        if blk < 2 * n_qk // LANES:
            t = t * c + pltpu.roll(t, LANES - half, 1) * s1 + pltpu.roll(t, half, 1) * s2
        if blk < n_qk // LANES:
            t = t * (DA_HEAD_DIM ** -0.5 * math.log2(math.e))
        qkv_ref[:, blk * LANES:(blk + 1) * LANES] = t.astype(BF16)
    zrw_ref[...] = z[:, n_da:]


def _inproj(x2, g, w_in_bf, rc, rs1, rs2, tm):
    n, d = x2.shape
    n_in = w_in_bf.shape[1]
    n_qk = DA_HEADS * 2 * DA_HEAD_DIM
    n_da = 3 * n_qk
    row = lambda i: (i, 0)
    fix = lambda i: (0, 0)
    return pl.pallas_call(
        functools.partial(_inproj_kernel, n_qk=n_qk, n_da=n_da),
        grid=(n // tm,),
        in_specs=[pl.BlockSpec((tm, d), row), pl.BlockSpec((1, d), fix), pl.BlockSpec((d, n_in), fix),
                  pl.BlockSpec((tm, LANES), row), pl.BlockSpec((tm, LANES), row), pl.BlockSpec((tm, LANES), row)],
        out_specs=[pl.BlockSpec((tm, n_da), row), pl.BlockSpec((tm, n_in - n_da), row)],
        out_shape=[jax.ShapeDtypeStruct((n, n_da), BF16), jax.ShapeDtypeStruct((n, n_in - n_da), F32)],
        compiler_params=_params("parallel"),
    )(x2, g, w_in_bf, rc, rs1, rs2)


def _attn_kernel(lam_ref, q_ref, k_ref, v_ref, sg_ref, o_ref, m_ref, acc_ref, *, tq):
    i = pl.program_id(2)
    q = q_ref[...]
    lane = lax.broadcasted_iota(jnp.int32, q.shape, 1)
    zero = jnp.zeros_like(q)
    qs = (jnp.where(lane < DA_HEAD_DIM, q, zero), jnp.where(lane >= DA_HEAD_DIM, q, zero))
    m_ref[...] = jnp.full(m_ref.shape, -jnp.inf, F32)
    acc_ref[...] = jnp.zeros(acc_ref.shape, F32)
    ones = jnp.ones((tq, LANES), BF16)

    def block(j, masked):
        kj = k_ref[pl.ds(pl.multiple_of(j * tq, tq), tq), :]
        vj = jnp.concatenate([v_ref[pl.ds(pl.multiple_of(j * tq, tq), tq), :], ones], axis=1)
        for c in range(2):
            s = lax.dot_general(qs[c], kj, NT, preferred_element_type=F32)
            if masked:
                r_id = lax.broadcasted_iota(jnp.int32, s.shape, 0)
                c_id = lax.broadcasted_iota(jnp.int32, s.shape, 1)
                s = jnp.where(c_id <= r_id, s, -jnp.inf)
            m_old = m_ref[c]
            m_new = jnp.maximum(m_old, jnp.max(s, axis=-1, keepdims=True))
            alpha = jnp.exp2(m_old - m_new)
            p = jnp.exp2(s - jnp.tile(m_new, (1, tq // LANES)))
            pv = jnp.dot(p.astype(BF16), vj, preferred_element_type=F32)
            acc_ref[c] = jnp.tile(alpha, (1, 2)) * acc_ref[c] + pv
            m_ref[c] = m_new

    def body(j, carry):
        block(j, False)
        return carry

    lax.fori_loop(0, i, body, 0)
    block(i, True)
    lam = lam_ref[0, 0]
    a0, a1 = acc_ref[0], acc_ref[1]
    o = a0[:, :LANES] / a0[:, LANES:] - lam * (a1[:, :LANES] / a1[:, LANES:])
    o = o * lax.rsqrt(jnp.mean(o * o, axis=-1, keepdims=True) + NORM_EPS) * sg_ref[...] * (1.0 - LAM_INIT)
    o_ref[...] = o.astype(o_ref.dtype)


def _attention(qkv, lam, subln_g, batch, seq, tq):
    n = qkv.shape[0]
    nq = seq // tq
    h = DA_HEADS
    return pl.pallas_call(
        functools.partial(_attn_kernel, tq=tq),
        grid=(batch, h, nq),
        in_specs=[pl.BlockSpec(memory_space=pltpu.SMEM),
                  pl.BlockSpec((tq, LANES), lambda b, hh, i: (b * nq + i, hh)),
                  pl.BlockSpec((seq, LANES), lambda b, hh, i: (b, h + hh)),
                  pl.BlockSpec((seq, LANES), lambda b, hh, i: (b, 2 * h + hh)),
                  pl.BlockSpec((1, LANES), lambda b, hh, i: (0, 0))],
        out_specs=pl.BlockSpec((tq, LANES), lambda b, hh, i: (b * nq + i, hh)),
        out_shape=jax.ShapeDtypeStruct((n, h * LANES), BF16),
        scratch_shapes=[pltpu.VMEM((2, tq, LANES), F32), pltpu.VMEM((2, tq, 2 * LANES), F32)],
        compiler_params=_params("parallel", "parallel", "arbitrary"),
    )(lam, qkv, qkv, qkv, subln_g)


def _rwprep_kernel(z_ref, zp_ref, mu_ref, w0_ref, wup_ref, a0_ref, aup_ref, gup_ref, kk_ref, ka_ref, bd_ref,
                   r_o, ld_o, k_o, v_o, kk_o, b_o, g_o, *, tiles_per_seq, width):
    i = pl.program_id(0)
    z = z_ref[...]
    last = zp_ref[7:8, :]
    first = jnp.where(i % tiles_per_seq == 0, jnp.zeros_like(last), last)
    row = lax.broadcasted_iota(jnp.int32, z.shape, 0)
    prev = jnp.where(row == 0, first, pltpu.roll(z, 1, 0))
    zs = z + (prev - z) * mu_ref[...]
    r = zs[:, 0:width]
    k = zs[:, width:2 * width]
    v = zs[:, 2 * width:3 * width]
    xwa = zs[:, 3 * width:3 * width + LANES]
    xg = zs[:, 3 * width + LANES:3 * width + 2 * LANES]
    w = -jax.nn.softplus(-(w0_ref[...] + _mm_hi(jnp.tanh(xwa), wup_ref[...]))) - 0.5
    a = jax.nn.sigmoid(a0_ref[...] + _mm_hi(xwa, aup_ref[...]))
    g = _mm_hi(jax.nn.sigmoid(xg), gup_ref[...])
    kk = k * kk_ref[...]
    kk = kk / jnp.maximum(jnp.sqrt(_mm_hi(kk * kk, bd_ref[...])), 1e-12)
    r_o[...] = r
    ld_o[...] = -jnp.exp(w)
    k_o[...] = k * (1.0 + (a - 1.0) * ka_ref[...])
    v_o[...] = v
    kk_o[...] = kk
    b_o[...] = kk * a
    g_o[...] = g


def _rwprep(zrw, mu, w0, wup_pad, a0, aup_pad, gup, k_k, k_a, bd, seq, tm):
    n, zin = zrw.shape
    width = w0.shape[1]
    row = lambda i: (i, 0)
    fix = lambda i: (0, 0)
    prev = lambda i: (jnp.maximum(i * (tm // 8) - 1, 0), 0)
    out = jax.ShapeDtypeStruct((n, width), F32)
    return pl.pallas_call(
        functools.partial(_rwprep_kernel, tiles_per_seq=seq // tm, width=width),
        grid=(n // tm,),
        in_specs=[pl.BlockSpec((tm, zin), row), pl.BlockSpec((8, zin), prev), pl.BlockSpec((1, zin), fix),
                  pl.BlockSpec((1, width), fix), pl.BlockSpec((LANES, width), fix),
                  pl.BlockSpec((1, width), fix), pl.BlockSpec((LANES, width), fix), pl.BlockSpec((LANES, width), fix),
                  pl.BlockSpec((1, width), fix), pl.BlockSpec((1, width), fix), pl.BlockSpec((width, width), fix)],
        out_specs=[pl.BlockSpec((tm, width), row)] * 7,
        out_shape=[out] * 7,
        compiler_params=_params("parallel"),
    )(zrw, zrw, mu, w0, wup_pad, a0, aup_pad, gup, k_k, k_a, bd)


def _rwcore_kernel(r_ref, ld_ref, k_ref, v_ref, kk_ref, b_ref, g_ref, lng_ref, lnb_ref, rk_ref, o_ref, s_ref, *,
                   units, groups):
    U, C, HD = RW_UNIT, RW_CHUNK, RW_HEAD

    @pl.when(pl.program_id(2) == 0)
    def _():
        s_ref[...] = jnp.zeros(s_ref.shape, F32)

    ri = lax.broadcasted_iota(jnp.int32, (U, U), 0)
    ci = lax.broadcasted_iota(jnp.int32, (U, U), 1)
    same = (ri // C) == (ci // C)
    tri_s = same & (ci < ri)
    tri_i = same & (ci <= ri)
    eye = (ri == ci).astype(F32)
    cum_w = tri_i.astype(F32)
    head_avg = same.astype(F32) * (1.0 / HD)
    head_sum = same.astype(F32)
    hmask = (ci < HD, ci >= HD)
    cmask = (ri < C, ri >= C)
    zero = jnp.zeros((U, U), F32)
    cells = [(u, q) for u in range(units) for q in range(groups)]

    def blk(ref, cell):
        u, q = cell
        return ref[pl.ds(u * U, U), q * LANES:(q + 1) * LANES]

    def vec(ref, q):
        return ref[:, q * LANES:(q + 1) * LANES]

    cell_v, cell_at, cell_bt, cell_rt, cell_bk, cell_gam = {}, {}, {}, {}, {}, {}
    for cell in cells:
        ld = blk(ld_ref, cell)
        cum = _mm_split(cum_w, ld, NN, 1, 3)
        gam = jnp.exp(cum)
        ginv = jnp.exp(-cum)
        bt = blk(b_ref, cell) * ginv
        cell_v[cell] = blk(v_ref, cell)
        cell_at[cell] = -blk(kk_ref, cell) * jnp.exp(cum - ld)
        cell_bt[cell] = bt
        cell_rt[cell] = blk(r_ref, cell) * gam
        cell_bk[cell] = jnp.concatenate([bt, blk(k_ref, cell) * ginv], axis=0)
        cell_gam[cell] = gam

    chains = [(cell, h) for cell in cells for h in range(2)]
    mab, mak, mrb, mrk = {}, {}, {}, {}
    for ch in chains:
        cell, h = ch
        ar = jnp.concatenate([jnp.where(hmask[h], cell_at[cell], zero), jnp.where(hmask[h], cell_rt[cell], zero)], axis=0)
        m = _mm_nt(ar, cell_bk[cell])
        mab[ch] = jnp.where(tri_s, m[0:U, 0:U], zero)
        mak[ch] = jnp.where(tri_s, m[0:U, U:2 * U], zero)
        mrb[ch] = jnp.where(tri_i, m[U:2 * U, 0:U], zero)
        mrk[ch] = jnp.where(tri_i, m[U:2 * U, U:2 * U], zero)

    tm = {ch: eye + mab[ch] for ch in chains}
    pw = dict(mab)
    for _ in range(int(math.log2(C)) - 1):
        pw = {ch: _mm(pw[ch], pw[ch]) for ch in chains}
        tm = {ch: tm[ch] + _mm(tm[ch], pw[ch]) for ch in chains}
    aph = {ch: _mm(tm[ch], cell_at[ch[0]]) for ch in chains}
    mv = {ch: _mm(mak[ch], cell_v[ch[0]]) for ch in chains}
    uh = {ch: _mm(tm[ch], mv[ch]) for ch in chains}
    rph = {ch: _mm(mrb[ch], aph[ch]) for ch in chains}
    yph = {ch: _mm(mrb[ch], uh[ch]) + _mm(mrk[ch], cell_v[ch[0]]) for ch in chains}

    def both_heads(d, cell):
        return jnp.where(hmask[0], d[(cell, 0)], d[(cell, 1)])

    rp, yp, gs, hs = {}, {}, {}, {}
    for cell in cells:
        ap, uu = both_heads(aph, cell), both_heads(uh, cell)
        rp[cell] = both_heads(rph, cell) + cell_rt[cell]
        yp[cell] = both_heads(yph, cell)
        for c in range(2):
            gl = cell_gam[cell][(c + 1) * C - 1:(c + 1) * C, :]
            apc = jnp.where(cmask[c], ap, zero)
            uvc = jnp.concatenate([jnp.where(cmask[c], uu, zero), jnp.where(cmask[c], cell_v[cell], zero)], axis=0)
            gs[cell, c] = jnp.where(same, eye + _mm_tn(apc, cell_bt[cell]), zero) * gl
            hs[cell, c] = jnp.where(same, _mm_tn(uvc, cell_bk[cell]), zero) * gl

    s = [s_ref[q] for q in range(groups)]
    ys = {}
    for u in range(units):
        for q in range(groups):
            cell = (u, q)
            y0 = _mm_split(rp[cell], s[q], NT, 2, 2)
            s[q] = _mm_split(s[q], gs[cell, 0], NN, 2, 2) + hs[cell, 0]
            y1 = _mm_split(rp[cell], s[q], NT, 2, 2)
            s[q] = _mm_split(s[q], gs[cell, 1], NN, 2, 2) + hs[cell, 1]
            ys[cell] = jnp.where(cmask[0], y0, y1) + yp[cell]
    for q in range(groups):
        s_ref[q] = s[q]

    for cell in cells:
        u, q = cell
        y = ys[cell]
        mean = _mm_split(y, head_avg, NN, 2, 1)
        yc = y - mean
        var = _mm_split(yc * yc, head_avg, NN, 2, 1)
        yn = yc * lax.rsqrt(var + RW_GN_EPS) * vec(lng_ref, q) + vec(lnb_ref, q)
        bonus = _mm_split(blk(r_ref, cell) * blk(k_ref, cell) * vec(rk_ref, q), head_sum, NN, 2, 1)
        yn = yn + bonus * cell_v[cell]
        o_ref[pl.ds(u * U, U), q * LANES:(q + 1) * LANES] = (yn * blk(g_ref, cell)).astype(o_ref.dtype)


def _rwcore(r, ld, k, v, kk, b, g, ln_g, ln_b, r_k, batch, seq, units, groups):
    n, width = r.shape
    rows = units * RW_UNIT
    steps = seq // rows
    lanes = groups * LANES
    blk = pl.BlockSpec((rows, lanes), lambda bb, hp, i: (bb * steps + i, hp))
    vec = pl.BlockSpec((1, lanes), lambda bb, hp, i: (0, hp))
    return pl.pallas_call(
        functools.partial(_rwcore_kernel, units=units, groups=groups),
        grid=(batch, width // lanes, steps),
        in_specs=[blk] * 7 + [vec] * 3,
        out_specs=blk,
        out_shape=jax.ShapeDtypeStruct((n, width), BF16),
        scratch_shapes=[pltpu.VMEM((groups, RW_UNIT, RW_UNIT), F32)],
        compiler_params=_params("parallel", "parallel", "arbitrary"),
    )(r, ld, k, v, kk, b, g, ln_g, ln_b, r_k)


def _outproj_kernel(x_ref, oda_ref, orw_ref, wa_ref, wb_ref, g_ref, h_ref, u_ref):
    h = (x_ref[...] + jnp.dot(oda_ref[...], wa_ref[...], preferred_element_type=F32)
         + jnp.dot(orw_ref[...], wb_ref[...], preferred_element_type=F32))
    h_ref[...] = h
    u_ref[...] = _rms(h, g_ref[...]).astype(BF16)


def _outproj(x2, o_da, o_rw, wa, wb, g, tm):
    n, d = x2.shape
    da = o_da.shape[1]
    rw = o_rw.shape[1]
    row = lambda i: (i, 0)
    fix = lambda i: (0, 0)
    return pl.pallas_call(
        _outproj_kernel,
        grid=(n // tm,),
        in_specs=[pl.BlockSpec((tm, d), row), pl.BlockSpec((tm, da), row), pl.BlockSpec((tm, rw), row),
                  pl.BlockSpec((da, d), fix), pl.BlockSpec((rw, d), fix), pl.BlockSpec((1, d), fix)],
        out_specs=[pl.BlockSpec((tm, d), row), pl.BlockSpec((tm, d), row)],
        out_shape=[jax.ShapeDtypeStruct((n, d), F32), jax.ShapeDtypeStruct((n, d), BF16)],
        compiler_params=_params("parallel"),
    )(x2, o_da, o_rw, wa, wb, g)


def _topk_rows(s, k, payload=None):
    rows = s.shape[0]
    iota = lax.broadcasted_iota(jnp.int32, s.shape, 0).astype(F32)
    vals, sel = [], []
    for _ in range(k):
        m = jnp.max(s, axis=0, keepdims=True)
        am = jnp.min(jnp.where(s == m, iota, float(rows)), axis=0, keepdims=True)
        hit = iota == am
        vals.append(m)
        sel.append(am if payload is None else jnp.sum(jnp.where(hit, payload, 0.0), axis=0, keepdims=True))
        s = jnp.where(hit, -jnp.inf, s)
    return vals, sel


def _stack_rows(rows_list):
    k = len(rows_list)
    iota = lax.broadcasted_iota(jnp.int32, (k, rows_list[0].shape[1]), 0)
    out = jnp.zeros(iota.shape, rows_list[0].dtype)
    for j, r in enumerate(rows_list):
        out = jnp.where(iota == j, r, out)
    return out


def _peertopk_kernel(u_ref, wq_ref, keys_ref, idx_ref, gate_ref):
    u = u_ref[...]
    half = N_KEYS
    idx_rows, gate_rows = [], []
    for h in range(PEER_HEADS):
        tops = []
        for p in range(2):
            hp = h * 2 + p
            q_t = lax.dot_general(wq_ref[hp * half:(hp + 1) * half, :], u, NT, preferred_element_type=F32)
            s_t = jnp.dot(keys_ref[hp], q_t.astype(BF16), preferred_element_type=F32)
            tops.append(_topk_rows(s_t, PEER_TOPK))
        (v1, i1), (v2, i2) = tops
        pairs = [(i, j) for i in range(PEER_TOPK) for j in range(PEER_TOPK) if (i + 1) * (j + 1) <= PEER_TOPK]
        pad = -len(pairs) % 8
        cand = _stack_rows([v1[i] + v2[j] for i, j in pairs] + [jnp.full_like(v1[0], -jnp.inf)] * pad)
        cidx = _stack_rows([i1[i] * float(N_KEYS) + i2[j] for i, j in pairs] + [jnp.zeros_like(i1[0])] * pad)
        best, idx = _topk_rows(cand, PEER_TOPK, payload=cidx)
        e = [jnp.exp(b - best[0]) for b in best]
        den = functools.reduce(lambda a, b: a + b, e)
        idx_rows.append(_stack_rows(idx))
        gate_rows.append(_stack_rows([x / den for x in e]))
    idx_ref[...] = jnp.concatenate(idx_rows, axis=0).T.astype(jnp.int32)
    gate_ref[...] = jnp.concatenate(gate_rows, axis=0).T


def _peertopk(u_bf, wq_t, keys, tt):
    n, d = u_bf.shape
    hk = PEER_HEADS * PEER_TOPK
    row = lambda i: (i, 0)
    return pl.pallas_call(
        _peertopk_kernel,
        grid=(n // tt,),
        in_specs=[pl.BlockSpec((tt, d), row), pl.BlockSpec(wq_t.shape, lambda i: (0, 0)),
                  pl.BlockSpec(keys.shape, lambda i: (0, 0, 0))],
        out_specs=[pl.BlockSpec((tt, hk), row), pl.BlockSpec((tt, hk), row)],
        out_shape=[jax.ShapeDtypeStruct((n, hk), jnp.int32), jax.ShapeDtypeStruct((n, hk), F32)],
        compiler_params=_params("parallel"),
    )(u_bf, wq_t, keys)


SC_CORES = 2
SC_SUBCORES = 16
SC_LANES = 16
HK = PEER_HEADS * PEER_TOPK
DOT_ROWS = 16
SUM_ROWS = 32
SUM_TOKENS = 8
SUM_UNROLL = 2


def _sc_mesh():
    return plsc.VectorSubcoreMesh(core_axis_name="core", subcore_axis_name="subcore")


def _sc_worker():
    return lax.axis_index("core") * SC_SUBCORES + lax.axis_index("subcore")


def _row_dots(tab, idx_flat, xw):
    p = idx_flat.shape[0]
    w = tab.shape[1]
    t_total = p // HK
    workers = SC_CORES * SC_SUBCORES
    tpw = t_total // workers
    g = min(SUM_TOKENS, tpw)
    r = DOT_ROWS
    ns = HK // r
    ln = SC_LANES
    per_row = LANES // ln
    out_rows = HK // per_row
    assert t_total % workers == 0 and tpw % g == 0 and g % 2 == 0 and w % (2 * ln) == 0 and r % per_row == 0
    buf = pltpu.VMEM((r, w), tab.dtype)
    res = pltpu.VMEM((out_rows, LANES), F32)
    sem = pltpu.SemaphoreType.DMA

    @pl.kernel(out_type=jax.ShapeDtypeStruct((t_total * out_rows, LANES), F32), mesh=_sc_mesh(),
               scratch_types=[pltpu.VMEM((g * HK,), jnp.int32), pltpu.VMEM((g, w), tab.dtype), res, res]
               + [buf] * ns + [sem] * (ns + 2),
               compiler_params=pltpu.CompilerParams(needs_layout_passes=False))
    def dots(t_hbm, i_hbm, x_hbm, o_hbm, idx_v, x_v, res0, res1, *scratch):
        bufs, gsem, osem = scratch[0:ns], scratch[ns:2 * ns], scratch[2 * ns:2 * ns + 2]
        ress = (res0, res1)
        wid = _sc_worker()
        zero = jnp.zeros((ln,), F32)

        @pl.loop(0, tpw // g)
        def _(win):
            tok0 = wid * tpw + win * g
            pltpu.sync_copy(i_hbm.at[pl.ds(tok0 * HK, g * HK)], idx_v)
            pltpu.sync_copy(x_hbm.at[pl.ds(tok0, g)], x_v)

            def gather_of(tl, s):
                return pltpu.make_async_copy(t_hbm.at[idx_v.at[pl.ds(tl * HK + s * r, r)]], bufs[s], gsem[s])

            def out_of(tl, par):
                return pltpu.make_async_copy(ress[par], o_hbm.at[pl.ds((tok0 + tl) * out_rows, out_rows)], osem[par])

            for s in range(ns):
                gather_of(0, s).start()

            @pl.loop(0, g // 2)
            def _(tp):
                for par in range(2):
                    tl = tp * 2 + par
                    ob = ress[par]

                    @pl.when(tl >= 2)
                    def _():
                        out_of(tl - 2, par).wait()

                    for s in range(ns):
                        gather_of(tl, s).wait()
                        rows = bufs[s]

                        def fold(c, accs):
                            col = pl.multiple_of(c * 2 * ln, 2 * ln)
                            xa = plsc.bitcast(x_v[tl, pl.ds(col, ln)], BF16)
                            xb = plsc.bitcast(x_v[tl, pl.ds(col + ln, ln)], BF16)
                            out = []
                            for q in range(r):
                                pa = plsc.bitcast(rows[q, pl.ds(col, ln)], BF16) * xa
                                pb = plsc.bitcast(rows[q, pl.ds(col + ln, ln)], BF16) * xb
                                pair = plsc.bitcast(pa + pb, jnp.uint32)
                                lo = plsc.bitcast(lax.shift_left(pair, jnp.uint32(16)), F32)
                                hi = plsc.bitcast(pair & jnp.uint32(0xFFFF0000), F32)
                                out.append(accs[q] + lo + hi)
                            return tuple(out)

                        accs = lax.fori_loop(0, w // (2 * ln), fold, (zero,) * r)
                        for q in range(r):
                            k = s * r + q
                            ob[k // per_row, pl.ds((k % per_row) * ln, ln)] = accs[q]

                        @pl.when(tl + 1 < g)
                        def _():
                            gather_of(tl + 1, s).start()

                    out_of(tl, par).start()

            for par in range(2):
                out_of(g - 2 + par, par).wait()

    return dots(tab, idx_flat, xw)


def _weighted_row_sum(tab, idx_flat, wrep):
    p = idx_flat.shape[0]
    w = tab.shape[1]
    t_total = p // HK
    workers = SC_CORES * SC_SUBCORES
    tpw = t_total // workers
    g = min(SUM_TOKENS, tpw)
    r = SUM_ROWS
    ns = HK // r
    ln = SC_LANES
    per_row = LANES // ln
    assert t_total % workers == 0 and tpw % g == 0 and g % 2 == 0 and w % ln == 0
    buf = pltpu.VMEM((r, w), tab.dtype)
    acc = pltpu.VMEM((2 * w,), F32)
    sem = pltpu.SemaphoreType.DMA

    @pl.kernel(out_type=jax.ShapeDtypeStruct((t_total, 2 * w), F32), mesh=_sc_mesh(),
               scratch_types=[pltpu.VMEM((g * HK,), jnp.int32), pltpu.VMEM((g * HK // per_row, LANES), F32), acc, acc]
               + [buf] * ns + [sem] * (ns + 2),
               compiler_params=pltpu.CompilerParams(needs_layout_passes=False))
    def wsum(t_hbm, i_hbm, w_hbm, o_hbm, idx_v, w_v, acc0, acc1, *scratch):
        bufs, gsem, osem = scratch[0:ns], scratch[ns:2 * ns], scratch[2 * ns:2 * ns + 2]
        accs = (acc0, acc1)
        wid = _sc_worker()
        zero = jnp.zeros((ln,), F32)

        @pl.loop(0, tpw // g)
        def _(win):
            tok0 = wid * tpw + win * g
            pltpu.sync_copy(i_hbm.at[pl.ds(tok0 * HK, g * HK)], idx_v)
            pltpu.sync_copy(w_hbm.at[pl.ds(tok0 * (HK // per_row), g * HK // per_row)], w_v)

            def gather_of(tl, s):
                return pltpu.make_async_copy(t_hbm.at[idx_v.at[pl.ds(tl * HK + s * r, r)]], bufs[s], gsem[s])

            def out_of(tl, par):
                return pltpu.make_async_copy(accs[par], o_hbm.at[tok0 + tl], osem[par])

            for s in range(ns):
                gather_of(0, s).start()

            @pl.loop(0, g // 2)
            def _(tp):
                for par in range(2):
                    tl = tp * 2 + par
                    ob = accs[par]

                    @pl.when(tl >= 2)
                    def _():
                        out_of(tl - 2, par).wait()

                    for c in range(2 * w // ln):
                        ob[pl.ds(c * ln, ln)] = zero
                    for s in range(ns):
                        gather_of(tl, s).wait()
                        wrow = tl * (HK // per_row) + s * (r // per_row)
                        wk = []
                        for q in range(r):
                            bits = plsc.bitcast(w_v[wrow + q // per_row, pl.ds((q % per_row) * ln, ln)], jnp.uint32)
                            wk.append(plsc.bitcast(bits | lax.shift_right_logical(bits, jnp.uint32(16)), BF16))
                        rows = bufs[s]

                        def fold(c, carry):
                            for half in range(SUM_UNROLL):
                                col = pl.multiple_of((c * SUM_UNROLL + half) * ln, ln)
                                lo_acc = hi_acc = None
                                for q in range(0, r, 2):
                                    pa = plsc.bitcast(rows[q, pl.ds(col, ln)], BF16) * wk[q]
                                    pb = plsc.bitcast(rows[q + 1, pl.ds(col, ln)], BF16) * wk[q + 1]
                                    pair = plsc.bitcast(pa + pb, jnp.uint32)
                                    lo = plsc.bitcast(lax.shift_left(pair, jnp.uint32(16)), F32)
                                    hi = plsc.bitcast(pair & jnp.uint32(0xFFFF0000), F32)
                                    lo_acc = lo if lo_acc is None else lo_acc + lo
                                    hi_acc = hi if hi_acc is None else hi_acc + hi
                                plsc.addupdate(ob.at[pl.ds(col, ln)], lo_acc)
                                plsc.addupdate(ob.at[pl.ds(w + col, ln)], hi_acc)
                            return carry

                        lax.fori_loop(0, w // (ln * SUM_UNROLL), fold, 0)

                        @pl.when(tl + 1 < g)
                        def _():
                            gather_of(tl + 1, s).start()

                    out_of(tl, par).start()

            for par in range(2):
                out_of(g - 2 + par, par).wait()

    return wsum(tab, idx_flat, wrep)


def _peerw_kernel(part_ref, gate_ref, fold_ref, rep_ref, o_ref):
    hid = _mm_split(part_ref[...], fold_ref[...], NN, 3, 1)
    w = gate_ref[...] * (0.5 * hid * (1.0 + lax.erf(hid * (2.0 ** -0.5))))
    o_ref[...] = jnp.dot(w.astype(BF16), rep_ref[...], preferred_element_type=F32)


def _peerw(part, gate, tt):
    n = gate.shape[0]
    wide = HK * SC_LANES
    lane = jnp.arange(wide)
    fold = (lane[:, None] // SC_LANES == jnp.arange(HK)[None, :]).astype(BF16)
    row = lambda i: (i, 0)
    fix = lambda i: (0, 0)
    return pl.pallas_call(
        _peerw_kernel,
        grid=(n // tt,),
        in_specs=[pl.BlockSpec((tt, wide), row), pl.BlockSpec((tt, HK), row), pl.BlockSpec((wide, HK), fix),
                  pl.BlockSpec((HK, wide), fix)],
        out_specs=pl.BlockSpec((tt, wide), row),
        out_shape=jax.ShapeDtypeStruct((n, wide), F32),
        compiler_params=_params("parallel"),
    )(part.reshape(n, wide), gate, fold, fold.T)


def _ple_kernel(h_ref, f_ref, p_ref, g_ref, wg_ref, wp_ref, gf_ref, o_ref):
    h = h_ref[...] + f_ref[...]
    gate = jax.nn.sigmoid(jnp.dot(_rms(h, g_ref[...]).astype(BF16), wg_ref[...], preferred_element_type=F32))
    pp = jnp.dot(p_ref[...].astype(BF16), wp_ref[...], preferred_element_type=F32)
    o_ref[...] = _rms(h + gate * pp, gf_ref[...])


def _ple(h1, ffn, p2, g, wg, wp, gf, tm):
    n, d = h1.shape
    pd = p2.shape[1]
    row = lambda i: (i, 0)
    fix = lambda i: (0, 0)
    return pl.pallas_call(
        _ple_kernel,
        grid=(n // tm,),
        in_specs=[pl.BlockSpec((tm, d), row), pl.BlockSpec((tm, d), row), pl.BlockSpec((tm, pd), row),
                  pl.BlockSpec((1, d), fix), pl.BlockSpec((d, d), fix), pl.BlockSpec((pd, d), fix),
                  pl.BlockSpec((1, d), fix)],
        out_specs=pl.BlockSpec((tm, d), row),
        out_shape=jax.ShapeDtypeStruct((n, d), F32),
        compiler_params=_params("parallel"),
    )(h1, ffn, p2, g, wg, wp, gf)


def _rope_tables(positions):
    half = ROT_DIM // 2
    inv_freq = ROPE_THETA ** (-jnp.arange(half, dtype=F32) * 2.0 / ROT_DIM)
    ang = positions.astype(F32).reshape(-1, 1) * inv_freq
    lane = jnp.arange(LANES)
    d = lane % DA_HEAD_DIM
    cos = jnp.take(jnp.cos(ang), d % half, axis=1)
    sin = jnp.take(jnp.sin(ang), d % half, axis=1)
    c = jnp.where(d < ROT_DIM, cos, 1.0)
    s1 = jnp.where(d < half, -sin, 0.0)
    s2 = jnp.where((d >= half) & (d < ROT_DIM), sin, 0.0)
    return c, s1, s2


def _pack_rows(tab):
    d = tab.shape[1]
    t = tab.astype(BF16)
    pair = jnp.stack([t[:, :d // 2], t[:, d // 2:]], axis=-1)
    return lax.bitcast_convert_type(pair, jnp.uint32)


def _block_diag_ones(width, head):
    i = jnp.arange(width)
    return (i[:, None] // head == i[None, :] // head).astype(F32)


def _tiles(seq):
    return dict(tm=min(256, seq), tq=min(512, seq), units=min(4, seq // RW_UNIT), groups=2, tt_topk=min(256, seq), tt_mix=min(256, seq),
                peer_chunks=2 if seq % 4096 == 0 else 1)


def kernel(x, p, positions, norm_mix_g, w_in, lam_q1, lam_k1, lam_q2, lam_k2, da_subln_g, rw_mu, rw_w0, rw_w_up, rw_a0, rw_a_up, rw_g_up, rw_k_k, rw_k_a, rw_r_k, rw_ln_g, rw_ln_b, w_out, norm_ffn_g, peer_w_q, peer_sub_keys, peer_u, peer_v, norm_ple_g, ple_gate_w, ple_proj_w, norm_final_g):
    batch, seq, d = x.shape
    t = _tiles(seq)
    row = lambda a: a.reshape(1, -1)
    f32 = F32

    w_in_bf = w_in[0].astype(BF16)
    lam = (jnp.exp(jnp.sum(lam_q1[0].astype(f32) * lam_k1[0].astype(f32)))
           - jnp.exp(jnp.sum(lam_q2[0].astype(f32) * lam_k2[0].astype(f32))) + LAM_INIT).reshape(1, 1)
    width = rw_w0.shape[1]
    wup_pad = jnp.concatenate([rw_w_up[0], jnp.zeros((LANES - rw_w_up.shape[1], width), f32)], axis=0)
    aup_pad = jnp.concatenate([jnp.zeros((LANES - rw_a_up.shape[1], width), f32), rw_a_up[0]], axis=0)
    head_ones = _block_diag_ones(width, RW_HEAD)
    w_out_bf = w_out[0].astype(BF16)
    da_w = DA_HEADS * 2 * DA_HEAD_DIM
    keys = peer_sub_keys[0].reshape(PEER_HEADS * 2, N_KEYS, -1).astype(BF16)
    wq_t = peer_w_q[0].T.astype(BF16)
    u_tab = _pack_rows(peer_u[0])
    v_tab = _pack_rows(peer_v[0])
    wg_bf = ple_gate_w[0].astype(BF16)
    wp_bf = ple_proj_w[0].astype(BF16)

    nc = seq // t["peer_chunks"]
    chunks = [slice(c * nc, (c + 1) * nc) for c in range(t["peer_chunks"])]

    tie = lax.optimization_barrier

    def mixers(xs, pos, sums_before):
        rc, rs1, rs2 = _rope_tables(pos)
        qkv, zrw = _inproj(xs, row(norm_mix_g[0]), w_in_bf, rc, rs1, rs2, t["tm"])
        o_da = _attention(qkv, lam, row(da_subln_g[0]), 1, seq, t["tq"])
        rw = _rwprep(zrw, row(rw_mu[0]), row(rw_w0[0]), wup_pad, row(rw_a0[0]), aup_pad, rw_g_up[0],
                     row(rw_k_k[0]), row(rw_k_a[0]), head_ones, seq, t["tm"])
        o_rw = _rwcore(*rw, row(rw_ln_g[0]), row(rw_ln_b[0]), row(rw_r_k[0]), 1, seq, t["units"], t["groups"])
        h1, u2 = _outproj(xs, o_da, o_rw, w_out_bf[:da_w], w_out_bf[da_w:], row(norm_ffn_g[0]), t["tm"])
        idx, gate = _peertopk(u2, wq_t, keys, t["tt_topk"])
        xw = _pack_rows(u2)
        idx_free = [idx[sl].reshape(-1) for sl in chunks]
        idx_c = idx_free
        if sums_before is not None:
            idx_c, sums_before = tie((idx_free, sums_before))
        part = [_row_dots(u_tab, i, xw[sl]) for i, sl in zip(idx_c, chunks)]
        return dict(h1=h1, gate=gate, idx_free=idx_free, idx_c=idx_c, part=part), sums_before

    def weights(s, part):
        return [_peerw(pt, s["gate"][sl], t["tt_mix"]) for pt, sl in zip(part, chunks)]

    def sums(s, wrep):
        return [_weighted_row_sum(v_tab, i, w.reshape(-1, LANES)) for i, w in zip(s["idx_c"], wrep)]

    def out(s, ffn, pb):
        ffn = jnp.concatenate(ffn, axis=0) if len(ffn) > 1 else ffn[0]
        return _ple(s["h1"], ffn, pb, row(norm_ple_g[0]), wg_bf, wp_bf, row(norm_final_g), t["tm"])

    outs = []
    prev, prev_ffn, older = None, None, None
    xs = x[0]
    for b in range(batch + 1):
        cur = None
        if b < batch:
            cur, _ = mixers(xs, positions[b], None)
        if prev is not None:
            part = prev["part"]
            tied = [part] + ([cur["idx_free"]] if cur is not None else []) + ([prev_ffn] if older is not None else [])
            tied = tie(tuple(tied))
            part = tied[0]
            if older is not None:
                outs.append(out(older, tied[-1], p[0, b - 2]))
            wrep = weights(prev, part)
            if b + 1 < batch:
                wrep, xs = tie((wrep, x[b + 1]))
            older, prev_ffn = prev, sums(prev, wrep)
        elif b + 1 < batch:
            xs = x[b + 1]
        prev = cur
    outs.append(out(older, prev_ffn, p[0, batch - 1]))
    return jnp.stack(outs, axis=0)
```

```python
import functools
import math

import jax
import jax.numpy as jnp
from jax import lax
from jax.experimental import pallas as pl
from jax.experimental.pallas import tpu as pltpu
from jax.experimental.pallas import tpu_sc as plsc

F32 = jnp.float32
BF16 = jnp.bfloat16

NORM_EPS = 1e-6
DA_HEADS = 4
DA_HEAD_DIM = 64
ROPE_THETA = 500000.0
ROT_DIM = DA_HEAD_DIM // 4
RW_HEAD = 64
RW_GN_EPS = 64e-5
PEER_HEADS = 8
N_KEYS = 128
PEER_TOPK = 16
LAM_INIT = 0.8 - 0.6 * math.exp(-0.3 * 0)

LANES = 128
VMEM_LIMIT = 56 * 1024 * 1024
RW_CHUNK = 64
RW_UNIT = 2 * RW_CHUNK

NT = (((1,), (1,)), ((), ()))
TN = (((0,), (0,)), ((), ()))
HI = lax.Precision.HIGHEST


def _mm(a, b):
    return jnp.dot(a.astype(BF16), b.astype(BF16), preferred_element_type=F32)


def _mm_nt(a, b):
    return lax.dot_general(a.astype(BF16), b.astype(BF16), NT, preferred_element_type=F32)


def _mm_tn(a, b):
    return lax.dot_general(a.astype(BF16), b.astype(BF16), TN, preferred_element_type=F32)


def _mm_hi(a, b):
    return jnp.dot(a, b, precision=HI, preferred_element_type=F32)


def _pieces(a, n):
    out = []
    for _ in range(n):
        piece = a.astype(BF16)
        out.append(piece)
        a = a - piece.astype(F32)
    return out


def _mm_split(a, b, dims, a_pieces, b_pieces):
    ap, bp = _pieces(a, a_pieces), _pieces(b, b_pieces)
    terms = [lax.dot_general(x, y, dims, preferred_element_type=F32)
             for i, x in enumerate(ap) for j, y in enumerate(bp) if i + j < max(a_pieces, b_pieces)]
    return functools.reduce(lambda u, v: u + v, terms)


NN = (((1,), (0,)), ((), ()))


def _params(*sem):
    return pltpu.CompilerParams(dimension_semantics=sem, vmem_limit_bytes=VMEM_LIMIT)


def _rms(x, g):
    return x * lax.rsqrt(jnp.mean(x * x, axis=-1, keepdims=True) + NORM_EPS) * g


def _inproj_kernel(x_ref, g_ref, w_ref, c_ref, s1_ref, s2_ref, qkv_ref, zrw_ref, *, n_qk, n_da):
    u = _rms(x_ref[...], g_ref[...]).astype(BF16)
    z = jnp.dot(u, w_ref[...], preferred_element_type=F32)
    c, s1, s2 = c_ref[...], s1_ref[...], s2_ref[...]
    half = ROT_DIM // 2
    for blk in range(n_da // LANES):
        t = z[:, blk * LANES:(blk + 1) * LANES]
        if blk < 2 * n_qk // LANES:
            t = t * c + pltpu.roll(t, LANES - half, 1) * s1 + pltpu.roll(t, half, 1) * s2
        if blk < n_qk // LANES:
            t = t * (DA_HEAD_DIM ** -0.5 * math.log2(math.e))
        qkv_ref[:, blk * LANES:(blk + 1) * LANES] = t.astype(BF16)
    zrw_ref[...] = z[:, n_da:]


def _inproj(x2, g, w_in_bf, rc, rs1, rs2, tm):
    n, d = x2.shape
    n_in = w_in_bf.shape[1]
    n_qk = DA_HEADS * 2 * DA_HEAD_DIM
    n_da = 3 * n_qk
    row = lambda i: (i, 0)
    fix = lambda i: (0, 0)
    return pl.pallas_call(
        functools.partial(_inproj_kernel, n_qk=n_qk, n_da=n_da),
        grid=(n // tm,),
        in_specs=[pl.BlockSpec((tm, d), row), pl.BlockSpec((1, d), fix), pl.BlockSpec((d, n_in), fix),
                  pl.BlockSpec((tm, LANES), row), pl.BlockSpec((tm, LANES), row), pl.BlockSpec((tm, LANES), row)],
        out_specs=[pl.BlockSpec((tm, n_da), row), pl.BlockSpec((tm, n_in - n_da), row)],
        out_shape=[jax.ShapeDtypeStruct((n, n_da), BF16), jax.ShapeDtypeStruct((n, n_in - n_da), F32)],
        compiler_params=_params("parallel"),
    )(x2, g, w_in_bf, rc, rs1, rs2)


def _attn_kernel(lam_ref, q_ref, k_ref, v_ref, sg_ref, o_ref, m_ref, acc_ref, *, tq, q_tile0):
    i = pl.program_id(1) + q_tile0
    q = q_ref[...]
    lane = lax.broadcasted_iota(jnp.int32, q.shape, 1)
    zero = jnp.zeros_like(q)
    qs = (jnp.where(lane < DA_HEAD_DIM, q, zero), jnp.where(lane >= DA_HEAD_DIM, q, zero))
    m_ref[...] = jnp.full(m_ref.shape, -jnp.inf, F32)
    acc_ref[...] = jnp.zeros(acc_ref.shape, F32)
    ones = jnp.ones((tq, LANES), BF16)

    def block(j, masked):
        kj = k_ref[pl.ds(pl.multiple_of(j * tq, tq), tq), :]
        vj = jnp.concatenate([v_ref[pl.ds(pl.multiple_of(j * tq, tq), tq), :], ones], axis=1)
        for c in range(2):
            s = lax.dot_general(qs[c], kj, NT, preferred_element_type=F32)
            if masked:
                r_id = lax.broadcasted_iota(jnp.int32, s.shape, 0)
                c_id = lax.broadcasted_iota(jnp.int32, s.shape, 1)
                s = jnp.where(c_id <= r_id, s, -jnp.inf)
            m_old = m_ref[c]
            m_new = jnp.maximum(m_old, jnp.max(s, axis=-1, keepdims=True))
            alpha = jnp.exp2(m_old - m_new)
            p = jnp.exp2(s - jnp.tile(m_new, (1, tq // LANES)))
            pv = jnp.dot(p.astype(BF16), vj, preferred_element_type=F32)
            acc_ref[c] = jnp.tile(alpha, (1, 2)) * acc_ref[c] + pv
            m_ref[c] = m_new

    def body(j, carry):
        block(j, False)
        return carry

    lax.fori_loop(0, i, body, 0)
    block(i, True)
    lam = lam_ref[0, 0]
    a0, a1 = acc_ref[0], acc_ref[1]
    o = a0[:, :LANES] / a0[:, LANES:] - lam * (a1[:, :LANES] / a1[:, LANES:])
    o = o * lax.rsqrt(jnp.mean(o * o, axis=-1, keepdims=True) + NORM_EPS) * sg_ref[...] * (1.0 - LAM_INIT)
    o_ref[...] = o.astype(o_ref.dtype)


def _attention(qkv, lam, subln_g, tq, q_rows):
    kv_rows = qkv.shape[0]
    nq = q_rows // tq
    q_tile0 = (kv_rows - q_rows) // tq
    h = DA_HEADS
    return pl.pallas_call(
        functools.partial(_attn_kernel, tq=tq, q_tile0=q_tile0),
        grid=(h, nq),
        in_specs=[pl.BlockSpec(memory_space=pltpu.SMEM),
                  pl.BlockSpec((tq, LANES), lambda hh, i: (q_tile0 + i, hh)),
                  pl.BlockSpec((kv_rows, LANES), lambda hh, i: (0, h + hh)),
                  pl.BlockSpec((kv_rows, LANES), lambda hh, i: (0, 2 * h + hh)),
                  pl.BlockSpec((1, LANES), lambda hh, i: (0, 0))],
        out_specs=pl.BlockSpec((tq, LANES), lambda hh, i: (i, hh)),
        out_shape=jax.ShapeDtypeStruct((q_rows, h * LANES), BF16),
        scratch_shapes=[pltpu.VMEM((2, tq, LANES), F32), pltpu.VMEM((2, tq, 2 * LANES), F32)],
        compiler_params=_params("parallel", "arbitrary"),
    )(lam, qkv, qkv, qkv, subln_g)


def _rwprep_kernel(z_ref, zp_ref, z0_ref, mu_ref, w0_ref, wup_ref, a0_ref, aup_ref, gup_ref, kk_ref, ka_ref, bd_ref,
                   r_o, ld_o, k_o, v_o, kk_o, b_o, g_o, *, width):
    i = pl.program_id(0)
    z = z_ref[...]
    first = jnp.where(i == 0, z0_ref[...], zp_ref[7:8, :])
    row = lax.broadcasted_iota(jnp.int32, z.shape, 0)
    prev = jnp.where(row == 0, first, pltpu.roll(z, 1, 0))
    zs = z + (prev - z) * mu_ref[...]
    r = zs[:, 0:width]
    k = zs[:, width:2 * width]
    v = zs[:, 2 * width:3 * width]
    xwa = zs[:, 3 * width:3 * width + LANES]
    xg = zs[:, 3 * width + LANES:3 * width + 2 * LANES]
    w = -jax.nn.softplus(-(w0_ref[...] + _mm_hi(jnp.tanh(xwa), wup_ref[...]))) - 0.5
    a = jax.nn.sigmoid(a0_ref[...] + _mm_hi(xwa, aup_ref[...]))
    g = _mm_hi(jax.nn.sigmoid(xg), gup_ref[...])
    kk = k * kk_ref[...]
    kk = kk / jnp.maximum(jnp.sqrt(_mm_hi(kk * kk, bd_ref[...])), 1e-12)
    r_o[...] = r
    ld_o[...] = -jnp.exp(w)
    k_o[...] = k * (1.0 + (a - 1.0) * ka_ref[...])
    v_o[...] = v
    kk_o[...] = kk
    b_o[...] = kk * a
    g_o[...] = g


def _rwprep(zrw, z_before, mu, w0, wup_pad, a0, aup_pad, gup, k_k, k_a, bd, tm):
    n, zin = zrw.shape
    width = w0.shape[1]
    row = lambda i: (i, 0)
    fix = lambda i: (0, 0)
    prev = lambda i: (jnp.maximum(i * (tm // 8) - 1, 0), 0)
    out = jax.ShapeDtypeStruct((n, width), F32)
    return pl.pallas_call(
        functools.partial(_rwprep_kernel, width=width),
        grid=(n // tm,),
        in_specs=[pl.BlockSpec((tm, zin), row), pl.BlockSpec((8, zin), prev), pl.BlockSpec((1, zin), fix),
                  pl.BlockSpec((1, zin), fix),
                  pl.BlockSpec((1, width), fix), pl.BlockSpec((LANES, width), fix),
                  pl.BlockSpec((1, width), fix), pl.BlockSpec((LANES, width), fix), pl.BlockSpec((LANES, width), fix),
                  pl.BlockSpec((1, width), fix), pl.BlockSpec((1, width), fix), pl.BlockSpec((width, width), fix)],
        out_specs=[pl.BlockSpec((tm, width), row)] * 7,
        out_shape=[out] * 7,
        compiler_params=_params("parallel"),
    )(zrw, zrw, z_before, mu, w0, wup_pad, a0, aup_pad, gup, k_k, k_a, bd)


def _rwcore_kernel(r_ref, ld_ref, k_ref, v_ref, kk_ref, b_ref, g_ref, lng_ref, lnb_ref, rk_ref, s0_ref,
                   o_ref, sout_ref, s_ref, *, units, groups):
    U, C, HD = RW_UNIT, RW_CHUNK, RW_HEAD

    @pl.when(pl.program_id(1) == 0)
    def _():
        s_ref[...] = s0_ref[0]

    ri = lax.broadcasted_iota(jnp.int32, (U, U), 0)
    ci = lax.broadcasted_iota(jnp.int32, (U, U), 1)
    same = (ri // C) == (ci // C)
    tri_s = same & (ci < ri)
    tri_i = same & (ci <= ri)
    eye = (ri == ci).astype(F32)
    cum_w = tri_i.astype(F32)
    head_avg = same.astype(F32) * (1.0 / HD)
    head_sum = same.astype(F32)
    hmask = (ci < HD, ci >= HD)
    cmask = (ri < C, ri >= C)
    zero = jnp.zeros((U, U), F32)
    cells = [(u, q) for u in range(units) for q in range(groups)]

    def blk(ref, cell):
        u, q = cell
        return ref[pl.ds(u * U, U), q * LANES:(q + 1) * LANES]

    def vec(ref, q):
        return ref[:, q * LANES:(q + 1) * LANES]

    cell_v, cell_at, cell_bt, cell_rt, cell_bk, cell_gam = {}, {}, {}, {}, {}, {}
    for cell in cells:
        ld = blk(ld_ref, cell)
        cum = _mm_split(cum_w, ld, NN, 1, 3)
        gam = jnp.exp(cum)
        ginv = jnp.exp(-cum)
        bt = blk(b_ref, cell) * ginv
        cell_v[cell] = blk(v_ref, cell)
        cell_at[cell] = -blk(kk_ref, cell) * jnp.exp(cum - ld)
        cell_bt[cell] = bt
        cell_rt[cell] = blk(r_ref, cell) * gam
        cell_bk[cell] = jnp.concatenate([bt, blk(k_ref, cell) * ginv], axis=0)
        cell_gam[cell] = gam

    chains = [(cell, h) for cell in cells for h in range(2)]
    mab, mak, mrb, mrk = {}, {}, {}, {}
    for ch in chains:
        cell, h = ch
        ar = jnp.concatenate([jnp.where(hmask[h], cell_at[cell], zero), jnp.where(hmask[h], cell_rt[cell], zero)], axis=0)
        m = _mm_nt(ar, cell_bk[cell])
        mab[ch] = jnp.where(tri_s, m[0:U, 0:U], zero)
        mak[ch] = jnp.where(tri_s, m[0:U, U:2 * U], zero)
        mrb[ch] = jnp.where(tri_i, m[U:2 * U, 0:U], zero)
        mrk[ch] = jnp.where(tri_i, m[U:2 * U, U:2 * U], zero)

    tm = {ch: eye + mab[ch] for ch in chains}
    pw = dict(mab)
    for _ in range(int(math.log2(C)) - 1):
        pw = {ch: _mm(pw[ch], pw[ch]) for ch in chains}
        tm = {ch: tm[ch] + _mm(tm[ch], pw[ch]) for ch in chains}
    aph = {ch: _mm(tm[ch], cell_at[ch[0]]) for ch in chains}
    mv = {ch: _mm(mak[ch], cell_v[ch[0]]) for ch in chains}
    uh = {ch: _mm(tm[ch], mv[ch]) for ch in chains}
    rph = {ch: _mm(mrb[ch], aph[ch]) for ch in chains}
    yph = {ch: _mm(mrb[ch], uh[ch]) + _mm(mrk[ch], cell_v[ch[0]]) for ch in chains}

    def both_heads(d, cell):
        return jnp.where(hmask[0], d[(cell, 0)], d[(cell, 1)])

    rp, yp, gs, hs = {}, {}, {}, {}
    for cell in cells:
        ap, uu = both_heads(aph, cell), both_heads(uh, cell)
        rp[cell] = both_heads(rph, cell) + cell_rt[cell]
        yp[cell] = both_heads(yph, cell)
        for c in range(2):
            gl = cell_gam[cell][(c + 1) * C - 1:(c + 1) * C, :]
            apc = jnp.where(cmask[c], ap, zero)
            uvc = jnp.concatenate([jnp.where(cmask[c], uu, zero), jnp.where(cmask[c], cell_v[cell], zero)], axis=0)
            gs[cell, c] = jnp.where(same, eye + _mm_tn(apc, cell_bt[cell]), zero) * gl
            hs[cell, c] = jnp.where(same, _mm_tn(uvc, cell_bk[cell]), zero) * gl

    s = [s_ref[q] for q in range(groups)]
    ys = {}
    for u in range(units):
        for q in range(groups):
            cell = (u, q)
            y0 = _mm_split(rp[cell], s[q], NT, 2, 2)
            s[q] = _mm_split(s[q], gs[cell, 0], NN, 2, 2) + hs[cell, 0]
            y1 = _mm_split(rp[cell], s[q], NT, 2, 2)
            s[q] = _mm_split(s[q], gs[cell, 1], NN, 2, 2) + hs[cell, 1]
            ys[cell] = jnp.where(cmask[0], y0, y1) + yp[cell]
    for q in range(groups):
        s_ref[q] = s[q]
        sout_ref[0, q] = s[q]

    for cell in cells:
        u, q = cell
        y = ys[cell]
        mean = _mm_split(y, head_avg, NN, 2, 1)
        yc = y - mean
        var = _mm_split(yc * yc, head_avg, NN, 2, 1)
        yn = yc * lax.rsqrt(var + RW_GN_EPS) * vec(lng_ref, q) + vec(lnb_ref, q)
        bonus = _mm_split(blk(r_ref, cell) * blk(k_ref, cell) * vec(rk_ref, q), head_sum, NN, 2, 1)
        yn = yn + bonus * cell_v[cell]
        o_ref[pl.ds(u * U, U), q * LANES:(q + 1) * LANES] = (yn * blk(g_ref, cell)).astype(o_ref.dtype)


def _rwcore(r, ld, k, v, kk, b, g, ln_g, ln_b, r_k, state, units, groups):
    n, width = r.shape
    rows = units * RW_UNIT
    lanes = groups * LANES
    blk = pl.BlockSpec((rows, lanes), lambda hp, i: (i, hp))
    vec = pl.BlockSpec((1, lanes), lambda hp, i: (0, hp))
    st = pl.BlockSpec((1, groups, RW_UNIT, RW_UNIT), lambda hp, i: (hp, 0, 0, 0))
    return pl.pallas_call(
        functools.partial(_rwcore_kernel, units=units, groups=groups),
        grid=(width // lanes, n // rows),
        in_specs=[blk] * 7 + [vec] * 3 + [st],
        out_specs=[blk, st],
        out_shape=[jax.ShapeDtypeStruct((n, width), BF16), jax.ShapeDtypeStruct(state.shape, F32)],
        scratch_shapes=[pltpu.VMEM((groups, RW_UNIT, RW_UNIT), F32)],
        compiler_params=_params("parallel", "arbitrary"),
    )(r, ld, k, v, kk, b, g, ln_g, ln_b, r_k, state)


def _outproj_kernel(x_ref, oda_ref, orw_ref, wa_ref, wb_ref, g_ref, h_ref, u_ref):
    h = (x_ref[...] + jnp.dot(oda_ref[...], wa_ref[...], preferred_element_type=F32)
         + jnp.dot(orw_ref[...], wb_ref[...], preferred_element_type=F32))
    h_ref[...] = h
    u_ref[...] = _rms(h, g_ref[...]).astype(BF16)


def _outproj(x2, o_da, o_rw, wa, wb, g, tm):
    n, d = x2.shape
    da = o_da.shape[1]
    rw = o_rw.shape[1]
    row = lambda i: (i, 0)
    fix = lambda i: (0, 0)
    return pl.pallas_call(
        _outproj_kernel,
        grid=(n // tm,),
        in_specs=[pl.BlockSpec((tm, d), row), pl.BlockSpec((tm, da), row), pl.BlockSpec((tm, rw), row),
                  pl.BlockSpec((da, d), fix), pl.BlockSpec((rw, d), fix), pl.BlockSpec((1, d), fix)],
        out_specs=[pl.BlockSpec((tm, d), row), pl.BlockSpec((tm, d), row)],
        out_shape=[jax.ShapeDtypeStruct((n, d), F32), jax.ShapeDtypeStruct((n, d), BF16)],
        compiler_params=_params("parallel"),
    )(x2, o_da, o_rw, wa, wb, g)


def _topk_rows(s, k, payload=None):
    rows = s.shape[0]
    iota = lax.broadcasted_iota(jnp.int32, s.shape, 0).astype(F32)
    vals, sel = [], []
    for _ in range(k):
        m = jnp.max(s, axis=0, keepdims=True)
        am = jnp.min(jnp.where(s == m, iota, float(rows)), axis=0, keepdims=True)
        hit = iota == am
        vals.append(m)
        sel.append(am if payload is None else jnp.sum(jnp.where(hit, payload, 0.0), axis=0, keepdims=True))
        s = jnp.where(hit, -jnp.inf, s)
    return vals, sel


def _stack_rows(rows_list):
    k = len(rows_list)
    iota = lax.broadcasted_iota(jnp.int32, (k, rows_list[0].shape[1]), 0)
    out = jnp.zeros(iota.shape, rows_list[0].dtype)
    for j, r in enumerate(rows_list):
        out = jnp.where(iota == j, r, out)
    return out


def _peertopk_kernel(u_ref, wq_ref, keys_ref, idx_ref, gate_ref):
    u = u_ref[...]
    half = N_KEYS
    idx_rows, gate_rows = [], []
    for h in range(PEER_HEADS):
        tops = []
        for p in range(2):
            hp = h * 2 + p
            q_t = lax.dot_general(wq_ref[hp * half:(hp + 1) * half, :], u, NT, preferred_element_type=F32)
            s_t = jnp.dot(keys_ref[hp], q_t.astype(BF16), preferred_element_type=F32)
            tops.append(_topk_rows(s_t, PEER_TOPK))
        (v1, i1), (v2, i2) = tops
        pairs = [(i, j) for i in range(PEER_TOPK) for j in range(PEER_TOPK) if (i + 1) * (j + 1) <= PEER_TOPK]
        pad = -len(pairs) % 8
        cand = _stack_rows([v1[i] + v2[j] for i, j in pairs] + [jnp.full_like(v1[0], -jnp.inf)] * pad)
        cidx = _stack_rows([i1[i] * float(N_KEYS) + i2[j] for i, j in pairs] + [jnp.zeros_like(i1[0])] * pad)
        best, idx = _topk_rows(cand, PEER_TOPK, payload=cidx)
        e = [jnp.exp(b - best[0]) for b in best]
        den = functools.reduce(lambda a, b: a + b, e)
        idx_rows.append(_stack_rows(idx))
        gate_rows.append(_stack_rows([x / den for x in e]))
    idx_ref[...] = jnp.concatenate(idx_rows, axis=0).T.astype(jnp.int32)
    gate_ref[...] = jnp.concatenate(gate_rows, axis=0).T


def _peertopk(u_bf, wq_t, keys, tt):
    n, d = u_bf.shape
    hk = PEER_HEADS * PEER_TOPK
    row = lambda i: (i, 0)
    return pl.pallas_call(
        _peertopk_kernel,
        grid=(n // tt,),
        in_specs=[pl.BlockSpec((tt, d), row), pl.BlockSpec(wq_t.shape, lambda i: (0, 0)),
                  pl.BlockSpec(keys.shape, lambda i: (0, 0, 0))],
        out_specs=[pl.BlockSpec((tt, hk), row), pl.BlockSpec((tt, hk), row)],
        out_shape=[jax.ShapeDtypeStruct((n, hk), jnp.int32), jax.ShapeDtypeStruct((n, hk), F32)],
        compiler_params=_params("parallel"),
    )(u_bf, wq_t, keys)


SC_CORES = 2
SC_SUBCORES = 16
SC_LANES = 16
HK = PEER_HEADS * PEER_TOPK
DOT_ROWS = 16
SUM_ROWS = 32
SUM_TOKENS = 8
SUM_UNROLL = 2


def _tree_sum(xs):
    while len(xs) > 1:
        xs = [xs[i] + xs[i + 1] for i in range(0, len(xs) - 1, 2)] + ([xs[-1]] if len(xs) % 2 else [])
    return xs[0]


def _sc_mesh():
    return plsc.VectorSubcoreMesh(core_axis_name="core", subcore_axis_name="subcore")


def _sc_worker():
    return lax.axis_index("core") * SC_SUBCORES + lax.axis_index("subcore")


def _row_dots(tab, idx_flat, xw):
    p = idx_flat.shape[0]
    w = tab.shape[1]
    t_total = p // HK
    workers = SC_CORES * SC_SUBCORES
    tpw = t_total // workers
    g = min(SUM_TOKENS, tpw)
    r = DOT_ROWS
    ns = HK // r
    ln = SC_LANES
    per_row = LANES // ln
    out_rows = HK // per_row
    assert t_total % workers == 0 and tpw % g == 0 and g % 2 == 0 and w % (2 * ln) == 0 and r % per_row == 0
    buf = pltpu.VMEM((r, w), tab.dtype)
    res = pltpu.VMEM((out_rows, LANES), F32)
    sem = pltpu.SemaphoreType.DMA

    @pl.kernel(out_type=jax.ShapeDtypeStruct((t_total * out_rows, LANES), F32), mesh=_sc_mesh(),
               scratch_types=[pltpu.VMEM((g * HK,), jnp.int32), pltpu.VMEM((g, w), tab.dtype), res, res]
               + [buf] * ns + [sem] * (ns + 2),
               compiler_params=pltpu.CompilerParams(needs_layout_passes=False))
    def dots(t_hbm, i_hbm, x_hbm, o_hbm, idx_v, x_v, res0, res1, *scratch):
        bufs, gsem, osem = scratch[0:ns], scratch[ns:2 * ns], scratch[2 * ns:2 * ns + 2]
        ress = (res0, res1)
        wid = _sc_worker()
        zero = jnp.zeros((ln,), F32)

        @pl.loop(0, tpw // g)
        def _(win):
            tok0 = wid * tpw + win * g
            pltpu.sync_copy(i_hbm.at[pl.ds(tok0 * HK, g * HK)], idx_v)
            pltpu.sync_copy(x_hbm.at[pl.ds(tok0, g)], x_v)

            def gather_of(tl, s):
                return pltpu.make_async_copy(t_hbm.at[idx_v.at[pl.ds(tl * HK + s * r, r)]], bufs[s], gsem[s])

            def out_of(tl, par):
                return pltpu.make_async_copy(ress[par], o_hbm.at[pl.ds((tok0 + tl) * out_rows, out_rows)], osem[par])

            for s in range(ns):
                gather_of(0, s).start()

            @pl.loop(0, g // 2)
            def _(tp):
                for par in range(2):
                    tl = tp * 2 + par
                    ob = ress[par]

                    @pl.when(tl >= 2)
                    def _():
                        out_of(tl - 2, par).wait()

                    for s in range(ns):
                        gather_of(tl, s).wait()
                        rows = bufs[s]

                        def fold(c, accs):
                            col = pl.multiple_of(c * 2 * ln, 2 * ln)
                            xa = plsc.bitcast(x_v[tl, pl.ds(col, ln)], BF16)
                            xb = plsc.bitcast(x_v[tl, pl.ds(col + ln, ln)], BF16)
                            out = []
                            for q in range(r):
                                pa = plsc.bitcast(rows[q, pl.ds(col, ln)], BF16) * xa
                                pb = plsc.bitcast(rows[q, pl.ds(col + ln, ln)], BF16) * xb
                                pair = plsc.bitcast(pa + pb, jnp.uint32)
                                lo = plsc.bitcast(lax.shift_left(pair, jnp.uint32(16)), F32)
                                hi = plsc.bitcast(pair & jnp.uint32(0xFFFF0000), F32)
                                out.append(accs[q] + lo + hi)
                            return tuple(out)

                        accs = lax.fori_loop(0, w // (2 * ln), fold, (zero,) * r)
                        for q in range(r):
                            k = s * r + q
                            ob[k // per_row, pl.ds((k % per_row) * ln, ln)] = accs[q]

                        @pl.when(tl + 1 < g)
                        def _():
                            gather_of(tl + 1, s).start()

                    out_of(tl, par).start()

            for par in range(2):
                out_of(g - 2 + par, par).wait()

    return dots(tab, idx_flat, xw)


def _weighted_row_sum(tab, idx_flat, wrep):
    p = idx_flat.shape[0]
    w = tab.shape[1]
    t_total = p // HK
    workers = SC_CORES * SC_SUBCORES
    tpw = t_total // workers
    g = min(SUM_TOKENS, tpw)
    r = SUM_ROWS
    ns = HK // r
    ln = SC_LANES
    per_row = LANES // ln
    assert t_total % workers == 0 and tpw % g == 0 and g % 2 == 0 and w % ln == 0
    buf = pltpu.VMEM((r, w), tab.dtype)
    acc = pltpu.VMEM((2 * w,), F32)
    sem = pltpu.SemaphoreType.DMA

    @pl.kernel(out_type=jax.ShapeDtypeStruct((t_total, 2 * w), F32), mesh=_sc_mesh(),
               scratch_types=[pltpu.VMEM((g * HK,), jnp.int32), pltpu.VMEM((g * HK // per_row, LANES), F32), acc, acc]
               + [buf] * ns + [sem] * (ns + 2),
               compiler_params=pltpu.CompilerParams(needs_layout_passes=False))
    def wsum(t_hbm, i_hbm, w_hbm, o_hbm, idx_v, w_v, acc0, acc1, *scratch):
        bufs, gsem, osem = scratch[0:ns], scratch[ns:2 * ns], scratch[2 * ns:2 * ns + 2]
        accs = (acc0, acc1)
        wid = _sc_worker()
        zero = jnp.zeros((ln,), F32)

        @pl.loop(0, tpw // g)
        def _(win):
            tok0 = wid * tpw + win * g
            pltpu.sync_copy(i_hbm.at[pl.ds(tok0 * HK, g * HK)], idx_v)
            pltpu.sync_copy(w_hbm.at[pl.ds(tok0 * (HK // per_row), g * HK // per_row)], w_v)

            def gather_of(tl, s):
                return pltpu.make_async_copy(t_hbm.at[idx_v.at[pl.ds(tl * HK + s * r, r)]], bufs[s], gsem[s])

            def out_of(tl, par):
                return pltpu.make_async_copy(accs[par], o_hbm.at[tok0 + tl], osem[par])

            for s in range(ns):
                gather_of(0, s).start()

            @pl.loop(0, g // 2)
            def _(tp):
                for par in range(2):
                    tl = tp * 2 + par
                    ob = accs[par]

                    @pl.when(tl >= 2)
                    def _():
                        out_of(tl - 2, par).wait()

                    for c in range(2 * w // ln):
                        ob[pl.ds(c * ln, ln)] = zero
                    for s in range(ns):
                        gather_of(tl, s).wait()
                        wrow = tl * (HK // per_row) + s * (r // per_row)
                        wk = []
                        for q in range(r):
                            bits = plsc.bitcast(w_v[wrow + q // per_row, pl.ds((q % per_row) * ln, ln)], jnp.uint32)
                            wk.append(plsc.bitcast(bits | lax.shift_right_logical(bits, jnp.uint32(16)), BF16))
                        rows = bufs[s]

                        def fold(c, carry):
                            for half in range(SUM_UNROLL):
                                col = pl.multiple_of((c * SUM_UNROLL + half) * ln, ln)
                                los, his = [], []
                                for q in range(0, r, 2):
                                    pa = plsc.bitcast(rows[q, pl.ds(col, ln)], BF16) * wk[q]
                                    pb = plsc.bitcast(rows[q + 1, pl.ds(col, ln)], BF16) * wk[q + 1]
                                    pair = plsc.bitcast(pa + pb, jnp.uint32)
                                    los.append(plsc.bitcast(lax.shift_left(pair, jnp.uint32(16)), F32))
                                    his.append(plsc.bitcast(pair & jnp.uint32(0xFFFF0000), F32))
                                plsc.addupdate(ob.at[pl.ds(col, ln)], _tree_sum(los))
                                plsc.addupdate(ob.at[pl.ds(w + col, ln)], _tree_sum(his))
                            return carry

                        lax.fori_loop(0, w // (ln * SUM_UNROLL), fold, 0)

                        @pl.when(tl + 1 < g)
                        def _():
                            gather_of(tl + 1, s).start()

                    out_of(tl, par).start()

            for par in range(2):
                out_of(g - 2 + par, par).wait()

    return wsum(tab, idx_flat, wrep)


def _peerw_kernel(part_ref, gate_ref, fold_ref, rep_ref, o_ref):
    hid = _mm_split(part_ref[...], fold_ref[...], NN, 3, 1)
    w = gate_ref[...] * (0.5 * hid * (1.0 + lax.erf(hid * (2.0 ** -0.5))))
    o_ref[...] = jnp.dot(w.astype(BF16), rep_ref[...], preferred_element_type=F32)


def _peerw(part, gate, tt):
    n = gate.shape[0]
    wide = HK * SC_LANES
    lane = jnp.arange(wide)
    fold = (lane[:, None] // SC_LANES == jnp.arange(HK)[None, :]).astype(BF16)
    row = lambda i: (i, 0)
    fix = lambda i: (0, 0)
    return pl.pallas_call(
        _peerw_kernel,
        grid=(n // tt,),
        in_specs=[pl.BlockSpec((tt, wide), row), pl.BlockSpec((tt, HK), row), pl.BlockSpec((wide, HK), fix),
                  pl.BlockSpec((HK, wide), fix)],
        out_specs=pl.BlockSpec((tt, wide), row),
        out_shape=jax.ShapeDtypeStruct((n, wide), F32),
        compiler_params=_params("parallel"),
    )(part.reshape(n, wide), gate, fold, fold.T)


def _ple_kernel(h_ref, f_ref, p_ref, g_ref, wg_ref, wp_ref, gf_ref, o_ref):
    h = h_ref[...] + f_ref[...]
    gate = jax.nn.sigmoid(jnp.dot(_rms(h, g_ref[...]).astype(BF16), wg_ref[...], preferred_element_type=F32))
    pp = jnp.dot(p_ref[...].astype(BF16), wp_ref[...], preferred_element_type=F32)
    o_ref[...] = _rms(h + gate * pp, gf_ref[...])


def _ple(h1, ffn, p2, g, wg, wp, gf, tm):
    n, d = h1.shape
    pd = p2.shape[1]
    row = lambda i: (i, 0)
    fix = lambda i: (0, 0)
    return pl.pallas_call(
        _ple_kernel,
        grid=(n // tm,),
        in_specs=[pl.BlockSpec((tm, d), row), pl.BlockSpec((tm, d), row), pl.BlockSpec((tm, pd), row),
                  pl.BlockSpec((1, d), fix), pl.BlockSpec((d, d), fix), pl.BlockSpec((pd, d), fix),
                  pl.BlockSpec((1, d), fix)],
        out_specs=pl.BlockSpec((tm, d), row),
        out_shape=jax.ShapeDtypeStruct((n, d), F32),
        compiler_params=_params("parallel"),
    )(h1, ffn, p2, g, wg, wp, gf)


def _rope_tables(positions):
    half = ROT_DIM // 2
    inv_freq = ROPE_THETA ** (-jnp.arange(half, dtype=F32) * 2.0 / ROT_DIM)
    ang = positions.astype(F32).reshape(-1, 1) * inv_freq
    lane = jnp.arange(LANES)
    d = lane % DA_HEAD_DIM
    cos = jnp.take(jnp.cos(ang), d % half, axis=1)
    sin = jnp.take(jnp.sin(ang), d % half, axis=1)
    c = jnp.where(d < ROT_DIM, cos, 1.0)
    s1 = jnp.where(d < half, -sin, 0.0)
    s2 = jnp.where((d >= half) & (d < ROT_DIM), sin, 0.0)
    return c, s1, s2


def _pack_rows(tab):
    d = tab.shape[1]
    t = tab.astype(BF16)
    pair = jnp.stack([t[:, :d // 2], t[:, d // 2:]], axis=-1)
    return lax.bitcast_convert_type(pair, jnp.uint32)


def _block_diag_ones(width, head):
    i = jnp.arange(width)
    return (i[:, None] // head == i[None, :] // head).astype(F32)


def _tiles(seq):
    pieces = 2 if seq % 4096 == 0 else 1
    rows = seq // pieces
    return dict(pieces=pieces, rows=rows, tm=min(256, rows), tq=min(512, rows), units=min(4, rows // RW_UNIT), groups=2,
                tt_topk=min(256, rows), tt_mix=min(256, rows))


def kernel(x, p, positions, norm_mix_g, w_in, lam_q1, lam_k1, lam_q2, lam_k2, da_subln_g, rw_mu, rw_w0, rw_w_up, rw_a0, rw_a_up, rw_g_up, rw_k_k, rw_k_a, rw_r_k, rw_ln_g, rw_ln_b, w_out, norm_ffn_g, peer_w_q, peer_sub_keys, peer_u, peer_v, norm_ple_g, ple_gate_w, ple_proj_w, norm_final_g):
    batch, seq, d = x.shape
    t = _tiles(seq)
    row = lambda a: a.reshape(1, -1)
    f32 = F32

    w_in_bf = w_in[0].astype(BF16)
    lam = (jnp.exp(jnp.sum(lam_q1[0].astype(f32) * lam_k1[0].astype(f32)))
           - jnp.exp(jnp.sum(lam_q2[0].astype(f32) * lam_k2[0].astype(f32))) + LAM_INIT).reshape(1, 1)
    width = rw_w0.shape[1]
    wup_pad = jnp.concatenate([rw_w_up[0], jnp.zeros((LANES - rw_w_up.shape[1], width), f32)], axis=0)
    aup_pad = jnp.concatenate([jnp.zeros((LANES - rw_a_up.shape[1], width), f32), rw_a_up[0]], axis=0)
    head_ones = _block_diag_ones(width, RW_HEAD)
    w_out_bf = w_out[0].astype(BF16)
    da_w = DA_HEADS * 2 * DA_HEAD_DIM
    keys = peer_sub_keys[0].reshape(PEER_HEADS * 2, N_KEYS, -1).astype(BF16)
    wq_t = peer_w_q[0].T.astype(BF16)
    u_tab = _pack_rows(peer_u[0])
    v_tab = _pack_rows(peer_v[0])
    wg_bf = ple_gate_w[0].astype(BF16)
    wp_bf = ple_proj_w[0].astype(BF16)

    rows = t["rows"]
    pieces = [(b, h) for b in range(batch) for h in range(t["pieces"])]
    rows_of = lambda a, piece: a[piece[0], piece[1] * rows:(piece[1] + 1) * rows]
    tie = lax.optimization_barrier

    def mixers(piece, xs, carry):
        rc, rs1, rs2 = _rope_tables(rows_of(positions, piece))
        qkv, zrw = _inproj(xs, row(norm_mix_g[0]), w_in_bf, rc, rs1, rs2, t["tm"])
        if carry is None:
            carry = dict(qkv=qkv[:0], z_last=jnp.zeros((1, zrw.shape[1]), f32),
                         state=jnp.zeros((width // (t["groups"] * LANES), t["groups"], RW_UNIT, RW_UNIT), f32))
        qkv = jnp.concatenate([carry["qkv"], qkv], axis=0)
        o_da = _attention(qkv, lam, row(da_subln_g[0]), t["tq"], rows)
        rw = _rwprep(zrw, carry["z_last"], row(rw_mu[0]), row(rw_w0[0]), wup_pad, row(rw_a0[0]), aup_pad, rw_g_up[0],
                     row(rw_k_k[0]), row(rw_k_a[0]), head_ones, t["tm"])
        o_rw, state = _rwcore(*rw, row(rw_ln_g[0]), row(rw_ln_b[0]), row(rw_r_k[0]), carry["state"], t["units"], t["groups"])
        h1, u2 = _outproj(xs, o_da, o_rw, w_out_bf[:da_w], w_out_bf[da_w:], row(norm_ffn_g[0]), t["tm"])
        idx, gate = _peertopk(u2, wq_t, keys, t["tt_topk"])
        idx = idx.reshape(-1)
        part = _row_dots(u_tab, idx, _pack_rows(u2))
        return dict(h1=h1, gate=gate, idx=idx, part=part), dict(qkv=qkv, z_last=zrw[-1:], state=state)

    def out(s, ffn, piece):
        return _ple(s["h1"], ffn, rows_of(p[0], piece), row(norm_ple_g[0]), wg_bf, wp_bf, row(norm_final_g), t["tm"])

    outs = []
    prev, older, older_ffn, carry = None, None, None, None
    xs = rows_of(x, pieces[0])
    for j in range(len(pieces) + 1):
        cur = None
        if j < len(pieces):
            cur, carry = mixers(pieces[j], xs, carry if pieces[j][1] > 0 else None)
        if prev is not None:
            tied = [prev["part"]] + ([cur["idx"]] if cur is not None else []) + ([older_ffn] if older is not None else [])
            tied = tie(tuple(tied))
            if older is not None:
                outs.append(out(older, tied[-1], pieces[j - 2]))
            wrep = _peerw(tied[0], prev["gate"], t["tt_mix"])
            if j + 1 < len(pieces):
                wrep, xs = tie((wrep, rows_of(x, pieces[j + 1])))
            older, older_ffn = prev, _weighted_row_sum(v_tab, prev["idx"], wrep.reshape(-1, LANES))
        elif j + 1 < len(pieces):
            xs = rows_of(x, pieces[j + 1])
        prev = cur
    outs.append(out(older, older_ffn, pieces[-1]))
    return jnp.concatenate(outs, axis=0).reshape(batch, seq, d)
```

```python
import functools
import math

import jax
import jax.numpy as jnp
from jax import lax
from jax.experimental import pallas as pl
from jax.experimental.pallas import tpu as pltpu
from jax.experimental.pallas import tpu_sc as plsc

F32 = jnp.float32
BF16 = jnp.bfloat16

NORM_EPS = 1e-6
DA_HEADS = 4
DA_HEAD_DIM = 64
ROPE_THETA = 500000.0
ROT_DIM = DA_HEAD_DIM // 4
RW_HEAD = 64
RW_GN_EPS = 64e-5
PEER_HEADS = 8
N_KEYS = 128
PEER_TOPK = 16
LAM_INIT = 0.8 - 0.6 * math.exp(-0.3 * 0)

LANES = 128
VMEM_LIMIT = 56 * 1024 * 1024
RW_CHUNK = 64
RW_UNIT = 2 * RW_CHUNK

NT = (((1,), (1,)), ((), ()))
TN = (((0,), (0,)), ((), ()))
HI = lax.Precision.HIGHEST


def _mm(a, b):
    return jnp.dot(a.astype(BF16), b.astype(BF16), preferred_element_type=F32)


def _mm_nt(a, b):
    return lax.dot_general(a.astype(BF16), b.astype(BF16), NT, preferred_element_type=F32)


def _mm_tn(a, b):
    return lax.dot_general(a.astype(BF16), b.astype(BF16), TN, preferred_element_type=F32)


def _mm_hi(a, b):
    return jnp.dot(a, b, precision=HI, preferred_element_type=F32)


def _pieces(a, n):
    out = []
    for _ in range(n):
        piece = a.astype(BF16)
        out.append(piece)
        a = a - piece.astype(F32)
    return out


def _mm_split(a, b, dims, a_pieces, b_pieces):
    ap, bp = _pieces(a, a_pieces), _pieces(b, b_pieces)
    terms = [lax.dot_general(x, y, dims, preferred_element_type=F32)
             for i, x in enumerate(ap) for j, y in enumerate(bp) if i + j < max(a_pieces, b_pieces)]
    return functools.reduce(lambda u, v: u + v, terms)


NN = (((1,), (0,)), ((), ()))


def _params(*sem):
    return pltpu.CompilerParams(dimension_semantics=sem, vmem_limit_bytes=VMEM_LIMIT)


def _rms(x, g):
    return x * lax.rsqrt(jnp.mean(x * x, axis=-1, keepdims=True) + NORM_EPS) * g


def _inproj_kernel(x_ref, g_ref, w_ref, c_ref, s1_ref, s2_ref, qkv_ref, zrw_ref, *, n_qk, n_da):
    u = _rms(x_ref[...], g_ref[...]).astype(BF16)
    z = jnp.dot(u, w_ref[...], preferred_element_type=F32)
    c, s1, s2 = c_ref[...], s1_ref[...], s2_ref[...]
    half = ROT_DIM // 2
    for blk in range(n_da // LANES):
        t = z[:, blk * LANES:(blk + 1) * LANES]
        if blk < 2 * n_qk // LANES:
            t = t * c + pltpu.roll(t, LANES - half, 1) * s1 + pltpu.roll(t, half, 1) * s2
        if blk < n_qk // LANES:
            t = t * (DA_HEAD_DIM ** -0.5 * math.log2(math.e))
        qkv_ref[:, blk * LANES:(blk + 1) * LANES] = t.astype(BF16)
    zrw_ref[...] = z[:, n_da:]


def _inproj(x2, g, w_in_bf, rc, rs1, rs2, tm):
    n, d = x2.shape
    n_in = w_in_bf.shape[1]
    n_qk = DA_HEADS * 2 * DA_HEAD_DIM
    n_da = 3 * n_qk
    row = lambda i: (i, 0)
    fix = lambda i: (0, 0)
    return pl.pallas_call(
        functools.partial(_inproj_kernel, n_qk=n_qk, n_da=n_da),
        grid=(n // tm,),
        in_specs=[pl.BlockSpec((tm, d), row), pl.BlockSpec((1, d), fix), pl.BlockSpec((d, n_in), fix),
                  pl.BlockSpec((tm, LANES), row), pl.BlockSpec((tm, LANES), row), pl.BlockSpec((tm, LANES), row)],
        out_specs=[pl.BlockSpec((tm, n_da), row), pl.BlockSpec((tm, n_in - n_da), row)],
        out_shape=[jax.ShapeDtypeStruct((n, n_da), BF16), jax.ShapeDtypeStruct((n, n_in - n_da), F32)],
        compiler_params=_params("parallel"),
    )(x2, g, w_in_bf, rc, rs1, rs2)


def _attn_kernel(lam_ref, q_ref, k_ref, v_ref, sg_ref, o_ref, m_ref, acc_ref, *, tq, q_tile0):
    i = pl.program_id(1) + q_tile0
    q = q_ref[...]
    lane = lax.broadcasted_iota(jnp.int32, q.shape, 1)
    zero = jnp.zeros_like(q)
    qs = (jnp.where(lane < DA_HEAD_DIM, q, zero), jnp.where(lane >= DA_HEAD_DIM, q, zero))
    m_ref[...] = jnp.full(m_ref.shape, -jnp.inf, F32)
    acc_ref[...] = jnp.zeros(acc_ref.shape, F32)
    ones = jnp.ones((tq, LANES), BF16)

    def block(j, masked):
        kj = k_ref[pl.ds(pl.multiple_of(j * tq, tq), tq), :]
        vj = jnp.concatenate([v_ref[pl.ds(pl.multiple_of(j * tq, tq), tq), :], ones], axis=1)
        for c in range(2):
            s = lax.dot_general(qs[c], kj, NT, preferred_element_type=F32)
            if masked:
                r_id = lax.broadcasted_iota(jnp.int32, s.shape, 0)
                c_id = lax.broadcasted_iota(jnp.int32, s.shape, 1)
                s = jnp.where(c_id <= r_id, s, -jnp.inf)
            m_old = m_ref[c]
            m_new = jnp.maximum(m_old, jnp.max(s, axis=-1, keepdims=True))
            alpha = jnp.exp2(m_old - m_new)
            p = jnp.exp2(s - jnp.tile(m_new, (1, tq // LANES)))
            pv = jnp.dot(p.astype(BF16), vj, preferred_element_type=F32)
            acc_ref[c] = jnp.tile(alpha, (1, 2)) * acc_ref[c] + pv
            m_ref[c] = m_new

    def body(j, carry):
        block(j, False)
        return carry

    lax.fori_loop(0, i, body, 0)
    block(i, True)
    lam = lam_ref[0, 0]
    a0, a1 = acc_ref[0], acc_ref[1]
    o = a0[:, :LANES] / a0[:, LANES:] - lam * (a1[:, :LANES] / a1[:, LANES:])
    o = o * lax.rsqrt(jnp.mean(o * o, axis=-1, keepdims=True) + NORM_EPS) * sg_ref[...] * (1.0 - LAM_INIT)
    o_ref[...] = o.astype(o_ref.dtype)


def _attention(qkv, lam, subln_g, tq, q_rows):
    kv_rows = qkv.shape[0]
    nq = q_rows // tq
    q_tile0 = (kv_rows - q_rows) // tq
    h = DA_HEADS
    return pl.pallas_call(
        functools.partial(_attn_kernel, tq=tq, q_tile0=q_tile0),
        grid=(h, nq),
        in_specs=[pl.BlockSpec(memory_space=pltpu.SMEM),
                  pl.BlockSpec((tq, LANES), lambda hh, i: (q_tile0 + i, hh)),
                  pl.BlockSpec((kv_rows, LANES), lambda hh, i: (0, h + hh)),
                  pl.BlockSpec((kv_rows, LANES), lambda hh, i: (0, 2 * h + hh)),
                  pl.BlockSpec((1, LANES), lambda hh, i: (0, 0))],
        out_specs=pl.BlockSpec((tq, LANES), lambda hh, i: (i, hh)),
        out_shape=jax.ShapeDtypeStruct((q_rows, h * LANES), BF16),
        scratch_shapes=[pltpu.VMEM((2, tq, LANES), F32), pltpu.VMEM((2, tq, 2 * LANES), F32)],
        compiler_params=_params("parallel", "arbitrary"),
    )(lam, qkv, qkv, qkv, subln_g)


def _rwprep_kernel(z_ref, zp_ref, z0_ref, mu_ref, w0_ref, wup_ref, a0_ref, aup_ref, gup_ref, kk_ref, ka_ref, bd_ref,
                   r_o, ld_o, k_o, v_o, kk_o, b_o, g_o, *, width):
    i = pl.program_id(0)
    z = z_ref[...]
    first = jnp.where(i == 0, z0_ref[...], zp_ref[7:8, :])
    row = lax.broadcasted_iota(jnp.int32, z.shape, 0)
    prev = jnp.where(row == 0, first, pltpu.roll(z, 1, 0))
    zs = z + (prev - z) * mu_ref[...]
    r = zs[:, 0:width]
    k = zs[:, width:2 * width]
    v = zs[:, 2 * width:3 * width]
    xwa = zs[:, 3 * width:3 * width + LANES]
    xg = zs[:, 3 * width + LANES:3 * width + 2 * LANES]
    w = -jax.nn.softplus(-(w0_ref[...] + _mm_hi(jnp.tanh(xwa), wup_ref[...]))) - 0.5
    a = jax.nn.sigmoid(a0_ref[...] + _mm_hi(xwa, aup_ref[...]))
    g = _mm_hi(jax.nn.sigmoid(xg), gup_ref[...])
    kk = k * kk_ref[...]
    kk = kk / jnp.maximum(jnp.sqrt(_mm_hi(kk * kk, bd_ref[...])), 1e-12)
    r_o[...] = r
    ld_o[...] = -jnp.exp(w)
    k_o[...] = k * (1.0 + (a - 1.0) * ka_ref[...])
    v_o[...] = v
    kk_o[...] = kk
    b_o[...] = kk * a
    g_o[...] = g


def _rwprep(zrw, z_before, mu, w0, wup_pad, a0, aup_pad, gup, k_k, k_a, bd, tm):
    n, zin = zrw.shape
    width = w0.shape[1]
    row = lambda i: (i, 0)
    fix = lambda i: (0, 0)
    prev = lambda i: (jnp.maximum(i * (tm // 8) - 1, 0), 0)
    out = jax.ShapeDtypeStruct((n, width), F32)
    return pl.pallas_call(
        functools.partial(_rwprep_kernel, width=width),
        grid=(n // tm,),
        in_specs=[pl.BlockSpec((tm, zin), row), pl.BlockSpec((8, zin), prev), pl.BlockSpec((1, zin), fix),
                  pl.BlockSpec((1, zin), fix),
                  pl.BlockSpec((1, width), fix), pl.BlockSpec((LANES, width), fix),
                  pl.BlockSpec((1, width), fix), pl.BlockSpec((LANES, width), fix), pl.BlockSpec((LANES, width), fix),
                  pl.BlockSpec((1, width), fix), pl.BlockSpec((1, width), fix), pl.BlockSpec((width, width), fix)],
        out_specs=[pl.BlockSpec((tm, width), row)] * 7,
        out_shape=[out] * 7,
        compiler_params=_params("parallel"),
    )(zrw, zrw, z_before, mu, w0, wup_pad, a0, aup_pad, gup, k_k, k_a, bd)


def _rwcore_kernel(r_ref, ld_ref, k_ref, v_ref, kk_ref, b_ref, g_ref, lng_ref, lnb_ref, rk_ref, s0_ref,
                   o_ref, sout_ref, s_ref, *, units, groups):
    U, C, HD = RW_UNIT, RW_CHUNK, RW_HEAD

    @pl.when(pl.program_id(1) == 0)
    def _():
        s_ref[...] = s0_ref[0]

    ri = lax.broadcasted_iota(jnp.int32, (U, U), 0)
    ci = lax.broadcasted_iota(jnp.int32, (U, U), 1)
    same = (ri // C) == (ci // C)
    tri_s = same & (ci < ri)
    tri_i = same & (ci <= ri)
    eye = (ri == ci).astype(F32)
    cum_w = tri_i.astype(F32)
    head_avg = same.astype(F32) * (1.0 / HD)
    head_sum = same.astype(F32)
    hmask = (ci < HD, ci >= HD)
    cmask = (ri < C, ri >= C)
    zero = jnp.zeros((U, U), F32)
    cells = [(u, q) for u in range(units) for q in range(groups)]

    def blk(ref, cell):
        u, q = cell
        return ref[pl.ds(u * U, U), q * LANES:(q + 1) * LANES]

    def vec(ref, q):
        return ref[:, q * LANES:(q + 1) * LANES]

    cell_v, cell_at, cell_bt, cell_rt, cell_bk, cell_gam = {}, {}, {}, {}, {}, {}
    for cell in cells:
        ld = blk(ld_ref, cell)
        cum = _mm_split(cum_w, ld, NN, 1, 3)
        gam = jnp.exp(cum)
        ginv = jnp.exp(-cum)
        bt = blk(b_ref, cell) * ginv
        cell_v[cell] = blk(v_ref, cell)
        cell_at[cell] = -blk(kk_ref, cell) * jnp.exp(cum - ld)
        cell_bt[cell] = bt
        cell_rt[cell] = blk(r_ref, cell) * gam
        cell_bk[cell] = jnp.concatenate([bt, blk(k_ref, cell) * ginv], axis=0)
        cell_gam[cell] = gam

    chains = [(cell, h) for cell in cells for h in range(2)]
    mab, mak, mrb, mrk = {}, {}, {}, {}
    for ch in chains:
        cell, h = ch
        ar = jnp.concatenate([jnp.where(hmask[h], cell_at[cell], zero), jnp.where(hmask[h], cell_rt[cell], zero)], axis=0)
        m = _mm_nt(ar, cell_bk[cell])
        mab[ch] = jnp.where(tri_s, m[0:U, 0:U], zero)
        mak[ch] = jnp.where(tri_s, m[0:U, U:2 * U], zero)
        mrb[ch] = jnp.where(tri_i, m[U:2 * U, 0:U], zero)
        mrk[ch] = jnp.where(tri_i, m[U:2 * U, U:2 * U], zero)

    tm = {ch: eye + mab[ch] for ch in chains}
    pw = dict(mab)
    for _ in range(int(math.log2(C)) - 1):
        pw = {ch: _mm(pw[ch], pw[ch]) for ch in chains}
        tm = {ch: tm[ch] + _mm(tm[ch], pw[ch]) for ch in chains}
    aph = {ch: _mm(tm[ch], cell_at[ch[0]]) for ch in chains}
    mv = {ch: _mm(mak[ch], cell_v[ch[0]]) for ch in chains}
    uh = {ch: _mm(tm[ch], mv[ch]) for ch in chains}
    rph = {ch: _mm(mrb[ch], aph[ch]) for ch in chains}
    yph = {ch: _mm(mrb[ch], uh[ch]) + _mm(mrk[ch], cell_v[ch[0]]) for ch in chains}

    def both_heads(d, cell):
        return jnp.where(hmask[0], d[(cell, 0)], d[(cell, 1)])

    rp, yp, gs, hs = {}, {}, {}, {}
    for cell in cells:
        ap, uu = both_heads(aph, cell), both_heads(uh, cell)
        rp[cell] = both_heads(rph, cell) + cell_rt[cell]
        yp[cell] = both_heads(yph, cell)
        for c in range(2):
            gl = cell_gam[cell][(c + 1) * C - 1:(c + 1) * C, :]
            apc = jnp.where(cmask[c], ap, zero)
            uvc = jnp.concatenate([jnp.where(cmask[c], uu, zero), jnp.where(cmask[c], cell_v[cell], zero)], axis=0)
            gs[cell, c] = jnp.where(same, eye + _mm_tn(apc, cell_bt[cell]), zero) * gl
            hs[cell, c] = jnp.where(same, _mm_tn(uvc, cell_bk[cell]), zero) * gl

    s = [s_ref[q] for q in range(groups)]
    ys = {}
    for u in range(units):
        for q in range(groups):
            cell = (u, q)
            y0 = _mm_split(rp[cell], s[q], NT, 2, 2)
            s[q] = _mm_split(s[q], gs[cell, 0], NN, 2, 2) + hs[cell, 0]
            y1 = _mm_split(rp[cell], s[q], NT, 2, 2)
            s[q] = _mm_split(s[q], gs[cell, 1], NN, 2, 2) + hs[cell, 1]
            ys[cell] = jnp.where(cmask[0], y0, y1) + yp[cell]
    for q in range(groups):
        s_ref[q] = s[q]
        sout_ref[0, q] = s[q]

    for cell in cells:
        u, q = cell
        y = ys[cell]
        mean = _mm_split(y, head_avg, NN, 2, 1)
        yc = y - mean
        var = _mm_split(yc * yc, head_avg, NN, 2, 1)
        yn = yc * lax.rsqrt(var + RW_GN_EPS) * vec(lng_ref, q) + vec(lnb_ref, q)
        bonus = _mm_split(blk(r_ref, cell) * blk(k_ref, cell) * vec(rk_ref, q), head_sum, NN, 2, 1)
        yn = yn + bonus * cell_v[cell]
        o_ref[pl.ds(u * U, U), q * LANES:(q + 1) * LANES] = (yn * blk(g_ref, cell)).astype(o_ref.dtype)


def _rwcore(r, ld, k, v, kk, b, g, ln_g, ln_b, r_k, state, units, groups):
    n, width = r.shape
    rows = units * RW_UNIT
    lanes = groups * LANES
    blk = pl.BlockSpec((rows, lanes), lambda hp, i: (i, hp))
    vec = pl.BlockSpec((1, lanes), lambda hp, i: (0, hp))
    st = pl.BlockSpec((1, groups, RW_UNIT, RW_UNIT), lambda hp, i: (hp, 0, 0, 0))
    return pl.pallas_call(
        functools.partial(_rwcore_kernel, units=units, groups=groups),
        grid=(width // lanes, n // rows),
        in_specs=[blk] * 7 + [vec] * 3 + [st],
        out_specs=[blk, st],
        out_shape=[jax.ShapeDtypeStruct((n, width), BF16), jax.ShapeDtypeStruct(state.shape, F32)],
        scratch_shapes=[pltpu.VMEM((groups, RW_UNIT, RW_UNIT), F32)],
        compiler_params=_params("parallel", "arbitrary"),
    )(r, ld, k, v, kk, b, g, ln_g, ln_b, r_k, state)


def _outproj_kernel(x_ref, oda_ref, orw_ref, wa_ref, wb_ref, g_ref, h_ref, u_ref):
    h = (x_ref[...] + jnp.dot(oda_ref[...], wa_ref[...], preferred_element_type=F32)
         + jnp.dot(orw_ref[...], wb_ref[...], preferred_element_type=F32))
    h_ref[...] = h
    u_ref[...] = _rms(h, g_ref[...]).astype(BF16)


def _outproj(x2, o_da, o_rw, wa, wb, g, tm):
    n, d = x2.shape
    da = o_da.shape[1]
    rw = o_rw.shape[1]
    row = lambda i: (i, 0)
    fix = lambda i: (0, 0)
    return pl.pallas_call(
        _outproj_kernel,
        grid=(n // tm,),
        in_specs=[pl.BlockSpec((tm, d), row), pl.BlockSpec((tm, da), row), pl.BlockSpec((tm, rw), row),
                  pl.BlockSpec((da, d), fix), pl.BlockSpec((rw, d), fix), pl.BlockSpec((1, d), fix)],
        out_specs=[pl.BlockSpec((tm, d), row), pl.BlockSpec((tm, d), row)],
        out_shape=[jax.ShapeDtypeStruct((n, d), F32), jax.ShapeDtypeStruct((n, d), BF16)],
        compiler_params=_params("parallel"),
    )(x2, o_da, o_rw, wa, wb, g)


def _topk_rows(s, k, payload=None):
    rows = s.shape[0]
    iota = lax.broadcasted_iota(jnp.int32, s.shape, 0).astype(F32)
    vals, sel = [], []
    for _ in range(k):
        m = jnp.max(s, axis=0, keepdims=True)
        am = jnp.min(jnp.where(s == m, iota, float(rows)), axis=0, keepdims=True)
        hit = iota == am
        vals.append(m)
        sel.append(am if payload is None else jnp.sum(jnp.where(hit, payload, 0.0), axis=0, keepdims=True))
        s = jnp.where(hit, -jnp.inf, s)
    return vals, sel


def _stack_rows(rows_list):
    k = len(rows_list)
    iota = lax.broadcasted_iota(jnp.int32, (k, rows_list[0].shape[1]), 0)
    out = jnp.zeros(iota.shape, rows_list[0].dtype)
    for j, r in enumerate(rows_list):
        out = jnp.where(iota == j, r, out)
    return out


def _peertopk_kernel(u_ref, wq_ref, keys_ref, idx_ref, gate_ref):
    u = u_ref[...]
    half = N_KEYS
    idx_rows, gate_rows = [], []
    for h in range(PEER_HEADS):
        tops = []
        for p in range(2):
            hp = h * 2 + p
            q_t = lax.dot_general(wq_ref[hp * half:(hp + 1) * half, :], u, NT, preferred_element_type=F32)
            s_t = jnp.dot(keys_ref[hp], q_t.astype(BF16), preferred_element_type=F32)
            tops.append(_topk_rows(s_t, PEER_TOPK))
        (v1, i1), (v2, i2) = tops
        pairs = [(i, j) for i in range(PEER_TOPK) for j in range(PEER_TOPK) if (i + 1) * (j + 1) <= PEER_TOPK]
        pad = -len(pairs) % 8
        cand = _stack_rows([v1[i] + v2[j] for i, j in pairs] + [jnp.full_like(v1[0], -jnp.inf)] * pad)
        cidx = _stack_rows([i1[i] * float(N_KEYS) + i2[j] for i, j in pairs] + [jnp.zeros_like(i1[0])] * pad)
        best, idx = _topk_rows(cand, PEER_TOPK, payload=cidx)
        e = [jnp.exp(b - best[0]) for b in best]
        den = functools.reduce(lambda a, b: a + b, e)
        idx_rows.append(_stack_rows(idx))
        gate_rows.append(_stack_rows([x / den for x in e]))
    idx_ref[...] = jnp.concatenate(idx_rows, axis=0).T.astype(jnp.int32)
    gate_ref[...] = jnp.concatenate(gate_rows, axis=0).T


def _peertopk(u_bf, wq_t, keys, tt):
    n, d = u_bf.shape
    hk = PEER_HEADS * PEER_TOPK
    row = lambda i: (i, 0)
    return pl.pallas_call(
        _peertopk_kernel,
        grid=(n // tt,),
        in_specs=[pl.BlockSpec((tt, d), row), pl.BlockSpec(wq_t.shape, lambda i: (0, 0)),
                  pl.BlockSpec(keys.shape, lambda i: (0, 0, 0))],
        out_specs=[pl.BlockSpec((tt, hk), row), pl.BlockSpec((tt, hk), row)],
        out_shape=[jax.ShapeDtypeStruct((n, hk), jnp.int32), jax.ShapeDtypeStruct((n, hk), F32)],
        compiler_params=_params("parallel"),
    )(u_bf, wq_t, keys)


SC_CORES = 2
SC_SUBCORES = 16
SC_LANES = 16
HK = PEER_HEADS * PEER_TOPK
DOT_ROWS = 16
SUM_ROWS = 32
SUM_TOKENS = 8
SUM_UNROLL = 2


def _tree_sum(xs):
    while len(xs) > 1:
        xs = [xs[i] + xs[i + 1] for i in range(0, len(xs) - 1, 2)] + ([xs[-1]] if len(xs) % 2 else [])
    return xs[0]


def _sc_mesh():
    return plsc.VectorSubcoreMesh(core_axis_name="core", subcore_axis_name="subcore")


def _sc_worker():
    return lax.axis_index("core") * SC_SUBCORES + lax.axis_index("subcore")


def _row_dots(tab, idx_flat, xw):
    p = idx_flat.shape[0]
    w = tab.shape[1]
    t_total = p // HK
    workers = SC_CORES * SC_SUBCORES
    tpw = t_total // workers
    g = min(SUM_TOKENS, tpw)
    r = DOT_ROWS
    ns = HK // r
    ln = SC_LANES
    per_row = LANES // ln
    out_rows = HK // per_row
    assert t_total % workers == 0 and tpw % g == 0 and g % 2 == 0 and w % (2 * ln) == 0 and r % per_row == 0
    buf = pltpu.VMEM((r, w), tab.dtype)
    res = pltpu.VMEM((out_rows, LANES), F32)
    sem = pltpu.SemaphoreType.DMA

    @pl.kernel(out_type=jax.ShapeDtypeStruct((t_total * out_rows, LANES), F32), mesh=_sc_mesh(),
               scratch_types=[pltpu.VMEM((g * HK,), jnp.int32), pltpu.VMEM((g, w), tab.dtype), res, res]
               + [buf] * ns + [sem] * (ns + 2),
               compiler_params=pltpu.CompilerParams(needs_layout_passes=False))
    def dots(t_hbm, i_hbm, x_hbm, o_hbm, idx_v, x_v, res0, res1, *scratch):
        bufs, gsem, osem = scratch[0:ns], scratch[ns:2 * ns], scratch[2 * ns:2 * ns + 2]
        ress = (res0, res1)
        wid = _sc_worker()
        zero = jnp.zeros((ln,), F32)

        @pl.loop(0, tpw // g)
        def _(win):
            tok0 = wid * tpw + win * g
            pltpu.sync_copy(i_hbm.at[pl.ds(tok0 * HK, g * HK)], idx_v)
            pltpu.sync_copy(x_hbm.at[pl.ds(tok0, g)], x_v)

            def gather_of(tl, s):
                return pltpu.make_async_copy(t_hbm.at[idx_v.at[pl.ds(tl * HK + s * r, r)]], bufs[s], gsem[s])

            def out_of(tl, par):
                return pltpu.make_async_copy(ress[par], o_hbm.at[pl.ds((tok0 + tl) * out_rows, out_rows)], osem[par])

            for s in range(ns):
                gather_of(0, s).start()

            @pl.loop(0, g // 2)
            def _(tp):
                for par in range(2):
                    tl = tp * 2 + par
                    ob = ress[par]

                    @pl.when(tl >= 2)
                    def _():
                        out_of(tl - 2, par).wait()

                    for s in range(ns):
                        gather_of(tl, s).wait()
                        rows = bufs[s]

                        def fold(c, accs):
                            col = pl.multiple_of(c * 2 * ln, 2 * ln)
                            xa = plsc.bitcast(x_v[tl, pl.ds(col, ln)], BF16)
                            xb = plsc.bitcast(x_v[tl, pl.ds(col + ln, ln)], BF16)
                            out = []
                            for q in range(r):
                                pa = plsc.bitcast(rows[q, pl.ds(col, ln)], BF16) * xa
                                pb = plsc.bitcast(rows[q, pl.ds(col + ln, ln)], BF16) * xb
                                pair = plsc.bitcast(pa + pb, jnp.uint32)
                                lo = plsc.bitcast(lax.shift_left(pair, jnp.uint32(16)), F32)
                                hi = plsc.bitcast(pair & jnp.uint32(0xFFFF0000), F32)
                                out.append(accs[q] + lo + hi)
                            return tuple(out)

                        accs = lax.fori_loop(0, w // (2 * ln), fold, (zero,) * r)
                        for q in range(r):
                            k = s * r + q
                            ob[k // per_row, pl.ds((k % per_row) * ln, ln)] = accs[q]

                        @pl.when(tl + 1 < g)
                        def _():
                            gather_of(tl + 1, s).start()

                    out_of(tl, par).start()

            for par in range(2):
                out_of(g - 2 + par, par).wait()

    return dots(tab, idx_flat, xw)


def _weighted_row_sum(tab, idx_flat, wrep):
    p = idx_flat.shape[0]
    w = tab.shape[1]
    t_total = p // HK
    workers = SC_CORES * SC_SUBCORES
    tpw = t_total // workers
    g = min(SUM_TOKENS, tpw)
    r = SUM_ROWS
    ns = HK // r
    ln = SC_LANES
    per_row = LANES // ln
    assert t_total % workers == 0 and tpw % g == 0 and g % 2 == 0 and w % ln == 0
    buf = pltpu.VMEM((r, w), tab.dtype)
    acc = pltpu.VMEM((2 * w,), F32)
    sem = pltpu.SemaphoreType.DMA

    @pl.kernel(out_type=jax.ShapeDtypeStruct((t_total, 2 * w), F32), mesh=_sc_mesh(),
               scratch_types=[pltpu.VMEM((g * HK,), jnp.int32), pltpu.VMEM((g * HK // per_row, LANES), F32), acc, acc]
               + [buf] * ns + [sem] * (ns + 2),
               compiler_params=pltpu.CompilerParams(needs_layout_passes=False))
    def wsum(t_hbm, i_hbm, w_hbm, o_hbm, idx_v, w_v, acc0, acc1, *scratch):
        bufs, gsem, osem = scratch[0:ns], scratch[ns:2 * ns], scratch[2 * ns:2 * ns + 2]
        accs = (acc0, acc1)
        wid = _sc_worker()
        zero = jnp.zeros((ln,), F32)

        @pl.loop(0, tpw // g)
        def _(win):
            tok0 = wid * tpw + win * g
            pltpu.sync_copy(i_hbm.at[pl.ds(tok0 * HK, g * HK)], idx_v)
            pltpu.sync_copy(w_hbm.at[pl.ds(tok0 * (HK // per_row), g * HK // per_row)], w_v)

            def gather_of(tl, s):
                return pltpu.make_async_copy(t_hbm.at[idx_v.at[pl.ds(tl * HK + s * r, r)]], bufs[s], gsem[s])

            def out_of(tl, par):
                return pltpu.make_async_copy(accs[par], o_hbm.at[tok0 + tl], osem[par])

            for s in range(ns):
                gather_of(0, s).start()

            @pl.loop(0, g // 2)
            def _(tp):
                for par in range(2):
                    tl = tp * 2 + par
                    ob = accs[par]

                    @pl.when(tl >= 2)
                    def _():
                        out_of(tl - 2, par).wait()

                    for c in range(2 * w // ln):
                        ob[pl.ds(c * ln, ln)] = zero
                    for s in range(ns):
                        gather_of(tl, s).wait()
                        wrow = tl * (HK // per_row) + s * (r // per_row)
                        wk = []
                        for q in range(r):
                            bits = plsc.bitcast(w_v[wrow + q // per_row, pl.ds((q % per_row) * ln, ln)], jnp.uint32)
                            wk.append(plsc.bitcast(bits | lax.shift_right_logical(bits, jnp.uint32(16)), BF16))
                        rows = bufs[s]

                        def fold(c, carry):
                            for half in range(SUM_UNROLL):
                                col = pl.multiple_of((c * SUM_UNROLL + half) * ln, ln)
                                los, his = [], []
                                for q in range(0, r, 2):
                                    pa = plsc.bitcast(rows[q, pl.ds(col, ln)], BF16) * wk[q]
                                    pb = plsc.bitcast(rows[q + 1, pl.ds(col, ln)], BF16) * wk[q + 1]
                                    pair = plsc.bitcast(pa + pb, jnp.uint32)
                                    los.append(plsc.bitcast(lax.shift_left(pair, jnp.uint32(16)), F32))
                                    his.append(plsc.bitcast(pair & jnp.uint32(0xFFFF0000), F32))
                                plsc.addupdate(ob.at[pl.ds(col, ln)], _tree_sum(los))
                                plsc.addupdate(ob.at[pl.ds(w + col, ln)], _tree_sum(his))
                            return carry

                        lax.fori_loop(0, w // (ln * SUM_UNROLL), fold, 0)

                        @pl.when(tl + 1 < g)
                        def _():
                            gather_of(tl + 1, s).start()

                    out_of(tl, par).start()

            for par in range(2):
                out_of(g - 2 + par, par).wait()

    return wsum(tab, idx_flat, wrep)


def _peerw_kernel(part_ref, gate_ref, fold_ref, rep_ref, o_ref):
    hid = _mm_split(part_ref[...], fold_ref[...], NN, 3, 1)
    w = gate_ref[...] * (0.5 * hid * (1.0 + lax.erf(hid * (2.0 ** -0.5))))
    o_ref[...] = jnp.dot(w.astype(BF16), rep_ref[...], preferred_element_type=F32)


def _peerw(part, gate, tt):
    n = gate.shape[0]
    wide = HK * SC_LANES
    lane = jnp.arange(wide)
    fold = (lane[:, None] // SC_LANES == jnp.arange(HK)[None, :]).astype(BF16)
    row = lambda i: (i, 0)
    fix = lambda i: (0, 0)
    return pl.pallas_call(
        _peerw_kernel,
        grid=(n // tt,),
        in_specs=[pl.BlockSpec((tt, wide), row), pl.BlockSpec((tt, HK), row), pl.BlockSpec((wide, HK), fix),
                  pl.BlockSpec((HK, wide), fix)],
        out_specs=pl.BlockSpec((tt, wide), row),
        out_shape=jax.ShapeDtypeStruct((n, wide), F32),
        compiler_params=_params("parallel"),
    )(part.reshape(n, wide), gate, fold, fold.T)


def _ple_kernel(h_ref, f_ref, p_ref, g_ref, wg_ref, wp_ref, gf_ref, o_ref):
    h = h_ref[...] + f_ref[...]
    gate = jax.nn.sigmoid(jnp.dot(_rms(h, g_ref[...]).astype(BF16), wg_ref[...], preferred_element_type=F32))
    pp = jnp.dot(p_ref[...].astype(BF16), wp_ref[...], preferred_element_type=F32)
    o_ref[...] = _rms(h + gate * pp, gf_ref[...])


def _ple(h1, ffn, p2, g, wg, wp, gf, tm):
    n, d = h1.shape
    pd = p2.shape[1]
    row = lambda i: (i, 0)
    fix = lambda i: (0, 0)
    return pl.pallas_call(
        _ple_kernel,
        grid=(n // tm,),
        in_specs=[pl.BlockSpec((tm, d), row), pl.BlockSpec((tm, d), row), pl.BlockSpec((tm, pd), row),
                  pl.BlockSpec((1, d), fix), pl.BlockSpec((d, d), fix), pl.BlockSpec((pd, d), fix),
                  pl.BlockSpec((1, d), fix)],
        out_specs=pl.BlockSpec((tm, d), row),
        out_shape=jax.ShapeDtypeStruct((n, d), F32),
        compiler_params=_params("parallel"),
    )(h1, ffn, p2, g, wg, wp, gf)


def _rope_tables(positions):
    half = ROT_DIM // 2
    inv_freq = ROPE_THETA ** (-jnp.arange(half, dtype=F32) * 2.0 / ROT_DIM)
    ang = positions.astype(F32).reshape(-1, 1) * inv_freq
    lane = jnp.arange(LANES)
    d = lane % DA_HEAD_DIM
    cos = jnp.take(jnp.cos(ang), d % half, axis=1)
    sin = jnp.take(jnp.sin(ang), d % half, axis=1)
    c = jnp.where(d < ROT_DIM, cos, 1.0)
    s1 = jnp.where(d < half, -sin, 0.0)
    s2 = jnp.where((d >= half) & (d < ROT_DIM), sin, 0.0)
    return c, s1, s2


def _pack_rows(tab):
    d = tab.shape[1]
    t = tab.astype(BF16)
    pair = jnp.stack([t[:, :d // 2], t[:, d // 2:]], axis=-1)
    return lax.bitcast_convert_type(pair, jnp.uint32)


def _block_diag_ones(width, head):
    i = jnp.arange(width)
    return (i[:, None] // head == i[None, :] // head).astype(F32)


def _tiles(seq):
    pieces = 4 if seq % 8192 == 0 else 1
    rows = seq // pieces
    return dict(pieces=pieces, rows=rows, tm=min(256, rows), tq=min(512, rows), units=min(4, rows // RW_UNIT), groups=2,
                tt_topk=min(256, rows), tt_mix=min(256, rows))


def kernel(x, p, positions, norm_mix_g, w_in, lam_q1, lam_k1, lam_q2, lam_k2, da_subln_g, rw_mu, rw_w0, rw_w_up, rw_a0, rw_a_up, rw_g_up, rw_k_k, rw_k_a, rw_r_k, rw_ln_g, rw_ln_b, w_out, norm_ffn_g, peer_w_q, peer_sub_keys, peer_u, peer_v, norm_ple_g, ple_gate_w, ple_proj_w, norm_final_g):
    batch, seq, d = x.shape
    t = _tiles(seq)
    row = lambda a: a.reshape(1, -1)
    f32 = F32

    w_in_bf = w_in[0].astype(BF16)
    lam = (jnp.exp(jnp.sum(lam_q1[0].astype(f32) * lam_k1[0].astype(f32)))
           - jnp.exp(jnp.sum(lam_q2[0].astype(f32) * lam_k2[0].astype(f32))) + LAM_INIT).reshape(1, 1)
    width = rw_w0.shape[1]
    wup_pad = jnp.concatenate([rw_w_up[0], jnp.zeros((LANES - rw_w_up.shape[1], width), f32)], axis=0)
    aup_pad = jnp.concatenate([jnp.zeros((LANES - rw_a_up.shape[1], width), f32), rw_a_up[0]], axis=0)
    head_ones = _block_diag_ones(width, RW_HEAD)
    w_out_bf = w_out[0].astype(BF16)
    da_w = DA_HEADS * 2 * DA_HEAD_DIM
    keys = peer_sub_keys[0].reshape(PEER_HEADS * 2, N_KEYS, -1).astype(BF16)
    wq_t = peer_w_q[0].T.astype(BF16)
    u_tab = _pack_rows(peer_u[0])
    v_tab = _pack_rows(peer_v[0])
    wg_bf = ple_gate_w[0].astype(BF16)
    wp_bf = ple_proj_w[0].astype(BF16)

    rows = t["rows"]
    pieces = [(b, h) for b in range(batch) for h in range(t["pieces"])]
    rows_of = lambda a, piece: a[piece[0], piece[1] * rows:(piece[1] + 1) * rows]
    tie = lax.optimization_barrier

    def mixers(piece, xs, carry):
        rc, rs1, rs2 = _rope_tables(rows_of(positions, piece))
        qkv, zrw = _inproj(xs, row(norm_mix_g[0]), w_in_bf, rc, rs1, rs2, t["tm"])
        if carry is None:
            carry = dict(qkv=qkv[:0], z_last=jnp.zeros((1, zrw.shape[1]), f32),
                         state=jnp.zeros((width // (t["groups"] * LANES), t["groups"], RW_UNIT, RW_UNIT), f32))
        qkv = jnp.concatenate([carry["qkv"], qkv], axis=0)
        o_da = _attention(qkv, lam, row(da_subln_g[0]), t["tq"], rows)
        rw = _rwprep(zrw, carry["z_last"], row(rw_mu[0]), row(rw_w0[0]), wup_pad, row(rw_a0[0]), aup_pad, rw_g_up[0],
                     row(rw_k_k[0]), row(rw_k_a[0]), head_ones, t["tm"])
        o_rw, state = _rwcore(*rw, row(rw_ln_g[0]), row(rw_ln_b[0]), row(rw_r_k[0]), carry["state"], t["units"], t["groups"])
        h1, u2 = _outproj(xs, o_da, o_rw, w_out_bf[:da_w], w_out_bf[da_w:], row(norm_ffn_g[0]), t["tm"])
        idx, gate = _peertopk(u2, wq_t, keys, t["tt_topk"])
        idx = idx.reshape(-1)
        part = _row_dots(u_tab, idx, _pack_rows(u2))
        return dict(h1=h1, gate=gate, idx=idx, part=part), dict(qkv=qkv, z_last=zrw[-1:], state=state)

    def out(s, ffn, piece):
        return _ple(s["h1"], ffn, rows_of(p[0], piece), row(norm_ple_g[0]), wg_bf, wp_bf, row(norm_final_g), t["tm"])

    outs = []
    prev, older, older_ffn, carry = None, None, None, None
    xs = rows_of(x, pieces[0])
    for j in range(len(pieces) + 1):
        cur = None
        if j < len(pieces):
            cur, carry = mixers(pieces[j], xs, carry if pieces[j][1] > 0 else None)
        if prev is not None:
            tied = [prev["part"]] + ([cur["idx"]] if cur is not None else []) + ([older_ffn] if older is not None else [])
            tied = tie(tuple(tied))
            if older is not None:
                outs.append(out(older, tied[-1], pieces[j - 2]))
            wrep = _peerw(tied[0], prev["gate"], t["tt_mix"])
            if j + 1 < len(pieces):
                wrep, xs = tie((wrep, rows_of(x, pieces[j + 1])))
            older, older_ffn = prev, _weighted_row_sum(v_tab, prev["idx"], wrep.reshape(-1, LANES))
        elif j + 1 < len(pieces):
            xs = rows_of(x, pieces[j + 1])
        prev = cur
    outs.append(out(older, older_ffn, pieces[-1]))
    return jnp.concatenate(outs, axis=0).reshape(batch, seq, d)
```

```python
import functools
import math

import jax
import jax.numpy as jnp
from jax import lax
from jax.experimental import pallas as pl
from jax.experimental.pallas import tpu as pltpu
from jax.experimental.pallas import tpu_sc as plsc

F32 = jnp.float32
BF16 = jnp.bfloat16

NORM_EPS = 1e-6
DA_HEADS = 4
DA_HEAD_DIM = 64
ROPE_THETA = 500000.0
ROT_DIM = DA_HEAD_DIM // 4
RW_HEAD = 64
RW_GN_EPS = 64e-5
PEER_HEADS = 8
N_KEYS = 128
PEER_TOPK = 16
LAM_INIT = 0.8 - 0.6 * math.exp(-0.3 * 0)

LANES = 128
VMEM_LIMIT = 56 * 1024 * 1024
RW_CHUNK = 64
RW_UNIT = 2 * RW_CHUNK

NT = (((1,), (1,)), ((), ()))
TN = (((0,), (0,)), ((), ()))
HI = lax.Precision.HIGHEST


def _mm(a, b):
    return jnp.dot(a.astype(BF16), b.astype(BF16), preferred_element_type=F32)


def _mm_nt(a, b):
    return lax.dot_general(a.astype(BF16), b.astype(BF16), NT, preferred_element_type=F32)


def _mm_tn(a, b):
    return lax.dot_general(a.astype(BF16), b.astype(BF16), TN, preferred_element_type=F32)


def _mm_hi(a, b):
    return jnp.dot(a, b, precision=HI, preferred_element_type=F32)


def _pieces(a, n):
    out = []
    for _ in range(n):
        piece = a.astype(BF16)
        out.append(piece)
        a = a - piece.astype(F32)
    return out


def _mm_split(a, b, dims, a_pieces, b_pieces):
    ap, bp = _pieces(a, a_pieces), _pieces(b, b_pieces)
    terms = [lax.dot_general(x, y, dims, preferred_element_type=F32)
             for i, x in enumerate(ap) for j, y in enumerate(bp) if i + j < max(a_pieces, b_pieces)]
    return functools.reduce(lambda u, v: u + v, terms)


NN = (((1,), (0,)), ((), ()))


def _params(*sem):
    return pltpu.CompilerParams(dimension_semantics=sem, vmem_limit_bytes=VMEM_LIMIT)


def _rms(x, g):
    return x * lax.rsqrt(jnp.mean(x * x, axis=-1, keepdims=True) + NORM_EPS) * g


def _inproj_kernel(x_ref, g_ref, w_ref, c_ref, s1_ref, s2_ref, qkv_ref, zrw_ref, *, n_qk, n_da):
    u = _rms(x_ref[...], g_ref[...]).astype(BF16)
    z = jnp.dot(u, w_ref[...], preferred_element_type=F32)
    c, s1, s2 = c_ref[...], s1_ref[...], s2_ref[...]
    half = ROT_DIM // 2
    for blk in range(n_da // LANES):
        t = z[:, blk * LANES:(blk + 1) * LANES]
        if blk < 2 * n_qk // LANES:
            t = t * c + pltpu.roll(t, LANES - half, 1) * s1 + pltpu.roll(t, half, 1) * s2
        if blk < n_qk // LANES:
            t = t * (DA_HEAD_DIM ** -0.5 * math.log2(math.e))
        qkv_ref[:, blk * LANES:(blk + 1) * LANES] = t.astype(BF16)
    zrw_ref[...] = z[:, n_da:]


def _inproj(x2, g, w_in_bf, rc, rs1, rs2, tm):
    n, d = x2.shape
    n_in = w_in_bf.shape[1]
    n_qk = DA_HEADS * 2 * DA_HEAD_DIM
    n_da = 3 * n_qk
    row = lambda i: (i, 0)
    fix = lambda i: (0, 0)
    return pl.pallas_call(
        functools.partial(_inproj_kernel, n_qk=n_qk, n_da=n_da),
        grid=(n // tm,),
        in_specs=[pl.BlockSpec((tm, d), row), pl.BlockSpec((1, d), fix), pl.BlockSpec((d, n_in), fix),
                  pl.BlockSpec((tm, LANES), row), pl.BlockSpec((tm, LANES), row), pl.BlockSpec((tm, LANES), row)],
        out_specs=[pl.BlockSpec((tm, n_da), row), pl.BlockSpec((tm, n_in - n_da), row)],
        out_shape=[jax.ShapeDtypeStruct((n, n_da), BF16), jax.ShapeDtypeStruct((n, n_in - n_da), F32)],
        compiler_params=_params("parallel"),
    )(x2, g, w_in_bf, rc, rs1, rs2)


def _attn_kernel(lam_ref, q_ref, k_ref, v_ref, sg_ref, o_ref, m_ref, acc_ref, *, tq, q_tile0):
    i = pl.program_id(1) + q_tile0
    q = q_ref[...]
    lane = lax.broadcasted_iota(jnp.int32, q.shape, 1)
    zero = jnp.zeros_like(q)
    qs = (jnp.where(lane < DA_HEAD_DIM, q, zero), jnp.where(lane >= DA_HEAD_DIM, q, zero))
    m_ref[...] = jnp.full(m_ref.shape, -jnp.inf, F32)
    acc_ref[...] = jnp.zeros(acc_ref.shape, F32)
    ones = jnp.ones((tq, LANES), BF16)

    def block(j, masked):
        kj = k_ref[pl.ds(pl.multiple_of(j * tq, tq), tq), :]
        vj = jnp.concatenate([v_ref[pl.ds(pl.multiple_of(j * tq, tq), tq), :], ones], axis=1)
        for c in range(2):
            s = lax.dot_general(qs[c], kj, NT, preferred_element_type=F32)
            if masked:
                r_id = lax.broadcasted_iota(jnp.int32, s.shape, 0)
                c_id = lax.broadcasted_iota(jnp.int32, s.shape, 1)
                s = jnp.where(c_id <= r_id, s, -jnp.inf)
            m_old = m_ref[c]
            m_new = jnp.maximum(m_old, jnp.max(s, axis=-1, keepdims=True))
            alpha = jnp.exp2(m_old - m_new)
            p = jnp.exp2(s - jnp.tile(m_new, (1, tq // LANES)))
            pv = jnp.dot(p.astype(BF16), vj, preferred_element_type=F32)
            acc_ref[c] = jnp.tile(alpha, (1, 2)) * acc_ref[c] + pv
            m_ref[c] = m_new

    def body(j, carry):
        block(j, False)
        return carry

    lax.fori_loop(0, i, body, 0)
    block(i, True)
    lam = lam_ref[0, 0]
    a0, a1 = acc_ref[0], acc_ref[1]
    o = a0[:, :LANES] / a0[:, LANES:] - lam * (a1[:, :LANES] / a1[:, LANES:])
    o = o * lax.rsqrt(jnp.mean(o * o, axis=-1, keepdims=True) + NORM_EPS) * sg_ref[...] * (1.0 - LAM_INIT)
    o_ref[...] = o.astype(o_ref.dtype)


def _attention(qkv, lam, subln_g, tq, q_rows):
    kv_rows = qkv.shape[0]
    nq = q_rows // tq
    q_tile0 = (kv_rows - q_rows) // tq
    h = DA_HEADS
    return pl.pallas_call(
        functools.partial(_attn_kernel, tq=tq, q_tile0=q_tile0),
        grid=(h, nq),
        in_specs=[pl.BlockSpec(memory_space=pltpu.SMEM),
                  pl.BlockSpec((tq, LANES), lambda hh, i: (q_tile0 + i, hh)),
                  pl.BlockSpec((kv_rows, LANES), lambda hh, i: (0, h + hh)),
                  pl.BlockSpec((kv_rows, LANES), lambda hh, i: (0, 2 * h + hh)),
                  pl.BlockSpec((1, LANES), lambda hh, i: (0, 0))],
        out_specs=pl.BlockSpec((tq, LANES), lambda hh, i: (i, hh)),
        out_shape=jax.ShapeDtypeStruct((q_rows, h * LANES), BF16),
        scratch_shapes=[pltpu.VMEM((2, tq, LANES), F32), pltpu.VMEM((2, tq, 2 * LANES), F32)],
        compiler_params=_params("parallel", "arbitrary"),
    )(lam, qkv, qkv, qkv, subln_g)


def _rwprep_kernel(z_ref, zp_ref, z0_ref, mu_ref, w0_ref, wup_ref, a0_ref, aup_ref, gup_ref, kk_ref, ka_ref, bd_ref,
                   r_o, ld_o, k_o, v_o, kk_o, b_o, g_o, *, width):
    i = pl.program_id(0)
    z = z_ref[...]
    first = jnp.where(i == 0, z0_ref[...], zp_ref[7:8, :])
    row = lax.broadcasted_iota(jnp.int32, z.shape, 0)
    prev = jnp.where(row == 0, first, pltpu.roll(z, 1, 0))
    zs = z + (prev - z) * mu_ref[...]
    r = zs[:, 0:width]
    k = zs[:, width:2 * width]
    v = zs[:, 2 * width:3 * width]
    xwa = zs[:, 3 * width:3 * width + LANES]
    xg = zs[:, 3 * width + LANES:3 * width + 2 * LANES]
    w = -jax.nn.softplus(-(w0_ref[...] + _mm_hi(jnp.tanh(xwa), wup_ref[...]))) - 0.5
    a = jax.nn.sigmoid(a0_ref[...] + _mm_hi(xwa, aup_ref[...]))
    g = _mm_hi(jax.nn.sigmoid(xg), gup_ref[...])
    kk = k * kk_ref[...]
    kk = kk / jnp.maximum(jnp.sqrt(_mm_hi(kk * kk, bd_ref[...])), 1e-12)
    r_o[...] = r
    ld_o[...] = -jnp.exp(w)
    k_o[...] = k * (1.0 + (a - 1.0) * ka_ref[...])
    v_o[...] = v
    kk_o[...] = kk
    b_o[...] = kk * a
    g_o[...] = g


def _rwprep(zrw, z_before, mu, w0, wup_pad, a0, aup_pad, gup, k_k, k_a, bd, tm):
    n, zin = zrw.shape
    width = w0.shape[1]
    row = lambda i: (i, 0)
    fix = lambda i: (0, 0)
    prev = lambda i: (jnp.maximum(i * (tm // 8) - 1, 0), 0)
    out = jax.ShapeDtypeStruct((n, width), F32)
    return pl.pallas_call(
        functools.partial(_rwprep_kernel, width=width),
        grid=(n // tm,),
        in_specs=[pl.BlockSpec((tm, zin), row), pl.BlockSpec((8, zin), prev), pl.BlockSpec((1, zin), fix),
                  pl.BlockSpec((1, zin), fix),
                  pl.BlockSpec((1, width), fix), pl.BlockSpec((LANES, width), fix),
                  pl.BlockSpec((1, width), fix), pl.BlockSpec((LANES, width), fix), pl.BlockSpec((LANES, width), fix),
                  pl.BlockSpec((1, width), fix), pl.BlockSpec((1, width), fix), pl.BlockSpec((width, width), fix)],
        out_specs=[pl.BlockSpec((tm, width), row)] * 7,
        out_shape=[out] * 7,
        compiler_params=_params("parallel"),
    )(zrw, zrw, z_before, mu, w0, wup_pad, a0, aup_pad, gup, k_k, k_a, bd)


def _rwcore_kernel(r_ref, ld_ref, k_ref, v_ref, kk_ref, b_ref, g_ref, lng_ref, lnb_ref, rk_ref, s0_ref,
                   o_ref, sout_ref, s_ref, *, units, groups):
    U, C, HD = RW_UNIT, RW_CHUNK, RW_HEAD

    @pl.when(pl.program_id(1) == 0)
    def _():
        s_ref[...] = s0_ref[0]

    ri = lax.broadcasted_iota(jnp.int32, (U, U), 0)
    ci = lax.broadcasted_iota(jnp.int32, (U, U), 1)
    same = (ri // C) == (ci // C)
    tri_s = same & (ci < ri)
    tri_i = same & (ci <= ri)
    eye = (ri == ci).astype(F32)
    cum_w = tri_i.astype(F32)
    head_avg = same.astype(F32) * (1.0 / HD)
    head_sum = same.astype(F32)
    hmask = (ci < HD, ci >= HD)
    cmask = (ri < C, ri >= C)
    zero = jnp.zeros((U, U), F32)
    cells = [(u, q) for u in range(units) for q in range(groups)]

    def blk(ref, cell):
        u, q = cell
        return ref[pl.ds(u * U, U), q * LANES:(q + 1) * LANES]

    def vec(ref, q):
        return ref[:, q * LANES:(q + 1) * LANES]

    cell_v, cell_at, cell_bt, cell_rt, cell_bk, cell_gam = {}, {}, {}, {}, {}, {}
    for cell in cells:
        ld = blk(ld_ref, cell)
        cum = _mm_split(cum_w, ld, NN, 1, 3)
        gam = jnp.exp(cum)
        ginv = jnp.exp(-cum)
        bt = blk(b_ref, cell) * ginv
        cell_v[cell] = blk(v_ref, cell)
        cell_at[cell] = -blk(kk_ref, cell) * jnp.exp(cum - ld)
        cell_bt[cell] = bt
        cell_rt[cell] = blk(r_ref, cell) * gam
        cell_bk[cell] = jnp.concatenate([bt, blk(k_ref, cell) * ginv], axis=0)
        cell_gam[cell] = gam

    chains = [(cell, h) for cell in cells for h in range(2)]
    mab, mak, mrb, mrk = {}, {}, {}, {}
    for ch in chains:
        cell, h = ch
        ar = jnp.concatenate([jnp.where(hmask[h], cell_at[cell], zero), jnp.where(hmask[h], cell_rt[cell], zero)], axis=0)
        m = _mm_nt(ar, cell_bk[cell])
        mab[ch] = jnp.where(tri_s, m[0:U, 0:U], zero)
        mak[ch] = jnp.where(tri_s, m[0:U, U:2 * U], zero)
        mrb[ch] = jnp.where(tri_i, m[U:2 * U, 0:U], zero)
        mrk[ch] = jnp.where(tri_i, m[U:2 * U, U:2 * U], zero)

    tm = {ch: eye + mab[ch] for ch in chains}
    pw = dict(mab)
    for _ in range(int(math.log2(C)) - 1):
        pw = {ch: _mm(pw[ch], pw[ch]) for ch in chains}
        tm = {ch: tm[ch] + _mm(tm[ch], pw[ch]) for ch in chains}
    aph = {ch: _mm(tm[ch], cell_at[ch[0]]) for ch in chains}
    mv = {ch: _mm(mak[ch], cell_v[ch[0]]) for ch in chains}
    uh = {ch: _mm(tm[ch], mv[ch]) for ch in chains}
    rph = {ch: _mm(mrb[ch], aph[ch]) for ch in chains}
    yph = {ch: _mm(mrb[ch], uh[ch]) + _mm(mrk[ch], cell_v[ch[0]]) for ch in chains}

    def both_heads(d, cell):
        return jnp.where(hmask[0], d[(cell, 0)], d[(cell, 1)])

    rp, yp, gs, hs = {}, {}, {}, {}
    for cell in cells:
        ap, uu = both_heads(aph, cell), both_heads(uh, cell)
        rp[cell] = both_heads(rph, cell) + cell_rt[cell]
        yp[cell] = both_heads(yph, cell)
        for c in range(2):
            gl = cell_gam[cell][(c + 1) * C - 1:(c + 1) * C, :]
            apc = jnp.where(cmask[c], ap, zero)
            uvc = jnp.concatenate([jnp.where(cmask[c], uu, zero), jnp.where(cmask[c], cell_v[cell], zero)], axis=0)
            gs[cell, c] = jnp.where(same, eye + _mm_tn(apc, cell_bt[cell]), zero) * gl
            hs[cell, c] = jnp.where(same, _mm_tn(uvc, cell_bk[cell]), zero) * gl

    s = [s_ref[q] for q in range(groups)]
    ys = {}
    for u in range(units):
        for q in range(groups):
            cell = (u, q)
            y0 = _mm_split(rp[cell], s[q], NT, 2, 2)
            s[q] = _mm_split(s[q], gs[cell, 0], NN, 2, 2) + hs[cell, 0]
            y1 = _mm_split(rp[cell], s[q], NT, 2, 2)
            s[q] = _mm_split(s[q], gs[cell, 1], NN, 2, 2) + hs[cell, 1]
            ys[cell] = jnp.where(cmask[0], y0, y1) + yp[cell]
    for q in range(groups):
        s_ref[q] = s[q]
        sout_ref[0, q] = s[q]

    for cell in cells:
        u, q = cell
        y = ys[cell]
        mean = _mm_split(y, head_avg, NN, 2, 1)
        yc = y - mean
        var = _mm_split(yc * yc, head_avg, NN, 2, 1)
        yn = yc * lax.rsqrt(var + RW_GN_EPS) * vec(lng_ref, q) + vec(lnb_ref, q)
        bonus = _mm_split(blk(r_ref, cell) * blk(k_ref, cell) * vec(rk_ref, q), head_sum, NN, 2, 1)
        yn = yn + bonus * cell_v[cell]
        o_ref[pl.ds(u * U, U), q * LANES:(q + 1) * LANES] = (yn * blk(g_ref, cell)).astype(o_ref.dtype)


def _rwcore(r, ld, k, v, kk, b, g, ln_g, ln_b, r_k, state, units, groups):
    n, width = r.shape
    rows = units * RW_UNIT
    lanes = groups * LANES
    blk = pl.BlockSpec((rows, lanes), lambda hp, i: (i, hp))
    vec = pl.BlockSpec((1, lanes), lambda hp, i: (0, hp))
    st = pl.BlockSpec((1, groups, RW_UNIT, RW_UNIT), lambda hp, i: (hp, 0, 0, 0))
    return pl.pallas_call(
        functools.partial(_rwcore_kernel, units=units, groups=groups),
        grid=(width // lanes, n // rows),
        in_specs=[blk] * 7 + [vec] * 3 + [st],
        out_specs=[blk, st],
        out_shape=[jax.ShapeDtypeStruct((n, width), BF16), jax.ShapeDtypeStruct(state.shape, F32)],
        scratch_shapes=[pltpu.VMEM((groups, RW_UNIT, RW_UNIT), F32)],
        compiler_params=_params("parallel", "arbitrary"),
    )(r, ld, k, v, kk, b, g, ln_g, ln_b, r_k, state)


def _outproj_kernel(x_ref, oda_ref, orw_ref, wa_ref, wb_ref, g_ref, h_ref, u_ref):
    h = (x_ref[...] + jnp.dot(oda_ref[...], wa_ref[...], preferred_element_type=F32)
         + jnp.dot(orw_ref[...], wb_ref[...], preferred_element_type=F32))
    h_ref[...] = h
    u_ref[...] = _rms(h, g_ref[...]).astype(BF16)


def _outproj(x2, o_da, o_rw, wa, wb, g, tm):
    n, d = x2.shape
    da = o_da.shape[1]
    rw = o_rw.shape[1]
    row = lambda i: (i, 0)
    fix = lambda i: (0, 0)
    return pl.pallas_call(
        _outproj_kernel,
        grid=(n // tm,),
        in_specs=[pl.BlockSpec((tm, d), row), pl.BlockSpec((tm, da), row), pl.BlockSpec((tm, rw), row),
                  pl.BlockSpec((da, d), fix), pl.BlockSpec((rw, d), fix), pl.BlockSpec((1, d), fix)],
        out_specs=[pl.BlockSpec((tm, d), row), pl.BlockSpec((tm, d), row)],
        out_shape=[jax.ShapeDtypeStruct((n, d), F32), jax.ShapeDtypeStruct((n, d), BF16)],
        compiler_params=_params("parallel"),
    )(x2, o_da, o_rw, wa, wb, g)


def _topk_rows(s, k, payload=None):
    rows = s.shape[0]
    iota = lax.broadcasted_iota(jnp.int32, s.shape, 0).astype(F32)
    vals, sel = [], []
    for _ in range(k):
        m = jnp.max(s, axis=0, keepdims=True)
        am = jnp.min(jnp.where(s == m, iota, float(rows)), axis=0, keepdims=True)
        hit = iota == am
        vals.append(m)
        sel.append(am if payload is None else jnp.sum(jnp.where(hit, payload, 0.0), axis=0, keepdims=True))
        s = jnp.where(hit, -jnp.inf, s)
    return vals, sel


def _stack_rows(rows_list):
    k = len(rows_list)
    iota = lax.broadcasted_iota(jnp.int32, (k, rows_list[0].shape[1]), 0)
    out = jnp.zeros(iota.shape, rows_list[0].dtype)
    for j, r in enumerate(rows_list):
        out = jnp.where(iota == j, r, out)
    return out


def _peertopk_kernel(u_ref, wq_ref, keys_ref, idx_ref, gate_ref):
    u = u_ref[...]
    half = N_KEYS
    idx_rows, gate_rows = [], []
    for h in range(PEER_HEADS):
        tops = []
        for p in range(2):
            hp = h * 2 + p
            q_t = lax.dot_general(wq_ref[hp * half:(hp + 1) * half, :], u, NT, preferred_element_type=F32)
            s_t = jnp.dot(keys_ref[hp], q_t.astype(BF16), preferred_element_type=F32)
            tops.append(_topk_rows(s_t, PEER_TOPK))
        (v1, i1), (v2, i2) = tops
        pairs = [(i, j) for i in range(PEER_TOPK) for j in range(PEER_TOPK) if (i + 1) * (j + 1) <= PEER_TOPK]
        pad = -len(pairs) % 8
        cand = _stack_rows([v1[i] + v2[j] for i, j in pairs] + [jnp.full_like(v1[0], -jnp.inf)] * pad)
        cidx = _stack_rows([i1[i] * float(N_KEYS) + i2[j] for i, j in pairs] + [jnp.zeros_like(i1[0])] * pad)
        best, idx = _topk_rows(cand, PEER_TOPK, payload=cidx)
        e = [jnp.exp(b - best[0]) for b in best]
        den = functools.reduce(lambda a, b: a + b, e)
        idx_rows.append(_stack_rows(idx))
        gate_rows.append(_stack_rows([x / den for x in e]))
    idx_ref[...] = jnp.concatenate(idx_rows, axis=0).T.astype(jnp.int32)
    gate_ref[...] = jnp.concatenate(gate_rows, axis=0).T


def _peertopk(u_bf, wq_t, keys, tt):
    n, d = u_bf.shape
    hk = PEER_HEADS * PEER_TOPK
    row = lambda i: (i, 0)
    return pl.pallas_call(
        _peertopk_kernel,
        grid=(n // tt,),
        in_specs=[pl.BlockSpec((tt, d), row), pl.BlockSpec(wq_t.shape, lambda i: (0, 0)),
                  pl.BlockSpec(keys.shape, lambda i: (0, 0, 0))],
        out_specs=[pl.BlockSpec((tt, hk), row), pl.BlockSpec((tt, hk), row)],
        out_shape=[jax.ShapeDtypeStruct((n, hk), jnp.int32), jax.ShapeDtypeStruct((n, hk), F32)],
        compiler_params=_params("parallel"),
    )(u_bf, wq_t, keys)


SC_CORES = 2
SC_SUBCORES = 16
SC_LANES = 16
HK = PEER_HEADS * PEER_TOPK
DOT_ROWS = 16
SUM_ROWS = 32
SUM_TOKENS = 8
SUM_UNROLL = 2
SUM_BF16_ROWS = 4


def _tree_sum(xs):
    while len(xs) > 1:
        xs = [xs[i] + xs[i + 1] for i in range(0, len(xs) - 1, 2)] + ([xs[-1]] if len(xs) % 2 else [])
    return xs[0]


def _sc_mesh():
    return plsc.VectorSubcoreMesh(core_axis_name="core", subcore_axis_name="subcore")


def _sc_worker():
    return lax.axis_index("core") * SC_SUBCORES + lax.axis_index("subcore")


def _row_dots(tab, idx_flat, xw):
    p = idx_flat.shape[0]
    w = tab.shape[1]
    t_total = p // HK
    workers = SC_CORES * SC_SUBCORES
    tpw = t_total // workers
    g = min(SUM_TOKENS, tpw)
    r = DOT_ROWS
    ns = HK // r
    ln = SC_LANES
    per_row = LANES // ln
    out_rows = HK // per_row
    assert t_total % workers == 0 and tpw % g == 0 and g % 2 == 0 and w % (2 * ln) == 0 and r % per_row == 0
    buf = pltpu.VMEM((r, w), tab.dtype)
    res = pltpu.VMEM((out_rows, LANES), F32)
    sem = pltpu.SemaphoreType.DMA

    @pl.kernel(out_type=jax.ShapeDtypeStruct((t_total * out_rows, LANES), F32), mesh=_sc_mesh(),
               scratch_types=[pltpu.VMEM((g * HK,), jnp.int32), pltpu.VMEM((g, w), tab.dtype), res, res]
               + [buf] * ns + [sem] * (ns + 2),
               compiler_params=pltpu.CompilerParams(needs_layout_passes=False))
    def dots(t_hbm, i_hbm, x_hbm, o_hbm, idx_v, x_v, res0, res1, *scratch):
        bufs, gsem, osem = scratch[0:ns], scratch[ns:2 * ns], scratch[2 * ns:2 * ns + 2]
        ress = (res0, res1)
        wid = _sc_worker()
        zero = jnp.zeros((ln,), F32)

        @pl.loop(0, tpw // g)
        def _(win):
            tok0 = wid * tpw + win * g
            pltpu.sync_copy(i_hbm.at[pl.ds(tok0 * HK, g * HK)], idx_v)
            pltpu.sync_copy(x_hbm.at[pl.ds(tok0, g)], x_v)

            def gather_of(tl, s):
                return pltpu.make_async_copy(t_hbm.at[idx_v.at[pl.ds(tl * HK + s * r, r)]], bufs[s], gsem[s])

            def out_of(tl, par):
                return pltpu.make_async_copy(ress[par], o_hbm.at[pl.ds((tok0 + tl) * out_rows, out_rows)], osem[par])

            for s in range(ns):
                gather_of(0, s).start()

            @pl.loop(0, g // 2)
            def _(tp):
                for par in range(2):
                    tl = tp * 2 + par
                    ob = ress[par]

                    @pl.when(tl >= 2)
                    def _():
                        out_of(tl - 2, par).wait()

                    for s in range(ns):
                        gather_of(tl, s).wait()
                        rows = bufs[s]

                        def fold(c, accs):
                            col = pl.multiple_of(c * 2 * ln, 2 * ln)
                            xa = plsc.bitcast(x_v[tl, pl.ds(col, ln)], BF16)
                            xb = plsc.bitcast(x_v[tl, pl.ds(col + ln, ln)], BF16)
                            out = []
                            for q in range(r):
                                pa = plsc.bitcast(rows[q, pl.ds(col, ln)], BF16) * xa
                                pb = plsc.bitcast(rows[q, pl.ds(col + ln, ln)], BF16) * xb
                                pair = plsc.bitcast(pa + pb, jnp.uint32)
                                lo = plsc.bitcast(lax.shift_left(pair, jnp.uint32(16)), F32)
                                hi = plsc.bitcast(pair & jnp.uint32(0xFFFF0000), F32)
                                out.append(accs[q] + lo + hi)
                            return tuple(out)

                        accs = lax.fori_loop(0, w // (2 * ln), fold, (zero,) * r)
                        for q in range(r):
                            k = s * r + q
                            ob[k // per_row, pl.ds((k % per_row) * ln, ln)] = accs[q]

                        @pl.when(tl + 1 < g)
                        def _():
                            gather_of(tl + 1, s).start()

                    out_of(tl, par).start()

            for par in range(2):
                out_of(g - 2 + par, par).wait()

    return dots(tab, idx_flat, xw)


def _weighted_row_sum(tab, idx_flat, wrep):
    p = idx_flat.shape[0]
    w = tab.shape[1]
    t_total = p // HK
    workers = SC_CORES * SC_SUBCORES
    tpw = t_total // workers
    g = min(SUM_TOKENS, tpw)
    r = SUM_ROWS
    ns = HK // r
    ln = SC_LANES
    per_row = LANES // ln
    assert t_total % workers == 0 and tpw % g == 0 and g % 2 == 0 and w % ln == 0
    buf = pltpu.VMEM((r, w), tab.dtype)
    acc = pltpu.VMEM((2 * w,), F32)
    sem = pltpu.SemaphoreType.DMA

    @pl.kernel(out_type=jax.ShapeDtypeStruct((t_total, 2 * w), F32), mesh=_sc_mesh(),
               scratch_types=[pltpu.VMEM((g * HK,), jnp.int32), pltpu.VMEM((g * HK // per_row, LANES), F32), acc, acc]
               + [buf] * ns + [sem] * (ns + 2),
               compiler_params=pltpu.CompilerParams(needs_layout_passes=False))
    def wsum(t_hbm, i_hbm, w_hbm, o_hbm, idx_v, w_v, acc0, acc1, *scratch):
        bufs, gsem, osem = scratch[0:ns], scratch[ns:2 * ns], scratch[2 * ns:2 * ns + 2]
        accs = (acc0, acc1)
        wid = _sc_worker()
        zero = jnp.zeros((ln,), F32)

        @pl.loop(0, tpw // g)
        def _(win):
            tok0 = wid * tpw + win * g
            pltpu.sync_copy(i_hbm.at[pl.ds(tok0 * HK, g * HK)], idx_v)
            pltpu.sync_copy(w_hbm.at[pl.ds(tok0 * (HK // per_row), g * HK // per_row)], w_v)

            def gather_of(tl, s):
                return pltpu.make_async_copy(t_hbm.at[idx_v.at[pl.ds(tl * HK + s * r, r)]], bufs[s], gsem[s])

            def out_of(tl, par):
                return pltpu.make_async_copy(accs[par], o_hbm.at[tok0 + tl], osem[par])

            for s in range(ns):
                gather_of(0, s).start()

            @pl.loop(0, g // 2)
            def _(tp):
                for par in range(2):
                    tl = tp * 2 + par
                    ob = accs[par]

                    @pl.when(tl >= 2)
                    def _():
                        out_of(tl - 2, par).wait()

                    for c in range(2 * w // ln):
                        ob[pl.ds(c * ln, ln)] = zero
                    for s in range(ns):
                        gather_of(tl, s).wait()
                        wrow = tl * (HK // per_row) + s * (r // per_row)
                        wk = []
                        for q in range(r):
                            bits = plsc.bitcast(w_v[wrow + q // per_row, pl.ds((q % per_row) * ln, ln)], jnp.uint32)
                            wk.append(plsc.bitcast(bits | lax.shift_right_logical(bits, jnp.uint32(16)), BF16))
                        rows = bufs[s]

                        def fold(c, carry):
                            for half in range(SUM_UNROLL):
                                col = pl.multiple_of((c * SUM_UNROLL + half) * ln, ln)
                                los, his = [], []
                                for q in range(0, r, SUM_BF16_ROWS):
                                    prods = [plsc.bitcast(rows[q + i, pl.ds(col, ln)], BF16) * wk[q + i]
                                             for i in range(SUM_BF16_ROWS)]
                                    pair = plsc.bitcast(_tree_sum(prods), jnp.uint32)
                                    los.append(plsc.bitcast(lax.shift_left(pair, jnp.uint32(16)), F32))
                                    his.append(plsc.bitcast(pair & jnp.uint32(0xFFFF0000), F32))
                                plsc.addupdate(ob.at[pl.ds(col, ln)], _tree_sum(los))
                                plsc.addupdate(ob.at[pl.ds(w + col, ln)], _tree_sum(his))
                            return carry

                        lax.fori_loop(0, w // (ln * SUM_UNROLL), fold, 0)

                        @pl.when(tl + 1 < g)
                        def _():
                            gather_of(tl + 1, s).start()

                    out_of(tl, par).start()

            for par in range(2):
                out_of(g - 2 + par, par).wait()

    return wsum(tab, idx_flat, wrep)


def _peerw_kernel(part_ref, gate_ref, fold_ref, rep_ref, o_ref):
    hid = _mm_split(part_ref[...], fold_ref[...], NN, 3, 1)
    w = gate_ref[...] * (0.5 * hid * (1.0 + lax.erf(hid * (2.0 ** -0.5))))
    o_ref[...] = jnp.dot(w.astype(BF16), rep_ref[...], preferred_element_type=F32)


def _peerw(part, gate, tt):
    n = gate.shape[0]
    wide = HK * SC_LANES
    lane = jnp.arange(wide)
    fold = (lane[:, None] // SC_LANES == jnp.arange(HK)[None, :]).astype(BF16)
    row = lambda i: (i, 0)
    fix = lambda i: (0, 0)
    return pl.pallas_call(
        _peerw_kernel,
        grid=(n // tt,),
        in_specs=[pl.BlockSpec((tt, wide), row), pl.BlockSpec((tt, HK), row), pl.BlockSpec((wide, HK), fix),
                  pl.BlockSpec((HK, wide), fix)],
        out_specs=pl.BlockSpec((tt, wide), row),
        out_shape=jax.ShapeDtypeStruct((n, wide), F32),
        compiler_params=_params("parallel"),
    )(part.reshape(n, wide), gate, fold, fold.T)


def _ple_kernel(h_ref, f_ref, p_ref, g_ref, wg_ref, wp_ref, gf_ref, o_ref):
    h = h_ref[...] + f_ref[...]
    gate = jax.nn.sigmoid(jnp.dot(_rms(h, g_ref[...]).astype(BF16), wg_ref[...], preferred_element_type=F32))
    pp = jnp.dot(p_ref[...].astype(BF16), wp_ref[...], preferred_element_type=F32)
    o_ref[...] = _rms(h + gate * pp, gf_ref[...])


def _ple(h1, ffn, p2, g, wg, wp, gf, tm):
    n, d = h1.shape
    pd = p2.shape[1]
    row = lambda i: (i, 0)
    fix = lambda i: (0, 0)
    return pl.pallas_call(
        _ple_kernel,
        grid=(n // tm,),
        in_specs=[pl.BlockSpec((tm, d), row), pl.BlockSpec((tm, d), row), pl.BlockSpec((tm, pd), row),
                  pl.BlockSpec((1, d), fix), pl.BlockSpec((d, d), fix), pl.BlockSpec((pd, d), fix),
                  pl.BlockSpec((1, d), fix)],
        out_specs=pl.BlockSpec((tm, d), row),
        out_shape=jax.ShapeDtypeStruct((n, d), F32),
        compiler_params=_params("parallel"),
    )(h1, ffn, p2, g, wg, wp, gf)


def _rope_tables(positions):
    half = ROT_DIM // 2
    inv_freq = ROPE_THETA ** (-jnp.arange(half, dtype=F32) * 2.0 / ROT_DIM)
    ang = positions.astype(F32).reshape(-1, 1) * inv_freq
    d = jnp.arange(LANES) % DA_HEAD_DIM
    cos = jnp.tile(jnp.cos(ang), (1, LANES // half))
    sin = jnp.tile(jnp.sin(ang), (1, LANES // half))
    c = jnp.where(d < ROT_DIM, cos, 1.0)
    s1 = jnp.where(d < half, -sin, 0.0)
    s2 = jnp.where((d >= half) & (d < ROT_DIM), sin, 0.0)
    return c, s1, s2


def _pack_rows(tab):
    d = tab.shape[1]
    t = tab.astype(BF16)
    pair = jnp.stack([t[:, :d // 2], t[:, d // 2:]], axis=-1)
    return lax.bitcast_convert_type(pair, jnp.uint32)


def _block_diag_ones(width, head):
    i = jnp.arange(width)
    return (i[:, None] // head == i[None, :] // head).astype(F32)


def _tiles(seq):
    pieces = 4 if seq % 8192 == 0 else 1
    rows = seq // pieces
    return dict(pieces=pieces, rows=rows, tm=min(256, rows), tq=min(512, rows), units=min(4, rows // RW_UNIT), groups=2,
                tt_topk=min(256, rows), tt_mix=min(256, rows))


def kernel(x, p, positions, norm_mix_g, w_in, lam_q1, lam_k1, lam_q2, lam_k2, da_subln_g, rw_mu, rw_w0, rw_w_up, rw_a0, rw_a_up, rw_g_up, rw_k_k, rw_k_a, rw_r_k, rw_ln_g, rw_ln_b, w_out, norm_ffn_g, peer_w_q, peer_sub_keys, peer_u, peer_v, norm_ple_g, ple_gate_w, ple_proj_w, norm_final_g):
    batch, seq, d = x.shape
    t = _tiles(seq)
    row = lambda a: a.reshape(1, -1)
    f32 = F32

    w_in_bf = w_in[0].astype(BF16)
    lam = (jnp.exp(jnp.sum(lam_q1[0].astype(f32) * lam_k1[0].astype(f32)))
           - jnp.exp(jnp.sum(lam_q2[0].astype(f32) * lam_k2[0].astype(f32))) + LAM_INIT).reshape(1, 1)
    width = rw_w0.shape[1]
    wup_pad = jnp.concatenate([rw_w_up[0], jnp.zeros((LANES - rw_w_up.shape[1], width), f32)], axis=0)
    aup_pad = jnp.concatenate([jnp.zeros((LANES - rw_a_up.shape[1], width), f32), rw_a_up[0]], axis=0)
    head_ones = _block_diag_ones(width, RW_HEAD)
    w_out_bf = w_out[0].astype(BF16)
    da_w = DA_HEADS * 2 * DA_HEAD_DIM
    keys = peer_sub_keys[0].reshape(PEER_HEADS * 2, N_KEYS, -1).astype(BF16)
    wq_t = peer_w_q[0].T.astype(BF16)
    u_tab = _pack_rows(peer_u[0])
    v_tab = None
    wg_bf = ple_gate_w[0].astype(BF16)
    wp_bf = ple_proj_w[0].astype(BF16)

    rope = [a.reshape(batch, seq, LANES) for a in _rope_tables(positions)]
    rows = t["rows"]
    pieces = [(b, h) for b in range(batch) for h in range(t["pieces"])]
    rows_of = lambda a, piece: a[piece[0], piece[1] * rows:(piece[1] + 1) * rows]
    tie = lax.optimization_barrier

    def mixers(piece, xs, carry):
        rc, rs1, rs2 = (rows_of(a, piece) for a in rope)
        qkv, zrw = _inproj(xs, row(norm_mix_g[0]), w_in_bf, rc, rs1, rs2, t["tm"])
        if carry is None:
            carry = dict(qkv=qkv[:0], z_last=jnp.zeros((1, zrw.shape[1]), f32),
                         state=jnp.zeros((width // (t["groups"] * LANES), t["groups"], RW_UNIT, RW_UNIT), f32))
        qkv = jnp.concatenate([carry["qkv"], qkv], axis=0)
        o_da = _attention(qkv, lam, row(da_subln_g[0]), t["tq"], rows)
        rw = _rwprep(zrw, carry["z_last"], row(rw_mu[0]), row(rw_w0[0]), wup_pad, row(rw_a0[0]), aup_pad, rw_g_up[0],
                     row(rw_k_k[0]), row(rw_k_a[0]), head_ones, t["tm"])
        o_rw, state = _rwcore(*rw, row(rw_ln_g[0]), row(rw_ln_b[0]), row(rw_r_k[0]), carry["state"], t["units"], t["groups"])
        h1, u2 = _outproj(xs, o_da, o_rw, w_out_bf[:da_w], w_out_bf[da_w:], row(norm_ffn_g[0]), t["tm"])
        idx, gate = _peertopk(u2, wq_t, keys, t["tt_topk"])
        idx = idx.reshape(-1)
        part = _row_dots(u_tab, idx, _pack_rows(u2))
        return dict(h1=h1, gate=gate, idx=idx, part=part), dict(qkv=qkv, z_last=zrw[-1:], state=state)

    def out(s, ffn, piece):
        return _ple(s["h1"], ffn, rows_of(p[0], piece), row(norm_ple_g[0]), wg_bf, wp_bf, row(norm_final_g), t["tm"])

    outs = []
    prev, older, older_ffn, carry = None, None, None, None
    xs = rows_of(x, pieces[0])
    for j in range(len(pieces) + 1):
        cur = None
        if j < len(pieces):
            cur, carry = mixers(pieces[j], xs, carry if pieces[j][1] > 0 else None)
        if prev is not None:
            tied = [prev["part"]] + ([cur["idx"]] if cur is not None else []) + ([older_ffn] if older is not None else [])
            tied = tie(tuple(tied))
            if older is not None:
                outs.append(out(older, tied[-1], pieces[j - 2]))
            if v_tab is None:
                v_tab = _pack_rows(tie((peer_v[0], (cur or prev)["idx"]))[0])
            wrep = _peerw(tied[0], prev["gate"], t["tt_mix"])
            if j + 1 < len(pieces):
                wrep, xs = tie((wrep, rows_of(x, pieces[j + 1])))
            older, older_ffn = prev, _weighted_row_sum(v_tab, prev["idx"], wrep.reshape(-1, LANES))
        elif j + 1 < len(pieces):
            xs = rows_of(x, pieces[j + 1])
        prev = cur
    outs.append(out(older, older_ffn, pieces[-1]))
    return jnp.concatenate(outs, axis=0).reshape(batch, seq, d)
```

```python
import functools
import math

import jax
import jax.numpy as jnp
from jax import lax
from jax.experimental import pallas as pl
from jax.experimental.pallas import tpu as pltpu
from jax.experimental.pallas import tpu_sc as plsc

F32 = jnp.float32
BF16 = jnp.bfloat16

NORM_EPS = 1e-6
DA_HEADS = 4
DA_HEAD_DIM = 64
ROPE_THETA = 500000.0
ROT_DIM = DA_HEAD_DIM // 4
RW_HEAD = 64
RW_GN_EPS = 64e-5
PEER_HEADS = 8
N_KEYS = 128
PEER_TOPK = 16
LAM_INIT = 0.8 - 0.6 * math.exp(-0.3 * 0)

LANES = 128
VMEM_LIMIT = 56 * 1024 * 1024
RW_CHUNK = 64
RW_UNIT = 2 * RW_CHUNK

NT = (((1,), (1,)), ((), ()))
TN = (((0,), (0,)), ((), ()))
HI = lax.Precision.HIGHEST


def _mm(a, b):
    return jnp.dot(a.astype(BF16), b.astype(BF16), preferred_element_type=F32)


def _mm_nt(a, b):
    return lax.dot_general(a.astype(BF16), b.astype(BF16), NT, preferred_element_type=F32)


def _mm_tn(a, b):
    return lax.dot_general(a.astype(BF16), b.astype(BF16), TN, preferred_element_type=F32)


def _mm_hi(a, b):
    return jnp.dot(a, b, precision=HI, preferred_element_type=F32)


def _pieces(a, n):
    out = []
    for _ in range(n):
        piece = a.astype(BF16)
        out.append(piece)
        a = a - piece.astype(F32)
    return out


def _mm_split(a, b, dims, a_pieces, b_pieces):
    ap, bp = _pieces(a, a_pieces), _pieces(b, b_pieces)
    terms = [lax.dot_general(x, y, dims, preferred_element_type=F32)
             for i, x in enumerate(ap) for j, y in enumerate(bp) if i + j < max(a_pieces, b_pieces)]
    return functools.reduce(lambda u, v: u + v, terms)


NN = (((1,), (0,)), ((), ()))


def _params(*sem):
    return pltpu.CompilerParams(dimension_semantics=sem, vmem_limit_bytes=VMEM_LIMIT)


def _rms(x, g):
    return x * lax.rsqrt(jnp.mean(x * x, axis=-1, keepdims=True) + NORM_EPS) * g


def _inproj_kernel(x_ref, g_ref, w_ref, c_ref, s1_ref, s2_ref, qkv_ref, zrw_ref, *, n_qk, n_da):
    u = _rms(x_ref[...], g_ref[...]).astype(BF16)
    z = jnp.dot(u, w_ref[...], preferred_element_type=F32)
    c, s1, s2 = c_ref[...], s1_ref[...], s2_ref[...]
    half = ROT_DIM // 2
    for blk in range(n_da // LANES):
        t = z[:, blk * LANES:(blk + 1) * LANES]
        if blk < 2 * n_qk // LANES:
            t = t * c + pltpu.roll(t, LANES - half, 1) * s1 + pltpu.roll(t, half, 1) * s2
        if blk < n_qk // LANES:
            t = t * (DA_HEAD_DIM ** -0.5 * math.log2(math.e))
        qkv_ref[:, blk * LANES:(blk + 1) * LANES] = t.astype(BF16)
    zrw_ref[...] = z[:, n_da:]


def _inproj(x2, g, w_in_bf, rc, rs1, rs2, tm):
    n, d = x2.shape
    n_in = w_in_bf.shape[1]
    n_qk = DA_HEADS * 2 * DA_HEAD_DIM
    n_da = 3 * n_qk
    row = lambda i: (i, 0)
    fix = lambda i: (0, 0)
    return pl.pallas_call(
        functools.partial(_inproj_kernel, n_qk=n_qk, n_da=n_da),
        grid=(n // tm,),
        in_specs=[pl.BlockSpec((tm, d), row), pl.BlockSpec((1, d), fix), pl.BlockSpec((d, n_in), fix),
                  pl.BlockSpec((tm, LANES), row), pl.BlockSpec((tm, LANES), row), pl.BlockSpec((tm, LANES), row)],
        out_specs=[pl.BlockSpec((tm, n_da), row), pl.BlockSpec((tm, n_in - n_da), row)],
        out_shape=[jax.ShapeDtypeStruct((n, n_da), BF16), jax.ShapeDtypeStruct((n, n_in - n_da), F32)],
        compiler_params=_params("parallel"),
    )(x2, g, w_in_bf, rc, rs1, rs2)


def _attn_kernel(lam_ref, q_ref, k_ref, v_ref, sg_ref, o_ref, m_ref, acc_ref, *, tq, q_tile0):
    i = pl.program_id(1) + q_tile0
    q = q_ref[...]
    lane = lax.broadcasted_iota(jnp.int32, q.shape, 1)
    zero = jnp.zeros_like(q)
    qs = (jnp.where(lane < DA_HEAD_DIM, q, zero), jnp.where(lane >= DA_HEAD_DIM, q, zero))
    m_ref[...] = jnp.full(m_ref.shape, -jnp.inf, F32)
    acc_ref[...] = jnp.zeros(acc_ref.shape, F32)
    ones = jnp.ones((tq, LANES), BF16)

    def block(j, masked):
        kj = k_ref[pl.ds(pl.multiple_of(j * tq, tq), tq), :]
        vj = jnp.concatenate([v_ref[pl.ds(pl.multiple_of(j * tq, tq), tq), :], ones], axis=1)
        for c in range(2):
            s = lax.dot_general(qs[c], kj, NT, preferred_element_type=F32)
            if masked:
                r_id = lax.broadcasted_iota(jnp.int32, s.shape, 0)
                c_id = lax.broadcasted_iota(jnp.int32, s.shape, 1)
                s = jnp.where(c_id <= r_id, s, -jnp.inf)
            m_old = m_ref[c]
            m_new = jnp.maximum(m_old, jnp.max(s, axis=-1, keepdims=True))
            alpha = jnp.exp2(m_old - m_new)
            p = jnp.exp2(s - jnp.tile(m_new, (1, tq // LANES)))
            pv = jnp.dot(p.astype(BF16), vj, preferred_element_type=F32)
            acc_ref[c] = jnp.tile(alpha, (1, 2)) * acc_ref[c] + pv
            m_ref[c] = m_new

    def body(j, carry):
        block(j, False)
        return carry

    lax.fori_loop(0, i, body, 0)
    block(i, True)
    lam = lam_ref[0, 0]
    a0, a1 = acc_ref[0], acc_ref[1]
    o = a0[:, :LANES] / a0[:, LANES:] - lam * (a1[:, :LANES] / a1[:, LANES:])
    o = o * lax.rsqrt(jnp.mean(o * o, axis=-1, keepdims=True) + NORM_EPS) * sg_ref[...] * (1.0 - LAM_INIT)
    o_ref[...] = o.astype(o_ref.dtype)


def _attention(qkv, lam, subln_g, tq, q_rows):
    kv_rows = qkv.shape[0]
    nq = q_rows // tq
    q_tile0 = (kv_rows - q_rows) // tq
    h = DA_HEADS
    return pl.pallas_call(
        functools.partial(_attn_kernel, tq=tq, q_tile0=q_tile0),
        grid=(h, nq),
        in_specs=[pl.BlockSpec(memory_space=pltpu.SMEM),
                  pl.BlockSpec((tq, LANES), lambda hh, i: (q_tile0 + i, hh)),
                  pl.BlockSpec((kv_rows, LANES), lambda hh, i: (0, h + hh)),
                  pl.BlockSpec((kv_rows, LANES), lambda hh, i: (0, 2 * h + hh)),
                  pl.BlockSpec((1, LANES), lambda hh, i: (0, 0))],
        out_specs=pl.BlockSpec((tq, LANES), lambda hh, i: (i, hh)),
        out_shape=jax.ShapeDtypeStruct((q_rows, h * LANES), BF16),
        scratch_shapes=[pltpu.VMEM((2, tq, LANES), F32), pltpu.VMEM((2, tq, 2 * LANES), F32)],
        compiler_params=_params("parallel", "arbitrary"),
    )(lam, qkv, qkv, qkv, subln_g)


def _rwprep_kernel(z_ref, zp_ref, z0_ref, mu_ref, w0_ref, wup_ref, a0_ref, aup_ref, gup_ref, kk_ref, ka_ref, bd_ref,
                   r_o, ld_o, k_o, v_o, kk_o, b_o, g_o, *, width):
    i = pl.program_id(0)
    z = z_ref[...]
    first = jnp.where(i == 0, z0_ref[...], zp_ref[7:8, :])
    row = lax.broadcasted_iota(jnp.int32, z.shape, 0)
    prev = jnp.where(row == 0, first, pltpu.roll(z, 1, 0))
    zs = z + (prev - z) * mu_ref[...]
    r = zs[:, 0:width]
    k = zs[:, width:2 * width]
    v = zs[:, 2 * width:3 * width]
    xwa = zs[:, 3 * width:3 * width + LANES]
    xg = zs[:, 3 * width + LANES:3 * width + 2 * LANES]
    w = -jax.nn.softplus(-(w0_ref[...] + _mm_hi(jnp.tanh(xwa), wup_ref[...]))) - 0.5
    a = jax.nn.sigmoid(a0_ref[...] + _mm_hi(xwa, aup_ref[...]))
    g = _mm_hi(jax.nn.sigmoid(xg), gup_ref[...])
    kk = k * kk_ref[...]
    kk = kk / jnp.maximum(jnp.sqrt(_mm_hi(kk * kk, bd_ref[...])), 1e-12)
    r_o[...] = r
    ld_o[...] = -jnp.exp(w)
    k_o[...] = k * (1.0 + (a - 1.0) * ka_ref[...])
    v_o[...] = v
    kk_o[...] = kk
    b_o[...] = kk * a
    g_o[...] = g


def _rwprep(zrw, z_before, mu, w0, wup_pad, a0, aup_pad, gup, k_k, k_a, bd, tm):
    n, zin = zrw.shape
    width = w0.shape[1]
    row = lambda i: (i, 0)
    fix = lambda i: (0, 0)
    prev = lambda i: (jnp.maximum(i * (tm // 8) - 1, 0), 0)
    out = jax.ShapeDtypeStruct((n, width), F32)
    return pl.pallas_call(
        functools.partial(_rwprep_kernel, width=width),
        grid=(n // tm,),
        in_specs=[pl.BlockSpec((tm, zin), row), pl.BlockSpec((8, zin), prev), pl.BlockSpec((1, zin), fix),
                  pl.BlockSpec((1, zin), fix),
                  pl.BlockSpec((1, width), fix), pl.BlockSpec((LANES, width), fix),
                  pl.BlockSpec((1, width), fix), pl.BlockSpec((LANES, width), fix), pl.BlockSpec((LANES, width), fix),
                  pl.BlockSpec((1, width), fix), pl.BlockSpec((1, width), fix), pl.BlockSpec((width, width), fix)],
        out_specs=[pl.BlockSpec((tm, width), row)] * 7,
        out_shape=[out] * 7,
        compiler_params=_params("parallel"),
    )(zrw, zrw, z_before, mu, w0, wup_pad, a0, aup_pad, gup, k_k, k_a, bd)


def _rwcore_kernel(r_ref, ld_ref, k_ref, v_ref, kk_ref, b_ref, g_ref, lng_ref, lnb_ref, rk_ref, s0_ref,
                   o_ref, sout_ref, s_ref, *, units, groups):
    U, C, HD = RW_UNIT, RW_CHUNK, RW_HEAD

    @pl.when(pl.program_id(1) == 0)
    def _():
        s_ref[...] = s0_ref[0]

    ri = lax.broadcasted_iota(jnp.int32, (U, U), 0)
    ci = lax.broadcasted_iota(jnp.int32, (U, U), 1)
    same = (ri // C) == (ci // C)
    tri_s = same & (ci < ri)
    tri_i = same & (ci <= ri)
    eye = (ri == ci).astype(F32)
    cum_w = tri_i.astype(F32)
    head_avg = same.astype(F32) * (1.0 / HD)
    head_sum = same.astype(F32)
    hmask = (ci < HD, ci >= HD)
    cmask = (ri < C, ri >= C)
    zero = jnp.zeros((U, U), F32)
    cells = [(u, q) for u in range(units) for q in range(groups)]

    def blk(ref, cell):
        u, q = cell
        return ref[pl.ds(u * U, U), q * LANES:(q + 1) * LANES]

    def vec(ref, q):
        return ref[:, q * LANES:(q + 1) * LANES]

    cell_v, cell_at, cell_bt, cell_rt, cell_bk, cell_gam = {}, {}, {}, {}, {}, {}
    for cell in cells:
        ld = blk(ld_ref, cell)
        cum = _mm_split(cum_w, ld, NN, 1, 3)
        gam = jnp.exp(cum)
        ginv = jnp.exp(-cum)
        bt = blk(b_ref, cell) * ginv
        cell_v[cell] = blk(v_ref, cell)
        cell_at[cell] = -blk(kk_ref, cell) * jnp.exp(cum - ld)
        cell_bt[cell] = bt
        cell_rt[cell] = blk(r_ref, cell) * gam
        cell_bk[cell] = jnp.concatenate([bt, blk(k_ref, cell) * ginv], axis=0)
        cell_gam[cell] = gam

    chains = [(cell, h) for cell in cells for h in range(2)]
    mab, mak, mrb, mrk = {}, {}, {}, {}
    for ch in chains:
        cell, h = ch
        ar = jnp.concatenate([jnp.where(hmask[h], cell_at[cell], zero), jnp.where(hmask[h], cell_rt[cell], zero)], axis=0)
        m = _mm_nt(ar, cell_bk[cell])
        mab[ch] = jnp.where(tri_s, m[0:U, 0:U], zero)
        mak[ch] = jnp.where(tri_s, m[0:U, U:2 * U], zero)
        mrb[ch] = jnp.where(tri_i, m[U:2 * U, 0:U], zero)
        mrk[ch] = jnp.where(tri_i, m[U:2 * U, U:2 * U], zero)

    tm = {ch: eye + mab[ch] for ch in chains}
    pw = dict(mab)
    for _ in range(int(math.log2(C)) - 1):
        pw = {ch: _mm(pw[ch], pw[ch]) for ch in chains}
        tm = {ch: tm[ch] + _mm(tm[ch], pw[ch]) for ch in chains}
    aph = {ch: _mm(tm[ch], cell_at[ch[0]]) for ch in chains}
    mv = {ch: _mm(mak[ch], cell_v[ch[0]]) for ch in chains}
    uh = {ch: _mm(tm[ch], mv[ch]) for ch in chains}
    rph = {ch: _mm(mrb[ch], aph[ch]) for ch in chains}
    yph = {ch: _mm(mrb[ch], uh[ch]) + _mm(mrk[ch], cell_v[ch[0]]) for ch in chains}

    def both_heads(d, cell):
        return jnp.where(hmask[0], d[(cell, 0)], d[(cell, 1)])

    rp, yp, gs, hs = {}, {}, {}, {}
    for cell in cells:
        ap, uu = both_heads(aph, cell), both_heads(uh, cell)
        rp[cell] = both_heads(rph, cell) + cell_rt[cell]
        yp[cell] = both_heads(yph, cell)
        for c in range(2):
            gl = cell_gam[cell][(c + 1) * C - 1:(c + 1) * C, :]
            apc = jnp.where(cmask[c], ap, zero)
            uvc = jnp.concatenate([jnp.where(cmask[c], uu, zero), jnp.where(cmask[c], cell_v[cell], zero)], axis=0)
            gs[cell, c] = jnp.where(same, eye + _mm_tn(apc, cell_bt[cell]), zero) * gl
            hs[cell, c] = jnp.where(same, _mm_tn(uvc, cell_bk[cell]), zero) * gl

    s = [s_ref[q] for q in range(groups)]
    ys = {}
    for u in range(units):
        for q in range(groups):
            cell = (u, q)
            y0 = _mm_split(rp[cell], s[q], NT, 2, 2)
            s[q] = _mm_split(s[q], gs[cell, 0], NN, 2, 2) + hs[cell, 0]
            y1 = _mm_split(rp[cell], s[q], NT, 2, 2)
            s[q] = _mm_split(s[q], gs[cell, 1], NN, 2, 2) + hs[cell, 1]
            ys[cell] = jnp.where(cmask[0], y0, y1) + yp[cell]
    for q in range(groups):
        s_ref[q] = s[q]
        sout_ref[0, q] = s[q]

    for cell in cells:
        u, q = cell
        y = ys[cell]
        mean = _mm_split(y, head_avg, NN, 2, 1)
        yc = y - mean
        var = _mm_split(yc * yc, head_avg, NN, 2, 1)
        yn = yc * lax.rsqrt(var + RW_GN_EPS) * vec(lng_ref, q) + vec(lnb_ref, q)
        bonus = _mm_split(blk(r_ref, cell) * blk(k_ref, cell) * vec(rk_ref, q), head_sum, NN, 2, 1)
        yn = yn + bonus * cell_v[cell]
        o_ref[pl.ds(u * U, U), q * LANES:(q + 1) * LANES] = (yn * blk(g_ref, cell)).astype(o_ref.dtype)


def _rwcore(r, ld, k, v, kk, b, g, ln_g, ln_b, r_k, state, units, groups):
    n, width = r.shape
    rows = units * RW_UNIT
    lanes = groups * LANES
    blk = pl.BlockSpec((rows, lanes), lambda hp, i: (i, hp))
    vec = pl.BlockSpec((1, lanes), lambda hp, i: (0, hp))
    st = pl.BlockSpec((1, groups, RW_UNIT, RW_UNIT), lambda hp, i: (hp, 0, 0, 0))
    return pl.pallas_call(
        functools.partial(_rwcore_kernel, units=units, groups=groups),
        grid=(width // lanes, n // rows),
        in_specs=[blk] * 7 + [vec] * 3 + [st],
        out_specs=[blk, st],
        out_shape=[jax.ShapeDtypeStruct((n, width), BF16), jax.ShapeDtypeStruct(state.shape, F32)],
        scratch_shapes=[pltpu.VMEM((groups, RW_UNIT, RW_UNIT), F32)],
        compiler_params=_params("parallel", "arbitrary"),
    )(r, ld, k, v, kk, b, g, ln_g, ln_b, r_k, state)


def _outproj_kernel(x_ref, oda_ref, orw_ref, wa_ref, wb_ref, g_ref, h_ref, u_ref):
    h = (x_ref[...] + jnp.dot(oda_ref[...], wa_ref[...], preferred_element_type=F32)
         + jnp.dot(orw_ref[...], wb_ref[...], preferred_element_type=F32))
    h_ref[...] = h
    u_ref[...] = _rms(h, g_ref[...]).astype(BF16)


def _outproj(x2, o_da, o_rw, wa, wb, g, tm):
    n, d = x2.shape
    da = o_da.shape[1]
    rw = o_rw.shape[1]
    row = lambda i: (i, 0)
    fix = lambda i: (0, 0)
    return pl.pallas_call(
        _outproj_kernel,
        grid=(n // tm,),
        in_specs=[pl.BlockSpec((tm, d), row), pl.BlockSpec((tm, da), row), pl.BlockSpec((tm, rw), row),
                  pl.BlockSpec((da, d), fix), pl.BlockSpec((rw, d), fix), pl.BlockSpec((1, d), fix)],
        out_specs=[pl.BlockSpec((tm, d), row), pl.BlockSpec((tm, d), row)],
        out_shape=[jax.ShapeDtypeStruct((n, d), F32), jax.ShapeDtypeStruct((n, d), BF16)],
        compiler_params=_params("parallel"),
    )(x2, o_da, o_rw, wa, wb, g)


def _topk_rows(s, k, payload=None):
    rows = s.shape[0]
    iota = lax.broadcasted_iota(jnp.int32, s.shape, 0).astype(F32)
    vals, sel = [], []
    for _ in range(k):
        m = jnp.max(s, axis=0, keepdims=True)
        am = jnp.min(jnp.where(s == m, iota, float(rows)), axis=0, keepdims=True)
        hit = iota == am
        vals.append(m)
        sel.append(am if payload is None else jnp.sum(jnp.where(hit, payload, 0.0), axis=0, keepdims=True))
        s = jnp.where(hit, -jnp.inf, s)
    return vals, sel


def _stack_rows(rows_list):
    k = len(rows_list)
    iota = lax.broadcasted_iota(jnp.int32, (k, rows_list[0].shape[1]), 0)
    out = jnp.zeros(iota.shape, rows_list[0].dtype)
    for j, r in enumerate(rows_list):
        out = jnp.where(iota == j, r, out)
    return out


def _peertopk_kernel(u_ref, wq_ref, keys_ref, idx_ref, gate_ref):
    u = u_ref[...]
    half = N_KEYS
    idx_rows, gate_rows = [], []
    for h in range(PEER_HEADS):
        tops = []
        for p in range(2):
            hp = h * 2 + p
            q_t = lax.dot_general(wq_ref[hp * half:(hp + 1) * half, :], u, NT, preferred_element_type=F32)
            s_t = jnp.dot(keys_ref[hp], q_t.astype(BF16), preferred_element_type=F32)
            tops.append(_topk_rows(s_t, PEER_TOPK))
        (v1, i1), (v2, i2) = tops
        pairs = [(i, j) for i in range(PEER_TOPK) for j in range(PEER_TOPK) if (i + 1) * (j + 1) <= PEER_TOPK]
        pad = -len(pairs) % 8
        cand = _stack_rows([v1[i] + v2[j] for i, j in pairs] + [jnp.full_like(v1[0], -jnp.inf)] * pad)
        cidx = _stack_rows([i1[i] * float(N_KEYS) + i2[j] for i, j in pairs] + [jnp.zeros_like(i1[0])] * pad)
        best, idx = _topk_rows(cand, PEER_TOPK, payload=cidx)
        e = [jnp.exp(b - best[0]) for b in best]
        den = functools.reduce(lambda a, b: a + b, e)
        idx_rows.append(_stack_rows(idx))
        gate_rows.append(_stack_rows([x / den for x in e]))
    idx_ref[...] = jnp.concatenate(idx_rows, axis=0).T.astype(jnp.int32)
    gate_ref[...] = jnp.concatenate(gate_rows, axis=0).T


def _peertopk(u_bf, wq_t, keys, tt):
    n, d = u_bf.shape
    hk = PEER_HEADS * PEER_TOPK
    row = lambda i: (i, 0)
    return pl.pallas_call(
        _peertopk_kernel,
        grid=(n // tt,),
        in_specs=[pl.BlockSpec((tt, d), row), pl.BlockSpec(wq_t.shape, lambda i: (0, 0)),
                  pl.BlockSpec(keys.shape, lambda i: (0, 0, 0))],
        out_specs=[pl.BlockSpec((tt, hk), row), pl.BlockSpec((tt, hk), row)],
        out_shape=[jax.ShapeDtypeStruct((n, hk), jnp.int32), jax.ShapeDtypeStruct((n, hk), F32)],
        compiler_params=_params("parallel"),
    )(u_bf, wq_t, keys)


SC_CORES = 2
SC_SUBCORES = 16
SC_LANES = 16
HK = PEER_HEADS * PEER_TOPK
DOT_ROWS = 16
SUM_ROWS = 32
SUM_TOKENS = 8
SUM_UNROLL = 2
SUM_BF16_ROWS = 4


def _tree_sum(xs):
    while len(xs) > 1:
        xs = [xs[i] + xs[i + 1] for i in range(0, len(xs) - 1, 2)] + ([xs[-1]] if len(xs) % 2 else [])
    return xs[0]


def _sc_mesh():
    return plsc.VectorSubcoreMesh(core_axis_name="core", subcore_axis_name="subcore")


def _sc_worker():
    return lax.axis_index("core") * SC_SUBCORES + lax.axis_index("subcore")


def _row_dots(tab, idx_flat, xw):
    p = idx_flat.shape[0]
    w = tab.shape[1]
    t_total = p // HK
    workers = SC_CORES * SC_SUBCORES
    tpw = t_total // workers
    g = min(SUM_TOKENS, tpw)
    r = DOT_ROWS
    ns = HK // r
    ln = SC_LANES
    per_row = LANES // ln
    out_rows = HK // per_row
    assert t_total % workers == 0 and tpw % g == 0 and g % 2 == 0 and w % (2 * ln) == 0 and r % per_row == 0
    buf = pltpu.VMEM((r, w), tab.dtype)
    res = pltpu.VMEM((out_rows, LANES), F32)
    sem = pltpu.SemaphoreType.DMA

    @pl.kernel(out_type=jax.ShapeDtypeStruct((t_total * out_rows, LANES), F32), mesh=_sc_mesh(),
               scratch_types=[pltpu.VMEM((g * HK,), jnp.int32), pltpu.VMEM((g, w), tab.dtype), res, res]
               + [buf] * ns + [sem] * (ns + 2),
               compiler_params=pltpu.CompilerParams(needs_layout_passes=False))
    def dots(t_hbm, i_hbm, x_hbm, o_hbm, idx_v, x_v, res0, res1, *scratch):
        bufs, gsem, osem = scratch[0:ns], scratch[ns:2 * ns], scratch[2 * ns:2 * ns + 2]
        ress = (res0, res1)
        wid = _sc_worker()
        zero = jnp.zeros((ln,), F32)

        @pl.loop(0, tpw // g)
        def _(win):
            tok0 = wid * tpw + win * g
            pltpu.sync_copy(i_hbm.at[pl.ds(tok0 * HK, g * HK)], idx_v)
            pltpu.sync_copy(x_hbm.at[pl.ds(tok0, g)], x_v)

            def gather_of(tl, s):
                return pltpu.make_async_copy(t_hbm.at[idx_v.at[pl.ds(tl * HK + s * r, r)]], bufs[s], gsem[s])

            def out_of(tl, par):
                return pltpu.make_async_copy(ress[par], o_hbm.at[pl.ds((tok0 + tl) * out_rows, out_rows)], osem[par])

            for s in range(ns):
                gather_of(0, s).start()

            @pl.loop(0, g // 2)
            def _(tp):
                for par in range(2):
                    tl = tp * 2 + par
                    ob = ress[par]

                    @pl.when(tl >= 2)
                    def _():
                        out_of(tl - 2, par).wait()

                    for s in range(ns):
                        gather_of(tl, s).wait()
                        rows = bufs[s]

                        def fold(c, accs):
                            col = pl.multiple_of(c * 2 * ln, 2 * ln)
                            xa = plsc.bitcast(x_v[tl, pl.ds(col, ln)], BF16)
                            xb = plsc.bitcast(x_v[tl, pl.ds(col + ln, ln)], BF16)
                            out = []
                            for q in range(r):
                                pa = plsc.bitcast(rows[q, pl.ds(col, ln)], BF16) * xa
                                pb = plsc.bitcast(rows[q, pl.ds(col + ln, ln)], BF16) * xb
                                pair = plsc.bitcast(pa + pb, jnp.uint32)
                                lo = plsc.bitcast(lax.shift_left(pair, jnp.uint32(16)), F32)
                                hi = plsc.bitcast(pair & jnp.uint32(0xFFFF0000), F32)
                                out.append(accs[q] + lo + hi)
                            return tuple(out)

                        accs = lax.fori_loop(0, w // (2 * ln), fold, (zero,) * r)
                        for q in range(r):
                            k = s * r + q
                            ob[k // per_row, pl.ds((k % per_row) * ln, ln)] = accs[q]

                        @pl.when(tl + 1 < g)
                        def _():
                            gather_of(tl + 1, s).start()

                    out_of(tl, par).start()

            for par in range(2):
                out_of(g - 2 + par, par).wait()

    return dots(tab, idx_flat, xw)


def _weighted_row_sum(tab, idx_flat, wrep):
    p = idx_flat.shape[0]
    w = tab.shape[1]
    t_total = p // HK
    workers = SC_CORES * SC_SUBCORES
    tpw = t_total // workers
    g = min(SUM_TOKENS, tpw)
    r = SUM_ROWS
    ns = HK // r
    ln = SC_LANES
    per_row = LANES // ln
    assert t_total % workers == 0 and tpw % g == 0 and g % 2 == 0 and w % ln == 0
    buf = pltpu.VMEM((r, w), tab.dtype)
    acc = pltpu.VMEM((2 * w,), F32)
    sem = pltpu.SemaphoreType.DMA

    @pl.kernel(out_type=jax.ShapeDtypeStruct((t_total, 2 * w), F32), mesh=_sc_mesh(),
               scratch_types=[pltpu.VMEM((g * HK,), jnp.int32), pltpu.VMEM((g * HK // per_row, LANES), F32), acc, acc]
               + [buf] * ns + [sem] * (ns + 2),
               compiler_params=pltpu.CompilerParams(needs_layout_passes=False))
    def wsum(t_hbm, i_hbm, w_hbm, o_hbm, idx_v, w_v, acc0, acc1, *scratch):
        bufs, gsem, osem = scratch[0:ns], scratch[ns:2 * ns], scratch[2 * ns:2 * ns + 2]
        accs = (acc0, acc1)
        wid = _sc_worker()
        zero = jnp.zeros((ln,), F32)

        @pl.loop(0, tpw // g)
        def _(win):
            tok0 = wid * tpw + win * g
            pltpu.sync_copy(i_hbm.at[pl.ds(tok0 * HK, g * HK)], idx_v)
            pltpu.sync_copy(w_hbm.at[pl.ds(tok0 * (HK // per_row), g * HK // per_row)], w_v)

            def gather_of(tl, s):
                return pltpu.make_async_copy(t_hbm.at[idx_v.at[pl.ds(tl * HK + s * r, r)]], bufs[s], gsem[s])

            def out_of(tl, par):
                return pltpu.make_async_copy(accs[par], o_hbm.at[tok0 + tl], osem[par])

            for s in range(ns):
                gather_of(0, s).start()

            @pl.loop(0, g // 2)
            def _(tp):
                for par in range(2):
                    tl = tp * 2 + par
                    ob = accs[par]

                    @pl.when(tl >= 2)
                    def _():
                        out_of(tl - 2, par).wait()

                    for c in range(2 * w // ln):
                        ob[pl.ds(c * ln, ln)] = zero
                    for s in range(ns):
                        gather_of(tl, s).wait()
                        wrow = tl * (HK // per_row) + s * (r // per_row)
                        wk = []
                        for q in range(r):
                            bits = plsc.bitcast(w_v[wrow + q // per_row, pl.ds((q % per_row) * ln, ln)], jnp.uint32)
                            wk.append(plsc.bitcast(bits | lax.shift_right_logical(bits, jnp.uint32(16)), BF16))
                        rows = bufs[s]

                        def fold(c, carry):
                            for half in range(SUM_UNROLL):
                                col = pl.multiple_of((c * SUM_UNROLL + half) * ln, ln)
                                los, his = [], []
                                for q in range(0, r, SUM_BF16_ROWS):
                                    prods = [plsc.bitcast(rows[q + i, pl.ds(col, ln)], BF16) * wk[q + i]
                                             for i in range(SUM_BF16_ROWS)]
                                    pair = plsc.bitcast(_tree_sum(prods), jnp.uint32)
                                    los.append(plsc.bitcast(lax.shift_left(pair, jnp.uint32(16)), F32))
                                    his.append(plsc.bitcast(pair & jnp.uint32(0xFFFF0000), F32))
                                plsc.addupdate(ob.at[pl.ds(col, ln)], _tree_sum(los))
                                plsc.addupdate(ob.at[pl.ds(w + col, ln)], _tree_sum(his))
                            return carry

                        lax.fori_loop(0, w // (ln * SUM_UNROLL), fold, 0)

                        @pl.when(tl + 1 < g)
                        def _():
                            gather_of(tl + 1, s).start()

                    out_of(tl, par).start()

            for par in range(2):
                out_of(g - 2 + par, par).wait()

    return wsum(tab, idx_flat, wrep)


def _peerw_kernel(part_ref, gate_ref, fold_ref, rep_ref, o_ref):
    hid = _mm_split(part_ref[...], fold_ref[...], NN, 3, 1)
    w = gate_ref[...] * (0.5 * hid * (1.0 + lax.erf(hid * (2.0 ** -0.5))))
    o_ref[...] = jnp.dot(w.astype(BF16), rep_ref[...], preferred_element_type=F32)


def _peerw(part, gate, tt):
    n = gate.shape[0]
    wide = HK * SC_LANES
    lane = jnp.arange(wide)
    fold = (lane[:, None] // SC_LANES == jnp.arange(HK)[None, :]).astype(BF16)
    row = lambda i: (i, 0)
    fix = lambda i: (0, 0)
    return pl.pallas_call(
        _peerw_kernel,
        grid=(n // tt,),
        in_specs=[pl.BlockSpec((tt, wide), row), pl.BlockSpec((tt, HK), row), pl.BlockSpec((wide, HK), fix),
                  pl.BlockSpec((HK, wide), fix)],
        out_specs=pl.BlockSpec((tt, wide), row),
        out_shape=jax.ShapeDtypeStruct((n, wide), F32),
        compiler_params=_params("parallel"),
    )(part.reshape(n, wide), gate, fold, fold.T)


def _ple_kernel(h_ref, f_ref, p_ref, g_ref, wg_ref, wp_ref, gf_ref, o_ref):
    h = h_ref[...] + f_ref[...]
    gate = jax.nn.sigmoid(jnp.dot(_rms(h, g_ref[...]).astype(BF16), wg_ref[...], preferred_element_type=F32))
    pp = jnp.dot(p_ref[...].astype(BF16), wp_ref[...], preferred_element_type=F32)
    o_ref[...] = _rms(h + gate * pp, gf_ref[...])


def _ple(h1, ffn, p2, g, wg, wp, gf, tm):
    n, d = h1.shape
    pd = p2.shape[1]
    row = lambda i: (i, 0)
    fix = lambda i: (0, 0)
    return pl.pallas_call(
        _ple_kernel,
        grid=(n // tm,),
        in_specs=[pl.BlockSpec((tm, d), row), pl.BlockSpec((tm, d), row), pl.BlockSpec((tm, pd), row),
                  pl.BlockSpec((1, d), fix), pl.BlockSpec((d, d), fix), pl.BlockSpec((pd, d), fix),
                  pl.BlockSpec((1, d), fix)],
        out_specs=pl.BlockSpec((tm, d), row),
        out_shape=jax.ShapeDtypeStruct((n, d), F32),
        compiler_params=_params("parallel"),
    )(h1, ffn, p2, g, wg, wp, gf)


def _rope_tables(positions):
    half = ROT_DIM // 2
    inv_freq = ROPE_THETA ** (-jnp.arange(half, dtype=F32) * 2.0 / ROT_DIM)
    ang = positions.astype(F32).reshape(-1, 1) * inv_freq
    d = jnp.arange(LANES) % DA_HEAD_DIM
    cos = jnp.tile(jnp.cos(ang), (1, LANES // half))
    sin = jnp.tile(jnp.sin(ang), (1, LANES // half))
    c = jnp.where(d < ROT_DIM, cos, 1.0)
    s1 = jnp.where(d < half, -sin, 0.0)
    s2 = jnp.where((d >= half) & (d < ROT_DIM), sin, 0.0)
    return c, s1, s2


def _pack_rows(tab):
    d = tab.shape[1]
    bits = lax.bitcast_convert_type(tab.astype(F32), jnp.uint32)
    rne = bits + jnp.uint32(0x7FFF) + ((bits >> 16) & jnp.uint32(1))
    return (rne[:, :d // 2] >> 16) | (rne[:, d // 2:] & jnp.uint32(0xFFFF0000))


def _block_diag_ones(width, head):
    i = jnp.arange(width)
    return (i[:, None] // head == i[None, :] // head).astype(F32)


def _tiles(seq):
    pieces = 4 if seq % 8192 == 0 else 1
    rows = seq // pieces
    return dict(pieces=pieces, rows=rows, tm=min(256, rows), tq=min(512, rows), units=min(4, rows // RW_UNIT), groups=2,
                tt_topk=min(256, rows), tt_mix=min(256, rows))


def kernel(x, p, positions, norm_mix_g, w_in, lam_q1, lam_k1, lam_q2, lam_k2, da_subln_g, rw_mu, rw_w0, rw_w_up, rw_a0, rw_a_up, rw_g_up, rw_k_k, rw_k_a, rw_r_k, rw_ln_g, rw_ln_b, w_out, norm_ffn_g, peer_w_q, peer_sub_keys, peer_u, peer_v, norm_ple_g, ple_gate_w, ple_proj_w, norm_final_g):
    batch, seq, d = x.shape
    t = _tiles(seq)
    row = lambda a: a.reshape(1, -1)
    f32 = F32

    w_in_bf = w_in[0].astype(BF16)
    lam = (jnp.exp(jnp.sum(lam_q1[0].astype(f32) * lam_k1[0].astype(f32)))
           - jnp.exp(jnp.sum(lam_q2[0].astype(f32) * lam_k2[0].astype(f32))) + LAM_INIT).reshape(1, 1)
    width = rw_w0.shape[1]
    wup_pad = jnp.concatenate([rw_w_up[0], jnp.zeros((LANES - rw_w_up.shape[1], width), f32)], axis=0)
    aup_pad = jnp.concatenate([jnp.zeros((LANES - rw_a_up.shape[1], width), f32), rw_a_up[0]], axis=0)
    head_ones = _block_diag_ones(width, RW_HEAD)
    w_out_bf = w_out[0].astype(BF16)
    da_w = DA_HEADS * 2 * DA_HEAD_DIM
    keys = peer_sub_keys[0].reshape(PEER_HEADS * 2, N_KEYS, -1).astype(BF16)
    wq_t = peer_w_q[0].T.astype(BF16)
    u_tab = _pack_rows(peer_u[0])
    v_tab = None
    wg_bf = ple_gate_w[0].astype(BF16)
    wp_bf = ple_proj_w[0].astype(BF16)

    rope = [a.reshape(batch, seq, LANES) for a in _rope_tables(positions)]
    rows = t["rows"]
    pieces = [(b, h) for b in range(batch) for h in range(t["pieces"])]
    rows_of = lambda a, piece: a[piece[0], piece[1] * rows:(piece[1] + 1) * rows]
    tie = lax.optimization_barrier

    def mixers(piece, xs, carry):
        rc, rs1, rs2 = (rows_of(a, piece) for a in rope)
        qkv, zrw = _inproj(xs, row(norm_mix_g[0]), w_in_bf, rc, rs1, rs2, t["tm"])
        if carry is None:
            carry = dict(qkv=qkv[:0], z_last=jnp.zeros((1, zrw.shape[1]), f32),
                         state=jnp.zeros((width // (t["groups"] * LANES), t["groups"], RW_UNIT, RW_UNIT), f32))
        qkv = jnp.concatenate([carry["qkv"], qkv], axis=0)
        o_da = _attention(qkv, lam, row(da_subln_g[0]), t["tq"], rows)
        rw = _rwprep(zrw, carry["z_last"], row(rw_mu[0]), row(rw_w0[0]), wup_pad, row(rw_a0[0]), aup_pad, rw_g_up[0],
                     row(rw_k_k[0]), row(rw_k_a[0]), head_ones, t["tm"])
        o_rw, state = _rwcore(*rw, row(rw_ln_g[0]), row(rw_ln_b[0]), row(rw_r_k[0]), carry["state"], t["units"], t["groups"])
        h1, u2 = _outproj(xs, o_da, o_rw, w_out_bf[:da_w], w_out_bf[da_w:], row(norm_ffn_g[0]), t["tm"])
        idx, gate = _peertopk(u2, wq_t, keys, t["tt_topk"])
        idx = idx.reshape(-1)
        part = _row_dots(u_tab, idx, _pack_rows(u2))
        return dict(h1=h1, gate=gate, idx=idx, part=part), dict(qkv=qkv, z_last=zrw[-1:], state=state)

    def out(s, ffn, piece):
        return _ple(s["h1"], ffn, rows_of(p[0], piece), row(norm_ple_g[0]), wg_bf, wp_bf, row(norm_final_g), t["tm"])

    outs = []
    prev, older, older_ffn, carry = None, None, None, None
    xs = rows_of(x, pieces[0])
    for j in range(len(pieces) + 1):
        cur = None
        if j < len(pieces):
            cur, carry = mixers(pieces[j], xs, carry if pieces[j][1] > 0 else None)
        if prev is not None:
            tied = [prev["part"]] + ([cur["idx"]] if cur is not None else []) + ([older_ffn] if older is not None else [])
            tied = tie(tuple(tied))
            if older is not None:
                outs.append(out(older, tied[-1], pieces[j - 2]))
            if v_tab is None:
                v_tab = _pack_rows(tie((peer_v[0], (cur or prev)["idx"]))[0])
            wrep = _peerw(tied[0], prev["gate"], t["tt_mix"])
            if j + 1 < len(pieces):
                wrep, xs = tie((wrep, rows_of(x, pieces[j + 1])))
            older, older_ffn = prev, _weighted_row_sum(v_tab, prev["idx"], wrep.reshape(-1, LANES))
        elif j + 1 < len(pieces):
            xs = rows_of(x, pieces[j + 1])
        prev = cur
    outs.append(out(older, older_ffn, pieces[-1]))
    return jnp.concatenate(outs, axis=0).reshape(batch, seq, d)
```

```python
import functools
import math

import jax
import jax.numpy as jnp
from jax import lax
from jax.experimental import pallas as pl
from jax.experimental.pallas import tpu as pltpu
from jax.experimental.pallas import tpu_sc as plsc

F32 = jnp.float32
BF16 = jnp.bfloat16

NORM_EPS = 1e-6
DA_HEADS = 4
DA_HEAD_DIM = 64
ROPE_THETA = 500000.0
ROT_DIM = DA_HEAD_DIM // 4
RW_HEAD = 64
RW_GN_EPS = 64e-5
PEER_HEADS = 8
N_KEYS = 128
PEER_TOPK = 16
LAM_INIT = 0.8 - 0.6 * math.exp(-0.3 * 0)

LANES = 128
VMEM_LIMIT = 56 * 1024 * 1024
RW_CHUNK = 64
RW_UNIT = 2 * RW_CHUNK

NT = (((1,), (1,)), ((), ()))
TN = (((0,), (0,)), ((), ()))
HI = lax.Precision.HIGHEST


def _mm(a, b):
    return jnp.dot(a.astype(BF16), b.astype(BF16), preferred_element_type=F32)


def _mm_nt(a, b):
    return lax.dot_general(a.astype(BF16), b.astype(BF16), NT, preferred_element_type=F32)


def _mm_tn(a, b):
    return lax.dot_general(a.astype(BF16), b.astype(BF16), TN, preferred_element_type=F32)


def _mm_hi(a, b):
    return jnp.dot(a, b, precision=HI, preferred_element_type=F32)


def _pieces(a, n):
    out = []
    for _ in range(n):
        piece = a.astype(BF16)
        out.append(piece)
        a = a - piece.astype(F32)
    return out


def _mm_split(a, b, dims, a_pieces, b_pieces):
    ap, bp = _pieces(a, a_pieces), _pieces(b, b_pieces)
    terms = [lax.dot_general(x, y, dims, preferred_element_type=F32)
             for i, x in enumerate(ap) for j, y in enumerate(bp) if i + j < max(a_pieces, b_pieces)]
    return functools.reduce(lambda u, v: u + v, terms)


NN = (((1,), (0,)), ((), ()))


def _params(*sem):
    return pltpu.CompilerParams(dimension_semantics=sem, vmem_limit_bytes=VMEM_LIMIT)


def _rms(x, g):
    return x * lax.rsqrt(jnp.mean(x * x, axis=-1, keepdims=True) + NORM_EPS) * g


def _inproj_kernel(x_ref, g_ref, w_ref, c_ref, s1_ref, s2_ref, qkv_ref, zrw_ref, *, n_qk, n_da):
    u = _rms(x_ref[...], g_ref[...]).astype(BF16)
    z = jnp.dot(u, w_ref[...], preferred_element_type=F32)
    c, s1, s2 = c_ref[...], s1_ref[...], s2_ref[...]
    half = ROT_DIM // 2
    for blk in range(n_da // LANES):
        t = z[:, blk * LANES:(blk + 1) * LANES]
        if blk < 2 * n_qk // LANES:
            t = t * c + pltpu.roll(t, LANES - half, 1) * s1 + pltpu.roll(t, half, 1) * s2
        if blk < n_qk // LANES:
            t = t * (DA_HEAD_DIM ** -0.5 * math.log2(math.e))
        qkv_ref[:, blk * LANES:(blk + 1) * LANES] = t.astype(BF16)
    zrw_ref[...] = z[:, n_da:]


def _inproj(x2, g, w_in_bf, rc, rs1, rs2, tm):
    n, d = x2.shape
    n_in = w_in_bf.shape[1]
    n_qk = DA_HEADS * 2 * DA_HEAD_DIM
    n_da = 3 * n_qk
    row = lambda i: (i, 0)
    fix = lambda i: (0, 0)
    return pl.pallas_call(
        functools.partial(_inproj_kernel, n_qk=n_qk, n_da=n_da),
        grid=(n // tm,),
        in_specs=[pl.BlockSpec((tm, d), row), pl.BlockSpec((1, d), fix), pl.BlockSpec((d, n_in), fix),
                  pl.BlockSpec((tm, LANES), row), pl.BlockSpec((tm, LANES), row), pl.BlockSpec((tm, LANES), row)],
        out_specs=[pl.BlockSpec((tm, n_da), row), pl.BlockSpec((tm, n_in - n_da), row)],
        out_shape=[jax.ShapeDtypeStruct((n, n_da), BF16), jax.ShapeDtypeStruct((n, n_in - n_da), F32)],
        compiler_params=_params("parallel"),
    )(x2, g, w_in_bf, rc, rs1, rs2)


def _attn_kernel(lam_ref, q_ref, k_ref, v_ref, sg_ref, o_ref, m_ref, acc_ref, *, tq, q_tile0):
    i = pl.program_id(1) + q_tile0
    q = q_ref[...]
    lane = lax.broadcasted_iota(jnp.int32, q.shape, 1)
    zero = jnp.zeros_like(q)
    qs = (jnp.where(lane < DA_HEAD_DIM, q, zero), jnp.where(lane >= DA_HEAD_DIM, q, zero))
    m_ref[...] = jnp.full(m_ref.shape, -jnp.inf, F32)
    acc_ref[...] = jnp.zeros(acc_ref.shape, F32)
    ones = jnp.ones((tq, LANES), BF16)

    def block(j, masked):
        kj = k_ref[pl.ds(pl.multiple_of(j * tq, tq), tq), :]
        vj = jnp.concatenate([v_ref[pl.ds(pl.multiple_of(j * tq, tq), tq), :], ones], axis=1)
        for c in range(2):
            s = lax.dot_general(qs[c], kj, NT, preferred_element_type=F32)
            if masked:
                r_id = lax.broadcasted_iota(jnp.int32, s.shape, 0)
                c_id = lax.broadcasted_iota(jnp.int32, s.shape, 1)
                s = jnp.where(c_id <= r_id, s, -jnp.inf)
            m_old = m_ref[c]
            m_new = jnp.maximum(m_old, jnp.max(s, axis=-1, keepdims=True))
            alpha = jnp.exp2(m_old - m_new)
            p = jnp.exp2(s - jnp.tile(m_new, (1, tq // LANES)))
            pv = jnp.dot(p.astype(BF16), vj, preferred_element_type=F32)
            acc_ref[c] = jnp.tile(alpha, (1, 2)) * acc_ref[c] + pv
            m_ref[c] = m_new

    def body(j, carry):
        block(j, False)
        return carry

    lax.fori_loop(0, i, body, 0)
    block(i, True)
    lam = lam_ref[0, 0]
    a0, a1 = acc_ref[0], acc_ref[1]
    o = a0[:, :LANES] / a0[:, LANES:] - lam * (a1[:, :LANES] / a1[:, LANES:])
    o = o * lax.rsqrt(jnp.mean(o * o, axis=-1, keepdims=True) + NORM_EPS) * sg_ref[...] * (1.0 - LAM_INIT)
    o_ref[...] = o.astype(o_ref.dtype)


def _attention(qkv, lam, subln_g, tq, q_rows):
    kv_rows = qkv.shape[0]
    nq = q_rows // tq
    q_tile0 = (kv_rows - q_rows) // tq
    h = DA_HEADS
    return pl.pallas_call(
        functools.partial(_attn_kernel, tq=tq, q_tile0=q_tile0),
        grid=(h, nq),
        in_specs=[pl.BlockSpec(memory_space=pltpu.SMEM),
                  pl.BlockSpec((tq, LANES), lambda hh, i: (q_tile0 + i, hh)),
                  pl.BlockSpec((kv_rows, LANES), lambda hh, i: (0, h + hh)),
                  pl.BlockSpec((kv_rows, LANES), lambda hh, i: (0, 2 * h + hh)),
                  pl.BlockSpec((1, LANES), lambda hh, i: (0, 0))],
        out_specs=pl.BlockSpec((tq, LANES), lambda hh, i: (i, hh)),
        out_shape=jax.ShapeDtypeStruct((q_rows, h * LANES), BF16),
        scratch_shapes=[pltpu.VMEM((2, tq, LANES), F32), pltpu.VMEM((2, tq, 2 * LANES), F32)],
        compiler_params=_params("parallel", "arbitrary"),
    )(lam, qkv, qkv, qkv, subln_g)


def _rwprep_kernel(z_ref, zp_ref, z0_ref, mu_ref, w0_ref, wup_ref, a0_ref, aup_ref, gup_ref, kk_ref, ka_ref, bd_ref,
                   r_o, ld_o, k_o, v_o, kk_o, b_o, g_o, *, width):
    i = pl.program_id(0)
    z = z_ref[...]
    first = jnp.where(i == 0, z0_ref[...], zp_ref[7:8, :])
    row = lax.broadcasted_iota(jnp.int32, z.shape, 0)
    prev = jnp.where(row == 0, first, pltpu.roll(z, 1, 0))
    zs = z + (prev - z) * mu_ref[...]
    r = zs[:, 0:width]
    k = zs[:, width:2 * width]
    v = zs[:, 2 * width:3 * width]
    xwa = zs[:, 3 * width:3 * width + LANES]
    xg = zs[:, 3 * width + LANES:3 * width + 2 * LANES]
    w = -jax.nn.softplus(-(w0_ref[...] + _mm_hi(jnp.tanh(xwa), wup_ref[...]))) - 0.5
    a = jax.nn.sigmoid(a0_ref[...] + _mm_hi(xwa, aup_ref[...]))
    g = _mm_hi(jax.nn.sigmoid(xg), gup_ref[...])
    kk = k * kk_ref[...]
    kk = kk / jnp.maximum(jnp.sqrt(_mm_hi(kk * kk, bd_ref[...])), 1e-12)
    r_o[...] = r
    ld_o[...] = -jnp.exp(w)
    k_o[...] = k * (1.0 + (a - 1.0) * ka_ref[...])
    v_o[...] = v
    kk_o[...] = kk
    b_o[...] = kk * a
    g_o[...] = g


def _rwprep(zrw, z_before, mu, w0, wup_pad, a0, aup_pad, gup, k_k, k_a, bd, tm):
    n, zin = zrw.shape
    width = w0.shape[1]
    row = lambda i: (i, 0)
    fix = lambda i: (0, 0)
    prev = lambda i: (jnp.maximum(i * (tm // 8) - 1, 0), 0)
    out = jax.ShapeDtypeStruct((n, width), F32)
    return pl.pallas_call(
        functools.partial(_rwprep_kernel, width=width),
        grid=(n // tm,),
        in_specs=[pl.BlockSpec((tm, zin), row), pl.BlockSpec((8, zin), prev), pl.BlockSpec((1, zin), fix),
                  pl.BlockSpec((1, zin), fix),
                  pl.BlockSpec((1, width), fix), pl.BlockSpec((LANES, width), fix),
                  pl.BlockSpec((1, width), fix), pl.BlockSpec((LANES, width), fix), pl.BlockSpec((LANES, width), fix),
                  pl.BlockSpec((1, width), fix), pl.BlockSpec((1, width), fix), pl.BlockSpec((width, width), fix)],
        out_specs=[pl.BlockSpec((tm, width), row)] * 7,
        out_shape=[out] * 7,
        compiler_params=_params("parallel"),
    )(zrw, zrw, z_before, mu, w0, wup_pad, a0, aup_pad, gup, k_k, k_a, bd)


def _rwcore_kernel(r_ref, ld_ref, k_ref, v_ref, kk_ref, b_ref, g_ref, lng_ref, lnb_ref, rk_ref, s0_ref,
                   o_ref, sout_ref, s_ref, *, units, groups):
    U, C, HD = RW_UNIT, RW_CHUNK, RW_HEAD

    @pl.when(pl.program_id(1) == 0)
    def _():
        s_ref[...] = s0_ref[0]

    ri = lax.broadcasted_iota(jnp.int32, (U, U), 0)
    ci = lax.broadcasted_iota(jnp.int32, (U, U), 1)
    same = (ri // C) == (ci // C)
    tri_s = same & (ci < ri)
    tri_i = same & (ci <= ri)
    eye = (ri == ci).astype(F32)
    cum_w = tri_i.astype(F32)
    head_avg = same.astype(F32) * (1.0 / HD)
    head_sum = same.astype(F32)
    hmask = (ci < HD, ci >= HD)
    cmask = (ri < C, ri >= C)
    zero = jnp.zeros((U, U), F32)
    cells = [(u, q) for u in range(units) for q in range(groups)]

    def blk(ref, cell):
        u, q = cell
        return ref[pl.ds(u * U, U), q * LANES:(q + 1) * LANES]

    def vec(ref, q):
        return ref[:, q * LANES:(q + 1) * LANES]

    cell_v, cell_at, cell_bt, cell_rt, cell_bk, cell_gam = {}, {}, {}, {}, {}, {}
    for cell in cells:
        ld = blk(ld_ref, cell)
        cum = _mm_split(cum_w, ld, NN, 1, 3)
        gam = jnp.exp(cum)
        ginv = jnp.exp(-cum)
        bt = blk(b_ref, cell) * ginv
        cell_v[cell] = blk(v_ref, cell)
        cell_at[cell] = -blk(kk_ref, cell) * jnp.exp(cum - ld)
        cell_bt[cell] = bt
        cell_rt[cell] = blk(r_ref, cell) * gam
        cell_bk[cell] = jnp.concatenate([bt, blk(k_ref, cell) * ginv], axis=0)
        cell_gam[cell] = gam

    chains = [(cell, h) for cell in cells for h in range(2)]
    mab, mak, mrb, mrk = {}, {}, {}, {}
    for ch in chains:
        cell, h = ch
        ar = jnp.concatenate([jnp.where(hmask[h], cell_at[cell], zero), jnp.where(hmask[h], cell_rt[cell], zero)], axis=0)
        m = _mm_nt(ar, cell_bk[cell])
        mab[ch] = jnp.where(tri_s, m[0:U, 0:U], zero)
        mak[ch] = jnp.where(tri_s, m[0:U, U:2 * U], zero)
        mrb[ch] = jnp.where(tri_i, m[U:2 * U, 0:U], zero)
        mrk[ch] = jnp.where(tri_i, m[U:2 * U, U:2 * U], zero)

    tm = {ch: eye + mab[ch] for ch in chains}
    pw = dict(mab)
    for _ in range(int(math.log2(C)) - 1):
        pw = {ch: _mm(pw[ch], pw[ch]) for ch in chains}
        tm = {ch: tm[ch] + _mm(tm[ch], pw[ch]) for ch in chains}
    aph = {ch: _mm(tm[ch], cell_at[ch[0]]) for ch in chains}
    mv = {ch: _mm(mak[ch], cell_v[ch[0]]) for ch in chains}
    uh = {ch: _mm(tm[ch], mv[ch]) for ch in chains}
    rph = {ch: _mm(mrb[ch], aph[ch]) for ch in chains}
    yph = {ch: _mm(mrb[ch], uh[ch]) + _mm(mrk[ch], cell_v[ch[0]]) for ch in chains}

    def both_heads(d, cell):
        return jnp.where(hmask[0], d[(cell, 0)], d[(cell, 1)])

    rp, yp, gs, hs = {}, {}, {}, {}
    for cell in cells:
        ap, uu = both_heads(aph, cell), both_heads(uh, cell)
        rp[cell] = both_heads(rph, cell) + cell_rt[cell]
        yp[cell] = both_heads(yph, cell)
        for c in range(2):
            gl = cell_gam[cell][(c + 1) * C - 1:(c + 1) * C, :]
            apc = jnp.where(cmask[c], ap, zero)
            uvc = jnp.concatenate([jnp.where(cmask[c], uu, zero), jnp.where(cmask[c], cell_v[cell], zero)], axis=0)
            gs[cell, c] = jnp.where(same, eye + _mm_tn(apc, cell_bt[cell]), zero) * gl
            hs[cell, c] = jnp.where(same, _mm_tn(uvc, cell_bk[cell]), zero) * gl

    s = [s_ref[q] for q in range(groups)]
    ys = {}
    for u in range(units):
        for q in range(groups):
            cell = (u, q)
            y0 = _mm_split(rp[cell], s[q], NT, 2, 2)
            s[q] = _mm_split(s[q], gs[cell, 0], NN, 2, 2) + hs[cell, 0]
            y1 = _mm_split(rp[cell], s[q], NT, 2, 2)
            s[q] = _mm_split(s[q], gs[cell, 1], NN, 2, 2) + hs[cell, 1]
            ys[cell] = jnp.where(cmask[0], y0, y1) + yp[cell]
    for q in range(groups):
        s_ref[q] = s[q]
        sout_ref[0, q] = s[q]

    for cell in cells:
        u, q = cell
        y = ys[cell]
        mean = _mm_split(y, head_avg, NN, 2, 1)
        yc = y - mean
        var = _mm_split(yc * yc, head_avg, NN, 2, 1)
        yn = yc * lax.rsqrt(var + RW_GN_EPS) * vec(lng_ref, q) + vec(lnb_ref, q)
        bonus = _mm_split(blk(r_ref, cell) * blk(k_ref, cell) * vec(rk_ref, q), head_sum, NN, 2, 1)
        yn = yn + bonus * cell_v[cell]
        o_ref[pl.ds(u * U, U), q * LANES:(q + 1) * LANES] = (yn * blk(g_ref, cell)).astype(o_ref.dtype)


def _rwcore(r, ld, k, v, kk, b, g, ln_g, ln_b, r_k, state, units, groups):
    n, width = r.shape
    rows = units * RW_UNIT
    lanes = groups * LANES
    blk = pl.BlockSpec((rows, lanes), lambda hp, i: (i, hp))
    vec = pl.BlockSpec((1, lanes), lambda hp, i: (0, hp))
    st = pl.BlockSpec((1, groups, RW_UNIT, RW_UNIT), lambda hp, i: (hp, 0, 0, 0))
    return pl.pallas_call(
        functools.partial(_rwcore_kernel, units=units, groups=groups),
        grid=(width // lanes, n // rows),
        in_specs=[blk] * 7 + [vec] * 3 + [st],
        out_specs=[blk, st],
        out_shape=[jax.ShapeDtypeStruct((n, width), BF16), jax.ShapeDtypeStruct(state.shape, F32)],
        scratch_shapes=[pltpu.VMEM((groups, RW_UNIT, RW_UNIT), F32)],
        compiler_params=_params("parallel", "arbitrary"),
    )(r, ld, k, v, kk, b, g, ln_g, ln_b, r_k, state)


def _outproj_kernel(x_ref, oda_ref, orw_ref, wa_ref, wb_ref, g_ref, h_ref, u_ref):
    h = (x_ref[...] + jnp.dot(oda_ref[...], wa_ref[...], preferred_element_type=F32)
         + jnp.dot(orw_ref[...], wb_ref[...], preferred_element_type=F32))
    h_ref[...] = h
    u_ref[...] = _rms(h, g_ref[...]).astype(BF16)


def _outproj(x2, o_da, o_rw, wa, wb, g, tm):
    n, d = x2.shape
    da = o_da.shape[1]
    rw = o_rw.shape[1]
    row = lambda i: (i, 0)
    fix = lambda i: (0, 0)
    return pl.pallas_call(
        _outproj_kernel,
        grid=(n // tm,),
        in_specs=[pl.BlockSpec((tm, d), row), pl.BlockSpec((tm, da), row), pl.BlockSpec((tm, rw), row),
                  pl.BlockSpec((da, d), fix), pl.BlockSpec((rw, d), fix), pl.BlockSpec((1, d), fix)],
        out_specs=[pl.BlockSpec((tm, d), row), pl.BlockSpec((tm, d), row)],
        out_shape=[jax.ShapeDtypeStruct((n, d), F32), jax.ShapeDtypeStruct((n, d), BF16)],
        compiler_params=_params("parallel"),
    )(x2, o_da, o_rw, wa, wb, g)


def _topk_rows(s, k, payload=None):
    rows = s.shape[0]
    iota = lax.broadcasted_iota(jnp.int32, s.shape, 0).astype(F32)
    vals, sel = [], []
    for _ in range(k):
        m = jnp.max(s, axis=0, keepdims=True)
        am = jnp.min(jnp.where(s == m, iota, float(rows)), axis=0, keepdims=True)
        hit = iota == am
        vals.append(m)
        sel.append(am if payload is None else jnp.sum(jnp.where(hit, payload, 0.0), axis=0, keepdims=True))
        s = jnp.where(hit, -jnp.inf, s)
    return vals, sel


def _stack_rows(rows_list):
    k = len(rows_list)
    iota = lax.broadcasted_iota(jnp.int32, (k, rows_list[0].shape[1]), 0)
    out = jnp.zeros(iota.shape, rows_list[0].dtype)
    for j, r in enumerate(rows_list):
        out = jnp.where(iota == j, r, out)
    return out


def _peertopk_kernel(u_ref, wq_ref, keys_ref, idx_ref, gate_ref):
    u = u_ref[...]
    half = N_KEYS
    idx_rows, gate_rows = [], []
    for h in range(PEER_HEADS):
        tops = []
        for p in range(2):
            hp = h * 2 + p
            q_t = lax.dot_general(wq_ref[hp * half:(hp + 1) * half, :], u, NT, preferred_element_type=F32)
            s_t = jnp.dot(keys_ref[hp], q_t.astype(BF16), preferred_element_type=F32)
            tops.append(_topk_rows(s_t, PEER_TOPK))
        (v1, i1), (v2, i2) = tops
        pairs = [(i, j) for i in range(PEER_TOPK) for j in range(PEER_TOPK) if (i + 1) * (j + 1) <= PEER_TOPK]
        pad = -len(pairs) % 8
        cand = _stack_rows([v1[i] + v2[j] for i, j in pairs] + [jnp.full_like(v1[0], -jnp.inf)] * pad)
        cidx = _stack_rows([i1[i] * float(N_KEYS) + i2[j] for i, j in pairs] + [jnp.zeros_like(i1[0])] * pad)
        best, idx = _topk_rows(cand, PEER_TOPK, payload=cidx)
        e = [jnp.exp(b - best[0]) for b in best]
        den = functools.reduce(lambda a, b: a + b, e)
        idx_rows.append(_stack_rows(idx))
        gate_rows.append(_stack_rows([x / den for x in e]))
    idx_ref[...] = jnp.concatenate(idx_rows, axis=0).T.astype(jnp.int32)
    gate_ref[...] = jnp.concatenate(gate_rows, axis=0).T


def _peertopk(u_bf, wq_t, keys, tt):
    n, d = u_bf.shape
    hk = PEER_HEADS * PEER_TOPK
    row = lambda i: (i, 0)
    return pl.pallas_call(
        _peertopk_kernel,
        grid=(n // tt,),
        in_specs=[pl.BlockSpec((tt, d), row), pl.BlockSpec(wq_t.shape, lambda i: (0, 0)),
                  pl.BlockSpec(keys.shape, lambda i: (0, 0, 0))],
        out_specs=[pl.BlockSpec((tt, hk), row), pl.BlockSpec((tt, hk), row)],
        out_shape=[jax.ShapeDtypeStruct((n, hk), jnp.int32), jax.ShapeDtypeStruct((n, hk), F32)],
        compiler_params=_params("parallel"),
    )(u_bf, wq_t, keys)


SC_CORES = 2
SC_SUBCORES = 16
SC_LANES = 16
HK = PEER_HEADS * PEER_TOPK
DOT_ROWS = 16
SUM_ROWS = 32
SUM_TOKENS = 8
SUM_UNROLL = 2
SUM_BF16_ROWS = 4


def _tree_sum(xs):
    while len(xs) > 1:
        xs = [xs[i] + xs[i + 1] for i in range(0, len(xs) - 1, 2)] + ([xs[-1]] if len(xs) % 2 else [])
    return xs[0]


def _sc_mesh():
    return plsc.VectorSubcoreMesh(core_axis_name="core", subcore_axis_name="subcore")


def _sc_worker():
    return lax.axis_index("core") * SC_SUBCORES + lax.axis_index("subcore")


def _row_dots(tab, idx_flat, xw):
    p = idx_flat.shape[0]
    w = tab.shape[1]
    t_total = p // HK
    workers = SC_CORES * SC_SUBCORES
    tpw = t_total // workers
    g = min(SUM_TOKENS, tpw)
    r = DOT_ROWS
    ns = HK // r
    ln = SC_LANES
    per_row = LANES // ln
    out_rows = HK // per_row
    assert t_total % workers == 0 and tpw % g == 0 and g % 2 == 0 and w % (2 * ln) == 0 and r % per_row == 0
    buf = pltpu.VMEM((r, w), tab.dtype)
    res = pltpu.VMEM((out_rows, LANES), F32)
    sem = pltpu.SemaphoreType.DMA

    @pl.kernel(out_type=jax.ShapeDtypeStruct((t_total * out_rows, LANES), F32), mesh=_sc_mesh(),
               scratch_types=[pltpu.VMEM((g * HK,), jnp.int32), pltpu.VMEM((g, w), tab.dtype), res, res]
               + [buf] * ns + [sem] * (ns + 2),
               compiler_params=pltpu.CompilerParams(needs_layout_passes=False))
    def dots(t_hbm, i_hbm, x_hbm, o_hbm, idx_v, x_v, res0, res1, *scratch):
        bufs, gsem, osem = scratch[0:ns], scratch[ns:2 * ns], scratch[2 * ns:2 * ns + 2]
        ress = (res0, res1)
        wid = _sc_worker()
        zero = jnp.zeros((ln,), F32)

        @pl.loop(0, tpw // g)
        def _(win):
            tok0 = wid * tpw + win * g
            pltpu.sync_copy(i_hbm.at[pl.ds(tok0 * HK, g * HK)], idx_v)
            pltpu.sync_copy(x_hbm.at[pl.ds(tok0, g)], x_v)

            def gather_of(tl, s):
                return pltpu.make_async_copy(t_hbm.at[idx_v.at[pl.ds(tl * HK + s * r, r)]], bufs[s], gsem[s])

            def out_of(tl, par):
                return pltpu.make_async_copy(ress[par], o_hbm.at[pl.ds((tok0 + tl) * out_rows, out_rows)], osem[par])

            for s in range(ns):
                gather_of(0, s).start()

            @pl.loop(0, g // 2)
            def _(tp):
                for par in range(2):
                    tl = tp * 2 + par
                    ob = ress[par]

                    @pl.when(tl >= 2)
                    def _():
                        out_of(tl - 2, par).wait()

                    for s in range(ns):
                        gather_of(tl, s).wait()
                        rows = bufs[s]

                        def fold(c, accs):
                            col = pl.multiple_of(c * 2 * ln, 2 * ln)
                            xa = plsc.bitcast(x_v[tl, pl.ds(col, ln)], BF16)
                            xb = plsc.bitcast(x_v[tl, pl.ds(col + ln, ln)], BF16)
                            out = []
                            for q in range(r):
                                pa = plsc.bitcast(rows[q, pl.ds(col, ln)], BF16) * xa
                                pb = plsc.bitcast(rows[q, pl.ds(col + ln, ln)], BF16) * xb
                                pair = plsc.bitcast(pa + pb, jnp.uint32)
                                lo = plsc.bitcast(lax.shift_left(pair, jnp.uint32(16)), F32)
                                hi = plsc.bitcast(pair & jnp.uint32(0xFFFF0000), F32)
                                out.append(accs[q] + lo + hi)
                            return tuple(out)

                        accs = lax.fori_loop(0, w // (2 * ln), fold, (zero,) * r)
                        for q in range(r):
                            k = s * r + q
                            ob[k // per_row, pl.ds((k % per_row) * ln, ln)] = accs[q]

                        @pl.when(tl + 1 < g)
                        def _():
                            gather_of(tl + 1, s).start()

                    out_of(tl, par).start()

            for par in range(2):
                out_of(g - 2 + par, par).wait()

    return dots(tab, idx_flat, xw)


def _weighted_row_sum(tab, idx_flat, wrep):
    p = idx_flat.shape[0]
    w = tab.shape[1]
    t_total = p // HK
    workers = SC_CORES * SC_SUBCORES
    tpw = t_total // workers
    g = min(SUM_TOKENS, tpw)
    r = SUM_ROWS
    ns = HK // r
    ln = SC_LANES
    per_row = LANES // ln
    assert t_total % workers == 0 and tpw % g == 0 and g % 2 == 0 and w % ln == 0
    buf = pltpu.VMEM((r, w), tab.dtype)
    acc = pltpu.VMEM((2 * w,), F32)
    sem = pltpu.SemaphoreType.DMA

    @pl.kernel(out_type=jax.ShapeDtypeStruct((t_total, 2 * w), F32), mesh=_sc_mesh(),
               scratch_types=[pltpu.VMEM((g * HK,), jnp.int32), pltpu.VMEM((g * HK // per_row, LANES), jnp.uint32), acc, acc]
               + [buf] * ns + [sem] * (ns + 2),
               compiler_params=pltpu.CompilerParams(needs_layout_passes=False))
    def wsum(t_hbm, i_hbm, w_hbm, o_hbm, idx_v, w_v, acc0, acc1, *scratch):
        bufs, gsem, osem = scratch[0:ns], scratch[ns:2 * ns], scratch[2 * ns:2 * ns + 2]
        accs = (acc0, acc1)
        wid = _sc_worker()

        @pl.loop(0, tpw // g)
        def _(win):
            tok0 = wid * tpw + win * g
            pltpu.sync_copy(i_hbm.at[pl.ds(tok0 * HK, g * HK)], idx_v)
            pltpu.sync_copy(w_hbm.at[pl.ds(tok0 * (HK // per_row), g * HK // per_row)], w_v)

            def gather_of(tl, s):
                return pltpu.make_async_copy(t_hbm.at[idx_v.at[pl.ds(tl * HK + s * r, r)]], bufs[s], gsem[s])

            def out_of(tl, par):
                return pltpu.make_async_copy(accs[par], o_hbm.at[tok0 + tl], osem[par])

            for s in range(ns):
                gather_of(0, s).start()

            @pl.loop(0, g // 2)
            def _(tp):
                for par in range(2):
                    tl = tp * 2 + par
                    ob = accs[par]

                    @pl.when(tl >= 2)
                    def _():
                        out_of(tl - 2, par).wait()

                    for s in range(ns):
                        gather_of(tl, s).wait()
                        wrow = tl * (HK // per_row) + s * (r // per_row)
                        wk = [plsc.bitcast(w_v[wrow + q // per_row, pl.ds((q % per_row) * ln, ln)], BF16)
                              for q in range(r)]
                        rows = bufs[s]

                        def fold(c, carry, first=(s == 0)):
                            for half in range(SUM_UNROLL):
                                col = pl.multiple_of((c * SUM_UNROLL + half) * ln, ln)
                                los, his = [], []
                                for q in range(0, r, SUM_BF16_ROWS):
                                    prods = [plsc.bitcast(rows[q + i, pl.ds(col, ln)], BF16) * wk[q + i]
                                             for i in range(SUM_BF16_ROWS)]
                                    pair = plsc.bitcast(_tree_sum(prods), jnp.uint32)
                                    los.append(plsc.bitcast(lax.shift_left(pair, jnp.uint32(16)), F32))
                                    his.append(plsc.bitcast(pair & jnp.uint32(0xFFFF0000), F32))
                                if first:
                                    ob[pl.ds(col, ln)] = _tree_sum(los)
                                    ob[pl.ds(w + col, ln)] = _tree_sum(his)
                                else:
                                    plsc.addupdate(ob.at[pl.ds(col, ln)], _tree_sum(los))
                                    plsc.addupdate(ob.at[pl.ds(w + col, ln)], _tree_sum(his))
                            return carry

                        lax.fori_loop(0, w // (ln * SUM_UNROLL), fold, 0)

                        @pl.when(tl + 1 < g)
                        def _():
                            gather_of(tl + 1, s).start()

                    out_of(tl, par).start()

            for par in range(2):
                out_of(g - 2 + par, par).wait()

    return wsum(tab, idx_flat, wrep)


def _peerw_kernel(part_ref, gate_ref, fold_ref, rep_ref, o_ref):
    hid = _mm_split(part_ref[...], fold_ref[...], NN, 3, 1)
    w = gate_ref[...] * (0.5 * hid * (1.0 + lax.erf(hid * (2.0 ** -0.5))))
    rep = jnp.dot(w.astype(BF16), rep_ref[...], preferred_element_type=F32)
    bits = pltpu.bitcast(rep, jnp.uint32)
    o_ref[...] = bits | lax.shift_right_logical(bits, jnp.uint32(16))


def _peerw(part, gate, tt):
    n = gate.shape[0]
    wide = HK * SC_LANES
    lane = jnp.arange(wide)
    fold = (lane[:, None] // SC_LANES == jnp.arange(HK)[None, :]).astype(BF16)
    row = lambda i: (i, 0)
    fix = lambda i: (0, 0)
    return pl.pallas_call(
        _peerw_kernel,
        grid=(n // tt,),
        in_specs=[pl.BlockSpec((tt, wide), row), pl.BlockSpec((tt, HK), row), pl.BlockSpec((wide, HK), fix),
                  pl.BlockSpec((HK, wide), fix)],
        out_specs=pl.BlockSpec((tt, wide), row),
        out_shape=jax.ShapeDtypeStruct((n, wide), jnp.uint32),
        compiler_params=_params("parallel"),
    )(part.reshape(n, wide), gate, fold, fold.T)


def _ple_kernel(h_ref, f_ref, p_ref, g_ref, wg_ref, wp_ref, gf_ref, o_ref):
    h = h_ref[...] + f_ref[...]
    gate = jax.nn.sigmoid(jnp.dot(_rms(h, g_ref[...]).astype(BF16), wg_ref[...], preferred_element_type=F32))
    pp = jnp.dot(p_ref[...].astype(BF16), wp_ref[...], preferred_element_type=F32)
    o_ref[...] = _rms(h + gate * pp, gf_ref[...])


def _ple(h1, ffn, p2, g, wg, wp, gf, tm):
    n, d = h1.shape
    pd = p2.shape[1]
    row = lambda i: (i, 0)
    fix = lambda i: (0, 0)
    return pl.pallas_call(
        _ple_kernel,
        grid=(n // tm,),
        in_specs=[pl.BlockSpec((tm, d), row), pl.BlockSpec((tm, d), row), pl.BlockSpec((tm, pd), row),
                  pl.BlockSpec((1, d), fix), pl.BlockSpec((d, d), fix), pl.BlockSpec((pd, d), fix),
                  pl.BlockSpec((1, d), fix)],
        out_specs=pl.BlockSpec((tm, d), row),
        out_shape=jax.ShapeDtypeStruct((n, d), F32),
        compiler_params=_params("parallel"),
    )(h1, ffn, p2, g, wg, wp, gf)


def _rope_tables(positions):
    half = ROT_DIM // 2
    inv_freq = ROPE_THETA ** (-jnp.arange(half, dtype=F32) * 2.0 / ROT_DIM)
    ang = positions.astype(F32).reshape(-1, 1) * inv_freq
    d = jnp.arange(LANES) % DA_HEAD_DIM
    cos = jnp.tile(jnp.cos(ang), (1, LANES // half))
    sin = jnp.tile(jnp.sin(ang), (1, LANES // half))
    c = jnp.where(d < ROT_DIM, cos, 1.0)
    s1 = jnp.where(d < half, -sin, 0.0)
    s2 = jnp.where((d >= half) & (d < ROT_DIM), sin, 0.0)
    return c, s1, s2


def _pack_rows(tab):
    d = tab.shape[1]
    bits = lax.bitcast_convert_type(tab.astype(F32), jnp.uint32)
    rne = bits + jnp.uint32(0x7FFF) + ((bits >> 16) & jnp.uint32(1))
    return (rne[:, :d // 2] >> 16) | (rne[:, d // 2:] & jnp.uint32(0xFFFF0000))


def _block_diag_ones(width, head):
    i = jnp.arange(width)
    return (i[:, None] // head == i[None, :] // head).astype(F32)


def _tiles(seq):
    pieces = 4 if seq % 8192 == 0 else 1
    rows = seq // pieces
    return dict(pieces=pieces, rows=rows, tm=min(256, rows), tq=min(512, rows), units=min(4, rows // RW_UNIT), groups=2,
                tt_topk=min(256, rows), tt_mix=min(256, rows))


def kernel(x, p, positions, norm_mix_g, w_in, lam_q1, lam_k1, lam_q2, lam_k2, da_subln_g, rw_mu, rw_w0, rw_w_up, rw_a0, rw_a_up, rw_g_up, rw_k_k, rw_k_a, rw_r_k, rw_ln_g, rw_ln_b, w_out, norm_ffn_g, peer_w_q, peer_sub_keys, peer_u, peer_v, norm_ple_g, ple_gate_w, ple_proj_w, norm_final_g):
    batch, seq, d = x.shape
    t = _tiles(seq)
    row = lambda a: a.reshape(1, -1)
    f32 = F32

    w_in_bf = w_in[0].astype(BF16)
    lam = (jnp.exp(jnp.sum(lam_q1[0].astype(f32) * lam_k1[0].astype(f32)))
           - jnp.exp(jnp.sum(lam_q2[0].astype(f32) * lam_k2[0].astype(f32))) + LAM_INIT).reshape(1, 1)
    width = rw_w0.shape[1]
    wup_pad = jnp.concatenate([rw_w_up[0], jnp.zeros((LANES - rw_w_up.shape[1], width), f32)], axis=0)
    aup_pad = jnp.concatenate([jnp.zeros((LANES - rw_a_up.shape[1], width), f32), rw_a_up[0]], axis=0)
    head_ones = _block_diag_ones(width, RW_HEAD)
    w_out_bf = w_out[0].astype(BF16)
    da_w = DA_HEADS * 2 * DA_HEAD_DIM
    keys = peer_sub_keys[0].reshape(PEER_HEADS * 2, N_KEYS, -1).astype(BF16)
    wq_t = peer_w_q[0].T.astype(BF16)
    u_tab = _pack_rows(peer_u[0])
    v_tab = None
    wg_bf = ple_gate_w[0].astype(BF16)
    wp_bf = ple_proj_w[0].astype(BF16)

    rope = [a.reshape(batch, seq, LANES) for a in _rope_tables(positions)]
    rows = t["rows"]
    pieces = [(b, h) for b in range(batch) for h in range(t["pieces"])]
    rows_of = lambda a, piece: a[piece[0], piece[1] * rows:(piece[1] + 1) * rows]
    tie = lax.optimization_barrier

    def mixers(piece, xs, carry):
        rc, rs1, rs2 = (rows_of(a, piece) for a in rope)
        qkv, zrw = _inproj(xs, row(norm_mix_g[0]), w_in_bf, rc, rs1, rs2, t["tm"])
        if carry is None:
            carry = dict(qkv=qkv[:0], z_last=jnp.zeros((1, zrw.shape[1]), f32),
                         state=jnp.zeros((width // (t["groups"] * LANES), t["groups"], RW_UNIT, RW_UNIT), f32))
        qkv = jnp.concatenate([carry["qkv"], qkv], axis=0)
        o_da = _attention(qkv, lam, row(da_subln_g[0]), t["tq"], rows)
        rw = _rwprep(zrw, carry["z_last"], row(rw_mu[0]), row(rw_w0[0]), wup_pad, row(rw_a0[0]), aup_pad, rw_g_up[0],
                     row(rw_k_k[0]), row(rw_k_a[0]), head_ones, t["tm"])
        o_rw, state = _rwcore(*rw, row(rw_ln_g[0]), row(rw_ln_b[0]), row(rw_r_k[0]), carry["state"], t["units"], t["groups"])
        h1, u2 = _outproj(xs, o_da, o_rw, w_out_bf[:da_w], w_out_bf[da_w:], row(norm_ffn_g[0]), t["tm"])
        idx, gate = _peertopk(u2, wq_t, keys, t["tt_topk"])
        idx = idx.reshape(-1)
        part = _row_dots(u_tab, idx, _pack_rows(u2))
        return dict(h1=h1, gate=gate, idx=idx, part=part), dict(qkv=qkv, z_last=zrw[-1:], state=state)

    def out(s, ffn, piece):
        return _ple(s["h1"], ffn, rows_of(p[0], piece), row(norm_ple_g[0]), wg_bf, wp_bf, row(norm_final_g), t["tm"])

    outs = []
    prev, older, older_ffn, carry = None, None, None, None
    xs = rows_of(x, pieces[0])
    for j in range(len(pieces) + 1):
        cur = None
        if j < len(pieces):
            cur, carry = mixers(pieces[j], xs, carry if pieces[j][1] > 0 else None)
        if prev is not None:
            tied = [prev["part"]] + ([cur["idx"]] if cur is not None else []) + ([older_ffn] if older is not None else [])
            tied = tie(tuple(tied))
            if older is not None:
                outs.append(out(older, tied[-1], pieces[j - 2]))
            if v_tab is None:
                v_tab = _pack_rows(tie((peer_v[0], (cur or prev)["idx"]))[0])
            wrep = _peerw(tied[0], prev["gate"], t["tt_mix"])
            if j + 1 < len(pieces):
                wrep, xs = tie((wrep, rows_of(x, pieces[j + 1])))
            older, older_ffn = prev, _weighted_row_sum(v_tab, prev["idx"], wrep.reshape(-1, LANES))
        elif j + 1 < len(pieces):
            xs = rows_of(x, pieces[j + 1])
        prev = cur
    outs.append(out(older, older_ffn, pieces[-1]))
    return jnp.concatenate(outs, axis=0).reshape(batch, seq, d)
```

```python
import functools
import math

import jax
import jax.numpy as jnp
import numpy as np
from jax import lax
from jax.experimental import pallas as pl
from jax.experimental.pallas import tpu as pltpu
from jax.experimental.pallas import tpu_sc as plsc

F32 = jnp.float32
BF16 = jnp.bfloat16

NORM_EPS = 1e-6
DA_HEADS = 4
DA_HEAD_DIM = 64
ROPE_THETA = 500000.0
ROT_DIM = DA_HEAD_DIM // 4
RW_HEAD = 64
RW_GN_EPS = 64e-5
PEER_HEADS = 8
N_KEYS = 128
PEER_TOPK = 16
LAM_INIT = 0.8 - 0.6 * math.exp(-0.3 * 0)

LANES = 128
VMEM_LIMIT = 56 * 1024 * 1024
RW_CHUNK = 64
RW_UNIT = 2 * RW_CHUNK

NN = (((1,), (0,)), ((), ()))
NT = (((1,), (1,)), ((), ()))
TN = (((0,), (0,)), ((), ()))
HI = lax.Precision.HIGHEST

HALF_WORD = np.uint32(16)
HIGH_HALF = np.uint32(0xFFFF0000)


def _mm(a, b):
    return jnp.dot(a.astype(BF16), b.astype(BF16), preferred_element_type=F32)


def _mm_nt(a, b):
    return lax.dot_general(a.astype(BF16), b.astype(BF16), NT, preferred_element_type=F32)


def _mm_tn(a, b):
    return lax.dot_general(a.astype(BF16), b.astype(BF16), TN, preferred_element_type=F32)


def _mm_hi(a, b):
    return jnp.dot(a, b, precision=HI, preferred_element_type=F32)


def _pieces(a, n):
    out = []
    for _ in range(n):
        piece = a.astype(BF16)
        out.append(piece)
        a = a - piece.astype(F32)
    return out


def _mm_split(a, b, dims, a_pieces, b_pieces):
    ap, bp = _pieces(a, a_pieces), _pieces(b, b_pieces)
    terms = [lax.dot_general(x, y, dims, preferred_element_type=F32)
             for i, x in enumerate(ap) for j, y in enumerate(bp) if i + j < max(a_pieces, b_pieces)]
    return functools.reduce(lambda u, v: u + v, terms)


def _params(*sem):
    return pltpu.CompilerParams(dimension_semantics=sem, vmem_limit_bytes=VMEM_LIMIT)


def _rms(x, g):
    return x * lax.rsqrt(jnp.mean(x * x, axis=-1, keepdims=True) + NORM_EPS) * g


def _inproj_kernel(x_ref, g_ref, w_ref, c_ref, s1_ref, s2_ref, qkv_ref, zrw_ref, *, n_qk, n_da):
    u = _rms(x_ref[...], g_ref[...]).astype(BF16)
    z = jnp.dot(u, w_ref[...], preferred_element_type=F32)
    c, s1, s2 = c_ref[...], s1_ref[...], s2_ref[...]
    half = ROT_DIM // 2
    for blk in range(n_da // LANES):
        t = z[:, blk * LANES:(blk + 1) * LANES]
        if blk < 2 * n_qk // LANES:
            t = t * c + pltpu.roll(t, LANES - half, 1) * s1 + pltpu.roll(t, half, 1) * s2
        if blk < n_qk // LANES:
            t = t * (DA_HEAD_DIM ** -0.5 * math.log2(math.e))
        qkv_ref[:, blk * LANES:(blk + 1) * LANES] = t.astype(BF16)
    zrw_ref[...] = z[:, n_da:]


def _inproj(x2, g, w_in_bf, rc, rs1, rs2, tm, row0, n):
    d = x2.shape[1]
    n_in = w_in_bf.shape[1]
    n_qk = DA_HEADS * 2 * DA_HEAD_DIM
    n_da = 3 * n_qk
    row = lambda i: (i, 0)
    off = lambda i: (row0 // tm + i, 0)
    fix = lambda i: (0, 0)
    return pl.pallas_call(
        functools.partial(_inproj_kernel, n_qk=n_qk, n_da=n_da),
        grid=(n // tm,),
        in_specs=[pl.BlockSpec((tm, d), off), pl.BlockSpec((1, d), fix), pl.BlockSpec((d, n_in), fix),
                  pl.BlockSpec((tm, LANES), off), pl.BlockSpec((tm, LANES), off), pl.BlockSpec((tm, LANES), off)],
        out_specs=[pl.BlockSpec((tm, n_da), row), pl.BlockSpec((tm, n_in - n_da), row)],
        out_shape=[jax.ShapeDtypeStruct((n, n_da), BF16), jax.ShapeDtypeStruct((n, n_in - n_da), F32)],
        compiler_params=_params("parallel"),
    )(x2, g, w_in_bf, rc, rs1, rs2)


def _attn_kernel(lam_ref, q_ref, k_ref, v_ref, sg_ref, o_ref, m_ref, acc_ref, *, tq, q_tile0):
    i = pl.program_id(1) + q_tile0
    q = q_ref[...]
    lane = lax.broadcasted_iota(jnp.int32, q.shape, 1)
    zero = jnp.zeros_like(q)
    qs = (jnp.where(lane < DA_HEAD_DIM, q, zero), jnp.where(lane >= DA_HEAD_DIM, q, zero))
    m_ref[...] = jnp.full(m_ref.shape, -jnp.inf, F32)
    acc_ref[...] = jnp.zeros(acc_ref.shape, F32)
    ones = jnp.ones((tq, LANES), BF16)

    def block(j, masked):
        kj = k_ref[pl.ds(pl.multiple_of(j * tq, tq), tq), :]
        vj = jnp.concatenate([v_ref[pl.ds(pl.multiple_of(j * tq, tq), tq), :], ones], axis=1)
        for c in range(2):
            s = lax.dot_general(qs[c], kj, NT, preferred_element_type=F32)
            if masked:
                r_id = lax.broadcasted_iota(jnp.int32, s.shape, 0)
                c_id = lax.broadcasted_iota(jnp.int32, s.shape, 1)
                s = jnp.where(c_id <= r_id, s, -jnp.inf)
            m_old = m_ref[c]
            m_new = jnp.maximum(m_old, jnp.max(s, axis=-1, keepdims=True))
            alpha = jnp.exp2(m_old - m_new)
            p = jnp.exp2(s - jnp.tile(m_new, (1, tq // LANES)))
            pv = jnp.dot(p.astype(BF16), vj, preferred_element_type=F32)
            acc_ref[c] = jnp.tile(alpha, (1, 2)) * acc_ref[c] + pv
            m_ref[c] = m_new

    def body(j, carry):
        block(j, False)
        return carry

    lax.fori_loop(0, i, body, 0)
    block(i, True)
    lam = lam_ref[0, 0]
    a0, a1 = acc_ref[0], acc_ref[1]
    o = a0[:, :LANES] / a0[:, LANES:] - lam * (a1[:, :LANES] / a1[:, LANES:])
    o = o * lax.rsqrt(jnp.mean(o * o, axis=-1, keepdims=True) + NORM_EPS) * sg_ref[...] * (1.0 - LAM_INIT)
    o_ref[...] = o.astype(o_ref.dtype)


def _attention(qkv, lam, subln_g, tq, q_rows):
    kv_rows = qkv.shape[0]
    nq = q_rows // tq
    q_tile0 = (kv_rows - q_rows) // tq
    h = DA_HEADS
    return pl.pallas_call(
        functools.partial(_attn_kernel, tq=tq, q_tile0=q_tile0),
        grid=(h, nq),
        in_specs=[pl.BlockSpec(memory_space=pltpu.SMEM),
                  pl.BlockSpec((tq, LANES), lambda hh, i: (q_tile0 + i, hh)),
                  pl.BlockSpec((kv_rows, LANES), lambda hh, i: (0, h + hh)),
                  pl.BlockSpec((kv_rows, LANES), lambda hh, i: (0, 2 * h + hh)),
                  pl.BlockSpec((1, LANES), lambda hh, i: (0, 0))],
        out_specs=pl.BlockSpec((tq, LANES), lambda hh, i: (i, hh)),
        out_shape=jax.ShapeDtypeStruct((q_rows, h * LANES), BF16),
        scratch_shapes=[pltpu.VMEM((2, tq, LANES), F32), pltpu.VMEM((2, tq, 2 * LANES), F32)],
        compiler_params=_params("parallel", "arbitrary"),
    )(lam, qkv, qkv, qkv, subln_g)


def _rwprep_kernel(z_ref, zp_ref, z0_ref, mu_ref, w0_ref, wup_ref, a0_ref, aup_ref, gup_ref, kk_ref, ka_ref, bd_ref,
                   r_o, ld_o, k_o, v_o, kk_o, b_o, g_o, *, width):
    i = pl.program_id(0)
    z = z_ref[...]
    first = jnp.where(i == 0, z0_ref[...], zp_ref[7:8, :])
    row = lax.broadcasted_iota(jnp.int32, z.shape, 0)
    prev = jnp.where(row == 0, first, pltpu.roll(z, 1, 0))
    zs = z + (prev - z) * mu_ref[...]
    r = zs[:, 0:width]
    k = zs[:, width:2 * width]
    v = zs[:, 2 * width:3 * width]
    xwa = zs[:, 3 * width:3 * width + LANES]
    xg = zs[:, 3 * width + LANES:3 * width + 2 * LANES]
    w = -jax.nn.softplus(-(w0_ref[...] + _mm_hi(jnp.tanh(xwa), wup_ref[...]))) - 0.5
    a = jax.nn.sigmoid(a0_ref[...] + _mm_hi(xwa, aup_ref[...]))
    g = _mm_hi(jax.nn.sigmoid(xg), gup_ref[...])
    kk = k * kk_ref[...]
    kk = kk / jnp.maximum(jnp.sqrt(_mm_hi(kk * kk, bd_ref[...])), 1e-12)
    r_o[...] = r
    ld_o[...] = -jnp.exp(w)
    k_o[...] = k * (1.0 + (a - 1.0) * ka_ref[...])
    v_o[...] = v
    kk_o[...] = kk
    b_o[...] = kk * a
    g_o[...] = g


def _rwprep(zrw, z_before, mu, w0, wup_pad, a0, aup_pad, gup, k_k, k_a, bd, tm):
    n, zin = zrw.shape
    width = w0.shape[1]
    row = lambda i: (i, 0)
    fix = lambda i: (0, 0)
    prev = lambda i: (jnp.maximum(i * (tm // 8) - 1, 0), 0)
    out = jax.ShapeDtypeStruct((n, width), F32)
    return pl.pallas_call(
        functools.partial(_rwprep_kernel, width=width),
        grid=(n // tm,),
        in_specs=[pl.BlockSpec((tm, zin), row), pl.BlockSpec((8, zin), prev), pl.BlockSpec((1, zin), fix),
                  pl.BlockSpec((1, zin), fix),
                  pl.BlockSpec((1, width), fix), pl.BlockSpec((LANES, width), fix),
                  pl.BlockSpec((1, width), fix), pl.BlockSpec((LANES, width), fix), pl.BlockSpec((LANES, width), fix),
                  pl.BlockSpec((1, width), fix), pl.BlockSpec((1, width), fix), pl.BlockSpec((width, width), fix)],
        out_specs=[pl.BlockSpec((tm, width), row)] * 7,
        out_shape=[out] * 7,
        compiler_params=_params("parallel"),
    )(zrw, zrw, z_before, mu, w0, wup_pad, a0, aup_pad, gup, k_k, k_a, bd)


def _rwcore_kernel(r_ref, ld_ref, k_ref, v_ref, kk_ref, b_ref, g_ref, lng_ref, lnb_ref, rk_ref, s0_ref,
                   o_ref, sout_ref, s_ref, *, units, groups):
    U, C, HD = RW_UNIT, RW_CHUNK, RW_HEAD

    @pl.when(pl.program_id(1) == 0)
    def _():
        s_ref[...] = s0_ref[0]

    ri = lax.broadcasted_iota(jnp.int32, (U, U), 0)
    ci = lax.broadcasted_iota(jnp.int32, (U, U), 1)
    same = (ri // C) == (ci // C)
    tri_s = same & (ci < ri)
    tri_i = same & (ci <= ri)
    eye = (ri == ci).astype(F32)
    cum_w = tri_i.astype(F32)
    head_avg = same.astype(F32) * (1.0 / HD)
    head_sum = same.astype(F32)
    hmask = (ci < HD, ci >= HD)
    cmask = (ri < C, ri >= C)
    zero = jnp.zeros((U, U), F32)
    cells = [(u, q) for u in range(units) for q in range(groups)]

    def blk(ref, cell):
        u, q = cell
        return ref[pl.ds(u * U, U), q * LANES:(q + 1) * LANES]

    def vec(ref, q):
        return ref[:, q * LANES:(q + 1) * LANES]

    cell_v, cell_at, cell_bt, cell_rt, cell_bk, cell_gam = {}, {}, {}, {}, {}, {}
    for cell in cells:
        ld = blk(ld_ref, cell)
        cum = _mm_split(cum_w, ld, NN, 1, 3)
        gam = jnp.exp(cum)
        ginv = jnp.exp(-cum)
        bt = blk(b_ref, cell) * ginv
        cell_v[cell] = blk(v_ref, cell)
        cell_at[cell] = -blk(kk_ref, cell) * jnp.exp(cum - ld)
        cell_bt[cell] = bt
        cell_rt[cell] = blk(r_ref, cell) * gam
        cell_bk[cell] = jnp.concatenate([bt, blk(k_ref, cell) * ginv], axis=0)
        cell_gam[cell] = gam

    chains = [(cell, h) for cell in cells for h in range(2)]
    mab, mak, mrb, mrk = {}, {}, {}, {}
    for ch in chains:
        cell, h = ch
        ar = jnp.concatenate([jnp.where(hmask[h], cell_at[cell], zero), jnp.where(hmask[h], cell_rt[cell], zero)], axis=0)
        m = _mm_nt(ar, cell_bk[cell])
        mab[ch] = jnp.where(tri_s, m[0:U, 0:U], zero)
        mak[ch] = jnp.where(tri_s, m[0:U, U:2 * U], zero)
        mrb[ch] = jnp.where(tri_i, m[U:2 * U, 0:U], zero)
        mrk[ch] = jnp.where(tri_i, m[U:2 * U, U:2 * U], zero)

    tm = {ch: eye + mab[ch] for ch in chains}
    pw = dict(mab)
    for _ in range(int(math.log2(C)) - 1):
        pw = {ch: _mm(pw[ch], pw[ch]) for ch in chains}
        tm = {ch: tm[ch] + _mm(tm[ch], pw[ch]) for ch in chains}
    aph = {ch: _mm(tm[ch], cell_at[ch[0]]) for ch in chains}
    mv = {ch: _mm(mak[ch], cell_v[ch[0]]) for ch in chains}
    uh = {ch: _mm(tm[ch], mv[ch]) for ch in chains}
    rph = {ch: _mm(mrb[ch], aph[ch]) for ch in chains}
    yph = {ch: _mm(mrb[ch], uh[ch]) + _mm(mrk[ch], cell_v[ch[0]]) for ch in chains}

    def both_heads(d, cell):
        return jnp.where(hmask[0], d[(cell, 0)], d[(cell, 1)])

    rp, yp, gs, hs = {}, {}, {}, {}
    for cell in cells:
        ap, uu = both_heads(aph, cell), both_heads(uh, cell)
        rp[cell] = both_heads(rph, cell) + cell_rt[cell]
        yp[cell] = both_heads(yph, cell)
        for c in range(2):
            gl = cell_gam[cell][(c + 1) * C - 1:(c + 1) * C, :]
            apc = jnp.where(cmask[c], ap, zero)
            uvc = jnp.concatenate([jnp.where(cmask[c], uu, zero), jnp.where(cmask[c], cell_v[cell], zero)], axis=0)
            gs[cell, c] = jnp.where(same, eye + _mm_tn(apc, cell_bt[cell]), zero) * gl
            hs[cell, c] = jnp.where(same, _mm_tn(uvc, cell_bk[cell]), zero) * gl

    s = [s_ref[q] for q in range(groups)]
    ys = {}
    for u in range(units):
        for q in range(groups):
            cell = (u, q)
            y0 = _mm_split(rp[cell], s[q], NT, 2, 2)
            s[q] = _mm_split(s[q], gs[cell, 0], NN, 2, 2) + hs[cell, 0]
            y1 = _mm_split(rp[cell], s[q], NT, 2, 2)
            s[q] = _mm_split(s[q], gs[cell, 1], NN, 2, 2) + hs[cell, 1]
            ys[cell] = jnp.where(cmask[0], y0, y1) + yp[cell]
    for q in range(groups):
        s_ref[q] = s[q]
        sout_ref[0, q] = s[q]

    for cell in cells:
        u, q = cell
        y = ys[cell]
        mean = _mm_split(y, head_avg, NN, 2, 1)
        yc = y - mean
        var = _mm_split(yc * yc, head_avg, NN, 2, 1)
        yn = yc * lax.rsqrt(var + RW_GN_EPS) * vec(lng_ref, q) + vec(lnb_ref, q)
        bonus = _mm_split(blk(r_ref, cell) * blk(k_ref, cell) * vec(rk_ref, q), head_sum, NN, 2, 1)
        yn = yn + bonus * cell_v[cell]
        o_ref[pl.ds(u * U, U), q * LANES:(q + 1) * LANES] = (yn * blk(g_ref, cell)).astype(o_ref.dtype)


def _rwcore(r, ld, k, v, kk, b, g, ln_g, ln_b, r_k, state, units, groups):
    n, width = r.shape
    rows = units * RW_UNIT
    lanes = groups * LANES
    blk = pl.BlockSpec((rows, lanes), lambda hp, i: (i, hp))
    vec = pl.BlockSpec((1, lanes), lambda hp, i: (0, hp))
    st = pl.BlockSpec((1, groups, RW_UNIT, RW_UNIT), lambda hp, i: (hp, 0, 0, 0))
    return pl.pallas_call(
        functools.partial(_rwcore_kernel, units=units, groups=groups),
        grid=(width // lanes, n // rows),
        in_specs=[blk] * 7 + [vec] * 3 + [st],
        out_specs=[blk, st],
        out_shape=[jax.ShapeDtypeStruct((n, width), BF16), jax.ShapeDtypeStruct(state.shape, F32)],
        scratch_shapes=[pltpu.VMEM((groups, RW_UNIT, RW_UNIT), F32)],
        compiler_params=_params("parallel", "arbitrary"),
    )(r, ld, k, v, kk, b, g, ln_g, ln_b, r_k, state)


def _outproj_kernel(x_ref, oda_ref, orw_ref, wa_ref, wb_ref, g_ref, h_ref, u_ref):
    h = (x_ref[...] + jnp.dot(oda_ref[...], wa_ref[...], preferred_element_type=F32)
         + jnp.dot(orw_ref[...], wb_ref[...], preferred_element_type=F32))
    h_ref[...] = h
    u_ref[...] = _rms(h, g_ref[...]).astype(BF16)


def _outproj(x2, o_da, o_rw, wa, wb, g, tm, row0):
    d = x2.shape[1]
    n, da = o_da.shape
    rw = o_rw.shape[1]
    row = lambda i: (i, 0)
    off = lambda i: (row0 // tm + i, 0)
    fix = lambda i: (0, 0)
    return pl.pallas_call(
        _outproj_kernel,
        grid=(n // tm,),
        in_specs=[pl.BlockSpec((tm, d), off), pl.BlockSpec((tm, da), row), pl.BlockSpec((tm, rw), row),
                  pl.BlockSpec((da, d), fix), pl.BlockSpec((rw, d), fix), pl.BlockSpec((1, d), fix)],
        out_specs=[pl.BlockSpec((tm, d), row), pl.BlockSpec((tm, d), row)],
        out_shape=[jax.ShapeDtypeStruct((n, d), F32), jax.ShapeDtypeStruct((n, d), BF16)],
        compiler_params=_params("parallel"),
    )(x2, o_da, o_rw, wa, wb, g)


def _topk_rows(s, k, payload=None):
    rows = s.shape[0]
    iota = lax.broadcasted_iota(jnp.int32, s.shape, 0).astype(F32)
    vals, sel = [], []
    for _ in range(k):
        m = jnp.max(s, axis=0, keepdims=True)
        am = jnp.min(jnp.where(s == m, iota, float(rows)), axis=0, keepdims=True)
        hit = iota == am
        vals.append(m)
        sel.append(am if payload is None else jnp.sum(jnp.where(hit, payload, 0.0), axis=0, keepdims=True))
        s = jnp.where(hit, -jnp.inf, s)
    return vals, sel


def _stack_rows(rows_list):
    k = len(rows_list)
    iota = lax.broadcasted_iota(jnp.int32, (k, rows_list[0].shape[1]), 0)
    out = jnp.zeros(iota.shape, rows_list[0].dtype)
    for j, r in enumerate(rows_list):
        out = jnp.where(iota == j, r, out)
    return out


def _peertopk_kernel(u_ref, wq_ref, keys_ref, idx_ref, gate_ref):
    u = u_ref[...]
    half = N_KEYS
    idx_rows, gate_rows = [], []
    for h in range(PEER_HEADS):
        tops = []
        for p in range(2):
            hp = h * 2 + p
            q_t = lax.dot_general(wq_ref[:, hp * half:(hp + 1) * half], u, (((0,), (1,)), ((), ())),
                                  preferred_element_type=F32)
            s_t = jnp.dot(keys_ref[hp], q_t.astype(BF16), preferred_element_type=F32)
            tops.append(_topk_rows(s_t, PEER_TOPK))
        (v1, i1), (v2, i2) = tops
        pairs = [(i, j) for i in range(PEER_TOPK) for j in range(PEER_TOPK) if (i + 1) * (j + 1) <= PEER_TOPK]
        pad = -len(pairs) % 8
        cand = _stack_rows([v1[i] + v2[j] for i, j in pairs] + [jnp.full_like(v1[0], -jnp.inf)] * pad)
        cidx = _stack_rows([i1[i] * float(N_KEYS) + i2[j] for i, j in pairs] + [jnp.zeros_like(i1[0])] * pad)
        best, idx = _topk_rows(cand, PEER_TOPK, payload=cidx)
        e = [jnp.exp(b - best[0]) for b in best]
        den = functools.reduce(lambda a, b: a + b, e)
        idx_rows.append(_stack_rows(idx))
        gate_rows.append(_stack_rows([x / den for x in e]))
    idx_ref[...] = jnp.concatenate(idx_rows, axis=0).T.astype(jnp.int32)
    gate_ref[...] = jnp.concatenate(gate_rows, axis=0).T


def _peertopk(u_bf, wq, keys, tt):
    n, d = u_bf.shape
    hk = PEER_HEADS * PEER_TOPK
    row = lambda i: (i, 0)
    return pl.pallas_call(
        _peertopk_kernel,
        grid=(n // tt,),
        in_specs=[pl.BlockSpec((tt, d), row), pl.BlockSpec(wq.shape, lambda i: (0, 0)),
                  pl.BlockSpec(keys.shape, lambda i: (0, 0, 0))],
        out_specs=[pl.BlockSpec((tt, hk), row), pl.BlockSpec((tt, hk), row)],
        out_shape=[jax.ShapeDtypeStruct((n, hk), jnp.int32), jax.ShapeDtypeStruct((n, hk), F32)],
        compiler_params=_params("parallel"),
    )(u_bf, wq, keys)


SC_CORES = 2
SC_SUBCORES = 16
SC_LANES = 16
HK = PEER_HEADS * PEER_TOPK
DOT_ROWS = 16
SUM_ROWS = 32
SUM_TOKENS = 8
SUM_UNROLL = 2
SUM_BF16_ROWS = 4


def _tree_sum(xs):
    while len(xs) > 1:
        xs = [xs[i] + xs[i + 1] for i in range(0, len(xs) - 1, 2)] + ([xs[-1]] if len(xs) % 2 else [])
    return xs[0]


def _sc_mesh():
    return plsc.VectorSubcoreMesh(core_axis_name="core", subcore_axis_name="subcore")


def _sc_worker():
    return lax.axis_index("core") * SC_SUBCORES + lax.axis_index("subcore")


def _row_dots(tab, idx_flat, xw):
    p = idx_flat.shape[0]
    w = tab.shape[1]
    t_total = p // HK
    workers = SC_CORES * SC_SUBCORES
    tpw = t_total // workers
    g = min(SUM_TOKENS, tpw)
    r = DOT_ROWS
    ns = HK // r
    ln = SC_LANES
    per_row = LANES // ln
    out_rows = HK // per_row
    assert t_total % workers == 0 and tpw % g == 0 and g % 2 == 0 and w % (2 * ln) == 0 and r % per_row == 0
    buf = pltpu.VMEM((r, w), tab.dtype)
    res = pltpu.VMEM((out_rows, LANES), F32)
    sem = pltpu.SemaphoreType.DMA

    @pl.kernel(out_type=jax.ShapeDtypeStruct((t_total * out_rows, LANES), F32), mesh=_sc_mesh(),
               scratch_types=[pltpu.VMEM((g * HK,), jnp.int32), pltpu.VMEM((g, w), tab.dtype), res, res]
               + [buf] * ns + [sem] * (ns + 2),
               compiler_params=pltpu.CompilerParams(needs_layout_passes=False))
    def dots(t_hbm, i_hbm, x_hbm, o_hbm, idx_v, x_v, res0, res1, *scratch):
        bufs, gsem, osem = scratch[0:ns], scratch[ns:2 * ns], scratch[2 * ns:2 * ns + 2]
        ress = (res0, res1)
        wid = _sc_worker()
        zero = jnp.zeros((ln,), F32)

        @pl.loop(0, tpw // g)
        def _(win):
            tok0 = wid * tpw + win * g
            pltpu.sync_copy(i_hbm.at[pl.ds(tok0 * HK, g * HK)], idx_v)
            pltpu.sync_copy(x_hbm.at[pl.ds(tok0, g)], x_v)

            def gather_of(tl, s):
                return pltpu.make_async_copy(t_hbm.at[idx_v.at[pl.ds(tl * HK + s * r, r)]], bufs[s], gsem[s])

            def out_of(tl, par):
                return pltpu.make_async_copy(ress[par], o_hbm.at[pl.ds((tok0 + tl) * out_rows, out_rows)], osem[par])

            for s in range(ns):
                gather_of(0, s).start()

            @pl.loop(0, g // 2)
            def _(tp):
                for par in range(2):
                    tl = tp * 2 + par
                    ob = ress[par]

                    @pl.when(tl >= 2)
                    def _():
                        out_of(tl - 2, par).wait()

                    for s in range(ns):
                        gather_of(tl, s).wait()
                        rows = bufs[s]

                        def fold(c, accs):
                            col = pl.multiple_of(c * 2 * ln, 2 * ln)
                            xa = plsc.bitcast(x_v[tl, pl.ds(col, ln)], BF16)
                            xb = plsc.bitcast(x_v[tl, pl.ds(col + ln, ln)], BF16)
                            out = []
                            for q in range(r):
                                pa = plsc.bitcast(rows[q, pl.ds(col, ln)], BF16) * xa
                                pb = plsc.bitcast(rows[q, pl.ds(col + ln, ln)], BF16) * xb
                                pair = plsc.bitcast(pa + pb, jnp.uint32)
                                lo = plsc.bitcast(lax.shift_left(pair, HALF_WORD), F32)
                                hi = plsc.bitcast(pair & HIGH_HALF, F32)
                                out.append(accs[q] + lo + hi)
                            return tuple(out)

                        accs = lax.fori_loop(0, w // (2 * ln), fold, (zero,) * r)
                        for q in range(r):
                            k = s * r + q
                            ob[k // per_row, pl.ds((k % per_row) * ln, ln)] = accs[q]

                        @pl.when(tl + 1 < g)
                        def _():
                            gather_of(tl + 1, s).start()

                    out_of(tl, par).start()

            for par in range(2):
                out_of(g - 2 + par, par).wait()

    return dots(tab, idx_flat, xw)


def _weighted_row_sum(tab, idx_flat, wrep):
    p = idx_flat.shape[0]
    w = tab.shape[1]
    t_total = p // HK
    workers = SC_CORES * SC_SUBCORES
    tpw = t_total // workers
    g = min(SUM_TOKENS, tpw)
    r = SUM_ROWS
    ns = HK // r
    ln = SC_LANES
    per_row = LANES // ln
    assert t_total % workers == 0 and tpw % g == 0 and g % 2 == 0 and w % ln == 0
    buf = pltpu.VMEM((r, w), tab.dtype)
    acc = pltpu.VMEM((2 * w,), F32)
    sem = pltpu.SemaphoreType.DMA

    @pl.kernel(out_type=jax.ShapeDtypeStruct((t_total, 2 * w), F32), mesh=_sc_mesh(),
               scratch_types=[pltpu.VMEM((g * HK,), jnp.int32), pltpu.VMEM((g * HK // per_row, LANES), jnp.uint32), acc, acc]
               + [buf] * ns + [sem] * (ns + 2),
               compiler_params=pltpu.CompilerParams(needs_layout_passes=False))
    def wsum(t_hbm, i_hbm, w_hbm, o_hbm, idx_v, w_v, acc0, acc1, *scratch):
        bufs, gsem, osem = scratch[0:ns], scratch[ns:2 * ns], scratch[2 * ns:2 * ns + 2]
        accs = (acc0, acc1)
        wid = _sc_worker()

        @pl.loop(0, tpw // g)
        def _(win):
            tok0 = wid * tpw + win * g
            pltpu.sync_copy(i_hbm.at[pl.ds(tok0 * HK, g * HK)], idx_v)
            pltpu.sync_copy(w_hbm.at[pl.ds(tok0 * (HK // per_row), g * HK // per_row)], w_v)

            def gather_of(tl, s):
                return pltpu.make_async_copy(t_hbm.at[idx_v.at[pl.ds(tl * HK + s * r, r)]], bufs[s], gsem[s])

            def out_of(tl, par):
                return pltpu.make_async_copy(accs[par], o_hbm.at[tok0 + tl], osem[par])

            for s in range(ns):
                gather_of(0, s).start()

            @pl.loop(0, g // 2)
            def _(tp):
                for par in range(2):
                    tl = tp * 2 + par
                    ob = accs[par]

                    @pl.when(tl >= 2)
                    def _():
                        out_of(tl - 2, par).wait()

                    for s in range(ns):
                        gather_of(tl, s).wait()
                        wrow = tl * (HK // per_row) + s * (r // per_row)
                        wk = [plsc.bitcast(w_v[wrow + q // per_row, pl.ds((q % per_row) * ln, ln)], BF16)
                              for q in range(r)]
                        rows = bufs[s]

                        def fold(c, carry, first=(s == 0)):
                            for half in range(SUM_UNROLL):
                                col = pl.multiple_of((c * SUM_UNROLL + half) * ln, ln)
                                los, his = [], []
                                for q in range(0, r, SUM_BF16_ROWS):
                                    prods = [plsc.bitcast(rows[q + i, pl.ds(col, ln)], BF16) * wk[q + i]
                                             for i in range(SUM_BF16_ROWS)]
                                    pair = plsc.bitcast(_tree_sum(prods), jnp.uint32)
                                    los.append(plsc.bitcast(lax.shift_left(pair, HALF_WORD), F32))
                                    his.append(plsc.bitcast(pair & HIGH_HALF, F32))
                                if first:
                                    ob[pl.ds(col, ln)] = _tree_sum(los)
                                    ob[pl.ds(w + col, ln)] = _tree_sum(his)
                                else:
                                    plsc.addupdate(ob.at[pl.ds(col, ln)], _tree_sum(los))
                                    plsc.addupdate(ob.at[pl.ds(w + col, ln)], _tree_sum(his))
                            return carry

                        lax.fori_loop(0, w // (ln * SUM_UNROLL), fold, 0)

                        @pl.when(tl + 1 < g)
                        def _():
                            gather_of(tl + 1, s).start()

                    out_of(tl, par).start()

            for par in range(2):
                out_of(g - 2 + par, par).wait()

    return wsum(tab, idx_flat, wrep)


def _peerw_kernel(part_ref, gate_ref, fold_ref, rep_ref, o_ref):
    hid = _mm_split(part_ref[...], fold_ref[...], NN, 3, 1)
    w = gate_ref[...] * (0.5 * hid * (1.0 + lax.erf(hid * (2.0 ** -0.5))))
    rep = jnp.dot(w.astype(BF16), rep_ref[...], preferred_element_type=F32)
    bits = pltpu.bitcast(rep, jnp.uint32)
    o_ref[...] = bits | lax.shift_right_logical(bits, HALF_WORD)


def _peerw(part, gate, tt):
    n = gate.shape[0]
    wide = HK * SC_LANES
    lane = jnp.arange(wide)
    fold = (lane[:, None] // SC_LANES == jnp.arange(HK)[None, :]).astype(BF16)
    row = lambda i: (i, 0)
    fix = lambda i: (0, 0)
    return pl.pallas_call(
        _peerw_kernel,
        grid=(n // tt,),
        in_specs=[pl.BlockSpec((tt, wide), row), pl.BlockSpec((tt, HK), row), pl.BlockSpec((wide, HK), fix),
                  pl.BlockSpec((HK, wide), fix)],
        out_specs=pl.BlockSpec((tt, wide), row),
        out_shape=jax.ShapeDtypeStruct((n, wide), jnp.uint32),
        compiler_params=_params("parallel"),
    )(part.reshape(n, wide), gate, fold, fold.T)


def _ple_kernel(h_ref, f_ref, p_ref, g_ref, wg_ref, wp_ref, gf_ref, o_ref):
    h = h_ref[...] + f_ref[...]
    gate = jax.nn.sigmoid(jnp.dot(_rms(h, g_ref[...]).astype(BF16), wg_ref[...], preferred_element_type=F32))
    pp = jnp.dot(p_ref[...].astype(BF16), wp_ref[...], preferred_element_type=F32)
    o_ref[...] = _rms(h + gate * pp, gf_ref[...])


def _ple(h1, ffn, p2, g, wg, wp, gf, tm, row0):
    n, d = h1.shape
    pd = p2.shape[1]
    row = lambda i: (i, 0)
    off = lambda i: (row0 // tm + i, 0)
    fix = lambda i: (0, 0)
    return pl.pallas_call(
        _ple_kernel,
        grid=(n // tm,),
        in_specs=[pl.BlockSpec((tm, d), row), pl.BlockSpec((tm, d), row), pl.BlockSpec((tm, pd), off),
                  pl.BlockSpec((1, d), fix), pl.BlockSpec((d, d), fix), pl.BlockSpec((pd, d), fix),
                  pl.BlockSpec((1, d), fix)],
        out_specs=pl.BlockSpec((tm, d), row),
        out_shape=jax.ShapeDtypeStruct((n, d), F32),
        compiler_params=_params("parallel"),
    )(h1, ffn, p2, g, wg, wp, gf)


def _rope_tables(positions):
    half = ROT_DIM // 2
    inv_freq = ROPE_THETA ** (-jnp.arange(half, dtype=F32) * 2.0 / ROT_DIM)
    ang = positions.astype(F32).reshape(-1, 1) * inv_freq
    d = jnp.arange(LANES) % DA_HEAD_DIM
    cos = jnp.tile(jnp.cos(ang), (1, LANES // half))
    sin = jnp.tile(jnp.sin(ang), (1, LANES // half))
    c = jnp.where(d < ROT_DIM, cos, 1.0)
    s1 = jnp.where(d < half, -sin, 0.0)
    s2 = jnp.where((d >= half) & (d < ROT_DIM), sin, 0.0)
    return c, s1, s2


def _pack_rows(tab):
    d = tab.shape[1]
    bits = lax.bitcast_convert_type(tab.astype(F32), jnp.uint32)
    rne = bits + jnp.uint32(0x7FFF) + ((bits >> HALF_WORD) & jnp.uint32(1))
    return (rne[:, :d // 2] >> HALF_WORD) | (rne[:, d // 2:] & HIGH_HALF)


def _block_diag_ones(width, head):
    i = jnp.arange(width)
    return (i[:, None] // head == i[None, :] // head).astype(F32)


def _tiles(seq):
    pieces = 4 if seq % 8192 == 0 else 1
    rows = seq // pieces
    return dict(pieces=pieces, rows=rows, tm=min(256, rows), tq=min(512, rows), units=min(4, rows // RW_UNIT), groups=2,
                tt_topk=min(256, rows), tt_mix=min(256, rows))


def kernel(x, p, positions, norm_mix_g, w_in, lam_q1, lam_k1, lam_q2, lam_k2, da_subln_g, rw_mu, rw_w0, rw_w_up, rw_a0, rw_a_up, rw_g_up, rw_k_k, rw_k_a, rw_r_k, rw_ln_g, rw_ln_b, w_out, norm_ffn_g, peer_w_q, peer_sub_keys, peer_u, peer_v, norm_ple_g, ple_gate_w, ple_proj_w, norm_final_g):
    batch, seq, d = x.shape
    t = _tiles(seq)
    row = lambda a: a.reshape(1, -1)
    f32 = F32

    w_in_bf = w_in[0].astype(BF16)
    lam = (jnp.exp(jnp.sum(lam_q1[0].astype(f32) * lam_k1[0].astype(f32)))
           - jnp.exp(jnp.sum(lam_q2[0].astype(f32) * lam_k2[0].astype(f32))) + LAM_INIT).reshape(1, 1)
    width = rw_w0.shape[1]
    wup_pad = jnp.concatenate([rw_w_up[0], jnp.zeros((LANES - rw_w_up.shape[1], width), f32)], axis=0)
    aup_pad = jnp.concatenate([jnp.zeros((LANES - rw_a_up.shape[1], width), f32), rw_a_up[0]], axis=0)
    head_ones = _block_diag_ones(width, RW_HEAD)
    w_out_bf = w_out[0].astype(BF16)
    da_w = DA_HEADS * 2 * DA_HEAD_DIM
    keys = peer_sub_keys[0].reshape(PEER_HEADS * 2, N_KEYS, -1).astype(BF16)
    wq = peer_w_q[0].astype(BF16)
    u_tab = _pack_rows(peer_u[0])
    v_tab = None
    wg_bf = ple_gate_w[0].astype(BF16)
    wp_bf = ple_proj_w[0].astype(BF16)

    rope = _rope_tables(positions)
    x2 = x.reshape(batch * seq, d)
    p2 = p[0].reshape(batch * seq, -1)
    rows = t["rows"]
    pieces = [(b, h) for b in range(batch) for h in range(t["pieces"])]
    row0_of = lambda piece: piece[0] * seq + piece[1] * rows
    gain = row(norm_mix_g[0])
    tie = lax.optimization_barrier

    def mixers(piece, gain_in, carry):
        row0 = row0_of(piece)
        qkv, zrw = _inproj(x2, gain_in, w_in_bf, *rope, t["tm"], row0, rows)
        if carry is None:
            carry = dict(qkv=qkv[:0], z_last=jnp.zeros((1, zrw.shape[1]), f32),
                         state=jnp.zeros((width // (t["groups"] * LANES), t["groups"], RW_UNIT, RW_UNIT), f32))
        qkv = jnp.concatenate([carry["qkv"], qkv], axis=0)
        o_da = _attention(qkv, lam, row(da_subln_g[0]), t["tq"], rows)
        rw = _rwprep(zrw, carry["z_last"], row(rw_mu[0]), row(rw_w0[0]), wup_pad, row(rw_a0[0]), aup_pad, rw_g_up[0],
                     row(rw_k_k[0]), row(rw_k_a[0]), head_ones, t["tm"])
        o_rw, state = _rwcore(*rw, row(rw_ln_g[0]), row(rw_ln_b[0]), row(rw_r_k[0]), carry["state"], t["units"], t["groups"])
        h1, u2 = _outproj(x2, o_da, o_rw, w_out_bf[:da_w], w_out_bf[da_w:], row(norm_ffn_g[0]), t["tm"], row0)
        idx, gate = _peertopk(u2, wq, keys, t["tt_topk"])
        idx = idx.reshape(-1)
        part = _row_dots(u_tab, idx, _pack_rows(u2))
        return dict(h1=h1, gate=gate, idx=idx, part=part), dict(qkv=qkv, z_last=zrw[-1:], state=state)

    def out(s, ffn, piece):
        return _ple(s["h1"], ffn, p2, row(norm_ple_g[0]), wg_bf, wp_bf, row(norm_final_g), t["tm"], row0_of(piece))

    outs = []
    prev, older, older_ffn, carry = None, None, None, None
    gain_in = gain
    for j in range(len(pieces) + 1):
        cur = None
        if j < len(pieces):
            cur, carry = mixers(pieces[j], gain_in, carry if pieces[j][1] > 0 else None)
        if prev is not None:
            tied = [prev["part"]] + ([cur["idx"]] if cur is not None else []) + ([older_ffn] if older is not None else [])
            tied = tie(tuple(tied))
            if older is not None:
                outs.append(out(older, tied[-1], pieces[j - 2]))
            if v_tab is None:
                v_tab = _pack_rows(tie((peer_v[0], (cur or prev)["idx"]))[0])
            wrep = _peerw(tied[0], prev["gate"], t["tt_mix"])
            if j + 1 < len(pieces):
                wrep, gain_in = tie((wrep, gain))
            older, older_ffn = prev, _weighted_row_sum(v_tab, prev["idx"], wrep.reshape(-1, LANES))
        prev = cur
    outs.append(out(older, older_ffn, pieces[-1]))
    return jnp.concatenate(outs, axis=0).reshape(batch, seq, d)
```

```python
import functools
import math

import jax
import jax.numpy as jnp
import numpy as np
from jax import lax
from jax.experimental import pallas as pl
from jax.experimental.pallas import tpu as pltpu
from jax.experimental.pallas import tpu_sc as plsc

F32 = jnp.float32
BF16 = jnp.bfloat16

NORM_EPS = 1e-6
DA_HEADS = 4
DA_HEAD_DIM = 64
ROPE_THETA = 500000.0
ROT_DIM = DA_HEAD_DIM // 4
RW_HEAD = 64
RW_GN_EPS = 64e-5
PEER_HEADS = 8
N_KEYS = 128
PEER_TOPK = 16
LAM_INIT = 0.8 - 0.6 * math.exp(-0.3 * 0)

LANES = 128
VMEM_LIMIT = 56 * 1024 * 1024
RW_CHUNK = 64
RW_UNIT = 2 * RW_CHUNK

NN = (((1,), (0,)), ((), ()))
NT = (((1,), (1,)), ((), ()))
TN = (((0,), (0,)), ((), ()))
HI = lax.Precision.HIGHEST

HALF_WORD = np.uint32(16)
HIGH_HALF = np.uint32(0xFFFF0000)


def _mm(a, b):
    return jnp.dot(a.astype(BF16), b.astype(BF16), preferred_element_type=F32)


def _mm_nt(a, b):
    return lax.dot_general(a.astype(BF16), b.astype(BF16), NT, preferred_element_type=F32)


def _mm_tn(a, b):
    return lax.dot_general(a.astype(BF16), b.astype(BF16), TN, preferred_element_type=F32)


def _mm_hi(a, b):
    return jnp.dot(a, b, precision=HI, preferred_element_type=F32)


def _pieces(a, n):
    out = []
    for _ in range(n):
        piece = a.astype(BF16)
        out.append(piece)
        a = a - piece.astype(F32)
    return out


def _mm_split(a, b, dims, a_pieces, b_pieces):
    ap, bp = _pieces(a, a_pieces), _pieces(b, b_pieces)
    terms = [lax.dot_general(x, y, dims, preferred_element_type=F32)
             for i, x in enumerate(ap) for j, y in enumerate(bp) if i + j < max(a_pieces, b_pieces)]
    return functools.reduce(lambda u, v: u + v, terms)


def _params(*sem):
    return pltpu.CompilerParams(dimension_semantics=sem, vmem_limit_bytes=VMEM_LIMIT)


def _rms(x, g):
    return x * lax.rsqrt(jnp.mean(x * x, axis=-1, keepdims=True) + NORM_EPS) * g


def _inproj_kernel(x_ref, g_ref, w_ref, c_ref, s1_ref, s2_ref, qkv_ref, zrw_ref, *, n_qk, n_da):
    u = _rms(x_ref[...], g_ref[...]).astype(BF16)
    z = jnp.dot(u, w_ref[...], preferred_element_type=F32)
    c, s1, s2 = c_ref[...], s1_ref[...], s2_ref[...]
    half = ROT_DIM // 2
    for blk in range(n_da // LANES):
        t = z[:, blk * LANES:(blk + 1) * LANES]
        if blk < 2 * n_qk // LANES:
            t = t * c + pltpu.roll(t, LANES - half, 1) * s1 + pltpu.roll(t, half, 1) * s2
        if blk < n_qk // LANES:
            t = t * (DA_HEAD_DIM ** -0.5 * math.log2(math.e))
        qkv_ref[:, blk * LANES:(blk + 1) * LANES] = t.astype(BF16)
    zrw_ref[...] = z[:, n_da:]


def _inproj(x2, g, w_in_bf, rc, rs1, rs2, tm, row0, n):
    d = x2.shape[1]
    n_in = w_in_bf.shape[1]
    n_qk = DA_HEADS * 2 * DA_HEAD_DIM
    n_da = 3 * n_qk
    row = lambda i: (i, 0)
    off = lambda i: (row0 // tm + i, 0)
    fix = lambda i: (0, 0)
    return pl.pallas_call(
        functools.partial(_inproj_kernel, n_qk=n_qk, n_da=n_da),
        grid=(n // tm,),
        in_specs=[pl.BlockSpec((tm, d), off), pl.BlockSpec((1, d), fix), pl.BlockSpec((d, n_in), fix),
                  pl.BlockSpec((tm, LANES), off), pl.BlockSpec((tm, LANES), off), pl.BlockSpec((tm, LANES), off)],
        out_specs=[pl.BlockSpec((tm, n_da), row), pl.BlockSpec((tm, n_in - n_da), row)],
        out_shape=[jax.ShapeDtypeStruct((n, n_da), BF16), jax.ShapeDtypeStruct((n, n_in - n_da), F32)],
        compiler_params=_params("parallel"),
    )(x2, g, w_in_bf, rc, rs1, rs2)


def _attn_kernel(lam_ref, q_ref, k_ref, v_ref, sg_ref, o_ref, m_ref, acc_ref, *, tq, q_tile0):
    i = pl.program_id(1) + q_tile0
    q = q_ref[...]
    lane = lax.broadcasted_iota(jnp.int32, q.shape, 1)
    zero = jnp.zeros_like(q)
    qs = (jnp.where(lane < DA_HEAD_DIM, q, zero), jnp.where(lane >= DA_HEAD_DIM, q, zero))
    m_ref[...] = jnp.full(m_ref.shape, -jnp.inf, F32)
    acc_ref[...] = jnp.zeros(acc_ref.shape, F32)
    ones = jnp.ones((tq, LANES), BF16)

    def block(j, masked):
        kj = k_ref[pl.ds(pl.multiple_of(j * tq, tq), tq), :]
        vj = jnp.concatenate([v_ref[pl.ds(pl.multiple_of(j * tq, tq), tq), :], ones], axis=1)
        for c in range(2):
            s = lax.dot_general(qs[c], kj, NT, preferred_element_type=F32)
            if masked:
                r_id = lax.broadcasted_iota(jnp.int32, s.shape, 0)
                c_id = lax.broadcasted_iota(jnp.int32, s.shape, 1)
                s = jnp.where(c_id <= r_id, s, -jnp.inf)
            m_old = m_ref[c]
            m_new = jnp.maximum(m_old, jnp.max(s, axis=-1, keepdims=True))
            alpha = jnp.exp2(m_old - m_new)
            p = jnp.exp2(s - jnp.tile(m_new, (1, tq // LANES)))
            pv = jnp.dot(p.astype(BF16), vj, preferred_element_type=F32)
            acc_ref[c] = jnp.tile(alpha, (1, 2)) * acc_ref[c] + pv
            m_ref[c] = m_new

    def body(j, carry):
        block(j, False)
        return carry

    lax.fori_loop(0, i, body, 0)
    block(i, True)
    lam = lam_ref[0, 0]
    a0, a1 = acc_ref[0], acc_ref[1]
    o = a0[:, :LANES] / a0[:, LANES:] - lam * (a1[:, :LANES] / a1[:, LANES:])
    o = o * lax.rsqrt(jnp.mean(o * o, axis=-1, keepdims=True) + NORM_EPS) * sg_ref[...] * (1.0 - LAM_INIT)
    o_ref[...] = o.astype(o_ref.dtype)


def _attention(qkv, lam, subln_g, tq, q_rows):
    kv_rows = qkv.shape[0]
    nq = q_rows // tq
    q_tile0 = (kv_rows - q_rows) // tq
    h = DA_HEADS
    return pl.pallas_call(
        functools.partial(_attn_kernel, tq=tq, q_tile0=q_tile0),
        grid=(h, nq),
        in_specs=[pl.BlockSpec(memory_space=pltpu.SMEM),
                  pl.BlockSpec((tq, LANES), lambda hh, i: (q_tile0 + i, hh)),
                  pl.BlockSpec((kv_rows, LANES), lambda hh, i: (0, h + hh)),
                  pl.BlockSpec((kv_rows, LANES), lambda hh, i: (0, 2 * h + hh)),
                  pl.BlockSpec((1, LANES), lambda hh, i: (0, 0))],
        out_specs=pl.BlockSpec((tq, LANES), lambda hh, i: (i, hh)),
        out_shape=jax.ShapeDtypeStruct((q_rows, h * LANES), BF16),
        scratch_shapes=[pltpu.VMEM((2, tq, LANES), F32), pltpu.VMEM((2, tq, 2 * LANES), F32)],
        compiler_params=_params("parallel", "arbitrary"),
    )(lam, qkv, qkv, qkv, subln_g)


def _rwprep_kernel(z_ref, zp_ref, z0_ref, mu_ref, w0_ref, wup_ref, a0_ref, aup_ref, gup_ref, kk_ref, ka_ref, bd_ref,
                   r_o, ld_o, k_o, v_o, kk_o, b_o, g_o, *, width):
    i = pl.program_id(0)
    z = z_ref[...]
    first = jnp.where(i == 0, z0_ref[...], zp_ref[7:8, :])
    row = lax.broadcasted_iota(jnp.int32, z.shape, 0)
    prev = jnp.where(row == 0, first, pltpu.roll(z, 1, 0))
    zs = z + (prev - z) * mu_ref[...]
    r = zs[:, 0:width]
    k = zs[:, width:2 * width]
    v = zs[:, 2 * width:3 * width]
    xwa = zs[:, 3 * width:3 * width + LANES]
    xg = zs[:, 3 * width + LANES:3 * width + 2 * LANES]
    w = -jax.nn.softplus(-(w0_ref[...] + _mm_hi(jnp.tanh(xwa), wup_ref[...]))) - 0.5
    a = jax.nn.sigmoid(a0_ref[...] + _mm_hi(xwa, aup_ref[...]))
    g = _mm_hi(jax.nn.sigmoid(xg), gup_ref[...])
    kk = k * kk_ref[...]
    kk = kk / jnp.maximum(jnp.sqrt(_mm_hi(kk * kk, bd_ref[...])), 1e-12)
    r_o[...] = r
    ld_o[...] = -jnp.exp(w)
    k_o[...] = k * (1.0 + (a - 1.0) * ka_ref[...])
    v_o[...] = v
    kk_o[...] = kk
    b_o[...] = kk * a
    g_o[...] = g


def _rwprep(zrw, z_before, mu, w0, wup_pad, a0, aup_pad, gup, k_k, k_a, bd, tm):
    n, zin = zrw.shape
    width = w0.shape[1]
    row = lambda i: (i, 0)
    fix = lambda i: (0, 0)
    prev = lambda i: (jnp.maximum(i * (tm // 8) - 1, 0), 0)
    out = jax.ShapeDtypeStruct((n, width), F32)
    return pl.pallas_call(
        functools.partial(_rwprep_kernel, width=width),
        grid=(n // tm,),
        in_specs=[pl.BlockSpec((tm, zin), row), pl.BlockSpec((8, zin), prev), pl.BlockSpec((1, zin), fix),
                  pl.BlockSpec((1, zin), fix),
                  pl.BlockSpec((1, width), fix), pl.BlockSpec((LANES, width), fix),
                  pl.BlockSpec((1, width), fix), pl.BlockSpec((LANES, width), fix), pl.BlockSpec((LANES, width), fix),
                  pl.BlockSpec((1, width), fix), pl.BlockSpec((1, width), fix), pl.BlockSpec((width, width), fix)],
        out_specs=[pl.BlockSpec((tm, width), row)] * 7,
        out_shape=[out] * 7,
        compiler_params=_params("parallel"),
    )(zrw, zrw, z_before, mu, w0, wup_pad, a0, aup_pad, gup, k_k, k_a, bd)


def _rwcore_kernel(r_ref, ld_ref, k_ref, v_ref, kk_ref, b_ref, g_ref, lng_ref, lnb_ref, rk_ref, s0_ref,
                   o_ref, sout_ref, s_ref, *, units, groups):
    U, C, HD = RW_UNIT, RW_CHUNK, RW_HEAD

    @pl.when(pl.program_id(1) == 0)
    def _():
        s_ref[...] = s0_ref[0]

    ri = lax.broadcasted_iota(jnp.int32, (U, U), 0)
    ci = lax.broadcasted_iota(jnp.int32, (U, U), 1)
    same = (ri // C) == (ci // C)
    tri_s = same & (ci < ri)
    tri_i = same & (ci <= ri)
    eye = (ri == ci).astype(F32)
    cum_w = tri_i.astype(F32)
    head_avg = same.astype(F32) * (1.0 / HD)
    head_sum = same.astype(F32)
    hmask = (ci < HD, ci >= HD)
    cmask = (ri < C, ri >= C)
    zero = jnp.zeros((U, U), F32)
    cells = [(u, q) for u in range(units) for q in range(groups)]

    def blk(ref, cell):
        u, q = cell
        return ref[pl.ds(u * U, U), q * LANES:(q + 1) * LANES]

    def vec(ref, q):
        return ref[:, q * LANES:(q + 1) * LANES]

    cell_v, cell_at, cell_bt, cell_rt, cell_bk, cell_gam = {}, {}, {}, {}, {}, {}
    for cell in cells:
        ld = blk(ld_ref, cell)
        cum = _mm_split(cum_w, ld, NN, 1, 3)
        gam = jnp.exp(cum)
        ginv = jnp.exp(-cum)
        bt = blk(b_ref, cell) * ginv
        cell_v[cell] = blk(v_ref, cell)
        cell_at[cell] = -blk(kk_ref, cell) * jnp.exp(cum - ld)
        cell_bt[cell] = bt
        cell_rt[cell] = blk(r_ref, cell) * gam
        cell_bk[cell] = jnp.concatenate([bt, blk(k_ref, cell) * ginv], axis=0)
        cell_gam[cell] = gam

    chains = [(cell, h) for cell in cells for h in range(2)]
    mab, mak, mrb, mrk = {}, {}, {}, {}
    for ch in chains:
        cell, h = ch
        ar = jnp.concatenate([jnp.where(hmask[h], cell_at[cell], zero), jnp.where(hmask[h], cell_rt[cell], zero)], axis=0)
        m = _mm_nt(ar, cell_bk[cell])
        mab[ch] = jnp.where(tri_s, m[0:U, 0:U], zero)
        mak[ch] = jnp.where(tri_s, m[0:U, U:2 * U], zero)
        mrb[ch] = jnp.where(tri_i, m[U:2 * U, 0:U], zero)
        mrk[ch] = jnp.where(tri_i, m[U:2 * U, U:2 * U], zero)

    tm = {ch: eye + mab[ch] for ch in chains}
    pw = dict(mab)
    for _ in range(int(math.log2(C)) - 1):
        pw = {ch: _mm(pw[ch], pw[ch]) for ch in chains}
        tm = {ch: tm[ch] + _mm(tm[ch], pw[ch]) for ch in chains}
    aph = {ch: _mm(tm[ch], cell_at[ch[0]]) for ch in chains}
    mv = {ch: _mm(mak[ch], cell_v[ch[0]]) for ch in chains}
    uh = {ch: _mm(tm[ch], mv[ch]) for ch in chains}
    rph = {ch: _mm(mrb[ch], aph[ch]) for ch in chains}
    yph = {ch: _mm(mrb[ch], uh[ch]) + _mm(mrk[ch], cell_v[ch[0]]) for ch in chains}

    def both_heads(d, cell):
        return jnp.where(hmask[0], d[(cell, 0)], d[(cell, 1)])

    rp, yp, gs, hs = {}, {}, {}, {}
    for cell in cells:
        ap, uu = both_heads(aph, cell), both_heads(uh, cell)
        rp[cell] = both_heads(rph, cell) + cell_rt[cell]
        yp[cell] = both_heads(yph, cell)
        for c in range(2):
            gl = cell_gam[cell][(c + 1) * C - 1:(c + 1) * C, :]
            apc = jnp.where(cmask[c], ap, zero)
            uvc = jnp.concatenate([jnp.where(cmask[c], uu, zero), jnp.where(cmask[c], cell_v[cell], zero)], axis=0)
            gs[cell, c] = jnp.where(same, eye + _mm_tn(apc, cell_bt[cell]), zero) * gl
            hs[cell, c] = jnp.where(same, _mm_tn(uvc, cell_bk[cell]), zero) * gl

    s = [s_ref[q] for q in range(groups)]
    ys = {}
    for u in range(units):
        for q in range(groups):
            cell = (u, q)
            y0 = _mm_split(rp[cell], s[q], NT, 2, 2)
            s[q] = _mm_split(s[q], gs[cell, 0], NN, 2, 2) + hs[cell, 0]
            y1 = _mm_split(rp[cell], s[q], NT, 2, 2)
            s[q] = _mm_split(s[q], gs[cell, 1], NN, 2, 2) + hs[cell, 1]
            ys[cell] = jnp.where(cmask[0], y0, y1) + yp[cell]
    for q in range(groups):
        s_ref[q] = s[q]
        sout_ref[0, q] = s[q]

    for cell in cells:
        u, q = cell
        y = ys[cell]
        mean = _mm_split(y, head_avg, NN, 2, 1)
        yc = y - mean
        var = _mm_split(yc * yc, head_avg, NN, 2, 1)
        yn = yc * lax.rsqrt(var + RW_GN_EPS) * vec(lng_ref, q) + vec(lnb_ref, q)
        bonus = _mm_split(blk(r_ref, cell) * blk(k_ref, cell) * vec(rk_ref, q), head_sum, NN, 2, 1)
        yn = yn + bonus * cell_v[cell]
        o_ref[pl.ds(u * U, U), q * LANES:(q + 1) * LANES] = (yn * blk(g_ref, cell)).astype(o_ref.dtype)


def _rwcore(r, ld, k, v, kk, b, g, ln_g, ln_b, r_k, state, units, groups):
    n, width = r.shape
    rows = units * RW_UNIT
    lanes = groups * LANES
    blk = pl.BlockSpec((rows, lanes), lambda hp, i: (i, hp))
    vec = pl.BlockSpec((1, lanes), lambda hp, i: (0, hp))
    st = pl.BlockSpec((1, groups, RW_UNIT, RW_UNIT), lambda hp, i: (hp, 0, 0, 0))
    return pl.pallas_call(
        functools.partial(_rwcore_kernel, units=units, groups=groups),
        grid=(width // lanes, n // rows),
        in_specs=[blk] * 7 + [vec] * 3 + [st],
        out_specs=[blk, st],
        out_shape=[jax.ShapeDtypeStruct((n, width), BF16), jax.ShapeDtypeStruct(state.shape, F32)],
        scratch_shapes=[pltpu.VMEM((groups, RW_UNIT, RW_UNIT), F32)],
        compiler_params=_params("parallel", "arbitrary"),
    )(r, ld, k, v, kk, b, g, ln_g, ln_b, r_k, state)


def _outproj_kernel(x_ref, oda_ref, orw_ref, wa_ref, wb_ref, g_ref, h_ref, u_ref):
    h = (x_ref[...] + jnp.dot(oda_ref[...], wa_ref[...], preferred_element_type=F32)
         + jnp.dot(orw_ref[...], wb_ref[...], preferred_element_type=F32))
    h_ref[...] = h
    u_ref[...] = _rms(h, g_ref[...]).astype(BF16)


def _outproj(x2, o_da, o_rw, wa, wb, g, tm, row0):
    d = x2.shape[1]
    n, da = o_da.shape
    rw = o_rw.shape[1]
    row = lambda i: (i, 0)
    off = lambda i: (row0 // tm + i, 0)
    fix = lambda i: (0, 0)
    return pl.pallas_call(
        _outproj_kernel,
        grid=(n // tm,),
        in_specs=[pl.BlockSpec((tm, d), off), pl.BlockSpec((tm, da), row), pl.BlockSpec((tm, rw), row),
                  pl.BlockSpec((da, d), fix), pl.BlockSpec((rw, d), fix), pl.BlockSpec((1, d), fix)],
        out_specs=[pl.BlockSpec((tm, d), row), pl.BlockSpec((tm, d), row)],
        out_shape=[jax.ShapeDtypeStruct((n, d), F32), jax.ShapeDtypeStruct((n, d), BF16)],
        compiler_params=_params("parallel"),
    )(x2, o_da, o_rw, wa, wb, g)


def _topk_rows(s, k, payload=None):
    rows = s.shape[0]
    iota = lax.broadcasted_iota(jnp.int32, s.shape, 0).astype(F32)
    vals, sel = [], []
    for _ in range(k):
        m = jnp.max(s, axis=0, keepdims=True)
        am = jnp.min(jnp.where(s == m, iota, float(rows)), axis=0, keepdims=True)
        hit = iota == am
        vals.append(m)
        sel.append(am if payload is None else jnp.sum(jnp.where(hit, payload, 0.0), axis=0, keepdims=True))
        s = jnp.where(hit, -jnp.inf, s)
    return vals, sel


def _stack_rows(rows_list):
    k = len(rows_list)
    iota = lax.broadcasted_iota(jnp.int32, (k, rows_list[0].shape[1]), 0)
    out = jnp.zeros(iota.shape, rows_list[0].dtype)
    for j, r in enumerate(rows_list):
        out = jnp.where(iota == j, r, out)
    return out


def _peertopk_kernel(u_ref, wq_ref, keys_ref, idx_ref, gate_ref):
    u = u_ref[...]
    half = N_KEYS
    idx_rows, gate_rows = [], []
    for h in range(PEER_HEADS):
        tops = []
        for p in range(2):
            hp = h * 2 + p
            q_t = lax.dot_general(wq_ref[:, hp * half:(hp + 1) * half], u, (((0,), (1,)), ((), ())),
                                  preferred_element_type=F32)
            s_t = jnp.dot(keys_ref[hp], q_t.astype(BF16), preferred_element_type=F32)
            tops.append(_topk_rows(s_t, PEER_TOPK))
        (v1, i1), (v2, i2) = tops
        pairs = [(i, j) for i in range(PEER_TOPK) for j in range(PEER_TOPK) if (i + 1) * (j + 1) <= PEER_TOPK]
        pad = -len(pairs) % 8
        cand = _stack_rows([v1[i] + v2[j] for i, j in pairs] + [jnp.full_like(v1[0], -jnp.inf)] * pad)
        cidx = _stack_rows([i1[i] * float(N_KEYS) + i2[j] for i, j in pairs] + [jnp.zeros_like(i1[0])] * pad)
        best, idx = _topk_rows(cand, PEER_TOPK, payload=cidx)
        e = [jnp.exp(b - best[0]) for b in best]
        den = functools.reduce(lambda a, b: a + b, e)
        idx_rows.append(_stack_rows(idx))
        gate_rows.append(_stack_rows([x / den for x in e]))
    idx_ref[...] = jnp.concatenate(idx_rows, axis=0).T.astype(jnp.int32)
    gate_ref[...] = jnp.concatenate(gate_rows, axis=0).T


def _peertopk(u_bf, wq, keys, tt):
    n, d = u_bf.shape
    hk = PEER_HEADS * PEER_TOPK
    row = lambda i: (i, 0)
    return pl.pallas_call(
        _peertopk_kernel,
        grid=(n // tt,),
        in_specs=[pl.BlockSpec((tt, d), row), pl.BlockSpec(wq.shape, lambda i: (0, 0)),
                  pl.BlockSpec(keys.shape, lambda i: (0, 0, 0))],
        out_specs=[pl.BlockSpec((tt, hk), row), pl.BlockSpec((tt, hk), row)],
        out_shape=[jax.ShapeDtypeStruct((n, hk), jnp.int32), jax.ShapeDtypeStruct((n, hk), F32)],
        compiler_params=_params("parallel"),
    )(u_bf, wq, keys)


SC_CORES = 2
SC_SUBCORES = 16
SC_LANES = 16
HK = PEER_HEADS * PEER_TOPK
DOT_ROWS = 16
SUM_ROWS = 32
SUM_TOKENS = 8
SUM_UNROLL = 2
SUM_BF16_ROWS = 4


def _tree_sum(xs):
    while len(xs) > 1:
        xs = [xs[i] + xs[i + 1] for i in range(0, len(xs) - 1, 2)] + ([xs[-1]] if len(xs) % 2 else [])
    return xs[0]


def _sc_mesh():
    return plsc.VectorSubcoreMesh(core_axis_name="core", subcore_axis_name="subcore")


def _sc_worker():
    return lax.axis_index("core") * SC_SUBCORES + lax.axis_index("subcore")


def _row_dots(tab, idx_flat, xw):
    p = idx_flat.shape[0]
    w = tab.shape[1]
    t_total = p // HK
    workers = SC_CORES * SC_SUBCORES
    tpw = t_total // workers
    g = min(SUM_TOKENS, tpw)
    r = DOT_ROWS
    ns = HK // r
    ln = SC_LANES
    per_row = LANES // ln
    out_rows = HK // per_row
    assert t_total % workers == 0 and tpw % g == 0 and g % 2 == 0 and w % (2 * ln) == 0 and r % per_row == 0
    buf = pltpu.VMEM((r, w), tab.dtype)
    res = pltpu.VMEM((out_rows, LANES), F32)
    sem = pltpu.SemaphoreType.DMA

    @pl.kernel(out_type=jax.ShapeDtypeStruct((t_total * out_rows, LANES), F32), mesh=_sc_mesh(),
               scratch_types=[pltpu.VMEM((g * HK,), jnp.int32), pltpu.VMEM((g, w), tab.dtype), res, res]
               + [buf] * ns + [sem] * (ns + 2),
               compiler_params=pltpu.CompilerParams(needs_layout_passes=False))
    def dots(t_hbm, i_hbm, x_hbm, o_hbm, idx_v, x_v, res0, res1, *scratch):
        bufs, gsem, osem = scratch[0:ns], scratch[ns:2 * ns], scratch[2 * ns:2 * ns + 2]
        ress = (res0, res1)
        wid = _sc_worker()
        zero = jnp.zeros((ln,), F32)

        @pl.loop(0, tpw // g)
        def _(win):
            tok0 = wid * tpw + win * g
            pltpu.sync_copy(i_hbm.at[pl.ds(tok0 * HK, g * HK)], idx_v)
            pltpu.sync_copy(x_hbm.at[pl.ds(tok0, g)], x_v)

            def gather_of(tl, s):
                return pltpu.make_async_copy(t_hbm.at[idx_v.at[pl.ds(tl * HK + s * r, r)]], bufs[s], gsem[s])

            def out_of(tl, par):
                return pltpu.make_async_copy(ress[par], o_hbm.at[pl.ds((tok0 + tl) * out_rows, out_rows)], osem[par])

            for s in range(ns):
                gather_of(0, s).start()

            @pl.loop(0, g // 2)
            def _(tp):
                for par in range(2):
                    tl = tp * 2 + par
                    ob = ress[par]

                    @pl.when(tl >= 2)
                    def _():
                        out_of(tl - 2, par).wait()

                    for s in range(ns):
                        gather_of(tl, s).wait()
                        rows = bufs[s]

                        def fold(c, accs):
                            col = pl.multiple_of(c * 2 * ln, 2 * ln)
                            xa = plsc.bitcast(x_v[tl, pl.ds(col, ln)], BF16)
                            xb = plsc.bitcast(x_v[tl, pl.ds(col + ln, ln)], BF16)
                            out = []
                            for q in range(r):
                                pa = plsc.bitcast(rows[q, pl.ds(col, ln)], BF16) * xa
                                pb = plsc.bitcast(rows[q, pl.ds(col + ln, ln)], BF16) * xb
                                pair = plsc.bitcast(pa + pb, jnp.uint32)
                                lo = plsc.bitcast(lax.shift_left(pair, HALF_WORD), F32)
                                hi = plsc.bitcast(pair & HIGH_HALF, F32)
                                out.append(accs[q] + lo + hi)
                            return tuple(out)

                        accs = lax.fori_loop(0, w // (2 * ln), fold, (zero,) * r)
                        for q in range(r):
                            k = s * r + q
                            ob[k // per_row, pl.ds((k % per_row) * ln, ln)] = accs[q]

                        @pl.when(tl + 1 < g)
                        def _():
                            gather_of(tl + 1, s).start()

                    out_of(tl, par).start()

            for par in range(2):
                out_of(g - 2 + par, par).wait()

    return dots(tab, idx_flat, xw)


def _weighted_row_sum(tab, idx_flat, wrep):
    p = idx_flat.shape[0]
    w = tab.shape[1]
    t_total = p // HK
    workers = SC_CORES * SC_SUBCORES
    tpw = t_total // workers
    g = min(SUM_TOKENS, tpw)
    r = SUM_ROWS
    ns = HK // r
    ln = SC_LANES
    per_row = LANES // ln
    assert t_total % workers == 0 and tpw % g == 0 and g % 2 == 0 and w % ln == 0
    buf = pltpu.VMEM((r, w), tab.dtype)
    acc = pltpu.VMEM((2 * w,), F32)
    sem = pltpu.SemaphoreType.DMA

    @pl.kernel(out_type=jax.ShapeDtypeStruct((t_total, 2 * w), F32), mesh=_sc_mesh(),
               scratch_types=[pltpu.VMEM((g * HK,), jnp.int32), pltpu.VMEM((g * HK // per_row, LANES), jnp.uint32), acc, acc]
               + [buf] * ns + [sem] * (ns + 2),
               compiler_params=pltpu.CompilerParams(needs_layout_passes=False))
    def wsum(t_hbm, i_hbm, w_hbm, o_hbm, idx_v, w_v, acc0, acc1, *scratch):
        bufs, gsem, osem = scratch[0:ns], scratch[ns:2 * ns], scratch[2 * ns:2 * ns + 2]
        accs = (acc0, acc1)
        wid = _sc_worker()

        @pl.loop(0, tpw // g)
        def _(win):
            tok0 = wid * tpw + win * g
            pltpu.sync_copy(i_hbm.at[pl.ds(tok0 * HK, g * HK)], idx_v)
            pltpu.sync_copy(w_hbm.at[pl.ds(tok0 * (HK // per_row), g * HK // per_row)], w_v)

            def gather_of(tl, s):
                return pltpu.make_async_copy(t_hbm.at[idx_v.at[pl.ds(tl * HK + s * r, r)]], bufs[s], gsem[s])

            def out_of(tl, par):
                return pltpu.make_async_copy(accs[par], o_hbm.at[tok0 + tl], osem[par])

            for s in range(ns):
                gather_of(0, s).start()

            @pl.loop(0, g // 2)
            def _(tp):
                for par in range(2):
                    tl = tp * 2 + par
                    ob = accs[par]

                    @pl.when(tl >= 2)
                    def _():
                        out_of(tl - 2, par).wait()

                    for s in range(ns):
                        gather_of(tl, s).wait()
                        wrow = tl * (HK // per_row) + s * (r // per_row)
                        wk = [plsc.bitcast(w_v[wrow + q // per_row, pl.ds((q % per_row) * ln, ln)], BF16)
                              for q in range(r)]
                        rows = bufs[s]

                        def fold(c, carry, first=(s == 0)):
                            for half in range(SUM_UNROLL):
                                col = pl.multiple_of((c * SUM_UNROLL + half) * ln, ln)
                                los, his = [], []
                                for q in range(0, r, SUM_BF16_ROWS):
                                    prods = [plsc.bitcast(rows[q + i, pl.ds(col, ln)], BF16) * wk[q + i]
                                             for i in range(SUM_BF16_ROWS)]
                                    pair = plsc.bitcast(_tree_sum(prods), jnp.uint32)
                                    los.append(plsc.bitcast(lax.shift_left(pair, HALF_WORD), F32))
                                    his.append(plsc.bitcast(pair & HIGH_HALF, F32))
                                if first:
                                    ob[pl.ds(col, ln)] = _tree_sum(los)
                                    ob[pl.ds(w + col, ln)] = _tree_sum(his)
                                else:
                                    plsc.addupdate(ob.at[pl.ds(col, ln)], _tree_sum(los))
                                    plsc.addupdate(ob.at[pl.ds(w + col, ln)], _tree_sum(his))
                            return carry

                        lax.fori_loop(0, w // (ln * SUM_UNROLL), fold, 0)

                        @pl.when(tl + 1 < g)
                        def _():
                            gather_of(tl + 1, s).start()

                    out_of(tl, par).start()

            for par in range(2):
                out_of(g - 2 + par, par).wait()

    return wsum(tab, idx_flat, wrep)


def _peerw_kernel(part_ref, gate_ref, fold_ref, rep_ref, o_ref):
    hid = _mm_split(part_ref[...], fold_ref[...], NN, 3, 1)
    w = gate_ref[...] * (0.5 * hid * (1.0 + lax.erf(hid * (2.0 ** -0.5))))
    rep = jnp.dot(w.astype(BF16), rep_ref[...], preferred_element_type=F32)
    bits = pltpu.bitcast(rep, jnp.uint32)
    o_ref[...] = bits | lax.shift_right_logical(bits, HALF_WORD)


def _peerw(part, gate, tt):
    n = gate.shape[0]
    wide = HK * SC_LANES
    lane = jnp.arange(wide)
    fold = (lane[:, None] // SC_LANES == jnp.arange(HK)[None, :]).astype(BF16)
    row = lambda i: (i, 0)
    fix = lambda i: (0, 0)
    return pl.pallas_call(
        _peerw_kernel,
        grid=(n // tt,),
        in_specs=[pl.BlockSpec((tt, wide), row), pl.BlockSpec((tt, HK), row), pl.BlockSpec((wide, HK), fix),
                  pl.BlockSpec((HK, wide), fix)],
        out_specs=pl.BlockSpec((tt, wide), row),
        out_shape=jax.ShapeDtypeStruct((n, wide), jnp.uint32),
        compiler_params=_params("parallel"),
    )(part.reshape(n, wide), gate, fold, fold.T)


def _ple_kernel(h_ref, f_ref, p_ref, g_ref, wg_ref, wp_ref, gf_ref, o_ref):
    h = h_ref[...] + f_ref[...]
    gate = jax.nn.sigmoid(jnp.dot(_rms(h, g_ref[...]).astype(BF16), wg_ref[...], preferred_element_type=F32))
    pp = jnp.dot(p_ref[...].astype(BF16), wp_ref[...], preferred_element_type=F32)
    o_ref[...] = _rms(h + gate * pp, gf_ref[...])


def _ple(h1, ffn, p2, g, wg, wp, gf, tm, row0):
    n, d = h1.shape
    pd = p2.shape[1]
    row = lambda i: (i, 0)
    off = lambda i: (row0 // tm + i, 0)
    fix = lambda i: (0, 0)
    return pl.pallas_call(
        _ple_kernel,
        grid=(n // tm,),
        in_specs=[pl.BlockSpec((tm, d), row), pl.BlockSpec((tm, d), row), pl.BlockSpec((tm, pd), off),
                  pl.BlockSpec((1, d), fix), pl.BlockSpec((d, d), fix), pl.BlockSpec((pd, d), fix),
                  pl.BlockSpec((1, d), fix)],
        out_specs=pl.BlockSpec((tm, d), row),
        out_shape=jax.ShapeDtypeStruct((n, d), F32),
        compiler_params=_params("parallel"),
    )(h1, ffn, p2, g, wg, wp, gf)


def _rope_tables(positions):
    half = ROT_DIM // 2
    inv_freq = ROPE_THETA ** (-jnp.arange(half, dtype=F32) * 2.0 / ROT_DIM)
    ang = positions.astype(F32).reshape(-1, 1) * inv_freq
    d = jnp.arange(LANES) % DA_HEAD_DIM
    cos = jnp.tile(jnp.cos(ang), (1, LANES // half))
    sin = jnp.tile(jnp.sin(ang), (1, LANES // half))
    c = jnp.where(d < ROT_DIM, cos, 1.0)
    s1 = jnp.where(d < half, -sin, 0.0)
    s2 = jnp.where((d >= half) & (d < ROT_DIM), sin, 0.0)
    return c, s1, s2


def _pack_rows(tab):
    d = tab.shape[1]
    bits = lax.bitcast_convert_type(tab.astype(F32), jnp.uint32)
    rne = bits + jnp.uint32(0x7FFF) + ((bits >> HALF_WORD) & jnp.uint32(1))
    return (rne[:, :d // 2] >> HALF_WORD) | (rne[:, d // 2:] & HIGH_HALF)


def _block_diag_ones(width, head):
    i = jnp.arange(width)
    return (i[:, None] // head == i[None, :] // head).astype(F32)


def _tiles(seq):
    if seq % 8192 == 0:
        quarter = seq // 4
        first, rest = [quarter // 2, quarter // 2] + [quarter] * 3, [quarter] * 4
    else:
        first = rest = [seq]
    small = min(first)
    return dict(first=first, rest=rest, tm=min(256, small), tq=min(512, small), units=min(4, small // RW_UNIT), groups=2,
                tt_topk=min(256, small), tt_mix=min(256, small))


def kernel(x, p, positions, norm_mix_g, w_in, lam_q1, lam_k1, lam_q2, lam_k2, da_subln_g, rw_mu, rw_w0, rw_w_up, rw_a0, rw_a_up, rw_g_up, rw_k_k, rw_k_a, rw_r_k, rw_ln_g, rw_ln_b, w_out, norm_ffn_g, peer_w_q, peer_sub_keys, peer_u, peer_v, norm_ple_g, ple_gate_w, ple_proj_w, norm_final_g):
    batch, seq, d = x.shape
    t = _tiles(seq)
    row = lambda a: a.reshape(1, -1)
    f32 = F32

    w_in_bf = w_in[0].astype(BF16)
    lam = (jnp.exp(jnp.sum(lam_q1[0].astype(f32) * lam_k1[0].astype(f32)))
           - jnp.exp(jnp.sum(lam_q2[0].astype(f32) * lam_k2[0].astype(f32))) + LAM_INIT).reshape(1, 1)
    width = rw_w0.shape[1]
    wup_pad = jnp.concatenate([rw_w_up[0], jnp.zeros((LANES - rw_w_up.shape[1], width), f32)], axis=0)
    aup_pad = jnp.concatenate([jnp.zeros((LANES - rw_a_up.shape[1], width), f32), rw_a_up[0]], axis=0)
    head_ones = _block_diag_ones(width, RW_HEAD)
    w_out_bf = w_out[0].astype(BF16)
    da_w = DA_HEADS * 2 * DA_HEAD_DIM
    keys = peer_sub_keys[0].reshape(PEER_HEADS * 2, N_KEYS, -1).astype(BF16)
    wq = peer_w_q[0].astype(BF16)
    u_tab = _pack_rows(peer_u[0])
    v_tab = None
    wg_bf = ple_gate_w[0].astype(BF16)
    wp_bf = ple_proj_w[0].astype(BF16)

    rope = _rope_tables(positions)
    x2 = x.reshape(batch * seq, d)
    p2 = p[0].reshape(batch * seq, -1)
    pieces = []
    for b in range(batch):
        start = 0
        for rows in (t["first"] if b == 0 else t["rest"]):
            pieces.append((b, start, rows))
            start += rows
    row0_of = lambda piece: piece[0] * seq + piece[1]
    gain = row(norm_mix_g[0])
    tie = lax.optimization_barrier

    def mixers(piece, gain_in, carry):
        row0, rows = row0_of(piece), piece[2]
        qkv, zrw = _inproj(x2, gain_in, w_in_bf, *rope, t["tm"], row0, rows)
        if carry is None:
            carry = dict(qkv=qkv[:0], z_last=jnp.zeros((1, zrw.shape[1]), f32),
                         state=jnp.zeros((width // (t["groups"] * LANES), t["groups"], RW_UNIT, RW_UNIT), f32))
        qkv = jnp.concatenate([carry["qkv"], qkv], axis=0)
        o_da = _attention(qkv, lam, row(da_subln_g[0]), t["tq"], rows)
        rw = _rwprep(zrw, carry["z_last"], row(rw_mu[0]), row(rw_w0[0]), wup_pad, row(rw_a0[0]), aup_pad, rw_g_up[0],
                     row(rw_k_k[0]), row(rw_k_a[0]), head_ones, t["tm"])
        o_rw, state = _rwcore(*rw, row(rw_ln_g[0]), row(rw_ln_b[0]), row(rw_r_k[0]), carry["state"], t["units"], t["groups"])
        h1, u2 = _outproj(x2, o_da, o_rw, w_out_bf[:da_w], w_out_bf[da_w:], row(norm_ffn_g[0]), t["tm"], row0)
        idx, gate = _peertopk(u2, wq, keys, t["tt_topk"])
        idx = idx.reshape(-1)
        part = _row_dots(u_tab, idx, _pack_rows(u2))
        return dict(h1=h1, gate=gate, idx=idx, part=part), dict(qkv=qkv, z_last=zrw[-1:], state=state)

    def out(s, ffn, piece):
        return _ple(s["h1"], ffn, p2, row(norm_ple_g[0]), wg_bf, wp_bf, row(norm_final_g), t["tm"], row0_of(piece))

    outs = []
    prev, older, older_ffn, carry = None, None, None, None
    gain_in = gain
    for j in range(len(pieces) + 1):
        cur = None
        if j < len(pieces):
            cur, carry = mixers(pieces[j], gain_in, carry if pieces[j][1] > 0 else None)
        if prev is not None:
            tied = [prev["part"]] + ([cur["idx"]] if cur is not None else []) + ([older_ffn] if older is not None else [])
            tied = tie(tuple(tied))
            if older is not None:
                outs.append(out(older, tied[-1], pieces[j - 2]))
            if v_tab is None:
                v_tab = _pack_rows(tie((peer_v[0], (cur or prev)["idx"]))[0])
            wrep = _peerw(tied[0], prev["gate"], t["tt_mix"])
            if j + 1 < len(pieces):
                wrep, gain_in = tie((wrep, gain))
            older, older_ffn = prev, _weighted_row_sum(v_tab, prev["idx"], wrep.reshape(-1, LANES))
        prev = cur
    outs.append(out(older, older_ffn, pieces[-1]))
    return jnp.concatenate(outs, axis=0).reshape(batch, seq, d)
```

```python
import functools
import math

import jax
import jax.numpy as jnp
import numpy as np
from jax import lax
from jax.experimental import pallas as pl
from jax.experimental.pallas import tpu as pltpu
from jax.experimental.pallas import tpu_sc as plsc

F32 = jnp.float32
BF16 = jnp.bfloat16

NORM_EPS = 1e-6
DA_HEADS = 4
DA_HEAD_DIM = 64
ROPE_THETA = 500000.0
ROT_DIM = DA_HEAD_DIM // 4
RW_HEAD = 64
RW_GN_EPS = 64e-5
PEER_HEADS = 8
N_KEYS = 128
PEER_TOPK = 16
LAM_INIT = 0.8 - 0.6 * math.exp(-0.3 * 0)

LANES = 128
VMEM_LIMIT = 56 * 1024 * 1024
RW_CHUNK = 64
RW_UNIT = 2 * RW_CHUNK

NN = (((1,), (0,)), ((), ()))
NT = (((1,), (1,)), ((), ()))
TN = (((0,), (0,)), ((), ()))
HI = lax.Precision.HIGHEST

HALF_WORD = np.uint32(16)
HIGH_HALF = np.uint32(0xFFFF0000)


def _mm(a, b):
    return jnp.dot(a.astype(BF16), b.astype(BF16), preferred_element_type=F32)


def _mm_nt(a, b):
    return lax.dot_general(a.astype(BF16), b.astype(BF16), NT, preferred_element_type=F32)


def _mm_tn(a, b):
    return lax.dot_general(a.astype(BF16), b.astype(BF16), TN, preferred_element_type=F32)


def _mm_hi(a, b):
    return jnp.dot(a, b, precision=HI, preferred_element_type=F32)


def _pieces(a, n):
    out = []
    for _ in range(n):
        piece = a.astype(BF16)
        out.append(piece)
        a = a - piece.astype(F32)
    return out


def _mm_split(a, b, dims, a_pieces, b_pieces):
    ap, bp = _pieces(a, a_pieces), _pieces(b, b_pieces)
    terms = [lax.dot_general(x, y, dims, preferred_element_type=F32)
             for i, x in enumerate(ap) for j, y in enumerate(bp) if i + j < max(a_pieces, b_pieces)]
    return functools.reduce(lambda u, v: u + v, terms)


def _params(*sem):
    return pltpu.CompilerParams(dimension_semantics=sem, vmem_limit_bytes=VMEM_LIMIT)


def _rms(x, g):
    return x * lax.rsqrt(jnp.mean(x * x, axis=-1, keepdims=True) + NORM_EPS) * g


def _inproj_kernel(x_ref, g_ref, w_ref, c_ref, s1_ref, s2_ref, qkv_ref, zrw_ref, *, n_qk, n_da):
    u = _rms(x_ref[...], g_ref[...]).astype(BF16)
    z = jnp.dot(u, w_ref[...], preferred_element_type=F32)
    c, s1, s2 = c_ref[...], s1_ref[...], s2_ref[...]
    half = ROT_DIM // 2
    for blk in range(n_da // LANES):
        t = z[:, blk * LANES:(blk + 1) * LANES]
        if blk < 2 * n_qk // LANES:
            t = t * c + pltpu.roll(t, LANES - half, 1) * s1 + pltpu.roll(t, half, 1) * s2
        if blk < n_qk // LANES:
            t = t * (DA_HEAD_DIM ** -0.5 * math.log2(math.e))
        qkv_ref[:, blk * LANES:(blk + 1) * LANES] = t.astype(BF16)
    zrw_ref[...] = z[:, n_da:]


def _inproj(x2, g, w_in_bf, rc, rs1, rs2, tm, row0, n):
    d = x2.shape[1]
    n_in = w_in_bf.shape[1]
    n_qk = DA_HEADS * 2 * DA_HEAD_DIM
    n_da = 3 * n_qk
    row = lambda i: (i, 0)
    off = lambda i: (row0 // tm + i, 0)
    fix = lambda i: (0, 0)
    return pl.pallas_call(
        functools.partial(_inproj_kernel, n_qk=n_qk, n_da=n_da),
        grid=(n // tm,),
        in_specs=[pl.BlockSpec((tm, d), off), pl.BlockSpec((1, d), fix), pl.BlockSpec((d, n_in), fix),
                  pl.BlockSpec((tm, LANES), off), pl.BlockSpec((tm, LANES), off), pl.BlockSpec((tm, LANES), off)],
        out_specs=[pl.BlockSpec((tm, n_da), row), pl.BlockSpec((tm, n_in - n_da), row)],
        out_shape=[jax.ShapeDtypeStruct((n, n_da), BF16), jax.ShapeDtypeStruct((n, n_in - n_da), F32)],
        compiler_params=_params("parallel"),
    )(x2, g, w_in_bf, rc, rs1, rs2)


def _attn_kernel(lam_ref, q_ref, k_ref, v_ref, sg_ref, o_ref, m_ref, acc_ref, *, tq, q_tile0):
    i = pl.program_id(1) + q_tile0
    q = q_ref[...]
    lane = lax.broadcasted_iota(jnp.int32, q.shape, 1)
    zero = jnp.zeros_like(q)
    qs = (jnp.where(lane < DA_HEAD_DIM, q, zero), jnp.where(lane >= DA_HEAD_DIM, q, zero))
    m_ref[...] = jnp.full(m_ref.shape, -jnp.inf, F32)
    acc_ref[...] = jnp.zeros(acc_ref.shape, F32)
    ones = jnp.ones((tq, LANES), BF16)

    def block(j, masked):
        kj = k_ref[pl.ds(pl.multiple_of(j * tq, tq), tq), :]
        vj = jnp.concatenate([v_ref[pl.ds(pl.multiple_of(j * tq, tq), tq), :], ones], axis=1)
        for c in range(2):
            s = lax.dot_general(qs[c], kj, NT, preferred_element_type=F32)
            if masked:
                r_id = lax.broadcasted_iota(jnp.int32, s.shape, 0)
                c_id = lax.broadcasted_iota(jnp.int32, s.shape, 1)
                s = jnp.where(c_id <= r_id, s, -jnp.inf)
            m_old = m_ref[c]
            m_new = jnp.maximum(m_old, jnp.max(s, axis=-1, keepdims=True))
            alpha = jnp.exp2(m_old - m_new)
            p = jnp.exp2(s - jnp.tile(m_new, (1, tq // LANES)))
            pv = jnp.dot(p.astype(BF16), vj, preferred_element_type=F32)
            acc_ref[c] = jnp.tile(alpha, (1, 2)) * acc_ref[c] + pv
            m_ref[c] = m_new

    def body(j, carry):
        block(j, False)
        return carry

    lax.fori_loop(0, i, body, 0)
    block(i, True)
    lam = lam_ref[0, 0]
    a0, a1 = acc_ref[0], acc_ref[1]
    o = a0[:, :LANES] / a0[:, LANES:] - lam * (a1[:, :LANES] / a1[:, LANES:])
    o = o * lax.rsqrt(jnp.mean(o * o, axis=-1, keepdims=True) + NORM_EPS) * sg_ref[...] * (1.0 - LAM_INIT)
    o_ref[...] = o.astype(o_ref.dtype)


def _attention(qkv, lam, subln_g, tq, q_rows):
    kv_rows = qkv.shape[0]
    nq = q_rows // tq
    q_tile0 = (kv_rows - q_rows) // tq
    h = DA_HEADS
    return pl.pallas_call(
        functools.partial(_attn_kernel, tq=tq, q_tile0=q_tile0),
        grid=(h, nq),
        in_specs=[pl.BlockSpec(memory_space=pltpu.SMEM),
                  pl.BlockSpec((tq, LANES), lambda hh, i: (q_tile0 + i, hh)),
                  pl.BlockSpec((kv_rows, LANES), lambda hh, i: (0, h + hh)),
                  pl.BlockSpec((kv_rows, LANES), lambda hh, i: (0, 2 * h + hh)),
                  pl.BlockSpec((1, LANES), lambda hh, i: (0, 0))],
        out_specs=pl.BlockSpec((tq, LANES), lambda hh, i: (i, hh)),
        out_shape=jax.ShapeDtypeStruct((q_rows, h * LANES), BF16),
        scratch_shapes=[pltpu.VMEM((2, tq, LANES), F32), pltpu.VMEM((2, tq, 2 * LANES), F32)],
        compiler_params=_params("parallel", "arbitrary"),
    )(lam, qkv, qkv, qkv, subln_g)


def _rwprep_kernel(z_ref, zp_ref, z0_ref, mu_ref, w0_ref, wup_ref, a0_ref, aup_ref, gup_ref, kk_ref, ka_ref, bd_ref,
                   r_o, ld_o, k_o, v_o, kk_o, b_o, g_o, *, width):
    i = pl.program_id(0)
    z = z_ref[...]
    first = jnp.where(i == 0, z0_ref[...], zp_ref[7:8, :])
    row = lax.broadcasted_iota(jnp.int32, z.shape, 0)
    prev = jnp.where(row == 0, first, pltpu.roll(z, 1, 0))
    zs = z + (prev - z) * mu_ref[...]
    r = zs[:, 0:width]
    k = zs[:, width:2 * width]
    v = zs[:, 2 * width:3 * width]
    xwa = zs[:, 3 * width:3 * width + LANES]
    xg = zs[:, 3 * width + LANES:3 * width + 2 * LANES]
    w = -jax.nn.softplus(-(w0_ref[...] + _mm_hi(jnp.tanh(xwa), wup_ref[...]))) - 0.5
    a = jax.nn.sigmoid(a0_ref[...] + _mm_hi(xwa, aup_ref[...]))
    g = _mm_hi(jax.nn.sigmoid(xg), gup_ref[...])
    kk = k * kk_ref[...]
    kk = kk / jnp.maximum(jnp.sqrt(_mm_hi(kk * kk, bd_ref[...])), 1e-12)
    r_o[...] = r
    ld_o[...] = -jnp.exp(w)
    k_o[...] = k * (1.0 + (a - 1.0) * ka_ref[...])
    v_o[...] = v
    kk_o[...] = kk
    b_o[...] = kk * a
    g_o[...] = g


def _rwprep(zrw, z_before, mu, w0, wup_pad, a0, aup_pad, gup, k_k, k_a, bd, tm):
    n, zin = zrw.shape
    width = w0.shape[1]
    row = lambda i: (i, 0)
    fix = lambda i: (0, 0)
    prev = lambda i: (jnp.maximum(i * (tm // 8) - 1, 0), 0)
    out = jax.ShapeDtypeStruct((n, width), F32)
    return pl.pallas_call(
        functools.partial(_rwprep_kernel, width=width),
        grid=(n // tm,),
        in_specs=[pl.BlockSpec((tm, zin), row), pl.BlockSpec((8, zin), prev), pl.BlockSpec((1, zin), fix),
                  pl.BlockSpec((1, zin), fix),
                  pl.BlockSpec((1, width), fix), pl.BlockSpec((LANES, width), fix),
                  pl.BlockSpec((1, width), fix), pl.BlockSpec((LANES, width), fix), pl.BlockSpec((LANES, width), fix),
                  pl.BlockSpec((1, width), fix), pl.BlockSpec((1, width), fix), pl.BlockSpec((width, width), fix)],
        out_specs=[pl.BlockSpec((tm, width), row)] * 7,
        out_shape=[out] * 7,
        compiler_params=_params("parallel"),
    )(zrw, zrw, z_before, mu, w0, wup_pad, a0, aup_pad, gup, k_k, k_a, bd)


def _rwcore_kernel(r_ref, ld_ref, k_ref, v_ref, kk_ref, b_ref, g_ref, lng_ref, lnb_ref, rk_ref, s0_ref,
                   o_ref, sout_ref, s_ref, *, units, groups):
    U, C, HD = RW_UNIT, RW_CHUNK, RW_HEAD

    @pl.when(pl.program_id(1) == 0)
    def _():
        s_ref[...] = s0_ref[0]

    ri = lax.broadcasted_iota(jnp.int32, (U, U), 0)
    ci = lax.broadcasted_iota(jnp.int32, (U, U), 1)
    same = (ri // C) == (ci // C)
    tri_s = same & (ci < ri)
    tri_i = same & (ci <= ri)
    eye = (ri == ci).astype(F32)
    cum_w = tri_i.astype(F32)
    head_avg = same.astype(F32) * (1.0 / HD)
    head_sum = same.astype(F32)
    hmask = (ci < HD, ci >= HD)
    cmask = (ri < C, ri >= C)
    zero = jnp.zeros((U, U), F32)
    cells = [(u, q) for u in range(units) for q in range(groups)]

    def blk(ref, cell):
        u, q = cell
        return ref[pl.ds(u * U, U), q * LANES:(q + 1) * LANES]

    def vec(ref, q):
        return ref[:, q * LANES:(q + 1) * LANES]

    cell_v, cell_at, cell_bt, cell_rt, cell_bk, cell_gam = {}, {}, {}, {}, {}, {}
    for cell in cells:
        ld = blk(ld_ref, cell)
        cum = _mm_split(cum_w, ld, NN, 1, 3)
        gam = jnp.exp(cum)
        ginv = jnp.exp(-cum)
        bt = blk(b_ref, cell) * ginv
        cell_v[cell] = blk(v_ref, cell)
        cell_at[cell] = -blk(kk_ref, cell) * jnp.exp(cum - ld)
        cell_bt[cell] = bt
        cell_rt[cell] = blk(r_ref, cell) * gam
        cell_bk[cell] = jnp.concatenate([bt, blk(k_ref, cell) * ginv], axis=0)
        cell_gam[cell] = gam

    chains = [(cell, h) for cell in cells for h in range(2)]
    mab, mak, mrb, mrk = {}, {}, {}, {}
    for ch in chains:
        cell, h = ch
        ar = jnp.concatenate([jnp.where(hmask[h], cell_at[cell], zero), jnp.where(hmask[h], cell_rt[cell], zero)], axis=0)
        m = _mm_nt(ar, cell_bk[cell])
        mab[ch] = jnp.where(tri_s, m[0:U, 0:U], zero)
        mak[ch] = jnp.where(tri_s, m[0:U, U:2 * U], zero)
        mrb[ch] = jnp.where(tri_i, m[U:2 * U, 0:U], zero)
        mrk[ch] = jnp.where(tri_i, m[U:2 * U, U:2 * U], zero)

    tm = {ch: eye + mab[ch] for ch in chains}
    pw = dict(mab)
    for _ in range(int(math.log2(C)) - 1):
        pw = {ch: _mm(pw[ch], pw[ch]) for ch in chains}
        tm = {ch: tm[ch] + _mm(tm[ch], pw[ch]) for ch in chains}
    aph = {ch: _mm(tm[ch], cell_at[ch[0]]) for ch in chains}
    mv = {ch: _mm(mak[ch], cell_v[ch[0]]) for ch in chains}
    uh = {ch: _mm(tm[ch], mv[ch]) for ch in chains}
    rph = {ch: _mm(mrb[ch], aph[ch]) for ch in chains}
    yph = {ch: _mm(mrb[ch], uh[ch]) + _mm(mrk[ch], cell_v[ch[0]]) for ch in chains}

    def both_heads(d, cell):
        return jnp.where(hmask[0], d[(cell, 0)], d[(cell, 1)])

    rp, yp, gs, hs = {}, {}, {}, {}
    for cell in cells:
        ap, uu = both_heads(aph, cell), both_heads(uh, cell)
        rp[cell] = both_heads(rph, cell) + cell_rt[cell]
        yp[cell] = both_heads(yph, cell)
        for c in range(2):
            gl = cell_gam[cell][(c + 1) * C - 1:(c + 1) * C, :]
            apc = jnp.where(cmask[c], ap, zero)
            uvc = jnp.concatenate([jnp.where(cmask[c], uu, zero), jnp.where(cmask[c], cell_v[cell], zero)], axis=0)
            gs[cell, c] = jnp.where(same, eye + _mm_tn(apc, cell_bt[cell]), zero) * gl
            hs[cell, c] = jnp.where(same, _mm_tn(uvc, cell_bk[cell]), zero) * gl

    s = [s_ref[q] for q in range(groups)]
    ys = {}
    for u in range(units):
        for q in range(groups):
            cell = (u, q)
            y0 = _mm_split(rp[cell], s[q], NT, 2, 2)
            s[q] = _mm_split(s[q], gs[cell, 0], NN, 2, 2) + hs[cell, 0]
            y1 = _mm_split(rp[cell], s[q], NT, 2, 2)
            s[q] = _mm_split(s[q], gs[cell, 1], NN, 2, 2) + hs[cell, 1]
            ys[cell] = jnp.where(cmask[0], y0, y1) + yp[cell]
    for q in range(groups):
        s_ref[q] = s[q]
        sout_ref[0, q] = s[q]

    for cell in cells:
        u, q = cell
        y = ys[cell]
        mean = _mm_split(y, head_avg, NN, 2, 1)
        yc = y - mean
        var = _mm_split(yc * yc, head_avg, NN, 2, 1)
        yn = yc * lax.rsqrt(var + RW_GN_EPS) * vec(lng_ref, q) + vec(lnb_ref, q)
        bonus = _mm_split(blk(r_ref, cell) * blk(k_ref, cell) * vec(rk_ref, q), head_sum, NN, 2, 1)
        yn = yn + bonus * cell_v[cell]
        o_ref[pl.ds(u * U, U), q * LANES:(q + 1) * LANES] = (yn * blk(g_ref, cell)).astype(o_ref.dtype)


def _rwcore(r, ld, k, v, kk, b, g, ln_g, ln_b, r_k, state, units, groups):
    n, width = r.shape
    rows = units * RW_UNIT
    lanes = groups * LANES
    blk = pl.BlockSpec((rows, lanes), lambda hp, i: (i, hp))
    vec = pl.BlockSpec((1, lanes), lambda hp, i: (0, hp))
    st = pl.BlockSpec((1, groups, RW_UNIT, RW_UNIT), lambda hp, i: (hp, 0, 0, 0))
    return pl.pallas_call(
        functools.partial(_rwcore_kernel, units=units, groups=groups),
        grid=(width // lanes, n // rows),
        in_specs=[blk] * 7 + [vec] * 3 + [st],
        out_specs=[blk, st],
        out_shape=[jax.ShapeDtypeStruct((n, width), BF16), jax.ShapeDtypeStruct(state.shape, F32)],
        scratch_shapes=[pltpu.VMEM((groups, RW_UNIT, RW_UNIT), F32)],
        compiler_params=_params("parallel", "arbitrary"),
    )(r, ld, k, v, kk, b, g, ln_g, ln_b, r_k, state)


def _outproj_kernel(x_ref, oda_ref, orw_ref, wa_ref, wb_ref, g_ref, h_ref, u_ref):
    h = (x_ref[...] + jnp.dot(oda_ref[...], wa_ref[...], preferred_element_type=F32)
         + jnp.dot(orw_ref[...], wb_ref[...], preferred_element_type=F32))
    h_ref[...] = h
    u_ref[...] = _rms(h, g_ref[...]).astype(BF16)


def _outproj(x2, o_da, o_rw, wa, wb, g, tm, row0):
    d = x2.shape[1]
    n, da = o_da.shape
    rw = o_rw.shape[1]
    row = lambda i: (i, 0)
    off = lambda i: (row0 // tm + i, 0)
    fix = lambda i: (0, 0)
    return pl.pallas_call(
        _outproj_kernel,
        grid=(n // tm,),
        in_specs=[pl.BlockSpec((tm, d), off), pl.BlockSpec((tm, da), row), pl.BlockSpec((tm, rw), row),
                  pl.BlockSpec((da, d), fix), pl.BlockSpec((rw, d), fix), pl.BlockSpec((1, d), fix)],
        out_specs=[pl.BlockSpec((tm, d), row), pl.BlockSpec((tm, d), row)],
        out_shape=[jax.ShapeDtypeStruct((n, d), F32), jax.ShapeDtypeStruct((n, d), BF16)],
        compiler_params=_params("parallel"),
    )(x2, o_da, o_rw, wa, wb, g)


def _topk_rows(s, k, payload=None):
    rows = s.shape[0]
    iota = lax.broadcasted_iota(jnp.int32, s.shape, 0).astype(F32)
    vals, sel = [], []
    for _ in range(k):
        m = jnp.max(s, axis=0, keepdims=True)
        am = jnp.min(jnp.where(s == m, iota, float(rows)), axis=0, keepdims=True)
        hit = iota == am
        vals.append(m)
        sel.append(am if payload is None else jnp.sum(jnp.where(hit, payload, 0.0), axis=0, keepdims=True))
        s = jnp.where(hit, -jnp.inf, s)
    return vals, sel


def _stack_rows(rows_list):
    k = len(rows_list)
    iota = lax.broadcasted_iota(jnp.int32, (k, rows_list[0].shape[1]), 0)
    out = jnp.zeros(iota.shape, rows_list[0].dtype)
    for j, r in enumerate(rows_list):
        out = jnp.where(iota == j, r, out)
    return out


def _peertopk_kernel(u_ref, wq_ref, keys_ref, idx_ref, gate_ref):
    u = u_ref[...]
    half = N_KEYS
    idx_rows, gate_rows = [], []
    for h in range(PEER_HEADS):
        tops = []
        for p in range(2):
            hp = h * 2 + p
            q_t = lax.dot_general(wq_ref[:, hp * half:(hp + 1) * half], u, (((0,), (1,)), ((), ())),
                                  preferred_element_type=F32)
            s_t = jnp.dot(keys_ref[hp], q_t.astype(BF16), preferred_element_type=F32)
            tops.append(_topk_rows(s_t, PEER_TOPK))
        (v1, i1), (v2, i2) = tops
        pairs = [(i, j) for i in range(PEER_TOPK) for j in range(PEER_TOPK) if (i + 1) * (j + 1) <= PEER_TOPK]
        pad = -len(pairs) % 8
        cand = _stack_rows([v1[i] + v2[j] for i, j in pairs] + [jnp.full_like(v1[0], -jnp.inf)] * pad)
        cidx = _stack_rows([i1[i] * float(N_KEYS) + i2[j] for i, j in pairs] + [jnp.zeros_like(i1[0])] * pad)
        best, idx = _topk_rows(cand, PEER_TOPK, payload=cidx)
        e = [jnp.exp(b - best[0]) for b in best]
        den = functools.reduce(lambda a, b: a + b, e)
        idx_rows.append(_stack_rows(idx))
        gate_rows.append(_stack_rows([x / den for x in e]))
    idx_ref[...] = jnp.concatenate(idx_rows, axis=0).T.astype(jnp.int32)
    gate_ref[...] = jnp.concatenate(gate_rows, axis=0).T


def _peertopk(u_bf, wq, keys, tt):
    n, d = u_bf.shape
    hk = PEER_HEADS * PEER_TOPK
    row = lambda i: (i, 0)
    return pl.pallas_call(
        _peertopk_kernel,
        grid=(n // tt,),
        in_specs=[pl.BlockSpec((tt, d), row), pl.BlockSpec(wq.shape, lambda i: (0, 0)),
                  pl.BlockSpec(keys.shape, lambda i: (0, 0, 0))],
        out_specs=[pl.BlockSpec((tt, hk), row), pl.BlockSpec((tt, hk), row)],
        out_shape=[jax.ShapeDtypeStruct((n, hk), jnp.int32), jax.ShapeDtypeStruct((n, hk), F32)],
        compiler_params=_params("parallel"),
    )(u_bf, wq, keys)


SC_CORES = 2
SC_SUBCORES = 16
SC_LANES = 16
HK = PEER_HEADS * PEER_TOPK
DOT_ROWS = 16
SUM_ROWS = 32
DOT_TOKENS = 32
SUM_TOKENS = 16
SUM_UNROLL = 2
SUM_BF16_ROWS = 4


def _tree_sum(xs):
    while len(xs) > 1:
        xs = [xs[i] + xs[i + 1] for i in range(0, len(xs) - 1, 2)] + ([xs[-1]] if len(xs) % 2 else [])
    return xs[0]


def _sc_mesh():
    return plsc.VectorSubcoreMesh(core_axis_name="core", subcore_axis_name="subcore")


def _sc_worker():
    return lax.axis_index("core") * SC_SUBCORES + lax.axis_index("subcore")


def _row_dots(tab, idx_flat, xw):
    p = idx_flat.shape[0]
    w = tab.shape[1]
    t_total = p // HK
    workers = SC_CORES * SC_SUBCORES
    tpw = t_total // workers
    g = min(DOT_TOKENS, tpw)
    r = DOT_ROWS
    ns = HK // r
    ln = SC_LANES
    per_row = LANES // ln
    out_rows = HK // per_row
    assert t_total % workers == 0 and tpw % g == 0 and g % 2 == 0 and w % (2 * ln) == 0 and r % per_row == 0
    buf = pltpu.VMEM((r, w), tab.dtype)
    res = pltpu.VMEM((out_rows, LANES), F32)
    sem = pltpu.SemaphoreType.DMA

    @pl.kernel(out_type=jax.ShapeDtypeStruct((t_total * out_rows, LANES), F32), mesh=_sc_mesh(),
               scratch_types=[pltpu.VMEM((g * HK,), jnp.int32), pltpu.VMEM((g, w), tab.dtype), res, res]
               + [buf] * ns + [sem] * (ns + 2),
               compiler_params=pltpu.CompilerParams(needs_layout_passes=False))
    def dots(t_hbm, i_hbm, x_hbm, o_hbm, idx_v, x_v, res0, res1, *scratch):
        bufs, gsem, osem = scratch[0:ns], scratch[ns:2 * ns], scratch[2 * ns:2 * ns + 2]
        ress = (res0, res1)
        wid = _sc_worker()
        zero = jnp.zeros((ln,), F32)

        @pl.loop(0, tpw // g)
        def _(win):
            tok0 = wid * tpw + win * g
            pltpu.sync_copy(i_hbm.at[pl.ds(tok0 * HK, g * HK)], idx_v)
            pltpu.sync_copy(x_hbm.at[pl.ds(tok0, g)], x_v)

            def gather_of(tl, s):
                return pltpu.make_async_copy(t_hbm.at[idx_v.at[pl.ds(tl * HK + s * r, r)]], bufs[s], gsem[s])

            def out_of(tl, par):
                return pltpu.make_async_copy(ress[par], o_hbm.at[pl.ds((tok0 + tl) * out_rows, out_rows)], osem[par])

            for s in range(ns):
                gather_of(0, s).start()

            @pl.loop(0, g // 2)
            def _(tp):
                for par in range(2):
                    tl = tp * 2 + par
                    ob = ress[par]

                    @pl.when(tl >= 2)
                    def _():
                        out_of(tl - 2, par).wait()

                    for s in range(ns):
                        gather_of(tl, s).wait()
                        rows = bufs[s]

                        def fold(c, accs):
                            col = pl.multiple_of(c * 2 * ln, 2 * ln)
                            xa = plsc.bitcast(x_v[tl, pl.ds(col, ln)], BF16)
                            xb = plsc.bitcast(x_v[tl, pl.ds(col + ln, ln)], BF16)
                            out = []
                            for q in range(r):
                                pa = plsc.bitcast(rows[q, pl.ds(col, ln)], BF16) * xa
                                pb = plsc.bitcast(rows[q, pl.ds(col + ln, ln)], BF16) * xb
                                pair = plsc.bitcast(pa + pb, jnp.uint32)
                                lo = plsc.bitcast(lax.shift_left(pair, HALF_WORD), F32)
                                hi = plsc.bitcast(pair & HIGH_HALF, F32)
                                out.append(accs[q] + lo + hi)
                            return tuple(out)

                        accs = lax.fori_loop(0, w // (2 * ln), fold, (zero,) * r)
                        for q in range(r):
                            k = s * r + q
                            ob[k // per_row, pl.ds((k % per_row) * ln, ln)] = accs[q]

                        @pl.when(tl + 1 < g)
                        def _():
                            gather_of(tl + 1, s).start()

                    out_of(tl, par).start()

            for par in range(2):
                out_of(g - 2 + par, par).wait()

    return dots(tab, idx_flat, xw)


def _weighted_row_sum(tab, idx_flat, wrep):
    p = idx_flat.shape[0]
    w = tab.shape[1]
    t_total = p // HK
    workers = SC_CORES * SC_SUBCORES
    tpw = t_total // workers
    g = min(SUM_TOKENS, tpw)
    r = SUM_ROWS
    ns = HK // r
    ln = SC_LANES
    per_row = LANES // ln
    assert t_total % workers == 0 and tpw % g == 0 and g % 2 == 0 and w % ln == 0
    buf = pltpu.VMEM((r, w), tab.dtype)
    acc = pltpu.VMEM((2 * w,), F32)
    sem = pltpu.SemaphoreType.DMA

    @pl.kernel(out_type=jax.ShapeDtypeStruct((t_total, 2 * w), F32), mesh=_sc_mesh(),
               scratch_types=[pltpu.VMEM((g * HK,), jnp.int32), pltpu.VMEM((g * HK // per_row, LANES), jnp.uint32), acc, acc]
               + [buf] * ns + [sem] * (ns + 2),
               compiler_params=pltpu.CompilerParams(needs_layout_passes=False))
    def wsum(t_hbm, i_hbm, w_hbm, o_hbm, idx_v, w_v, acc0, acc1, *scratch):
        bufs, gsem, osem = scratch[0:ns], scratch[ns:2 * ns], scratch[2 * ns:2 * ns + 2]
        accs = (acc0, acc1)
        wid = _sc_worker()

        @pl.loop(0, tpw // g)
        def _(win):
            tok0 = wid * tpw + win * g
            pltpu.sync_copy(i_hbm.at[pl.ds(tok0 * HK, g * HK)], idx_v)
            pltpu.sync_copy(w_hbm.at[pl.ds(tok0 * (HK // per_row), g * HK // per_row)], w_v)

            def gather_of(tl, s):
                return pltpu.make_async_copy(t_hbm.at[idx_v.at[pl.ds(tl * HK + s * r, r)]], bufs[s], gsem[s])

            def out_of(tl, par):
                return pltpu.make_async_copy(accs[par], o_hbm.at[tok0 + tl], osem[par])

            for s in range(ns):
                gather_of(0, s).start()

            @pl.loop(0, g // 2)
            def _(tp):
                for par in range(2):
                    tl = tp * 2 + par
                    ob = accs[par]

                    @pl.when(tl >= 2)
                    def _():
                        out_of(tl - 2, par).wait()

                    for s in range(ns):
                        gather_of(tl, s).wait()
                        wrow = tl * (HK // per_row) + s * (r // per_row)
                        wk = [plsc.bitcast(w_v[wrow + q // per_row, pl.ds((q % per_row) * ln, ln)], BF16)
                              for q in range(r)]
                        rows = bufs[s]

                        def fold(c, carry, first=(s == 0)):
                            for half in range(SUM_UNROLL):
                                col = pl.multiple_of((c * SUM_UNROLL + half) * ln, ln)
                                los, his = [], []
                                for q in range(0, r, SUM_BF16_ROWS):
                                    prods = [plsc.bitcast(rows[q + i, pl.ds(col, ln)], BF16) * wk[q + i]
                                             for i in range(SUM_BF16_ROWS)]
                                    pair = plsc.bitcast(_tree_sum(prods), jnp.uint32)
                                    los.append(plsc.bitcast(lax.shift_left(pair, HALF_WORD), F32))
                                    his.append(plsc.bitcast(pair & HIGH_HALF, F32))
                                if first:
                                    ob[pl.ds(col, ln)] = _tree_sum(los)
                                    ob[pl.ds(w + col, ln)] = _tree_sum(his)
                                else:
                                    plsc.addupdate(ob.at[pl.ds(col, ln)], _tree_sum(los))
                                    plsc.addupdate(ob.at[pl.ds(w + col, ln)], _tree_sum(his))
                            return carry

                        lax.fori_loop(0, w // (ln * SUM_UNROLL), fold, 0)

                        @pl.when(tl + 1 < g)
                        def _():
                            gather_of(tl + 1, s).start()

                    out_of(tl, par).start()

            for par in range(2):
                out_of(g - 2 + par, par).wait()

    return wsum(tab, idx_flat, wrep)


def _peerw_kernel(part_ref, gate_ref, fold_ref, rep_ref, o_ref):
    hid = _mm_split(part_ref[...], fold_ref[...], NN, 3, 1)
    w = gate_ref[...] * (0.5 * hid * (1.0 + lax.erf(hid * (2.0 ** -0.5))))
    rep = jnp.dot(w.astype(BF16), rep_ref[...], preferred_element_type=F32)
    bits = pltpu.bitcast(rep, jnp.uint32)
    o_ref[...] = bits | lax.shift_right_logical(bits, HALF_WORD)


def _peerw(part, gate, tt):
    n = gate.shape[0]
    wide = HK * SC_LANES
    lane = jnp.arange(wide)
    fold = (lane[:, None] // SC_LANES == jnp.arange(HK)[None, :]).astype(BF16)
    row = lambda i: (i, 0)
    fix = lambda i: (0, 0)
    return pl.pallas_call(
        _peerw_kernel,
        grid=(n // tt,),
        in_specs=[pl.BlockSpec((tt, wide), row), pl.BlockSpec((tt, HK), row), pl.BlockSpec((wide, HK), fix),
                  pl.BlockSpec((HK, wide), fix)],
        out_specs=pl.BlockSpec((tt, wide), row),
        out_shape=jax.ShapeDtypeStruct((n, wide), jnp.uint32),
        compiler_params=_params("parallel"),
    )(part.reshape(n, wide), gate, fold, fold.T)


def _ple_kernel(h_ref, f_ref, p_ref, g_ref, wg_ref, wp_ref, gf_ref, o_ref):
    h = h_ref[...] + f_ref[...]
    gate = jax.nn.sigmoid(jnp.dot(_rms(h, g_ref[...]).astype(BF16), wg_ref[...], preferred_element_type=F32))
    pp = jnp.dot(p_ref[...].astype(BF16), wp_ref[...], preferred_element_type=F32)
    o_ref[...] = _rms(h + gate * pp, gf_ref[...])


def _ple(h1, ffn, p2, g, wg, wp, gf, tm, row0):
    n, d = h1.shape
    pd = p2.shape[1]
    row = lambda i: (i, 0)
    off = lambda i: (row0 // tm + i, 0)
    fix = lambda i: (0, 0)
    return pl.pallas_call(
        _ple_kernel,
        grid=(n // tm,),
        in_specs=[pl.BlockSpec((tm, d), row), pl.BlockSpec((tm, d), row), pl.BlockSpec((tm, pd), off),
                  pl.BlockSpec((1, d), fix), pl.BlockSpec((d, d), fix), pl.BlockSpec((pd, d), fix),
                  pl.BlockSpec((1, d), fix)],
        out_specs=pl.BlockSpec((tm, d), row),
        out_shape=jax.ShapeDtypeStruct((n, d), F32),
        compiler_params=_params("parallel"),
    )(h1, ffn, p2, g, wg, wp, gf)


def _rope_tables(positions):
    half = ROT_DIM // 2
    inv_freq = ROPE_THETA ** (-jnp.arange(half, dtype=F32) * 2.0 / ROT_DIM)
    ang = positions.astype(F32).reshape(-1, 1) * inv_freq
    d = jnp.arange(LANES) % DA_HEAD_DIM
    cos = jnp.tile(jnp.cos(ang), (1, LANES // half))
    sin = jnp.tile(jnp.sin(ang), (1, LANES // half))
    c = jnp.where(d < ROT_DIM, cos, 1.0)
    s1 = jnp.where(d < half, -sin, 0.0)
    s2 = jnp.where((d >= half) & (d < ROT_DIM), sin, 0.0)
    return c, s1, s2


def _pack_rows(tab):
    d = tab.shape[1]
    bits = lax.bitcast_convert_type(tab.astype(F32), jnp.uint32)
    rne = bits + jnp.uint32(0x7FFF) + ((bits >> HALF_WORD) & jnp.uint32(1))
    return (rne[:, :d // 2] >> HALF_WORD) | (rne[:, d // 2:] & HIGH_HALF)


def _block_diag_ones(width, head):
    i = jnp.arange(width)
    return (i[:, None] // head == i[None, :] // head).astype(F32)


def _tiles(seq):
    pieces = 4 if seq % 8192 == 0 else 1
    rows = seq // pieces
    return dict(pieces=pieces, rows=rows, tm=min(256, rows), tq=min(512, rows), units=min(4, rows // RW_UNIT), groups=2,
                tt_topk=min(256, rows), tt_mix=min(256, rows))


def kernel(x, p, positions, norm_mix_g, w_in, lam_q1, lam_k1, lam_q2, lam_k2, da_subln_g, rw_mu, rw_w0, rw_w_up, rw_a0, rw_a_up, rw_g_up, rw_k_k, rw_k_a, rw_r_k, rw_ln_g, rw_ln_b, w_out, norm_ffn_g, peer_w_q, peer_sub_keys, peer_u, peer_v, norm_ple_g, ple_gate_w, ple_proj_w, norm_final_g):
    batch, seq, d = x.shape
    t = _tiles(seq)
    row = lambda a: a.reshape(1, -1)
    f32 = F32

    w_in_bf = w_in[0].astype(BF16)
    lam = (jnp.exp(jnp.sum(lam_q1[0].astype(f32) * lam_k1[0].astype(f32)))
           - jnp.exp(jnp.sum(lam_q2[0].astype(f32) * lam_k2[0].astype(f32))) + LAM_INIT).reshape(1, 1)
    width = rw_w0.shape[1]
    wup_pad = jnp.concatenate([rw_w_up[0], jnp.zeros((LANES - rw_w_up.shape[1], width), f32)], axis=0)
    aup_pad = jnp.concatenate([jnp.zeros((LANES - rw_a_up.shape[1], width), f32), rw_a_up[0]], axis=0)
    head_ones = _block_diag_ones(width, RW_HEAD)
    w_out_bf = w_out[0].astype(BF16)
    da_w = DA_HEADS * 2 * DA_HEAD_DIM
    keys = peer_sub_keys[0].reshape(PEER_HEADS * 2, N_KEYS, -1).astype(BF16)
    wq = peer_w_q[0].astype(BF16)
    u_tab = _pack_rows(peer_u[0])
    v_tab = None
    wg_bf = ple_gate_w[0].astype(BF16)
    wp_bf = ple_proj_w[0].astype(BF16)

    rope = _rope_tables(positions)
    x2 = x.reshape(batch * seq, d)
    p2 = p[0].reshape(batch * seq, -1)
    rows = t["rows"]
    pieces = [(b, h) for b in range(batch) for h in range(t["pieces"])]
    row0_of = lambda piece: piece[0] * seq + piece[1] * rows
    gain = row(norm_mix_g[0])
    tie = lax.optimization_barrier

    def mixers(piece, gain_in, carry):
        row0 = row0_of(piece)
        qkv, zrw = _inproj(x2, gain_in, w_in_bf, *rope, t["tm"], row0, rows)
        if carry is None:
            carry = dict(qkv=qkv[:0], z_last=jnp.zeros((1, zrw.shape[1]), f32),
                         state=jnp.zeros((width // (t["groups"] * LANES), t["groups"], RW_UNIT, RW_UNIT), f32))
        qkv = jnp.concatenate([carry["qkv"], qkv], axis=0)
        o_da = _attention(qkv, lam, row(da_subln_g[0]), t["tq"], rows)
        rw = _rwprep(zrw, carry["z_last"], row(rw_mu[0]), row(rw_w0[0]), wup_pad, row(rw_a0[0]), aup_pad, rw_g_up[0],
                     row(rw_k_k[0]), row(rw_k_a[0]), head_ones, t["tm"])
        o_rw, state = _rwcore(*rw, row(rw_ln_g[0]), row(rw_ln_b[0]), row(rw_r_k[0]), carry["state"], t["units"], t["groups"])
        h1, u2 = _outproj(x2, o_da, o_rw, w_out_bf[:da_w], w_out_bf[da_w:], row(norm_ffn_g[0]), t["tm"], row0)
        idx, gate = _peertopk(u2, wq, keys, t["tt_topk"])
        idx = idx.reshape(-1)
        part = _row_dots(u_tab, idx, _pack_rows(u2))
        return dict(h1=h1, gate=gate, idx=idx, part=part), dict(qkv=qkv, z_last=zrw[-1:], state=state)

    def out(s, ffn, piece):
        return _ple(s["h1"], ffn, p2, row(norm_ple_g[0]), wg_bf, wp_bf, row(norm_final_g), t["tm"], row0_of(piece))

    outs = []
    prev, older, older_ffn, carry = None, None, None, None
    gain_in = gain
    for j in range(len(pieces) + 1):
        cur = None
        if j < len(pieces):
            cur, carry = mixers(pieces[j], gain_in, carry if pieces[j][1] > 0 else None)
        if prev is not None:
            tied = [prev["part"]] + ([cur["idx"]] if cur is not None else []) + ([older_ffn] if older is not None else [])
            tied = tie(tuple(tied))
            if older is not None:
                outs.append(out(older, tied[-1], pieces[j - 2]))
            if v_tab is None:
                v_tab = _pack_rows(tie((peer_v[0], (cur or prev)["idx"]))[0])
            wrep = _peerw(tied[0], prev["gate"], t["tt_mix"])
            if j + 1 < len(pieces):
                wrep, gain_in = tie((wrep, gain))
            older, older_ffn = prev, _weighted_row_sum(v_tab, prev["idx"], wrep.reshape(-1, LANES))
        prev = cur
    outs.append(out(older, older_ffn, pieces[-1]))
    return jnp.concatenate(outs, axis=0).reshape(batch, seq, d)
```

```python
import functools
import math

import jax
import jax.numpy as jnp
import numpy as np
from jax import lax
from jax.experimental import pallas as pl
from jax.experimental.pallas import tpu as pltpu
from jax.experimental.pallas import tpu_sc as plsc

F32 = jnp.float32
BF16 = jnp.bfloat16

NORM_EPS = 1e-6
DA_HEADS = 4
DA_HEAD_DIM = 64
ROPE_THETA = 500000.0
ROT_DIM = DA_HEAD_DIM // 4
RW_HEAD = 64
RW_GN_EPS = 64e-5
PEER_HEADS = 8
N_KEYS = 128
PEER_TOPK = 16
LAM_INIT = 0.8 - 0.6 * math.exp(-0.3 * 0)

LANES = 128
VMEM_LIMIT = 56 * 1024 * 1024
RW_CHUNK = 64
RW_UNIT = 2 * RW_CHUNK

NN = (((1,), (0,)), ((), ()))
NT = (((1,), (1,)), ((), ()))
TN = (((0,), (0,)), ((), ()))
HI = lax.Precision.HIGHEST

HALF_WORD = np.uint32(16)
HIGH_HALF = np.uint32(0xFFFF0000)


def _mm(a, b):
    return jnp.dot(a.astype(BF16), b.astype(BF16), preferred_element_type=F32)


def _mm_nt(a, b):
    return lax.dot_general(a.astype(BF16), b.astype(BF16), NT, preferred_element_type=F32)


def _mm_tn(a, b):
    return lax.dot_general(a.astype(BF16), b.astype(BF16), TN, preferred_element_type=F32)


def _mm_hi(a, b):
    return jnp.dot(a, b, precision=HI, preferred_element_type=F32)


def _pieces(a, n):
    out = []
    for _ in range(n):
        piece = a.astype(BF16)
        out.append(piece)
        a = a - piece.astype(F32)
    return out


def _mm_split(a, b, dims, a_pieces, b_pieces):
    ap, bp = _pieces(a, a_pieces), _pieces(b, b_pieces)
    terms = [lax.dot_general(x, y, dims, preferred_element_type=F32)
             for i, x in enumerate(ap) for j, y in enumerate(bp) if i + j < max(a_pieces, b_pieces)]
    return functools.reduce(lambda u, v: u + v, terms)


def _params(*sem):
    return pltpu.CompilerParams(dimension_semantics=sem, vmem_limit_bytes=VMEM_LIMIT)


def _rms(x, g):
    return x * lax.rsqrt(jnp.mean(x * x, axis=-1, keepdims=True) + NORM_EPS) * g


def _inproj_kernel(x_ref, g_ref, w_ref, c_ref, s1_ref, s2_ref, qkv_ref, zrw_ref, *, n_qk, n_da):
    u = _rms(x_ref[...], g_ref[...]).astype(BF16)
    z = jnp.dot(u, w_ref[...], preferred_element_type=F32)
    c, s1, s2 = c_ref[...], s1_ref[...], s2_ref[...]
    half = ROT_DIM // 2
    for blk in range(n_da // LANES):
        t = z[:, blk * LANES:(blk + 1) * LANES]
        if blk < 2 * n_qk // LANES:
            t = t * c + pltpu.roll(t, LANES - half, 1) * s1 + pltpu.roll(t, half, 1) * s2
        if blk < n_qk // LANES:
            t = t * (DA_HEAD_DIM ** -0.5 * math.log2(math.e))
        qkv_ref[:, blk * LANES:(blk + 1) * LANES] = t.astype(BF16)
    zrw_ref[...] = z[:, n_da:]


def _inproj(x2, g, w_in_bf, rc, rs1, rs2, tm, row0, n):
    d = x2.shape[1]
    n_in = w_in_bf.shape[1]
    n_qk = DA_HEADS * 2 * DA_HEAD_DIM
    n_da = 3 * n_qk
    row = lambda i: (i, 0)
    off = lambda i: (row0 // tm + i, 0)
    fix = lambda i: (0, 0)
    return pl.pallas_call(
        functools.partial(_inproj_kernel, n_qk=n_qk, n_da=n_da),
        grid=(n // tm,),
        in_specs=[pl.BlockSpec((tm, d), off), pl.BlockSpec((1, d), fix), pl.BlockSpec((d, n_in), fix),
                  pl.BlockSpec((tm, LANES), off), pl.BlockSpec((tm, LANES), off), pl.BlockSpec((tm, LANES), off)],
        out_specs=[pl.BlockSpec((tm, n_da), row), pl.BlockSpec((tm, n_in - n_da), row)],
        out_shape=[jax.ShapeDtypeStruct((n, n_da), BF16), jax.ShapeDtypeStruct((n, n_in - n_da), F32)],
        compiler_params=_params("parallel"),
    )(x2, g, w_in_bf, rc, rs1, rs2)


def _attn_kernel(lam_ref, q_ref, k_ref, v_ref, sg_ref, o_ref, m_ref, acc_ref, *, tq, q_tile0):
    i = pl.program_id(1) + q_tile0
    q = q_ref[...]
    lane = lax.broadcasted_iota(jnp.int32, q.shape, 1)
    zero = jnp.zeros_like(q)
    qs = (jnp.where(lane < DA_HEAD_DIM, q, zero), jnp.where(lane >= DA_HEAD_DIM, q, zero))
    m_ref[...] = jnp.full(m_ref.shape, -jnp.inf, F32)
    acc_ref[...] = jnp.zeros(acc_ref.shape, F32)
    ones = jnp.ones((tq, LANES), BF16)

    def block(j, masked):
        kj = k_ref[pl.ds(pl.multiple_of(j * tq, tq), tq), :]
        vj = jnp.concatenate([v_ref[pl.ds(pl.multiple_of(j * tq, tq), tq), :], ones], axis=1)
        for c in range(2):
            s = lax.dot_general(qs[c], kj, NT, preferred_element_type=F32)
            if masked:
                r_id = lax.broadcasted_iota(jnp.int32, s.shape, 0)
                c_id = lax.broadcasted_iota(jnp.int32, s.shape, 1)
                s = jnp.where(c_id <= r_id, s, -jnp.inf)
            m_old = m_ref[c]
            m_new = jnp.maximum(m_old, jnp.max(s, axis=-1, keepdims=True))
            alpha = jnp.exp2(m_old - m_new)
            p = jnp.exp2(s - jnp.tile(m_new, (1, tq // LANES)))
            pv = jnp.dot(p.astype(BF16), vj, preferred_element_type=F32)
            acc_ref[c] = jnp.tile(alpha, (1, 2)) * acc_ref[c] + pv
            m_ref[c] = m_new

    def body(j, carry):
        block(j, False)
        return carry

    lax.fori_loop(0, i, body, 0)
    block(i, True)
    lam = lam_ref[0, 0]
    a0, a1 = acc_ref[0], acc_ref[1]
    o = a0[:, :LANES] / a0[:, LANES:] - lam * (a1[:, :LANES] / a1[:, LANES:])
    o = o * lax.rsqrt(jnp.mean(o * o, axis=-1, keepdims=True) + NORM_EPS) * sg_ref[...] * (1.0 - LAM_INIT)
    o_ref[...] = o.astype(o_ref.dtype)


def _attention(qkv, lam, subln_g, tq, q_rows):
    kv_rows = qkv.shape[0]
    nq = q_rows // tq
    q_tile0 = (kv_rows - q_rows) // tq
    h = DA_HEADS
    return pl.pallas_call(
        functools.partial(_attn_kernel, tq=tq, q_tile0=q_tile0),
        grid=(h, nq),
        in_specs=[pl.BlockSpec(memory_space=pltpu.SMEM),
                  pl.BlockSpec((tq, LANES), lambda hh, i: (q_tile0 + i, hh)),
                  pl.BlockSpec((kv_rows, LANES), lambda hh, i: (0, h + hh)),
                  pl.BlockSpec((kv_rows, LANES), lambda hh, i: (0, 2 * h + hh)),
                  pl.BlockSpec((1, LANES), lambda hh, i: (0, 0))],
        out_specs=pl.BlockSpec((tq, LANES), lambda hh, i: (i, hh)),
        out_shape=jax.ShapeDtypeStruct((q_rows, h * LANES), BF16),
        scratch_shapes=[pltpu.VMEM((2, tq, LANES), F32), pltpu.VMEM((2, tq, 2 * LANES), F32)],
        compiler_params=_params("parallel", "arbitrary"),
    )(lam, qkv, qkv, qkv, subln_g)


def _rwprep_kernel(z_ref, zp_ref, z0_ref, mu_ref, w0_ref, wup_ref, a0_ref, aup_ref, gup_ref, kk_ref, ka_ref, bd_ref,
                   r_o, ld_o, k_o, v_o, kk_o, b_o, g_o, *, width):
    i = pl.program_id(0)
    z = z_ref[...]
    first = jnp.where(i == 0, z0_ref[...], zp_ref[7:8, :])
    row = lax.broadcasted_iota(jnp.int32, z.shape, 0)
    prev = jnp.where(row == 0, first, pltpu.roll(z, 1, 0))
    zs = z + (prev - z) * mu_ref[...]
    r = zs[:, 0:width]
    k = zs[:, width:2 * width]
    v = zs[:, 2 * width:3 * width]
    xwa = zs[:, 3 * width:3 * width + LANES]
    xg = zs[:, 3 * width + LANES:3 * width + 2 * LANES]
    w = -jax.nn.softplus(-(w0_ref[...] + _mm_hi(jnp.tanh(xwa), wup_ref[...]))) - 0.5
    a = jax.nn.sigmoid(a0_ref[...] + _mm_hi(xwa, aup_ref[...]))
    g = _mm_hi(jax.nn.sigmoid(xg), gup_ref[...])
    kk = k * kk_ref[...]
    kk = kk / jnp.maximum(jnp.sqrt(_mm_hi(kk * kk, bd_ref[...])), 1e-12)
    r_o[...] = r
    ld_o[...] = -jnp.exp(w)
    k_o[...] = k * (1.0 + (a - 1.0) * ka_ref[...])
    v_o[...] = v
    kk_o[...] = kk
    b_o[...] = kk * a
    g_o[...] = g


def _rwprep(zrw, z_before, mu, w0, wup_pad, a0, aup_pad, gup, k_k, k_a, bd, tm):
    n, zin = zrw.shape
    width = w0.shape[1]
    row = lambda i: (i, 0)
    fix = lambda i: (0, 0)
    prev = lambda i: (jnp.maximum(i * (tm // 8) - 1, 0), 0)
    out = jax.ShapeDtypeStruct((n, width), F32)
    return pl.pallas_call(
        functools.partial(_rwprep_kernel, width=width),
        grid=(n // tm,),
        in_specs=[pl.BlockSpec((tm, zin), row), pl.BlockSpec((8, zin), prev), pl.BlockSpec((1, zin), fix),
                  pl.BlockSpec((1, zin), fix),
                  pl.BlockSpec((1, width), fix), pl.BlockSpec((LANES, width), fix),
                  pl.BlockSpec((1, width), fix), pl.BlockSpec((LANES, width), fix), pl.BlockSpec((LANES, width), fix),
                  pl.BlockSpec((1, width), fix), pl.BlockSpec((1, width), fix), pl.BlockSpec((width, width), fix)],
        out_specs=[pl.BlockSpec((tm, width), row)] * 7,
        out_shape=[out] * 7,
        compiler_params=_params("parallel"),
    )(zrw, zrw, z_before, mu, w0, wup_pad, a0, aup_pad, gup, k_k, k_a, bd)


def _rwcore_kernel(r_ref, ld_ref, k_ref, v_ref, kk_ref, b_ref, g_ref, lng_ref, lnb_ref, rk_ref, s0_ref,
                   o_ref, sout_ref, s_ref, *, units, groups):
    U, C, HD = RW_UNIT, RW_CHUNK, RW_HEAD

    @pl.when(pl.program_id(1) == 0)
    def _():
        s_ref[...] = s0_ref[0]

    ri = lax.broadcasted_iota(jnp.int32, (U, U), 0)
    ci = lax.broadcasted_iota(jnp.int32, (U, U), 1)
    same = (ri // C) == (ci // C)
    tri_s = same & (ci < ri)
    tri_i = same & (ci <= ri)
    eye = (ri == ci).astype(F32)
    cum_w = tri_i.astype(F32)
    head_avg = same.astype(F32) * (1.0 / HD)
    head_sum = same.astype(F32)
    hmask = (ci < HD, ci >= HD)
    cmask = (ri < C, ri >= C)
    zero = jnp.zeros((U, U), F32)
    cells = [(u, q) for u in range(units) for q in range(groups)]

    def blk(ref, cell):
        u, q = cell
        return ref[pl.ds(u * U, U), q * LANES:(q + 1) * LANES]

    def vec(ref, q):
        return ref[:, q * LANES:(q + 1) * LANES]

    cell_v, cell_at, cell_bt, cell_rt, cell_bk, cell_gam = {}, {}, {}, {}, {}, {}
    for cell in cells:
        ld = blk(ld_ref, cell)
        cum = _mm_split(cum_w, ld, NN, 1, 3)
        gam = jnp.exp(cum)
        ginv = jnp.exp(-cum)
        bt = blk(b_ref, cell) * ginv
        cell_v[cell] = blk(v_ref, cell)
        cell_at[cell] = -blk(kk_ref, cell) * jnp.exp(cum - ld)
        cell_bt[cell] = bt
        cell_rt[cell] = blk(r_ref, cell) * gam
        cell_bk[cell] = jnp.concatenate([bt, blk(k_ref, cell) * ginv], axis=0)
        cell_gam[cell] = gam

    chains = [(cell, h) for cell in cells for h in range(2)]
    mab, mak, mrb, mrk = {}, {}, {}, {}
    for ch in chains:
        cell, h = ch
        ar = jnp.concatenate([jnp.where(hmask[h], cell_at[cell], zero), jnp.where(hmask[h], cell_rt[cell], zero)], axis=0)
        m = _mm_nt(ar, cell_bk[cell])
        mab[ch] = jnp.where(tri_s, m[0:U, 0:U], zero)
        mak[ch] = jnp.where(tri_s, m[0:U, U:2 * U], zero)
        mrb[ch] = jnp.where(tri_i, m[U:2 * U, 0:U], zero)
        mrk[ch] = jnp.where(tri_i, m[U:2 * U, U:2 * U], zero)

    tm = {ch: eye + mab[ch] for ch in chains}
    pw = dict(mab)
    for _ in range(int(math.log2(C)) - 1):
        pw = {ch: _mm(pw[ch], pw[ch]) for ch in chains}
        tm = {ch: tm[ch] + _mm(tm[ch], pw[ch]) for ch in chains}
    aph = {ch: _mm(tm[ch], cell_at[ch[0]]) for ch in chains}
    mv = {ch: _mm(mak[ch], cell_v[ch[0]]) for ch in chains}
    uh = {ch: _mm(tm[ch], mv[ch]) for ch in chains}
    rph = {ch: _mm(mrb[ch], aph[ch]) for ch in chains}
    yph = {ch: _mm(mrb[ch], uh[ch]) + _mm(mrk[ch], cell_v[ch[0]]) for ch in chains}

    def both_heads(d, cell):
        return jnp.where(hmask[0], d[(cell, 0)], d[(cell, 1)])

    rp, yp, gs, hs = {}, {}, {}, {}
    for cell in cells:
        ap, uu = both_heads(aph, cell), both_heads(uh, cell)
        rp[cell] = both_heads(rph, cell) + cell_rt[cell]
        yp[cell] = both_heads(yph, cell)
        for c in range(2):
            gl = cell_gam[cell][(c + 1) * C - 1:(c + 1) * C, :]
            apc = jnp.where(cmask[c], ap, zero)
            uvc = jnp.concatenate([jnp.where(cmask[c], uu, zero), jnp.where(cmask[c], cell_v[cell], zero)], axis=0)
            gs[cell, c] = jnp.where(same, eye + _mm_tn(apc, cell_bt[cell]), zero) * gl
            hs[cell, c] = jnp.where(same, _mm_tn(uvc, cell_bk[cell]), zero) * gl

    s = [s_ref[q] for q in range(groups)]
    ys = {}
    for u in range(units):
        for q in range(groups):
            cell = (u, q)
            y0 = _mm_split(rp[cell], s[q], NT, 2, 2)
            s[q] = _mm_split(s[q], gs[cell, 0], NN, 2, 2) + hs[cell, 0]
            y1 = _mm_split(rp[cell], s[q], NT, 2, 2)
            s[q] = _mm_split(s[q], gs[cell, 1], NN, 2, 2) + hs[cell, 1]
            ys[cell] = jnp.where(cmask[0], y0, y1) + yp[cell]
    for q in range(groups):
        s_ref[q] = s[q]
        sout_ref[0, q] = s[q]

    for cell in cells:
        u, q = cell
        y = ys[cell]
        mean = _mm_split(y, head_avg, NN, 2, 1)
        yc = y - mean
        var = _mm_split(yc * yc, head_avg, NN, 2, 1)
        yn = yc * lax.rsqrt(var + RW_GN_EPS) * vec(lng_ref, q) + vec(lnb_ref, q)
        bonus = _mm_split(blk(r_ref, cell) * blk(k_ref, cell) * vec(rk_ref, q), head_sum, NN, 2, 1)
        yn = yn + bonus * cell_v[cell]
        o_ref[pl.ds(u * U, U), q * LANES:(q + 1) * LANES] = (yn * blk(g_ref, cell)).astype(o_ref.dtype)


def _rwcore(r, ld, k, v, kk, b, g, ln_g, ln_b, r_k, state, units, groups):
    n, width = r.shape
    rows = units * RW_UNIT
    lanes = groups * LANES
    blk = pl.BlockSpec((rows, lanes), lambda hp, i: (i, hp))
    vec = pl.BlockSpec((1, lanes), lambda hp, i: (0, hp))
    st = pl.BlockSpec((1, groups, RW_UNIT, RW_UNIT), lambda hp, i: (hp, 0, 0, 0))
    return pl.pallas_call(
        functools.partial(_rwcore_kernel, units=units, groups=groups),
        grid=(width // lanes, n // rows),
        in_specs=[blk] * 7 + [vec] * 3 + [st],
        out_specs=[blk, st],
        out_shape=[jax.ShapeDtypeStruct((n, width), BF16), jax.ShapeDtypeStruct(state.shape, F32)],
        scratch_shapes=[pltpu.VMEM((groups, RW_UNIT, RW_UNIT), F32)],
        compiler_params=_params("parallel", "arbitrary"),
    )(r, ld, k, v, kk, b, g, ln_g, ln_b, r_k, state)


def _outproj_kernel(x_ref, oda_ref, orw_ref, wa_ref, wb_ref, g_ref, h_ref, u_ref):
    h = (x_ref[...] + jnp.dot(oda_ref[...], wa_ref[...], preferred_element_type=F32)
         + jnp.dot(orw_ref[...], wb_ref[...], preferred_element_type=F32))
    h_ref[...] = h
    u_ref[...] = _rms(h, g_ref[...]).astype(BF16)


def _outproj(x2, o_da, o_rw, wa, wb, g, tm, row0):
    d = x2.shape[1]
    n, da = o_da.shape
    rw = o_rw.shape[1]
    row = lambda i: (i, 0)
    off = lambda i: (row0 // tm + i, 0)
    fix = lambda i: (0, 0)
    return pl.pallas_call(
        _outproj_kernel,
        grid=(n // tm,),
        in_specs=[pl.BlockSpec((tm, d), off), pl.BlockSpec((tm, da), row), pl.BlockSpec((tm, rw), row),
                  pl.BlockSpec((da, d), fix), pl.BlockSpec((rw, d), fix), pl.BlockSpec((1, d), fix)],
        out_specs=[pl.BlockSpec((tm, d), row), pl.BlockSpec((tm, d), row)],
        out_shape=[jax.ShapeDtypeStruct((n, d), F32), jax.ShapeDtypeStruct((n, d), BF16)],
        compiler_params=_params("parallel"),
    )(x2, o_da, o_rw, wa, wb, g)


def _topk_rows(s, k, payload=None):
    rows = s.shape[0]
    iota = lax.broadcasted_iota(jnp.int32, s.shape, 0).astype(F32)
    vals, sel = [], []
    for _ in range(k):
        m = jnp.max(s, axis=0, keepdims=True)
        am = jnp.min(jnp.where(s == m, iota, float(rows)), axis=0, keepdims=True)
        hit = iota == am
        vals.append(m)
        sel.append(am if payload is None else jnp.sum(jnp.where(hit, payload, 0.0), axis=0, keepdims=True))
        s = jnp.where(hit, -jnp.inf, s)
    return vals, sel


def _stack_rows(rows_list):
    k = len(rows_list)
    iota = lax.broadcasted_iota(jnp.int32, (k, rows_list[0].shape[1]), 0)
    out = jnp.zeros(iota.shape, rows_list[0].dtype)
    for j, r in enumerate(rows_list):
        out = jnp.where(iota == j, r, out)
    return out


def _peertopk_kernel(u_ref, wq_ref, keys_ref, idx_ref, gate_ref):
    u = u_ref[...]
    half = N_KEYS
    idx_rows, gate_rows = [], []
    for h in range(PEER_HEADS):
        tops = []
        for p in range(2):
            hp = h * 2 + p
            q_t = lax.dot_general(wq_ref[:, hp * half:(hp + 1) * half], u, (((0,), (1,)), ((), ())),
                                  preferred_element_type=F32)
            s_t = jnp.dot(keys_ref[hp], q_t.astype(BF16), preferred_element_type=F32)
            tops.append(_topk_rows(s_t, PEER_TOPK))
        (v1, i1), (v2, i2) = tops
        pairs = [(i, j) for i in range(PEER_TOPK) for j in range(PEER_TOPK) if (i + 1) * (j + 1) <= PEER_TOPK]
        pad = -len(pairs) % 8
        cand = _stack_rows([v1[i] + v2[j] for i, j in pairs] + [jnp.full_like(v1[0], -jnp.inf)] * pad)
        cidx = _stack_rows([i1[i] * float(N_KEYS) + i2[j] for i, j in pairs] + [jnp.zeros_like(i1[0])] * pad)
        best, idx = _topk_rows(cand, PEER_TOPK, payload=cidx)
        e = [jnp.exp(b - best[0]) for b in best]
        den = functools.reduce(lambda a, b: a + b, e)
        idx_rows.append(_stack_rows(idx))
        gate_rows.append(_stack_rows([x / den for x in e]))
    idx_ref[...] = jnp.concatenate(idx_rows, axis=0).T.astype(jnp.int32)
    gate_ref[...] = jnp.concatenate(gate_rows, axis=0).T


def _peertopk(u_bf, wq, keys, tt):
    n, d = u_bf.shape
    hk = PEER_HEADS * PEER_TOPK
    row = lambda i: (i, 0)
    return pl.pallas_call(
        _peertopk_kernel,
        grid=(n // tt,),
        in_specs=[pl.BlockSpec((tt, d), row), pl.BlockSpec(wq.shape, lambda i: (0, 0)),
                  pl.BlockSpec(keys.shape, lambda i: (0, 0, 0))],
        out_specs=[pl.BlockSpec((tt, hk), row), pl.BlockSpec((tt, hk), row)],
        out_shape=[jax.ShapeDtypeStruct((n, hk), jnp.int32), jax.ShapeDtypeStruct((n, hk), F32)],
        compiler_params=_params("parallel"),
    )(u_bf, wq, keys)


SC_CORES = 2
SC_SUBCORES = 16
SC_LANES = 16
HK = PEER_HEADS * PEER_TOPK
DOT_ROWS = 16
SUM_ROWS = 32
DOT_TOKENS = 64
SUM_TOKENS = 16
SUM_UNROLL = 2
SUM_BF16_ROWS = 4


def _tree_sum(xs):
    while len(xs) > 1:
        xs = [xs[i] + xs[i + 1] for i in range(0, len(xs) - 1, 2)] + ([xs[-1]] if len(xs) % 2 else [])
    return xs[0]


def _sc_mesh():
    return plsc.VectorSubcoreMesh(core_axis_name="core", subcore_axis_name="subcore")


def _sc_worker():
    return lax.axis_index("core") * SC_SUBCORES + lax.axis_index("subcore")


def _row_dots(tab, idx_flat, xw):
    p = idx_flat.shape[0]
    w = tab.shape[1]
    t_total = p // HK
    workers = SC_CORES * SC_SUBCORES
    tpw = t_total // workers
    g = min(DOT_TOKENS, tpw)
    r = DOT_ROWS
    ns = HK // r
    ln = SC_LANES
    per_row = LANES // ln
    out_rows = HK // per_row
    assert t_total % workers == 0 and tpw % g == 0 and g % 2 == 0 and w % (2 * ln) == 0 and r % per_row == 0
    buf = pltpu.VMEM((r, w), tab.dtype)
    res = pltpu.VMEM((out_rows, LANES), F32)
    sem = pltpu.SemaphoreType.DMA

    @pl.kernel(out_type=jax.ShapeDtypeStruct((t_total * out_rows, LANES), F32), mesh=_sc_mesh(),
               scratch_types=[pltpu.VMEM((g * HK,), jnp.int32), pltpu.VMEM((g, w), tab.dtype), res, res]
               + [buf] * ns + [sem] * (ns + 2),
               compiler_params=pltpu.CompilerParams(needs_layout_passes=False))
    def dots(t_hbm, i_hbm, x_hbm, o_hbm, idx_v, x_v, res0, res1, *scratch):
        bufs, gsem, osem = scratch[0:ns], scratch[ns:2 * ns], scratch[2 * ns:2 * ns + 2]
        ress = (res0, res1)
        wid = _sc_worker()
        zero = jnp.zeros((ln,), F32)

        @pl.loop(0, tpw // g)
        def _(win):
            tok0 = wid * tpw + win * g
            pltpu.sync_copy(i_hbm.at[pl.ds(tok0 * HK, g * HK)], idx_v)
            pltpu.sync_copy(x_hbm.at[pl.ds(tok0, g)], x_v)

            def gather_of(tl, s):
                return pltpu.make_async_copy(t_hbm.at[idx_v.at[pl.ds(tl * HK + s * r, r)]], bufs[s], gsem[s])

            def out_of(tl, par):
                return pltpu.make_async_copy(ress[par], o_hbm.at[pl.ds((tok0 + tl) * out_rows, out_rows)], osem[par])

            for s in range(ns):
                gather_of(0, s).start()

            @pl.loop(0, g // 2)
            def _(tp):
                for par in range(2):
                    tl = tp * 2 + par
                    ob = ress[par]

                    @pl.when(tl >= 2)
                    def _():
                        out_of(tl - 2, par).wait()

                    for s in range(ns):
                        gather_of(tl, s).wait()
                        rows = bufs[s]

                        def fold(c, accs):
                            col = pl.multiple_of(c * 2 * ln, 2 * ln)
                            xa = plsc.bitcast(x_v[tl, pl.ds(col, ln)], BF16)
                            xb = plsc.bitcast(x_v[tl, pl.ds(col + ln, ln)], BF16)
                            out = []
                            for q in range(r):
                                pa = plsc.bitcast(rows[q, pl.ds(col, ln)], BF16) * xa
                                pb = plsc.bitcast(rows[q, pl.ds(col + ln, ln)], BF16) * xb
                                pair = plsc.bitcast(pa + pb, jnp.uint32)
                                lo = plsc.bitcast(lax.shift_left(pair, HALF_WORD), F32)
                                hi = plsc.bitcast(pair & HIGH_HALF, F32)
                                out.append(accs[q] + lo + hi)
                            return tuple(out)

                        accs = lax.fori_loop(0, w // (2 * ln), fold, (zero,) * r)
                        for q in range(r):
                            k = s * r + q
                            ob[k // per_row, pl.ds((k % per_row) * ln, ln)] = accs[q]

                        @pl.when(tl + 1 < g)
                        def _():
                            gather_of(tl + 1, s).start()

                    out_of(tl, par).start()

            for par in range(2):
                out_of(g - 2 + par, par).wait()

    return dots(tab, idx_flat, xw)


def _weighted_row_sum(tab, idx_flat, wrep):
    p = idx_flat.shape[0]
    w = tab.shape[1]
    t_total = p // HK
    workers = SC_CORES * SC_SUBCORES
    tpw = t_total // workers
    g = min(SUM_TOKENS, tpw)
    r = SUM_ROWS
    ns = HK // r
    ln = SC_LANES
    per_row = LANES // ln
    assert t_total % workers == 0 and tpw % g == 0 and g % 2 == 0 and w % ln == 0
    buf = pltpu.VMEM((r, w), tab.dtype)
    acc = pltpu.VMEM((2 * w,), F32)
    sem = pltpu.SemaphoreType.DMA

    @pl.kernel(out_type=jax.ShapeDtypeStruct((t_total, 2 * w), F32), mesh=_sc_mesh(),
               scratch_types=[pltpu.VMEM((tpw * HK,), jnp.int32), pltpu.VMEM((g * HK // per_row, LANES), jnp.uint32), acc, acc]
               + [buf] * ns + [sem] * (ns + 2),
               compiler_params=pltpu.CompilerParams(needs_layout_passes=False))
    def wsum(t_hbm, i_hbm, w_hbm, o_hbm, idx_v, w_v, acc0, acc1, *scratch):
        bufs, gsem, osem = scratch[0:ns], scratch[ns:2 * ns], scratch[2 * ns:2 * ns + 2]
        accs = (acc0, acc1)
        wid = _sc_worker()
        pltpu.sync_copy(i_hbm.at[pl.ds(wid * tpw * HK, tpw * HK)], idx_v)

        @pl.loop(0, tpw // g)
        def _(win):
            tok0 = wid * tpw + win * g
            pltpu.sync_copy(w_hbm.at[pl.ds(tok0 * (HK // per_row), g * HK // per_row)], w_v)

            def gather_of(tl, s):
                return pltpu.make_async_copy(t_hbm.at[idx_v.at[pl.ds((win * g + tl) * HK + s * r, r)]], bufs[s], gsem[s])

            def out_of(tl, par):
                return pltpu.make_async_copy(accs[par], o_hbm.at[tok0 + tl], osem[par])

            @pl.when(win == 0)
            def _():
                for s in range(ns):
                    gather_of(0, s).start()

            @pl.loop(0, g // 2)
            def _(tp):
                for par in range(2):
                    tl = tp * 2 + par
                    ob = accs[par]

                    @pl.when(tl >= 2)
                    def _():
                        out_of(tl - 2, par).wait()

                    for s in range(ns):
                        gather_of(tl, s).wait()
                        wrow = tl * (HK // per_row) + s * (r // per_row)
                        wk = [plsc.bitcast(w_v[wrow + q // per_row, pl.ds((q % per_row) * ln, ln)], BF16)
                              for q in range(r)]
                        rows = bufs[s]

                        def fold(c, carry, first=(s == 0)):
                            for half in range(SUM_UNROLL):
                                col = pl.multiple_of((c * SUM_UNROLL + half) * ln, ln)
                                los, his = [], []
                                for q in range(0, r, SUM_BF16_ROWS):
                                    prods = [plsc.bitcast(rows[q + i, pl.ds(col, ln)], BF16) * wk[q + i]
                                             for i in range(SUM_BF16_ROWS)]
                                    pair = plsc.bitcast(_tree_sum(prods), jnp.uint32)
                                    los.append(plsc.bitcast(lax.shift_left(pair, HALF_WORD), F32))
                                    his.append(plsc.bitcast(pair & HIGH_HALF, F32))
                                if first:
                                    ob[pl.ds(col, ln)] = _tree_sum(los)
                                    ob[pl.ds(w + col, ln)] = _tree_sum(his)
                                else:
                                    plsc.addupdate(ob.at[pl.ds(col, ln)], _tree_sum(los))
                                    plsc.addupdate(ob.at[pl.ds(w + col, ln)], _tree_sum(his))
                            return carry

                        lax.fori_loop(0, w // (ln * SUM_UNROLL), fold, 0)

                        @pl.when(win * g + tl + 1 < tpw)
                        def _():
                            gather_of(tl + 1, s).start()

                    out_of(tl, par).start()

            for par in range(2):
                out_of(g - 2 + par, par).wait()

    return wsum(tab, idx_flat, wrep)


def _peerw_kernel(part_ref, gate_ref, fold_ref, rep_ref, o_ref):
    hid = _mm_split(part_ref[...], fold_ref[...], NN, 3, 1)
    w = gate_ref[...] * (0.5 * hid * (1.0 + lax.erf(hid * (2.0 ** -0.5))))
    rep = jnp.dot(w.astype(BF16), rep_ref[...], preferred_element_type=F32)
    bits = pltpu.bitcast(rep, jnp.uint32)
    o_ref[...] = bits | lax.shift_right_logical(bits, HALF_WORD)


def _peerw(part, gate, tt):
    n = gate.shape[0]
    wide = HK * SC_LANES
    lane = jnp.arange(wide)
    fold = (lane[:, None] // SC_LANES == jnp.arange(HK)[None, :]).astype(BF16)
    row = lambda i: (i, 0)
    fix = lambda i: (0, 0)
    return pl.pallas_call(
        _peerw_kernel,
        grid=(n // tt,),
        in_specs=[pl.BlockSpec((tt, wide), row), pl.BlockSpec((tt, HK), row), pl.BlockSpec((wide, HK), fix),
                  pl.BlockSpec((HK, wide), fix)],
        out_specs=pl.BlockSpec((tt, wide), row),
        out_shape=jax.ShapeDtypeStruct((n, wide), jnp.uint32),
        compiler_params=_params("parallel"),
    )(part.reshape(n, wide), gate, fold, fold.T)


def _ple_kernel(h_ref, f_ref, p_ref, g_ref, wg_ref, wp_ref, gf_ref, o_ref):
    h = h_ref[...] + f_ref[...]
    gate = jax.nn.sigmoid(jnp.dot(_rms(h, g_ref[...]).astype(BF16), wg_ref[...], preferred_element_type=F32))
    pp = jnp.dot(p_ref[...].astype(BF16), wp_ref[...], preferred_element_type=F32)
    o_ref[...] = _rms(h + gate * pp, gf_ref[...])


def _ple(h1, ffn, p2, g, wg, wp, gf, tm, row0):
    n, d = h1.shape
    pd = p2.shape[1]
    row = lambda i: (i, 0)
    off = lambda i: (row0 // tm + i, 0)
    fix = lambda i: (0, 0)
    return pl.pallas_call(
        _ple_kernel,
        grid=(n // tm,),
        in_specs=[pl.BlockSpec((tm, d), row), pl.BlockSpec((tm, d), row), pl.BlockSpec((tm, pd), off),
                  pl.BlockSpec((1, d), fix), pl.BlockSpec((d, d), fix), pl.BlockSpec((pd, d), fix),
                  pl.BlockSpec((1, d), fix)],
        out_specs=pl.BlockSpec((tm, d), row),
        out_shape=jax.ShapeDtypeStruct((n, d), F32),
        compiler_params=_params("parallel"),
    )(h1, ffn, p2, g, wg, wp, gf)


def _rope_tables(positions):
    half = ROT_DIM // 2
    inv_freq = ROPE_THETA ** (-jnp.arange(half, dtype=F32) * 2.0 / ROT_DIM)
    ang = positions.astype(F32).reshape(-1, 1) * inv_freq
    d = jnp.arange(LANES) % DA_HEAD_DIM
    cos = jnp.tile(jnp.cos(ang), (1, LANES // half))
    sin = jnp.tile(jnp.sin(ang), (1, LANES // half))
    c = jnp.where(d < ROT_DIM, cos, 1.0)
    s1 = jnp.where(d < half, -sin, 0.0)
    s2 = jnp.where((d >= half) & (d < ROT_DIM), sin, 0.0)
    return c, s1, s2


def _pack_rows(tab):
    d = tab.shape[1]
    bits = lax.bitcast_convert_type(tab.astype(F32), jnp.uint32)
    rne = bits + jnp.uint32(0x7FFF) + ((bits >> HALF_WORD) & jnp.uint32(1))
    return (rne[:, :d // 2] >> HALF_WORD) | (rne[:, d // 2:] & HIGH_HALF)


def _block_diag_ones(width, head):
    i = jnp.arange(width)
    return (i[:, None] // head == i[None, :] // head).astype(F32)


def _tiles(seq):
    pieces = 4 if seq % 8192 == 0 else 1
    rows = seq // pieces
    return dict(pieces=pieces, rows=rows, tm=min(256, rows), tq=min(512, rows), units=min(4, rows // RW_UNIT), groups=2,
                tt_topk=min(256, rows), tt_mix=min(256, rows))


def kernel(x, p, positions, norm_mix_g, w_in, lam_q1, lam_k1, lam_q2, lam_k2, da_subln_g, rw_mu, rw_w0, rw_w_up, rw_a0, rw_a_up, rw_g_up, rw_k_k, rw_k_a, rw_r_k, rw_ln_g, rw_ln_b, w_out, norm_ffn_g, peer_w_q, peer_sub_keys, peer_u, peer_v, norm_ple_g, ple_gate_w, ple_proj_w, norm_final_g):
    batch, seq, d = x.shape
    t = _tiles(seq)
    row = lambda a: a.reshape(1, -1)
    f32 = F32

    w_in_bf = w_in[0].astype(BF16)
    lam = (jnp.exp(jnp.sum(lam_q1[0].astype(f32) * lam_k1[0].astype(f32)))
           - jnp.exp(jnp.sum(lam_q2[0].astype(f32) * lam_k2[0].astype(f32))) + LAM_INIT).reshape(1, 1)
    width = rw_w0.shape[1]
    wup_pad = jnp.concatenate([rw_w_up[0], jnp.zeros((LANES - rw_w_up.shape[1], width), f32)], axis=0)
    aup_pad = jnp.concatenate([jnp.zeros((LANES - rw_a_up.shape[1], width), f32), rw_a_up[0]], axis=0)
    head_ones = _block_diag_ones(width, RW_HEAD)
    w_out_bf = w_out[0].astype(BF16)
    da_w = DA_HEADS * 2 * DA_HEAD_DIM
    keys = peer_sub_keys[0].reshape(PEER_HEADS * 2, N_KEYS, -1).astype(BF16)
    wq = peer_w_q[0].astype(BF16)
    u_tab = _pack_rows(peer_u[0])
    v_tab = None
    wg_bf = ple_gate_w[0].astype(BF16)
    wp_bf = ple_proj_w[0].astype(BF16)

    rope = _rope_tables(positions)
    x2 = x.reshape(batch * seq, d)
    p2 = p[0].reshape(batch * seq, -1)
    rows = t["rows"]
    pieces = [(b, h) for b in range(batch) for h in range(t["pieces"])]
    row0_of = lambda piece: piece[0] * seq + piece[1] * rows
    gain = row(norm_mix_g[0])
    tie = lax.optimization_barrier

    def mixers(piece, gain_in, carry):
        row0 = row0_of(piece)
        qkv, zrw = _inproj(x2, gain_in, w_in_bf, *rope, t["tm"], row0, rows)
        if carry is None:
            carry = dict(qkv=qkv[:0], z_last=jnp.zeros((1, zrw.shape[1]), f32),
                         state=jnp.zeros((width // (t["groups"] * LANES), t["groups"], RW_UNIT, RW_UNIT), f32))
        qkv = jnp.concatenate([carry["qkv"], qkv], axis=0)
        o_da = _attention(qkv, lam, row(da_subln_g[0]), t["tq"], rows)
        rw = _rwprep(zrw, carry["z_last"], row(rw_mu[0]), row(rw_w0[0]), wup_pad, row(rw_a0[0]), aup_pad, rw_g_up[0],
                     row(rw_k_k[0]), row(rw_k_a[0]), head_ones, t["tm"])
        o_rw, state = _rwcore(*rw, row(rw_ln_g[0]), row(rw_ln_b[0]), row(rw_r_k[0]), carry["state"], t["units"], t["groups"])
        h1, u2 = _outproj(x2, o_da, o_rw, w_out_bf[:da_w], w_out_bf[da_w:], row(norm_ffn_g[0]), t["tm"], row0)
        idx, gate = _peertopk(u2, wq, keys, t["tt_topk"])
        idx = idx.reshape(-1)
        part = _row_dots(u_tab, idx, _pack_rows(u2))
        return dict(h1=h1, gate=gate, idx=idx, part=part), dict(qkv=qkv, z_last=zrw[-1:], state=state)

    def out(s, ffn, piece):
        return _ple(s["h1"], ffn, p2, row(norm_ple_g[0]), wg_bf, wp_bf, row(norm_final_g), t["tm"], row0_of(piece))

    outs = []
    prev, older, older_ffn, carry = None, None, None, None
    gain_in = gain
    for j in range(len(pieces) + 1):
        cur = None
        if j < len(pieces):
            cur, carry = mixers(pieces[j], gain_in, carry if pieces[j][1] > 0 else None)
        if prev is not None:
            tied = [prev["part"]] + ([cur["idx"]] if cur is not None else []) + ([older_ffn] if older is not None else [])
            tied = tie(tuple(tied))
            if older is not None:
                outs.append(out(older, tied[-1], pieces[j - 2]))
            if v_tab is None:
                v_tab = _pack_rows(tie((peer_v[0], (cur or prev)["idx"]))[0])
            wrep = _peerw(tied[0], prev["gate"], t["tt_mix"])
            if j + 1 < len(pieces):
                wrep, gain_in = tie((wrep, gain))
            older, older_ffn = prev, _weighted_row_sum(v_tab, prev["idx"], wrep.reshape(-1, LANES))
        prev = cur
    outs.append(out(older, older_ffn, pieces[-1]))
    return jnp.concatenate(outs, axis=0).reshape(batch, seq, d)
```
